```python
import math
import jax, jax.numpy as jnp
from jax import lax
import numpy as np

D_MODEL = 1024
BATCH = 8
SEQ = 2048
DEPTH = 2
DEC_BATCH = 32
DEC_SEQ = 1
PAST_LEN = 16384
PAGE_SIZE = 128

N_EVEN = (DEPTH + 1) // 2
N_ODD = DEPTH // 2
D_A = D_MODEL // 2
D_B = D_MODEL // 2
N_HEADS = 8
HEAD_DIM = D_A // N_HEADS
N_KV_HEADS = 2
Q_PER_KV = N_HEADS // N_KV_HEADS
IDX_HEADS = 8
IDX_DIM = 64
TOPK_MAX = 256
Q_BLOCK = 128
N_BUCKETS = 32
REL_MAX_EXACT = N_BUCKETS // 2
REL_MAX_DIST = 128
RG_BLOCKS = 8
RG_BLOCK = D_B // RG_BLOCKS
RG_CONV = 4
RG_C = 8.0
D_C = D_MODEL
CHUNK = 128
C_GROUPS = 8
C_GROUP_DIM = D_C // C_GROUPS
D_FF = 256 * ((8 * D_MODEL // 3 + 255) // 256)
FFN_CONV = 3
EPS = 1e-6
AB_SPLITS = (N_HEADS * HEAD_DIM, N_KV_HEADS * HEAD_DIM, N_KV_HEADS * HEAD_DIM,
             IDX_HEADS * IDX_DIM, IDX_DIM, IDX_HEADS, D_B, D_B)
P_AB = sum(AB_SPLITS)

kernel_name = 'hybrid_dsa_rglru_gmlp_step'


def split_cols(z, sizes):
    offs = np.cumsum(sizes)[:-1].tolist()
    return jnp.split(z, offs, axis=-1)


def batch_gather(arr, idx):
    return jax.vmap(lambda a, i: a[i])(arr, idx)


def rmsnorm(x, g):
    xf = x.astype(jnp.float32)
    y = xf * lax.rsqrt(jnp.mean(xf * xf, axis=-1, keepdims=True) + EPS)
    return (y * g.astype(jnp.float32)).astype(x.dtype)


def causal_dwconv(x, buf, w, b):
    width = w.shape[0]
    t = x.shape[1]
    full = jnp.concatenate([buf.astype(x.dtype), x], axis=1)
    y = b
    for j in range(width):
        y = y + w[j] * full[:, j:j + t]
    return y, full[:, t:]


def t5_bucket(n):
    n = jnp.maximum(n, 0)
    large = REL_MAX_EXACT + (jnp.log(jnp.maximum(n, 1).astype(jnp.float32) / REL_MAX_EXACT)
                             / math.log(REL_MAX_DIST / REL_MAX_EXACT)
                             * (N_BUCKETS - REL_MAX_EXACT)).astype(jnp.int32)
    large = jnp.minimum(large, N_BUCKETS - 1)
    return jnp.where(n < REL_MAX_EXACT, n, large)


def indexer_scores(qi, wi, ki):
    s = jnp.einsum('bqhd,bld->bqhl', qi, ki).astype(jnp.float32) * IDX_DIM ** -0.5
    return jnp.einsum('bqhl,bqh->bql', jax.nn.relu(s), wi.astype(jnp.float32)) * IDX_HEADS ** -0.5


def select_topk(scores, q_pos, k_pos, topk):
    admissible = k_pos[None, None, :] <= q_pos[None, :, None]
    _, idx = lax.top_k(jnp.where(admissible, scores, -jnp.inf), topk)
    valid = idx <= q_pos[None, :, None]
    return idx, valid


def gathered_attention(q, k_sel, v_sel, sel_pos, valid, q_pos, rel_bias):
    b, nq = q.shape[:2]
    qg = q.reshape(b, nq, N_KV_HEADS, Q_PER_KV, HEAD_DIM)
    logits = jnp.einsum('bqkgd,bqjkd->bqkgj', qg, k_sel).astype(jnp.float32) * HEAD_DIM ** -0.5
    bias = rel_bias[t5_bucket(q_pos[None, :, None] - sel_pos)].astype(jnp.float32)
    bias = jnp.moveaxis(bias, -1, 2).reshape(b, nq, N_KV_HEADS, Q_PER_KV, -1)
    logits = jnp.where(valid[:, :, None, None, :], logits + bias, -jnp.inf)
    p = jax.nn.softmax(logits, axis=-1).astype(v_sel.dtype)
    out = jnp.einsum('bqkgj,bqjkd->bqkgd', p, v_sel)
    return out.reshape(b, nq, N_HEADS * HEAD_DIM)


def prompt_sparse_attention(q, k, v, qi, ki, wi, rel_bias):
    b, s = q.shape[:2]
    topk = min(TOPK_MAX, s // 4)
    nb = s // Q_BLOCK
    pos = jnp.arange(s, dtype=jnp.int32)

    def block(args):
        q_b, qi_b, wi_b, qpos_b = args
        idx, valid = select_topk(indexer_scores(qi_b, wi_b, ki), qpos_b, pos, topk)
        return gathered_attention(q_b, batch_gather(k, idx), batch_gather(v, idx), idx, valid, qpos_b, rel_bias)

    to_blocks = lambda a: a.reshape((b, nb, Q_BLOCK) + a.shape[2:]).swapaxes(0, 1)
    out = lax.map(block, (to_blocks(q), to_blocks(qi), to_blocks(wi), pos.reshape(nb, Q_BLOCK)))
    return out.swapaxes(0, 1).reshape(b, s, -1)


def sample_sparse_attention(q, k_new, v_new, qi, ki_new, wi, ck, cv, cik, page_table, rel_bias):
    db, t = q.shape[:2]
    past = page_table.shape[1] * PAGE_SIZE
    topk = min(TOPK_MAX, (past + t) // 4)
    ki_past = cik[page_table].reshape(db, past, IDX_DIM)
    ki_all = jnp.concatenate([ki_past, ki_new.astype(ki_past.dtype)], axis=1)
    q_pos = past + jnp.arange(t, dtype=jnp.int32)
    k_pos = jnp.arange(past + t, dtype=jnp.int32)
    idx, valid = select_topk(indexer_scores(qi, wi, ki_all), q_pos, k_pos, topk)
    in_past = (idx < past)[..., None, None]
    pidx = jnp.minimum(idx, past - 1)
    phys = batch_gather(page_table, pidx // PAGE_SIZE)
    off = pidx % PAGE_SIZE
    nidx = jnp.clip(idx - past, 0, t - 1)
    k_sel = jnp.where(in_past, ck[phys, off], batch_gather(k_new, nidx))
    v_sel = jnp.where(in_past, cv[phys, off], batch_gather(v_new, nidx))
    return gathered_attention(q, k_sel, v_sel, idx, valid, q_pos, rel_bias)


def linear_recurrence(a, u, h0):
    def step(h, au):
        h = au[0] * h + au[1]
        return h, h
    h_last, hs = lax.scan(step, h0, (a.swapaxes(0, 1), u.swapaxes(0, 1)))
    return hs.swapaxes(0, 1), h_last


def rglru_branch(g, xr, buf, h0, conv_w, conv_b, wa, ba, wx, bx, lam):
    b, t, _ = xr.shape
    xc, new_buf = causal_dwconv(xr, buf, conv_w, conv_b)
    xb = xc.reshape(b, t, RG_BLOCKS, RG_BLOCK)
    r = jax.nn.sigmoid((jnp.einsum('btnc,ncd->btnd', xb, wa).reshape(b, t, D_B) + ba).astype(jnp.float32))
    i = jax.nn.sigmoid((jnp.einsum('btnc,ncd->btnd', xb, wx).reshape(b, t, D_B) + bx).astype(jnp.float32))
    log_a = -RG_C * r * jax.nn.softplus(-lam.astype(jnp.float32))
    u = jnp.sqrt(-jnp.expm1(2.0 * log_a)) * (i * xc.astype(jnp.float32))
    hs, h_last = linear_recurrence(jnp.exp(log_a), u, h0.astype(jnp.float32))
    return jax.nn.gelu(g) * hs.astype(g.dtype), new_buf, h_last.astype(h0.dtype)


def mixer_ab(hn, attend, conv_buf, h0, w_in, w_out, conv_w, conv_b, wa, ba, wx, bx, lam):
    b, t, _ = hn.shape
    q, k, v, qi, ki, wi, g, xr = split_cols(hn @ w_in, AB_SPLITS)
    q = q.reshape(b, t, N_HEADS, HEAD_DIM)
    k = k.reshape(b, t, N_KV_HEADS, HEAD_DIM)
    v = v.reshape(b, t, N_KV_HEADS, HEAD_DIM)
    qi = qi.reshape(b, t, IDX_HEADS, IDX_DIM)
    attn = attend(q, k, v, qi, ki, wi)
    rg, new_buf, h_last = rglru_branch(g, xr, conv_buf, h0, conv_w, conv_b, wa, ba, wx, bx, lam)
    return jnp.concatenate([attn, rg], axis=-1) @ w_out, k, v, ki, h_last, new_buf


def chunk_mix(v, w, b):
    bsz, t, c = v.shape
    n_chunks = -(-t // CHUNK)
    vp = jnp.pad(v, ((0, 0), (0, n_chunks * CHUNK - t), (0, 0))).reshape(bsz, n_chunks, CHUNK, C_GROUPS, C_GROUP_DIM)
    wm = jnp.where(jnp.tril(jnp.ones((CHUNK, CHUNK), dtype=bool)), w, 0)
    out = jnp.einsum('gts,bnsgc->bntgc', wm, vp) + b.T[None, None, :, :, None]
    return out.reshape(bsz, n_chunks * CHUNK, c)[:, :t]


def mixer_c(hn, w_in, b_in, s_norm, s_w, s_b, w_out):
    u, v = split_cols(jax.nn.gelu(hn @ w_in + b_in), (D_C, D_C))
    v = rmsnorm(v, s_norm)
    return (u * chunk_mix(v, s_w, s_b)) @ w_out, v


def conv_ffn(hn, buf, w_up, cw, cb, w_down):
    g, u = split_cols(hn @ w_up, (D_FF, D_FF))
    gc, new_buf = causal_dwconv(g, buf, cw, cb)
    return (jax.nn.gelu(gc) * u) @ w_down, new_buf


def setup_inputs(seed: int = 0) -> dict:
    key = jax.random.key(seed)
    ks = iter(jax.random.split(key, 40))
    nrm = lambda shape, scale: scale * jax.random.normal(next(ks), shape, jnp.float32)
    n_pages = PAST_LEN // PAGE_SIZE
    n_used = DEC_BATCH * n_pages
    n_pool = n_used + max(1, n_used // 4)
    page_table = jax.random.permutation(next(ks), n_pool)[:n_used].reshape(DEC_BATCH, n_pages).astype(jnp.int32)
    a8 = jax.random.uniform(next(ks), (N_EVEN, D_B), jnp.float32, minval=0.9, maxval=0.999)
    a = a8 ** (1.0 / RG_C)
    lam = jnp.log(a) - jnp.log1p(-a)
    return {
        'x_prompt': nrm((BATCH, SEQ, D_MODEL), 1.0),
        'x_sample': nrm((DEC_BATCH, DEC_SEQ, D_MODEL), 1.0),
        'cache_k': nrm((N_EVEN, n_pool, PAGE_SIZE, N_KV_HEADS, HEAD_DIM), 1.0),
        'cache_v': nrm((N_EVEN, n_pool, PAGE_SIZE, N_KV_HEADS, HEAD_DIM), 1.0),
        'cache_idx_k': nrm((N_EVEN, n_pool, PAGE_SIZE, IDX_DIM), 1.0),
        'state_rglru_h': nrm((N_EVEN, DEC_BATCH, D_B), 0.5),
        'state_rglru_conv': nrm((N_EVEN, DEC_BATCH, RG_CONV - 1, D_B), 1.0),
        'state_ffn_conv': nrm((DEPTH, DEC_BATCH, FFN_CONV - 1, D_FF), 1.0),
        'page_table': page_table,
        'norm_mix': 1.0 + nrm((DEPTH, D_MODEL), 0.05),
        'norm_ffn': 1.0 + nrm((DEPTH, D_MODEL), 0.05),
        'norm_final': 1.0 + nrm((D_MODEL,), 0.05),
        'rel_bias': nrm((N_BUCKETS, N_HEADS), 0.5),
        'w_in_ab': nrm((N_EVEN, D_MODEL, P_AB), D_MODEL ** -0.5),
        'w_out_ab': nrm((N_EVEN, D_A + D_B, D_MODEL), (D_A + D_B) ** -0.5),
        'rg_conv_w': nrm((N_EVEN, RG_CONV, D_B), RG_CONV ** -0.5),
        'rg_conv_b': nrm((N_EVEN, D_B), 0.02),
        'rg_wa': nrm((N_EVEN, RG_BLOCKS, RG_BLOCK, RG_BLOCK), RG_BLOCK ** -0.5),
        'rg_ba': nrm((N_EVEN, D_B), 0.1),
        'rg_wx': nrm((N_EVEN, RG_BLOCKS, RG_BLOCK, RG_BLOCK), RG_BLOCK ** -0.5),
        'rg_bx': nrm((N_EVEN, D_B), 0.1),
        'rg_lambda': lam,
        'w_in_c': nrm((N_ODD, D_MODEL, 2 * D_C), D_MODEL ** -0.5),
        'b_in_c': nrm((N_ODD, 2 * D_C), 0.02),
        'sgu_norm': 1.0 + nrm((N_ODD, D_C), 0.05),
        'sgu_w': nrm((N_ODD, C_GROUPS, CHUNK, CHUNK), 0.5 * CHUNK ** -0.5),
        'sgu_b': 1.0 + nrm((N_ODD, C_GROUPS, CHUNK), 0.1),
        'w_out_c': nrm((N_ODD, D_C, D_MODEL), D_C ** -0.5),
        'ffn_w_up': nrm((DEPTH, D_MODEL, 2 * D_FF), D_MODEL ** -0.5),
        'ffn_conv_w': nrm((DEPTH, FFN_CONV, D_FF), FFN_CONV ** -0.5),
        'ffn_conv_b': nrm((DEPTH, D_FF), 0.02),
        'ffn_w_down': nrm((DEPTH, D_FF, D_MODEL), D_FF ** -0.5),
    }


def reference(x_prompt, x_sample, cache_k, cache_v, cache_idx_k, state_rglru_h, state_rglru_conv, state_ffn_conv,
              page_table, norm_mix, norm_ffn, norm_final, rel_bias, w_in_ab, w_out_ab, rg_conv_w, rg_conv_b,
              rg_wa, rg_ba, rg_wx, rg_bx, rg_lambda, w_in_c, b_in_c, sgu_norm, sgu_w, sgu_b, w_out_c,
              ffn_w_up, ffn_conv_w, ffn_conv_b, ffn_w_down):
    yp, ys = x_prompt, x_sample
    bp = x_prompt.shape[0]
    kp_l, vp_l, ikp_l, ks_l, vs_l, iks_l = [], [], [], [], [], []
    hp_l, cp_l, hs_l, cs_l, cv_l, fp_l, fs_l = [], [], [], [], [], [], []
    for layer in range(DEPTH):
        if layer % 2 == 0:
            e = layer // 2
            rg = (rg_conv_w[e], rg_conv_b[e], rg_wa[e], rg_ba[e], rg_wx[e], rg_bx[e], rg_lambda[e])
            attend_p = lambda q, k, v, qi, ki, wi: prompt_sparse_attention(q, k, v, qi, ki, wi, rel_bias)
            out, k, v, ki, h_last, buf = mixer_ab(
                rmsnorm(yp, norm_mix[layer]), attend_p,
                jnp.zeros((bp, RG_CONV - 1, D_B), yp.dtype), jnp.zeros((bp, D_B), yp.dtype),
                w_in_ab[e], w_out_ab[e], *rg)
            yp = yp + out
            kp_l.append(k); vp_l.append(v); ikp_l.append(ki); hp_l.append(h_last); cp_l.append(buf)
            ck, cvv, cik = cache_k[e], cache_v[e], cache_idx_k[e]
            attend_s = lambda q, k, v, qi, ki, wi: sample_sparse_attention(
                q, k, v, qi, ki, wi, ck, cvv, cik, page_table, rel_bias)
            out, k, v, ki, h_last, buf = mixer_ab(
                rmsnorm(ys, norm_mix[layer]), attend_s, state_rglru_conv[e], state_rglru_h[e],
                w_in_ab[e], w_out_ab[e], *rg)
            ys = ys + out
            ks_l.append(k); vs_l.append(v); iks_l.append(ki); hs_l.append(h_last); cs_l.append(buf)
        else:
            o = layer // 2
            cp = (w_in_c[o], b_in_c[o], sgu_norm[o], sgu_w[o], sgu_b[o], w_out_c[o])
            out, _ = mixer_c(rmsnorm(yp, norm_mix[layer]), *cp)
            yp = yp + out
            out, v_rows = mixer_c(rmsnorm(ys, norm_mix[layer]), *cp)
            ys = ys + out
            cv_l.append(v_rows)
        fw = (ffn_w_up[layer], ffn_conv_w[layer], ffn_conv_b[layer], ffn_w_down[layer])
        out, buf = conv_ffn(rmsnorm(yp, norm_ffn[layer]), jnp.zeros((bp, FFN_CONV - 1, D_FF), yp.dtype), *fw)
        yp = yp + out
        fp_l.append(buf)
        out, buf = conv_ffn(rmsnorm(ys, norm_ffn[layer]), state_ffn_conv[layer], *fw)
        ys = ys + out
        fs_l.append(buf)
    y_prompt = rmsnorm(yp, norm_final)
    y_sample = rmsnorm(ys, norm_final)
    return (y_prompt, y_sample,
            jnp.stack(kp_l), jnp.stack(vp_l), jnp.stack(ikp_l),
            jnp.stack(ks_l), jnp.stack(vs_l), jnp.stack(iks_l),
            jnp.stack(hp_l), jnp.stack(cp_l), jnp.stack(hs_l), jnp.stack(cs_l),
            jnp.stack(cv_l), jnp.stack(fp_l), jnp.stack(fs_l))
```

```python
import functools
import math

import numpy as np
import jax
import jax.numpy as jnp
from jax import lax
from jax.experimental import pallas as pl
from jax.experimental.pallas import tpu as pltpu

F32 = jnp.float32
BF16 = jnp.bfloat16
I32 = jnp.int32

N_HEADS = 8
HEAD_DIM = 64
N_KV_HEADS = 2
Q_PER_KV = N_HEADS // N_KV_HEADS
IDX_HEADS = 8
IDX_DIM = 64
TOPK_MAX = 256
N_BUCKETS = 32
REL_MAX_EXACT = N_BUCKETS // 2
REL_MAX_DIST = 128
RG_C = 8.0
CHUNK = 128
EPS = 1e-6

LANES = 128
QB = 128
INT_MIN = -(2 ** 31)
HEAD_PERM = (0, 4, 1, 5, 2, 6, 3, 7)
VMEM_LIMIT = 56 * 1024 * 1024


def _cparams(n_grid):
    return pltpu.CompilerParams(dimension_semantics=("arbitrary",) * n_grid, vmem_limit_bytes=VMEM_LIMIT)


def _full_spec(shape):
    nd = len(shape)
    return pl.BlockSpec(shape, lambda *_: (0,) * nd, pipeline_mode=pl.Buffered(1))


def _rms(x, g):
    return x * lax.rsqrt(jnp.mean(x * x, axis=-1, keepdims=True) + EPS) * g


def _gelu(x):
    return x * (0.5 * (1.0 + jnp.tanh(math.sqrt(2.0 / math.pi) * (x + 0.044715 * (x * x * x)))))


def _sigmoid(x):
    return 1.0 / (1.0 + jnp.exp(-x))


def _softplus(x):
    return jnp.maximum(x, 0.0) + jnp.log(1.0 + jnp.exp(-jnp.abs(x)))


def _dot(a, b):
    return jnp.dot(a.astype(BF16), b, preferred_element_type=F32)


def _dot_nt(a, b):
    return lax.dot_general(a, b, (((1,), (1,)), ((), ())), preferred_element_type=F32)


def _float_key(x):
    bits = pltpu.bitcast(x, I32)
    key = jnp.where(bits < 0, bits ^ jnp.int32(0x7FFFFFFF), bits)
    return jnp.where(bits == jnp.int32(INT_MIN), jnp.int32(0), key)


def _t5_bucket_np(n):
    n = np.maximum(n, 0)
    nf = np.maximum(n, 1).astype(np.float32)
    large = REL_MAX_EXACT + (np.log(nf / np.float32(REL_MAX_EXACT)) / np.float32(math.log(REL_MAX_DIST / REL_MAX_EXACT))
                             * np.float32(N_BUCKETS - REL_MAX_EXACT)).astype(np.int32)
    large = np.minimum(large, N_BUCKETS - 1)
    return np.where(n < REL_MAX_EXACT, n, large).astype(np.int32)


_C_Q, _C_KV, _C_QI, _C_IX, _C_G, _C_X, _C_END = 0, 512, 768, 1280, 1408, 1920, 2432


def _inproj_kernel(x_ref, g_ref, w_ref, q_ref, qi_ref, kv_ref, ix_ref, gate_ref, xr_ref, *, stack):
    hn = _rms(x_ref[...], g_ref[...])
    z = _dot(hn, w_ref[...])
    q = z[:, _C_Q:_C_KV] * HEAD_DIM ** -0.5
    qi = z[:, _C_QI:_C_IX] * IDX_DIM ** -0.5
    kv_ref[...] = z[:, _C_KV:_C_QI]
    ix_ref[...] = z[:, _C_IX:_C_G]
    gate_ref[...] = z[:, _C_G:_C_X]
    xr_ref[...] = z[:, _C_X:_C_END]
    if stack:
        qb, qib = q.astype(BF16), qi.astype(BF16)
        for r in range(q.shape[0] // QB):
            for p in range(4):
                q_ref[r, p * QB:(p + 1) * QB, :] = qb[r * QB:(r + 1) * QB, p * LANES:(p + 1) * LANES]
                qi_ref[r, p * QB:(p + 1) * QB, :] = qib[r * QB:(r + 1) * QB, p * LANES:(p + 1) * LANES]
    else:
        lo = lax.broadcasted_iota(I32, (q.shape[0], LANES), 1) < HEAD_DIM
        for p in range(4):
            qp = q[:, p * LANES:(p + 1) * LANES]
            q_ref[2 * p] = jnp.where(lo, qp, 0.0)
            q_ref[2 * p + 1] = jnp.where(lo, 0.0, qp)
        qi_ref[...] = qi


def _inproj(x2d, g, w, *, stack, tm):
    m, d = x2d.shape
    if stack:
        q_shape, q_spec = (m // QB, 4 * QB, LANES), pl.BlockSpec((tm // QB, 4 * QB, LANES), lambda i: (i, 0, 0))
        qi_shape, qi_spec, qdt = q_shape, q_spec, BF16
    else:
        q_shape, q_spec = (N_HEADS, m, LANES), pl.BlockSpec((N_HEADS, tm, LANES), lambda i: (0, i, 0))
        qi_shape, qi_spec, qdt = (m, 512), pl.BlockSpec((tm, 512), lambda i: (i, 0)), F32
    row = lambda n: pl.BlockSpec((tm, n), lambda i: (i, 0))
    return pl.pallas_call(
        functools.partial(_inproj_kernel, stack=stack),
        grid=(m // tm,),
        in_specs=[row(d), _full_spec((1, d)), _full_spec(w.shape)],
        out_specs=[q_spec, qi_spec, row(256), row(128), row(512), row(512)],
        out_shape=[jax.ShapeDtypeStruct(q_shape, qdt), jax.ShapeDtypeStruct(qi_shape, qdt),
                   jax.ShapeDtypeStruct((m, 256), F32), jax.ShapeDtypeStruct((m, 128), F32),
                   jax.ShapeDtypeStruct((m, 512), F32), jax.ShapeDtypeStruct((m, 512), F32)],
        compiler_params=_cparams(1),
        name="inproj_stack" if stack else "inproj_dec",
    )(x2d, g, w)


def _bias_kernel(rb_ref, bk_ref, bkd_ref, o_ref, od_ref):
    for d in range(3):
        bk = bk_ref[d]
        for p in range(4):
            for a in range(2):
                h = p + 4 * a
                acc = jnp.zeros((QB, LANES), F32)
                for b in range(N_BUCKETS):
                    acc = jnp.where(bk == b, rb_ref[b, h], acc)
                o_ref[d, p * QB:(p + 1) * QB, a * LANES:(a + 1) * LANES] = acc
    bkd = bkd_ref[...]
    rowi = lax.broadcasted_iota(I32, (N_HEADS, 2 * LANES), 0)
    acc = jnp.zeros((N_HEADS, 2 * LANES), F32)
    for r in range(N_HEADS):
        h = r // 2 + 4 * (r % 2)
        for b in range(N_BUCKETS):
            acc = jnp.where((rowi == r) & (bkd == b), rb_ref[b, h], acc)
    od_ref[...] = acc


def _bias_tables(rel_bias, page):
    r = np.arange(QB)[:, None]
    c = np.arange(LANES)[None, :]
    bk = np.stack([_t5_bucket_np(d * QB + r - c) for d in range(3)])
    assert (_t5_bucket_np(np.arange(2 * QB + 1 - LANES, 4 * QB)) == N_BUCKETS - 1).all()
    assert (_t5_bucket_np(np.arange(page, 8 * page)) == N_BUCKETS - 1).all()
    dec = np.zeros((2 * LANES,), np.int64)
    dec[:page] = page - np.arange(page)
    dec[LANES] = 2 * REL_MAX_DIST
    dec[LANES + 1] = 0
    bkd = np.broadcast_to(_t5_bucket_np(dec)[None, :], (N_HEADS, 2 * LANES))
    return pl.pallas_call(
        _bias_kernel,
        in_specs=[pl.BlockSpec(memory_space=pltpu.SMEM), pl.BlockSpec(memory_space=pltpu.VMEM),
                  pl.BlockSpec(memory_space=pltpu.VMEM)],
        out_shape=[jax.ShapeDtypeStruct((3, 4 * QB, 2 * LANES), F32), jax.ShapeDtypeStruct((N_HEADS, 2 * LANES), F32)],
        name="bias_tables",
    )(rel_bias, jnp.asarray(bk, I32), jnp.asarray(bkd, I32))


def _attn_prompt_kernel(q_ref, qi_ref, ixq_ref, ixk_ref, kv_ref, bias_ref, o_ref,
                        kblk, vblk, kiblk, keys, logits, mrun, lsum, acc, *, n_chunks, topk):
    j = pl.program_id(1)
    lane = lax.broadcasted_iota(I32, (QB, LANES), 1)
    row = lax.broadcasted_iota(I32, (QB, LANES), 0)
    lo = lane < HEAD_DIM
    blocks = [(p, a) for p in range(4) for a in range(2)]
    rs = lambda p: slice(p * QB, (p + 1) * QB)
    cs = lambda a: slice(a * LANES, (a + 1) * LANES)

    @pl.when(j == 0)
    def _build_block_diagonal_keys():
        def body(c, carry):
            s = pl.multiple_of(c * QB, QB)
            kc = kv_ref[0, pl.ds(s, QB), 0:LANES]
            vc = kv_ref[0, pl.ds(s, QB), LANES:2 * LANES]
            kia = jnp.where(lo, ixk_ref[0, pl.ds(s, QB), :], 0.0)
            kblk[c, 0:QB, :] = jnp.where(lo, kc, 0.0).astype(BF16)
            kblk[c, QB:2 * QB, :] = jnp.where(lo, 0.0, kc).astype(BF16)
            vblk[c, 0:QB, :] = jnp.where(lo, vc, 0.0).astype(BF16)
            vblk[c, QB:2 * QB, :] = jnp.where(lo, 0.0, vc).astype(BF16)
            kiblk[c, 0:QB, :] = kia.astype(BF16)
            kiblk[c, QB:2 * QB, :] = pltpu.roll(kia, HEAD_DIM, 1).astype(BF16)
            return carry
        lax.fori_loop(0, n_chunks, body, 0)

    qi = qi_ref[0]
    wq = ixq_ref[0]
    wb = {(p, a): jnp.broadcast_to(wq[:, IDX_DIM + 2 * p + a:IDX_DIM + 2 * p + a + 1], (QB, LANES)) for p, a in blocks}
    qpos = j * QB + row

    def score_body(c, carry):
        s = _dot_nt(qi, kiblk[c])
        sc = jnp.zeros((QB, LANES), F32)
        for p, a in blocks:
            sc = sc + jnp.maximum(s[rs(p), cs(a)], 0.0) * wb[(p, a)]
        key = _float_key(sc * IDX_HEADS ** -0.5)
        key = jnp.where(c * QB + lane <= qpos, key, jnp.int32(INT_MIN))
        keys[:, pl.ds(pl.multiple_of(c * QB, QB), QB)] = key
        return carry
    lax.fori_loop(0, j + 1, score_body, 0)

    def fill_body(c, carry):
        keys[:, pl.ds(pl.multiple_of(c * QB, QB), QB)] = jnp.full((QB, LANES), INT_MIN, I32)
        return carry
    lax.fori_loop(j + 1, n_chunks, fill_body, 0)

    kf = jnp.float32(topk)

    def count(pred):
        return jnp.sum(jnp.where(pred, 1.0, 0.0), axis=1, keepdims=True)

    def search_body(i, ans):
        cand = ans | jnp.left_shift(jnp.int32(1), 31 - i)
        cnt = count(keys[...] >= (cand ^ jnp.int32(INT_MIN)))
        return jnp.where(cnt >= kf, cand, ans)
    ans = lax.fori_loop(0, 32, search_body, jnp.zeros((QB, 1), I32))
    thr = jnp.maximum(ans ^ jnp.int32(INT_MIN), jnp.int32(INT_MIN + 1))

    @pl.when(jnp.max(count(keys[...] >= thr)) > kf)
    def _break_ties_by_position():
        kk = keys[...]
        s_len = kk.shape[1]
        need = kf - count(kk > thr)
        big = jnp.int32(2 * s_len)
        eqcol = jnp.where(kk == thr, lax.broadcasted_iota(I32, kk.shape, 1), big)
        nbits = int(math.log2(s_len))

        def tie_body(i, best):
            cand = best | jnp.left_shift(jnp.int32(1), nbits - 1 - i)
            return jnp.where(count(eqcol < cand) < need, cand, best)
        last = lax.fori_loop(0, nbits, tie_body, jnp.zeros((QB, 1), I32))
        keys[...] = jnp.where((eqcol > last) & (eqcol < big), thr - 1, kk)

    q = q_ref[0]
    mrun[...] = jnp.full(mrun.shape, -jnp.inf, F32)
    lsum[...] = jnp.zeros(lsum.shape, F32)
    acc[...] = jnp.zeros(acc.shape, F32)

    def logits_body(c, carry):
        lg = _dot_nt(q, kblk[c])
        bias = bias_ref[jnp.minimum(j - c, 2)]
        sel = keys[:, pl.ds(pl.multiple_of(c * QB, QB), QB)] >= thr
        for p, a in blocks:
            blk = jnp.where(sel, lg[rs(p), cs(a)] + bias[rs(p), cs(a)], -jnp.inf)
            logits[c, rs(p), cs(a)] = blk
            mrun[rs(p), cs(a)] = jnp.maximum(mrun[rs(p), cs(a)], blk)
        return carry
    lax.fori_loop(0, j + 1, logits_body, 0)
    mx = {(p, a): jnp.max(mrun[rs(p), cs(a)], axis=1, keepdims=True) for p, a in blocks}

    def pv_body(c, carry):
        rows = []
        for p in range(4):
            halves = []
            for a in range(2):
                e = jnp.exp(logits[c, rs(p), cs(a)] - mx[(p, a)])
                lsum[rs(p), cs(a)] = lsum[rs(p), cs(a)] + e
                halves.append(e.astype(BF16))
            rows.append(jnp.concatenate(halves, axis=1))
        pmat = jnp.concatenate(rows, axis=0)
        acc[...] = acc[...] + jnp.dot(pmat, vblk[c], preferred_element_type=F32)
        return carry
    lax.fori_loop(0, j + 1, pv_body, 0)

    for p in range(4):
        inv = jnp.where(lo, 1.0 / jnp.sum(lsum[rs(p), cs(0)], axis=1, keepdims=True),
                        1.0 / jnp.sum(lsum[rs(p), cs(1)], axis=1, keepdims=True))
        o_ref[0, :, p * LANES:(p + 1) * LANES] = (acc[rs(p), :] * inv).astype(BF16)


def _attn_prompt(q_st, qi_st, ix, kv, bias_st, *, batch, seq):
    nq = seq // QB
    topk = min(TOPK_MAX, seq // 4)
    ix3 = ix.reshape(batch, seq, LANES)
    kv3 = kv.reshape(batch, seq, 2 * LANES)
    return pl.pallas_call(
        functools.partial(_attn_prompt_kernel, n_chunks=nq, topk=topk),
        grid=(batch, nq),
        in_specs=[pl.BlockSpec((1, 4 * QB, LANES), lambda b, j: (b * nq + j, 0, 0)),
                  pl.BlockSpec((1, 4 * QB, LANES), lambda b, j: (b * nq + j, 0, 0)),
                  pl.BlockSpec((1, QB, LANES), lambda b, j: (b, j, 0)),
                  pl.BlockSpec((1, seq, LANES), lambda b, j: (b, 0, 0)),
                  pl.BlockSpec((1, seq, 2 * LANES), lambda b, j: (b, 0, 0)),
                  _full_spec(bias_st.shape)],
        out_specs=pl.BlockSpec((1, QB, 4 * LANES), lambda b, j: (b, j, 0)),
        out_shape=jax.ShapeDtypeStruct((batch, seq, 4 * LANES), BF16),
        scratch_shapes=[pltpu.VMEM((nq, 2 * QB, LANES), BF16), pltpu.VMEM((nq, 2 * QB, LANES), BF16),
                        pltpu.VMEM((nq, 2 * QB, LANES), BF16), pltpu.VMEM((QB, seq), I32),
                        pltpu.VMEM((nq, 4 * QB, 2 * LANES), F32), pltpu.VMEM((4 * QB, 2 * LANES), F32),
                        pltpu.VMEM((4 * QB, 2 * LANES), F32), pltpu.VMEM((4 * QB, LANES), F32)],
        compiler_params=_cparams(2),
        name="attn_prompt",
    )(q_st, qi_st, ix3, ix3, kv3, bias_st)


def _rglru_gates(xc, wa, ba, wx, bx, lam):
    r = _sigmoid(_dot(xc, wa) + ba)
    i = _sigmoid(_dot(xc, wx) + bx)
    log_a = -RG_C * r * _softplus(-lam)
    a = jnp.exp(log_a)
    u = jnp.sqrt(1.0 - jnp.exp(2.0 * log_a)) * (i * xc)
    return a, u


def _rglru_prompt_kernel(g_ref, xr_ref, buf_ref, h0_ref, cw_ref, cb_ref, wa_ref, ba_ref, wx_ref, bx_ref, lam_ref,
                         o_ref, hl_ref, nb_ref, xs, a_s, u_s, tail, hc, *, tc):
    t = pl.program_id(1)
    width = cw_ref.shape[0]

    @pl.when(t == 0)
    def _load_state():
        tail[...] = jnp.zeros(tail.shape, F32)
        tail[8 - (width - 1):8, :] = buf_ref[0]
        hc[...] = h0_ref[0]

    xs[0:8, :] = tail[...]
    xs[8:8 + tc, :] = xr_ref[0]
    tail[...] = xs[tc:tc + 8, :]
    xc = cb_ref[...]
    for jj in range(width):
        off = 8 - (width - 1) + jj
        xc = xc + cw_ref[jj:jj + 1, :] * xs[off:off + tc, :]
    a, u = _rglru_gates(xc, wa_ref[...], ba_ref[...], wx_ref[...], bx_ref[...], lam_ref[...])
    a_s[...] = a
    u_s[...] = u

    def scan_body(i, h):
        h = a_s[pl.ds(i, 1), :] * h + u_s[pl.ds(i, 1), :]
        u_s[pl.ds(i, 1), :] = h
        return h
    h = lax.fori_loop(0, tc, scan_body, hc[...], unroll=8)
    hc[...] = h
    o_ref[0] = (_gelu(g_ref[0]) * u_s[...]).astype(BF16)
    hl_ref[0] = h
    nb_ref[0] = xs[tc + 8 - (width - 1):tc + 8, :]


def _rglru_prompt(gate, xr, buf, h0, rg, *, batch, seq, tc=512):
    d = gate.shape[-1]
    width = rg["cw"].shape[0]
    g3, x3 = gate.reshape(batch, seq, d), xr.reshape(batch, seq, d)
    blk = pl.BlockSpec((1, tc, d), lambda b, t: (b, t, 0))
    per_b = lambda n: pl.BlockSpec((1, n, d), lambda b, t: (b, 0, 0))
    vec = _full_spec((1, d))
    return pl.pallas_call(
        functools.partial(_rglru_prompt_kernel, tc=tc),
        grid=(batch, seq // tc),
        in_specs=[blk, blk, per_b(width - 1), per_b(1), _full_spec((width, d)), vec,
                  _full_spec((d, d)), vec, _full_spec((d, d)), vec, vec],
        out_specs=[blk, per_b(1), per_b(width - 1)],
        out_shape=[jax.ShapeDtypeStruct((batch, seq, d), BF16), jax.ShapeDtypeStruct((batch, 1, d), F32),
                   jax.ShapeDtypeStruct((batch, width - 1, d), F32)],
        scratch_shapes=[pltpu.VMEM((tc + 8, d), F32), pltpu.VMEM((tc, d), F32), pltpu.VMEM((tc, d), F32),
                        pltpu.VMEM((8, d), F32), pltpu.VMEM((1, d), F32)],
        compiler_params=_cparams(2),
        name="rglru_prompt",
    )(g3, x3, buf, h0.reshape(batch, 1, d), rg["cw"], rg["cb"], rg["wa"], rg["ba"], rg["wx"], rg["bx"], rg["lam"])


def _rglru_dec_kernel(g_ref, xr_ref, buf_ref, h0_ref, cw_ref, cb_ref, wa_ref, ba_ref, wx_ref, bx_ref, lam_ref,
                      o_ref, hl_ref):
    width = cw_ref.shape[0]
    xc = cb_ref[...]
    for jj in range(width - 1):
        xc = xc + cw_ref[jj:jj + 1, :] * buf_ref[jj]
    xc = xc + cw_ref[width - 1:width, :] * xr_ref[...]
    a, u = _rglru_gates(xc, wa_ref[...], ba_ref[...], wx_ref[...], bx_ref[...], lam_ref[...])
    h = a * h0_ref[...] + u
    hl_ref[...] = h
    o_ref[...] = (_gelu(g_ref[...]) * h).astype(BF16)


def _rglru_dec(gate, xr, buf_t, h0, rg):
    m, d = gate.shape
    return pl.pallas_call(
        _rglru_dec_kernel,
        out_shape=[jax.ShapeDtypeStruct((m, d), BF16), jax.ShapeDtypeStruct((m, d), F32)],
        name="rglru_dec",
    )(gate, xr, buf_t, h0, rg["cw"], rg["cb"], rg["wa"], rg["ba"], rg["wx"], rg["bx"], rg["lam"])


def _ffn_tile(y1, gf_ref, wup_ref, cw_ref, cb_ref, wdn_ref, conv_prev, n_split):
    d_ff = wdn_ref.shape[0]
    cf = d_ff // n_split
    hn = _rms(y1, gf_ref[...]).astype(BF16)
    out = jnp.zeros(y1.shape, F32)
    gates = []
    for k in range(n_split):
        c0 = k * cf
        g = jnp.dot(hn, wup_ref[:, c0:c0 + cf], preferred_element_type=F32)
        u = jnp.dot(hn, wup_ref[:, d_ff + c0:d_ff + c0 + cf], preferred_element_type=F32)
        g1, g2 = conv_prev(k, g)
        gc = cb_ref[:, c0:c0 + cf] + cw_ref[0:1, c0:c0 + cf] * g2 + cw_ref[1:2, c0:c0 + cf] * g1 \
            + cw_ref[2:3, c0:c0 + cf] * g
        act = (_gelu(gc) * u).astype(BF16)
        out = out + jnp.dot(act, wdn_ref[c0:c0 + cf, :], preferred_element_type=F32)
        gates.append(g)
    return out, gates


def _prompt_conv_prev(gs, carry, fb_ref, nb_ref, tm, cf):
    t = pl.program_id(1)

    @pl.when(t == 0)
    def _load_state():
        carry[...] = jnp.zeros(carry.shape, F32)
        for k in range(carry.shape[0]):
            carry[k, 6:8, :] = fb_ref[0, :, k * cf:(k + 1) * cf]

    def conv_prev(k, g):
        gs[0:8, :] = carry[k]
        gs[8:8 + tm, :] = g
        carry[k] = g[tm - 8:tm, :]
        nb_ref[0, :, k * cf:(k + 1) * cf] = g[tm - 2:tm, :]
        return gs[7:7 + tm, :], gs[6:6 + tm, :]
    return conv_prev


def _mix_ab_tile(y_ref, a_ref, r_ref, woa_ref, wob_ref):
    return y_ref[0] + jnp.dot(a_ref[0], woa_ref[...], preferred_element_type=F32) \
        + jnp.dot(r_ref[0], wob_ref[...], preferred_element_type=F32)


def _post_ab_prompt_kernel(y_ref, a_ref, r_ref, fb_ref, woa_ref, wob_ref, gf_ref, wup_ref, cw_ref, cb_ref, wdn_ref,
                           o_ref, nb_ref, gs, carry, *, tm, n_split):
    y1 = _mix_ab_tile(y_ref, a_ref, r_ref, woa_ref, wob_ref)
    cf = wdn_ref.shape[0] // n_split
    out, _ = _ffn_tile(y1, gf_ref, wup_ref, cw_ref, cb_ref, wdn_ref,
                       _prompt_conv_prev(gs, carry, fb_ref, nb_ref, tm, cf), n_split)
    o_ref[0] = y1 + out


def _ffn_specs(d, d_ff):
    return [_full_spec((1, d)), _full_spec((d, 2 * d_ff)), _full_spec((3, d_ff)), _full_spec((1, d_ff)),
            _full_spec((d_ff, d))]


def _post_ab_prompt(y, attn, rgo, fbuf, wo_a, wo_b, ffn, *, tm=512, n_split=2):
    batch, seq, d = y.shape
    d_ff = ffn["wdn"].shape[0]
    cf = d_ff // n_split
    blk = lambda n: pl.BlockSpec((1, tm, n), lambda b, t: (b, t, 0))
    fb = pl.BlockSpec((1, 2, d_ff), lambda b, t: (b, 0, 0))
    return pl.pallas_call(
        functools.partial(_post_ab_prompt_kernel, tm=tm, n_split=n_split),
        grid=(batch, seq // tm),
        in_specs=[blk(d), blk(attn.shape[-1]), blk(rgo.shape[-1]), fb, _full_spec(wo_a.shape), _full_spec(wo_b.shape)]
        + _ffn_specs(d, d_ff),
        out_specs=[blk(d), fb],
        out_shape=[jax.ShapeDtypeStruct((batch, seq, d), F32), jax.ShapeDtypeStruct((batch, 2, d_ff), F32)],
        scratch_shapes=[pltpu.VMEM((tm + 8, cf), F32), pltpu.VMEM((n_split, 8, cf), F32)],
        compiler_params=_cparams(2),
        name="post_ab_prompt",
    )(y, attn, rgo, fbuf, wo_a, wo_b, ffn["g"], ffn["wup"], ffn["cw"], ffn["cb"], ffn["wdn"])


def _dec_conv_prev(fb_ref, cf):
    def conv_prev(k, g):
        return fb_ref[1, :, k * cf:(k + 1) * cf], fb_ref[0, :, k * cf:(k + 1) * cf]
    return conv_prev


def _post_ab_dec_kernel(y_ref, a_ref, r_ref, fb_ref, woa_ref, wob_ref, gf_ref, wup_ref, cw_ref, cb_ref, wdn_ref,
                        o_ref, g_ref, *, n_split):
    y1 = y_ref[...] + jnp.dot(a_ref[...], woa_ref[...], preferred_element_type=F32) \
        + jnp.dot(r_ref[...], wob_ref[...], preferred_element_type=F32)
    cf = wdn_ref.shape[0] // n_split
    out, gates = _ffn_tile(y1, gf_ref, wup_ref, cw_ref, cb_ref, wdn_ref, _dec_conv_prev(fb_ref, cf), n_split)
    o_ref[...] = y1 + out
    for k, g in enumerate(gates):
        g_ref[:, k * cf:(k + 1) * cf] = g


def _post_ab_dec(y, attn, rgo, fbuf_t, wo_a, wo_b, ffn, *, n_split=2):
    m, d = y.shape
    d_ff = ffn["wdn"].shape[0]
    return pl.pallas_call(
        functools.partial(_post_ab_dec_kernel, n_split=n_split),
        out_shape=[jax.ShapeDtypeStruct((m, d), F32), jax.ShapeDtypeStruct((m, d_ff), F32)],
        compiler_params=pltpu.CompilerParams(vmem_limit_bytes=VMEM_LIMIT),
        name="post_ab_dec",
    )(y, attn, rgo, fbuf_t, wo_a, wo_b, ffn["g"], ffn["wup"], ffn["cw"], ffn["cb"], ffn["wdn"])


def _gmlp_in(y, gm_ref, win_ref, bin_ref, sn_ref):
    d_c = win_ref.shape[1] // 2
    z = _gelu(_dot(_rms(y, gm_ref[...]), win_ref[...]) + bin_ref[...])
    return z[:, :d_c], _rms(z[:, d_c:], sn_ref[...])


def _layer_c_prompt_kernel(y_ref, fb_ref, gm_ref, win_ref, bin_ref, sn_ref, sw_ref, sbt_ref, woc_ref,
                           gf_ref, wup_ref, cw_ref, cb_ref, wdn_ref, gfin_ref,
                           o_ref, nb_ref, gs, carry, *, tm, n_split):
    y = y_ref[0]
    u, v = _gmlp_in(y, gm_ref, win_ref, bin_ref, sn_ref)
    vb = v.astype(BF16)
    n_groups = sw_ref.shape[0]
    tril = lax.broadcasted_iota(I32, (CHUNK, CHUNK), 0) >= lax.broadcasted_iota(I32, (CHUNK, CHUNK), 1)
    wm = [jnp.where(tril, sw_ref[gi], 0.0).astype(BF16) for gi in range(n_groups)]
    rows = []
    for r in range(tm // CHUNK):
        cols = []
        for gi in range(n_groups):
            mixed = jnp.dot(wm[gi], vb[r * CHUNK:(r + 1) * CHUNK, gi * LANES:(gi + 1) * LANES],
                            preferred_element_type=F32)
            cols.append(mixed + sbt_ref[:, gi:gi + 1])
        rows.append(jnp.concatenate(cols, axis=1))
    gated = u * jnp.concatenate(rows, axis=0)
    y1 = y + _dot(gated, woc_ref[...])
    cf = wdn_ref.shape[0] // n_split
    out, _ = _ffn_tile(y1, gf_ref, wup_ref, cw_ref, cb_ref, wdn_ref,
                       _prompt_conv_prev(gs, carry, fb_ref, nb_ref, tm, cf), n_split)
    o_ref[0] = _rms(y1 + out, gfin_ref[...])


def _layer_c_prompt(y, fbuf, cp, ffn, g_final, *, tm=512, n_split=2):
    batch, seq, d = y.shape
    d_ff = ffn["wdn"].shape[0]
    cf = d_ff // n_split
    blk = pl.BlockSpec((1, tm, d), lambda b, t: (b, t, 0))
    fb = pl.BlockSpec((1, 2, d_ff), lambda b, t: (b, 0, 0))
    consts = [cp["g"], cp["win"], cp["bin"], cp["sn"], cp["sw"], cp["sbt"], cp["woc"],
              ffn["g"], ffn["wup"], ffn["cw"], ffn["cb"], ffn["wdn"], g_final]
    return pl.pallas_call(
        functools.partial(_layer_c_prompt_kernel, tm=tm, n_split=n_split),
        grid=(batch, seq // tm),
        in_specs=[blk, fb] + [_full_spec(c.shape) for c in consts],
        out_specs=[blk, fb],
        out_shape=[jax.ShapeDtypeStruct((batch, seq, d), F32), jax.ShapeDtypeStruct((batch, 2, d_ff), F32)],
        scratch_shapes=[pltpu.VMEM((tm + 8, cf), F32), pltpu.VMEM((n_split, 8, cf), F32)],
        compiler_params=_cparams(2),
        name="layer_c_prompt",
    )(y, fbuf, *consts)


def _layer_c_dec_kernel(y_ref, fb_ref, gm_ref, win_ref, bin_ref, sn_ref, sw0_ref, sb0_ref, woc_ref,
                        gf_ref, wup_ref, cw_ref, cb_ref, wdn_ref, gfin_ref, o_ref, g_ref, v_ref, *, n_split):
    y = y_ref[...]
    u, v = _gmlp_in(y, gm_ref, win_ref, bin_ref, sn_ref)
    v_ref[...] = v
    y1 = y + _dot(u * (sw0_ref[...] * v + sb0_ref[...]), woc_ref[...])
    cf = wdn_ref.shape[0] // n_split
    out, gates = _ffn_tile(y1, gf_ref, wup_ref, cw_ref, cb_ref, wdn_ref, _dec_conv_prev(fb_ref, cf), n_split)
    o_ref[...] = _rms(y1 + out, gfin_ref[...])
    for k, g in enumerate(gates):
        g_ref[:, k * cf:(k + 1) * cf] = g


def _layer_c_dec(y, fbuf_t, cp, ffn, g_final, *, n_split=2):
    m, d = y.shape
    d_ff = ffn["wdn"].shape[0]
    d_c = cp["woc"].shape[0]
    return pl.pallas_call(
        functools.partial(_layer_c_dec_kernel, n_split=n_split),
        out_shape=[jax.ShapeDtypeStruct((m, d), F32), jax.ShapeDtypeStruct((m, d_ff), F32),
                   jax.ShapeDtypeStruct((m, d_c), F32)],
        compiler_params=pltpu.CompilerParams(vmem_limit_bytes=VMEM_LIMIT),
        name="layer_c_dec",
    )(y, fbuf_t, cp["g"], cp["win"], cp["bin"], cp["sn"], cp["sw0"], cp["sb0"], cp["woc"],
      ffn["g"], ffn["wup"], ffn["cw"], ffn["cb"], ffn["wdn"], g_final)


def _page_copies(src_ref, pt_ref, b, dst_ref, sem, n_pages, page, fn):
    def body(pg, carry):
        col = pl.multiple_of(pg * page, page)
        fn(pltpu.make_async_copy(src_ref.at[pt_ref[b, pg]], dst_ref.at[:, pl.ds(col, page)], sem))
        return carry
    lax.fori_loop(0, n_pages, body, 0)


def _dec_score_kernel(pt_ref, qi_ref, wi_ref, ixn_ref, cik_ref, keys_ref, knew_ref, ibuf, sems, *, n_pages, page):
    b = pl.program_id(0)
    nb = pl.num_programs(0)
    slot = lax.rem(b, 2)

    def copies(bb, sl, fn):
        _page_copies(cik_ref, pt_ref, bb, ibuf.at[sl], sems.at[sl], n_pages, page, fn)

    @pl.when(b == 0)
    def _first():
        copies(0, 0, lambda c: c.start())

    @pl.when(b + 1 < nb)
    def _prefetch_next():
        copies(b + 1, 1 - slot, lambda c: c.start())

    copies(b, slot, lambda c: c.wait())
    qi = qi_ref[0].astype(BF16)
    wi = wi_ref[0]
    s = jnp.dot(qi, ibuf[slot].astype(BF16), preferred_element_type=F32)
    sc = jnp.sum(jnp.maximum(s, 0.0) * wi, axis=0, keepdims=True) * IDX_HEADS ** -0.5
    keys_ref[0] = _float_key(sc)
    kin = ixn_ref[0][:, 0:IDX_DIM]
    sn = jnp.sum(qi_ref[0] * kin, axis=1, keepdims=True)
    scn = jnp.sum(jnp.maximum(sn, 0.0) * wi, axis=0, keepdims=True) * IDX_HEADS ** -0.5
    knew_ref[0] = jnp.broadcast_to(_float_key(scn), (1, LANES))


def _dec_scores(page_table, qi3, wi3, ix3, cik_t, *, page):
    db, n_pages = page_table.shape
    past = n_pages * page
    return pl.pallas_call(
        functools.partial(_dec_score_kernel, n_pages=n_pages, page=page),
        grid_spec=pltpu.PrefetchScalarGridSpec(
            num_scalar_prefetch=1,
            grid=(db,),
            in_specs=[pl.BlockSpec((1, IDX_HEADS, IDX_DIM), lambda b, pt: (b, 0, 0)),
                      pl.BlockSpec((1, IDX_HEADS, 1), lambda b, pt: (b, 0, 0)),
                      pl.BlockSpec((1, 1, LANES), lambda b, pt: (b, 0, 0)),
                      pl.BlockSpec(memory_space=pl.ANY)],
            out_specs=[pl.BlockSpec((1, 1, past), lambda b, pt: (b, 0, 0)),
                       pl.BlockSpec((1, 1, LANES), lambda b, pt: (b, 0, 0))],
            scratch_shapes=[pltpu.VMEM((2, IDX_DIM, past), F32), pltpu.SemaphoreType.DMA((2,))]),
        out_shape=[jax.ShapeDtypeStruct((db, 1, past), I32), jax.ShapeDtypeStruct((db, 1, LANES), I32)],
        compiler_params=_cparams(1),
        name="dec_scores",
    )(page_table, qi3, wi3, ix3, cik_t)


def _dec_select_kernel(keys_ref, knew_ref, ko_ref, kno_ref, thr_ref, *, topk):
    kk = keys_ref[...]
    kn = knew_ref[...][:, 0:1]
    kf = jnp.float32(topk)

    def count(pred, pred_new):
        return jnp.sum(jnp.where(pred, 1.0, 0.0), axis=1, keepdims=True) + jnp.where(pred_new, 1.0, 0.0)

    def search_body(i, ans):
        cand = ans | jnp.left_shift(jnp.int32(1), 31 - i)
        cs = cand ^ jnp.int32(INT_MIN)
        return jnp.where(count(kk >= cs, kn >= cs) >= kf, cand, ans)
    ans = lax.fori_loop(0, 32, search_body, jnp.zeros((kk.shape[0], 1), I32))
    thr = jnp.maximum(ans ^ jnp.int32(INT_MIN), jnp.int32(INT_MIN + 1))
    need = kf - count(kk > thr, kn > thr)
    past = kk.shape[1]
    big = jnp.int32(4 * past)
    eqcol = jnp.where(kk == thr, lax.broadcasted_iota(I32, kk.shape, 1), big)
    eqnew = jnp.where(kn == thr, jnp.int32(past), big)
    nbits = int(math.log2(past)) + 1

    def tie_body(i, best):
        cand = best | jnp.left_shift(jnp.int32(1), nbits - 1 - i)
        return jnp.where(count(eqcol < cand, eqnew < cand) < need, cand, best)
    last = lax.fori_loop(0, nbits, tie_body, jnp.zeros((kk.shape[0], 1), I32))
    ko_ref[...] = jnp.where((eqcol > last) & (eqcol < big), thr - 1, kk)
    kno_ref[...] = jnp.broadcast_to(jnp.where((eqnew > last) & (eqnew < big), thr - 1, kn), kno_ref.shape)
    thr_ref[...] = jnp.broadcast_to(thr, thr_ref.shape)


def _dec_select(keys, knew, *, topk):
    db, past = keys.shape
    return pl.pallas_call(
        functools.partial(_dec_select_kernel, topk=topk),
        out_shape=[jax.ShapeDtypeStruct((db, past), I32), jax.ShapeDtypeStruct((db, LANES), I32),
                   jax.ShapeDtypeStruct((db, LANES), I32)],
        name="dec_select",
    )(keys, knew)


def _dec_attn_kernel(pt_ref, q_ref, kvn_ref, keys_ref, knew_ref, thr_ref, bias_ref, ck_ref, cv_ref, o_ref,
                     kbuf, vbuf, sems, *, n_pages, page):
    b = pl.program_id(0)
    nb = pl.num_programs(0)
    slot = lax.rem(b, 2)
    past = n_pages * page

    def copies(bb, sl, fn):
        _page_copies(ck_ref, pt_ref, bb, kbuf.at[sl], sems.at[0, sl], n_pages, page, fn)
        _page_copies(cv_ref, pt_ref, bb, vbuf.at[sl], sems.at[1, sl], n_pages, page, fn)

    @pl.when(b == 0)
    def _first():
        copies(0, 0, lambda c: c.start())

    @pl.when(b + 1 < nb)
    def _prefetch_next():
        copies(b + 1, 1 - slot, lambda c: c.start())

    copies(b, slot, lambda c: c.wait())
    qm = q_ref[0]
    thr = thr_ref[0][:, 0:1]
    sel = keys_ref[0] >= thr
    sel_new = knew_ref[0][:, 0:1] >= thr
    bias = bias_ref[...]
    far, last, bnew = bias[:, LANES:LANES + 1], bias[:, 0:page], bias[:, LANES + 1:LANES + 2]
    lg = jnp.dot(qm.astype(BF16), kbuf[slot].astype(BF16), preferred_element_type=F32)
    lane = lax.broadcasted_iota(I32, lg.shape, 1)
    lastp = jnp.concatenate([jnp.zeros((N_HEADS, past - page), F32), last], axis=1)
    lg = jnp.where(sel, lg + jnp.where(lane >= past - page, lastp, far), -jnp.inf)
    kvn = kvn_ref[0]
    lgn = jnp.sum(qm * kvn[:, 0:LANES], axis=1, keepdims=True) + bnew
    lgn = jnp.where(sel_new, lgn, -jnp.inf)
    m = jnp.maximum(jnp.max(lg, axis=1, keepdims=True), lgn)
    e = jnp.exp(lg - m)
    en = jnp.exp(lgn - m)
    den = jnp.sum(e, axis=1, keepdims=True) + en
    pv = _dot_nt(e.astype(BF16), vbuf[slot].astype(BF16)) + en * kvn[:, LANES:2 * LANES]
    pv = pv / den
    lo = lax.broadcasted_iota(I32, (1, LANES), 1) < HEAD_DIM
    o_ref[0] = jnp.concatenate([jnp.where(lo, pv[2 * p:2 * p + 1], pv[2 * p + 1:2 * p + 2]) for p in range(4)],
                               axis=1).astype(BF16)


def _dec_attn(page_table, qm, kvn, keys, knew, thr, bias_dec, ck_t, cv_t, *, page):
    db, n_pages = page_table.shape
    past = n_pages * page
    row = lambda n: pl.BlockSpec((1, 1, n), lambda b, pt: (b, 0, 0))
    return pl.pallas_call(
        functools.partial(_dec_attn_kernel, n_pages=n_pages, page=page),
        grid_spec=pltpu.PrefetchScalarGridSpec(
            num_scalar_prefetch=1,
            grid=(db,),
            in_specs=[pl.BlockSpec((1, N_HEADS, LANES), lambda b, pt: (b, 0, 0)), row(2 * LANES), row(past),
                      row(LANES), row(LANES), pl.BlockSpec((N_HEADS, 2 * LANES), lambda b, pt: (0, 0)),
                      pl.BlockSpec(memory_space=pl.ANY), pl.BlockSpec(memory_space=pl.ANY)],
            out_specs=row(4 * LANES),
            scratch_shapes=[pltpu.VMEM((2, 2 * HEAD_DIM, past), F32), pltpu.VMEM((2, 2 * HEAD_DIM, past), F32),
                            pltpu.SemaphoreType.DMA((2, 2))]),
        out_shape=jax.ShapeDtypeStruct((db, 1, 4 * LANES), BF16),
        compiler_params=_cparams(1),
        name="dec_attn",
    )(page_table, qm, kvn, keys, knew, thr, bias_dec, ck_t, cv_t)


def _prep_in_ab(w):
    d = w.shape[0]
    nq, nkv = N_HEADS * HEAD_DIM, N_KV_HEADS * HEAD_DIM
    offs = np.cumsum([nq, nkv, nkv, IDX_HEADS * IDX_DIM, IDX_DIM, IDX_HEADS, 512])
    q, k, v, qi, ki, wi, g, xr = jnp.split(w, offs.tolist(), axis=1)
    q = q.reshape(d, N_HEADS, HEAD_DIM)[:, np.array(HEAD_PERM), :].reshape(d, nq)
    pad = jnp.zeros((d, _C_G - _C_IX - IDX_DIM - IDX_HEADS), w.dtype)
    return jnp.concatenate([q, k, v, qi, ki, wi, pad, g, xr], axis=1).astype(BF16)


def _block_diag(w):
    n, c, _ = w.shape
    return (jnp.eye(n, dtype=w.dtype)[:, None, :, None] * w[:, :, None, :]).reshape(n * c, n * c).astype(BF16)


def _ffn_params(layer, norm_ffn, w_up, cw, cb, w_down):
    return {"g": norm_ffn[layer][None], "wup": w_up[layer].astype(BF16), "cw": cw[layer], "cb": cb[layer][None],
            "wdn": w_down[layer].astype(BF16)}


def kernel(x_prompt, x_sample, cache_k, cache_v, cache_idx_k, state_rglru_h, state_rglru_conv, state_ffn_conv,
           page_table, norm_mix, norm_ffn, norm_final, rel_bias, w_in_ab, w_out_ab, rg_conv_w, rg_conv_b,
           rg_wa, rg_ba, rg_wx, rg_bx, rg_lambda, w_in_c, b_in_c, sgu_norm, sgu_w, sgu_b, w_out_c,
           ffn_w_up, ffn_conv_w, ffn_conv_b, ffn_w_down):
    batch, seq, d = x_prompt.shape
    db = x_sample.shape[0]
    page = cache_k.shape[2]
    d_a = N_HEADS * HEAD_DIM
    d_b = rg_conv_w.shape[-1]
    d_ff = ffn_w_down.shape[1]
    assert x_sample.shape[1] == 1 and seq % 512 == 0 and page == LANES and w_in_ab.shape[0] == 1

    w_in0 = _prep_in_ab(w_in_ab[0])
    wo = w_out_ab[0]
    wo_a = wo[:d_a].reshape(N_HEADS, HEAD_DIM, d)[np.array(HEAD_PERM)].reshape(d_a, d).astype(BF16)
    wo_b = wo[d_a:].astype(BF16)
    rg = {"cw": rg_conv_w[0], "cb": rg_conv_b[0][None], "wa": _block_diag(rg_wa[0]), "ba": rg_ba[0][None],
          "wx": _block_diag(rg_wx[0]), "bx": rg_bx[0][None], "lam": rg_lambda[0][None]}
    ffn0 = _ffn_params(0, norm_ffn, ffn_w_up, ffn_conv_w, ffn_conv_b, ffn_w_down)
    ffn1 = _ffn_params(1, norm_ffn, ffn_w_up, ffn_conv_w, ffn_conv_b, ffn_w_down)
    cp = {"g": norm_mix[1][None], "win": w_in_c[0].astype(BF16), "bin": b_in_c[0][None], "sn": sgu_norm[0][None],
          "sw": sgu_w[0], "sbt": sgu_b[0].T, "woc": w_out_c[0].astype(BF16),
          "sw0": jnp.repeat(sgu_w[0][:, 0, 0], d // sgu_w.shape[1])[None],
          "sb0": jnp.repeat(sgu_b[0][:, 0], d // sgu_w.shape[1])[None]}
    g_mix0 = norm_mix[0][None]
    g_final = norm_final[None]
    bias_st, bias_dec = _bias_tables(rel_bias, page)

    xp = x_prompt.reshape(batch * seq, d)
    q_st, qi_st, kv_p, ix_p, gate_p, xr_p = _inproj(xp, g_mix0, w_in0, stack=True, tm=512)
    attn_p = _attn_prompt(q_st, qi_st, ix_p, kv_p, bias_st, batch=batch, seq=seq)
    rg_p, h_p, cbuf_p = _rglru_prompt(gate_p, xr_p, jnp.zeros((batch, rg["cw"].shape[0] - 1, d_b), F32),
                                      jnp.zeros((batch, d_b), F32), rg, batch=batch, seq=seq)
    zero_fb = jnp.zeros((batch, 2, d_ff), F32)
    y1_p, fb0_p = _post_ab_prompt(x_prompt, attn_p, rg_p, zero_fb, wo_a, wo_b, ffn0)
    y_p, fb1_p = _layer_c_prompt(y1_p, zero_fb, cp, ffn1, g_final)

    xs = x_sample.reshape(db, d)
    qm_s, qi_s, kv_s, ix_s, gate_s, xr_s = _inproj(xs, g_mix0, w_in0, stack=False, tm=db)
    cik_t = jnp.transpose(cache_idx_k[0], (0, 2, 1))
    ck_t = jnp.transpose(cache_k[0], (0, 2, 3, 1)).reshape(-1, 2 * HEAD_DIM, page)
    cv_t = jnp.transpose(cache_v[0], (0, 2, 3, 1)).reshape(-1, 2 * HEAD_DIM, page)
    topk_s = min(TOPK_MAX, (page_table.shape[1] * page + 1) // 4)
    keys_s, knew_s = _dec_scores(page_table, qi_s.reshape(db, IDX_HEADS, IDX_DIM),
                                 ix_s[:, IDX_DIM:IDX_DIM + IDX_HEADS].reshape(db, IDX_HEADS, 1),
                                 ix_s.reshape(db, 1, LANES), cik_t, page=page)
    keys_s, knew_s, thr_s = _dec_select(keys_s.reshape(db, -1), knew_s.reshape(db, LANES), topk=topk_s)
    attn_s = _dec_attn(page_table, jnp.transpose(qm_s, (1, 0, 2)), kv_s.reshape(db, 1, 2 * LANES),
                       keys_s.reshape(db, 1, -1), knew_s.reshape(db, 1, LANES), thr_s.reshape(db, 1, LANES),
                       bias_dec, ck_t, cv_t, page=page).reshape(db, d_a)
    cbuf_s_in = state_rglru_conv[0]
    rg_s, h_s = _rglru_dec(gate_s, xr_s, jnp.transpose(cbuf_s_in, (1, 0, 2)), state_rglru_h[0], rg)
    y1_s, g0_s = _post_ab_dec(xs, attn_s, rg_s, jnp.transpose(state_ffn_conv[0], (1, 0, 2)), wo_a, wo_b, ffn0)
    y_s, g1_s, v_s = _layer_c_dec(y1_s, jnp.transpose(state_ffn_conv[1], (1, 0, 2)), cp, ffn1, g_final)

    kv4 = kv_p.reshape(batch, seq, 2, N_KV_HEADS, HEAD_DIM)
    kvs = kv_s.reshape(db, 1, 2, N_KV_HEADS, HEAD_DIM)
    fbuf_s = lambda layer, g: jnp.concatenate([state_ffn_conv[layer][:, 1:], g[:, None]], axis=1)
    return (y_p, y_s.reshape(db, 1, d),
            kv4[None, :, :, 0], kv4[None, :, :, 1], ix_p.reshape(batch, seq, LANES)[None, :, :, :IDX_DIM],
            kvs[None, :, :, 0], kvs[None, :, :, 1], ix_s.reshape(db, 1, LANES)[None, :, :, :IDX_DIM],
            h_p.reshape(batch, d_b)[None], cbuf_p[None],
            h_s[None], jnp.concatenate([cbuf_s_in[:, 1:], xr_s[:, None]], axis=1)[None],
            v_s.reshape(db, 1, -1)[None],
            jnp.stack([fb0_p, fb1_p]), jnp.stack([fbuf_s(0, g0_s), fbuf_s(1, g1_s)]))
```

```python
import functools
import math

import numpy as np
import jax
import jax.numpy as jnp
from jax import lax
from jax.experimental import pallas as pl
from jax.experimental.pallas import tpu as pltpu

F32 = jnp.float32
BF16 = jnp.bfloat16
I32 = jnp.int32

N_HEADS = 8
HEAD_DIM = 64
N_KV_HEADS = 2
Q_PER_KV = N_HEADS // N_KV_HEADS
IDX_HEADS = 8
IDX_DIM = 64
TOPK_MAX = 256
N_BUCKETS = 32
REL_MAX_EXACT = N_BUCKETS // 2
REL_MAX_DIST = 128
RG_C = 8.0
CHUNK = 128
EPS = 1e-6

LANES = 128
QB = 128
INT_MIN = -(2 ** 31)
HEAD_PERM = (0, 4, 1, 5, 2, 6, 3, 7)
VMEM_LIMIT = 56 * 1024 * 1024


def _cparams(n_grid):
    return pltpu.CompilerParams(dimension_semantics=("arbitrary",) * n_grid, vmem_limit_bytes=VMEM_LIMIT)


def _full_spec(shape):
    nd = len(shape)
    return pl.BlockSpec(shape, lambda *_: (0,) * nd, pipeline_mode=pl.Buffered(1))


def _rms(x, g):
    return x * lax.rsqrt(jnp.mean(x * x, axis=-1, keepdims=True) + EPS) * g


def _gelu(x):
    return x * (0.5 * (1.0 + jnp.tanh(math.sqrt(2.0 / math.pi) * (x + 0.044715 * (x * x * x)))))


def _sigmoid(x):
    return 1.0 / (1.0 + jnp.exp(-x))


def _softplus(x):
    return jnp.maximum(x, 0.0) + jnp.log(1.0 + jnp.exp(-jnp.abs(x)))


def _dot(a, b):
    return jnp.dot(a.astype(BF16), b, preferred_element_type=F32)


def _dot_nt(a, b):
    return lax.dot_general(a, b, (((1,), (1,)), ((), ())), preferred_element_type=F32)


def _float_key(x):
    bits = pltpu.bitcast(x, I32)
    key = jnp.where(bits < 0, bits ^ jnp.int32(0x7FFFFFFF), bits)
    return jnp.where(bits == jnp.int32(INT_MIN), jnp.int32(0), key)


def _t5_bucket_np(n):
    n = np.maximum(n, 0)
    nf = np.maximum(n, 1).astype(np.float32)
    large = REL_MAX_EXACT + (np.log(nf / np.float32(REL_MAX_EXACT)) / np.float32(math.log(REL_MAX_DIST / REL_MAX_EXACT))
                             * np.float32(N_BUCKETS - REL_MAX_EXACT)).astype(np.int32)
    large = np.minimum(large, N_BUCKETS - 1)
    return np.where(n < REL_MAX_EXACT, n, large).astype(np.int32)


_C_Q, _C_KV, _C_QI, _C_IX, _C_G, _C_X, _C_END = 0, 512, 768, 1280, 1408, 1920, 2432


def _inproj_kernel(x_ref, g_ref, w_ref, q_ref, qi_ref, kv_ref, ix_ref, gate_ref, xr_ref, *, stack):
    hn = _rms(x_ref[...], g_ref[...])
    z = _dot(hn, w_ref[...])
    q = z[:, _C_Q:_C_KV] * HEAD_DIM ** -0.5
    qi = z[:, _C_QI:_C_IX] * IDX_DIM ** -0.5
    kv_ref[...] = z[:, _C_KV:_C_QI]
    ix_ref[...] = z[:, _C_IX:_C_G]
    gate_ref[...] = z[:, _C_G:_C_X]
    xr_ref[...] = z[:, _C_X:_C_END]
    if stack:
        qb, qib = q.astype(BF16), qi.astype(BF16)
        for r in range(q.shape[0] // QB):
            for p in range(4):
                q_ref[r, p * QB:(p + 1) * QB, :] = qb[r * QB:(r + 1) * QB, p * LANES:(p + 1) * LANES]
                qi_ref[r, p * QB:(p + 1) * QB, :] = qib[r * QB:(r + 1) * QB, p * LANES:(p + 1) * LANES]
    else:
        lo = lax.broadcasted_iota(I32, (q.shape[0], LANES), 1) < HEAD_DIM
        for p in range(4):
            qp = q[:, p * LANES:(p + 1) * LANES]
            q_ref[2 * p] = jnp.where(lo, qp, 0.0)
            q_ref[2 * p + 1] = jnp.where(lo, 0.0, qp)
        qi_ref[...] = qi


def _inproj(x2d, g, w, *, stack, tm):
    m, d = x2d.shape
    if stack:
        q_shape, q_spec = (m // QB, 4 * QB, LANES), pl.BlockSpec((tm // QB, 4 * QB, LANES), lambda i: (i, 0, 0))
        qi_shape, qi_spec, qdt = q_shape, q_spec, BF16
    else:
        q_shape, q_spec = (N_HEADS, m, LANES), pl.BlockSpec((N_HEADS, tm, LANES), lambda i: (0, i, 0))
        qi_shape, qi_spec, qdt = (m, 512), pl.BlockSpec((tm, 512), lambda i: (i, 0)), F32
    row = lambda n: pl.BlockSpec((tm, n), lambda i: (i, 0))
    return pl.pallas_call(
        functools.partial(_inproj_kernel, stack=stack),
        grid=(m // tm,),
        in_specs=[row(d), _full_spec((1, d)), _full_spec(w.shape)],
        out_specs=[q_spec, qi_spec, row(256), row(128), row(512), row(512)],
        out_shape=[jax.ShapeDtypeStruct(q_shape, qdt), jax.ShapeDtypeStruct(qi_shape, qdt),
                   jax.ShapeDtypeStruct((m, 256), F32), jax.ShapeDtypeStruct((m, 128), F32),
                   jax.ShapeDtypeStruct((m, 512), F32), jax.ShapeDtypeStruct((m, 512), F32)],
        compiler_params=_cparams(1),
        name="inproj_stack" if stack else "inproj_dec",
    )(x2d, g, w)


def _bias_kernel(rb_ref, bk_ref, bkd_ref, o_ref, od_ref):
    for d in range(3):
        bk = bk_ref[d]
        for p in range(4):
            for a in range(2):
                h = p + 4 * a
                acc = jnp.zeros((QB, LANES), F32)
                for b in range(N_BUCKETS):
                    acc = jnp.where(bk == b, rb_ref[b, h], acc)
                o_ref[d, a * QB:(a + 1) * QB, p * LANES:(p + 1) * LANES] = acc
    bkd = bkd_ref[...]
    rowi = lax.broadcasted_iota(I32, (N_HEADS, 2 * LANES), 0)
    acc = jnp.zeros((N_HEADS, 2 * LANES), F32)
    for r in range(N_HEADS):
        h = r // 2 + 4 * (r % 2)
        for b in range(N_BUCKETS):
            acc = jnp.where((rowi == r) & (bkd == b), rb_ref[b, h], acc)
    od_ref[...] = acc


def _bias_tables(rel_bias, page):
    key = np.arange(QB)[:, None]
    qry = np.arange(LANES)[None, :]
    bk = np.stack([_t5_bucket_np(d * QB + qry - key) for d in range(3)])
    assert (_t5_bucket_np(np.arange(2 * QB + 1 - LANES, 4 * QB)) == N_BUCKETS - 1).all()
    assert (_t5_bucket_np(np.arange(page, 8 * page)) == N_BUCKETS - 1).all()
    dec = np.zeros((2 * LANES,), np.int64)
    dec[:page] = page - np.arange(page)
    dec[LANES] = 2 * REL_MAX_DIST
    dec[LANES + 1] = 0
    bkd = np.broadcast_to(_t5_bucket_np(dec)[None, :], (N_HEADS, 2 * LANES))
    return pl.pallas_call(
        _bias_kernel,
        in_specs=[pl.BlockSpec(memory_space=pltpu.SMEM), pl.BlockSpec(memory_space=pltpu.VMEM),
                  pl.BlockSpec(memory_space=pltpu.VMEM)],
        out_shape=[jax.ShapeDtypeStruct((3, 2 * QB, 4 * LANES), F32), jax.ShapeDtypeStruct((N_HEADS, 2 * LANES), F32)],
        name="bias_tables",
    )(rel_bias, jnp.asarray(bk, I32), jnp.asarray(bkd, I32))


def _search_widths(n_chunks):
    cuts = sorted({min(c, n_chunks) for c in (2, 4, 8, 12, 16)} | {n_chunks})
    return [c for c in cuts if c <= n_chunks]


def _attn_prompt_kernel(q_ref, qi_ref, ixq_ref, ixk_ref, kv_ref, bias_ref, o_ref,
                        kblk, vblk_t, kiblk, keys, logits, acc, thr_ref, cge_ref, *, n_chunks, topk):
    j = pl.program_id(1)
    lane = lax.broadcasted_iota(I32, (QB, LANES), 1)
    row = lax.broadcasted_iota(I32, (QB, LANES), 0)
    lo = lane < HEAD_DIM
    blocks = [(a, p) for a in range(2) for p in range(4)]
    rs = lambda a: slice(a * QB, (a + 1) * QB)
    cs = lambda p: slice(p * LANES, (p + 1) * LANES)
    chunk = lambda c: pl.ds(pl.multiple_of(c * QB, QB), QB)

    @pl.when(j == 0)
    def _build_block_diagonal_keys():
        def body(c, carry):
            kc = kv_ref[0, chunk(c), 0:LANES]
            vt = kv_ref[0, chunk(c), LANES:2 * LANES].T
            kia = jnp.where(lo, ixk_ref[0, chunk(c), :], 0.0)
            kblk[c, 0:QB, :] = jnp.where(lo, kc, 0.0).astype(BF16)
            kblk[c, QB:2 * QB, :] = jnp.where(lo, 0.0, kc).astype(BF16)
            vblk_t[c, :, 0:QB] = jnp.where(row < HEAD_DIM, vt, 0.0).astype(BF16)
            vblk_t[c, :, QB:2 * QB] = jnp.where(row < HEAD_DIM, 0.0, vt).astype(BF16)
            kiblk[c, 0:QB, :] = kia.astype(BF16)
            kiblk[c, QB:2 * QB, :] = pltpu.roll(kia, HEAD_DIM, 1).astype(BF16)
            return carry
        lax.fori_loop(0, n_chunks, body, 0)

    qi = qi_ref[0]
    wt = ixq_ref[0].T
    w_row = {(a, p): wt[IDX_DIM + 2 * p + a:IDX_DIM + 2 * p + a + 1, :] for a, p in blocks}
    qpos = j * QB + lane

    n_pairs = (j + 2) // 2

    def chunk_loop(body, carry):
        n_quads = n_pairs // 2
        carry = lax.fori_loop(0, n_quads, lambda i, cr: body([4 * i + u for u in range(4)], cr), carry)
        return lax.fori_loop(2 * n_quads, n_pairs, lambda i, cr: body([2 * i, 2 * i + 1], cr), carry)

    def score_body(cs_, carry):
        for c in cs_:
            s = _dot_nt(kiblk[c], qi)
            sc = jnp.zeros((QB, LANES), F32)
            for a, p in blocks:
                sc = sc + jnp.maximum(s[rs(a), cs(p)], 0.0) * w_row[(a, p)]
            key = _float_key(sc * IDX_HEADS ** -0.5)
            keys[chunk(c), :] = jnp.where(c * QB + row <= qpos, key, jnp.int32(INT_MIN))
        return carry
    chunk_loop(score_body, 0)

    def fill_body(c, carry):
        keys[chunk(c), :] = jnp.full((QB, LANES), INT_MIN, I32)
        return carry
    lax.fori_loop(2 * n_pairs, n_chunks, fill_body, 0)

    kf = jnp.float32(topk)

    def count(pred):
        ones = jnp.where(pred, 1.0, 0.0)
        part = ones.reshape(ones.shape[0] // 64, 64, LANES).sum(axis=0)
        return jnp.sum(part, axis=0, keepdims=True)

    def search(width):
        def search_body(i, ans):
            cand = ans | jnp.left_shift(jnp.int32(1), 31 - i)
            cnt = count(keys[0:width, :] >= (cand ^ jnp.int32(INT_MIN)))
            return jnp.where(cnt >= kf, cand, ans)
        ans = lax.fori_loop(0, 32, search_body, jnp.zeros((1, LANES), I32))
        t = jnp.maximum(ans ^ jnp.int32(INT_MIN), jnp.int32(INT_MIN + 1))
        thr_ref[...] = t
        cge_ref[...] = count(keys[0:width, :] >= t)

    prev = 0
    for n in _search_widths(n_chunks):
        pl.when((j >= prev) & (j < n))(functools.partial(search, n * QB))
        prev = n
    thr = thr_ref[...]

    @pl.when(jnp.max(cge_ref[...]) > kf)
    def _break_ties_by_position():
        kk = keys[...]
        s_len = kk.shape[0]
        need = kf - count(kk > thr)
        big = jnp.int32(2 * s_len)
        eqrow = jnp.where(kk == thr, lax.broadcasted_iota(I32, kk.shape, 0), big)
        nbits = int(math.log2(s_len))

        def tie_body(i, best):
            cand = best | jnp.left_shift(jnp.int32(1), nbits - 1 - i)
            return jnp.where(count(eqrow < cand) < need, cand, best)
        last = lax.fori_loop(0, nbits, tie_body, jnp.zeros((1, LANES), I32))
        keys[...] = jnp.where((eqrow > last) & (eqrow < big), thr - 1, kk)

    q = q_ref[0]

    def logits_body(cs_, mx):
        mx = list(mx)
        for c in cs_:
            lg = _dot_nt(kblk[c], q)
            bias = bias_ref[jnp.clip(j - c, 0, 2)]
            sel = keys[chunk(c), :] >= thr
            for n, (a, p) in enumerate(blocks):
                blk = jnp.where(sel, lg[rs(a), cs(p)] + bias[rs(a), cs(p)], -jnp.inf)
                logits[c, rs(a), cs(p)] = blk
                mx[n] = jnp.maximum(mx[n], jnp.max(blk, axis=0, keepdims=True))
        return tuple(mx)
    mx = chunk_loop(logits_body, tuple(jnp.full((1, LANES), -jnp.inf, F32) for _ in blocks))

    acc[...] = jnp.zeros(acc.shape, F32)

    def pv_body(cs_, ls):
        ls = list(ls)
        pv = jnp.zeros(acc.shape, F32)
        for c in cs_:
            rows = []
            for a in range(2):
                cols = []
                for p in range(4):
                    e = jnp.exp(logits[c, rs(a), cs(p)] - mx[a * 4 + p])
                    ls[a * 4 + p] = ls[a * 4 + p] + jnp.sum(e, axis=0, keepdims=True)
                    cols.append(e.astype(BF16))
                rows.append(jnp.concatenate(cols, axis=1))
            pmat = jnp.concatenate(rows, axis=0)
            pv = pv + jnp.dot(vblk_t[c], pmat, preferred_element_type=F32)
        acc[...] = acc[...] + pv
        return tuple(ls)
    ls = chunk_loop(pv_body, tuple(jnp.zeros((1, LANES), F32) for _ in blocks))

    for p in range(4):
        inv = jnp.where(row < HEAD_DIM, 1.0 / ls[p], 1.0 / ls[4 + p])
        o_ref[0, :, cs(p)] = (acc[:, cs(p)] * inv).T.astype(BF16)


def _attn_prompt(q_st, qi_st, ix, kv, bias_st, *, batch, seq):
    nq = seq // QB
    assert nq % 2 == 0
    topk = min(TOPK_MAX, seq // 4)
    ix3 = ix.reshape(batch, seq, LANES)
    kv3 = kv.reshape(batch, seq, 2 * LANES)
    return pl.pallas_call(
        functools.partial(_attn_prompt_kernel, n_chunks=nq, topk=topk),
        grid=(batch, nq),
        in_specs=[pl.BlockSpec((1, 4 * QB, LANES), lambda b, j: (b * nq + j, 0, 0)),
                  pl.BlockSpec((1, 4 * QB, LANES), lambda b, j: (b * nq + j, 0, 0)),
                  pl.BlockSpec((1, QB, LANES), lambda b, j: (b, j, 0)),
                  pl.BlockSpec((1, seq, LANES), lambda b, j: (b, 0, 0)),
                  pl.BlockSpec((1, seq, 2 * LANES), lambda b, j: (b, 0, 0)),
                  _full_spec(bias_st.shape)],
        out_specs=pl.BlockSpec((1, QB, 4 * LANES), lambda b, j: (b, j, 0)),
        out_shape=jax.ShapeDtypeStruct((batch, seq, 4 * LANES), BF16),
        scratch_shapes=[pltpu.VMEM((nq, 2 * QB, LANES), BF16), pltpu.VMEM((nq, LANES, 2 * QB), BF16),
                        pltpu.VMEM((nq, 2 * QB, LANES), BF16), pltpu.VMEM((seq, LANES), I32),
                        pltpu.VMEM((nq, 2 * QB, 4 * LANES), F32), pltpu.VMEM((LANES, 4 * LANES), F32),
                        pltpu.VMEM((1, LANES), I32), pltpu.VMEM((1, LANES), F32)],
        compiler_params=_cparams(2),
        name="attn_prompt",
    )(q_st, qi_st, ix3, ix3, kv3, bias_st)


def _rglru_gates(xc, wa, ba, wx, bx, lam):
    r = _sigmoid(_dot(xc, wa) + ba)
    i = _sigmoid(_dot(xc, wx) + bx)
    log_a = -RG_C * r * _softplus(-lam)
    a = jnp.exp(log_a)
    u = jnp.sqrt(1.0 - jnp.exp(2.0 * log_a)) * (i * xc)
    return a, u


def _rglru_prompt_kernel(g_ref, xr_ref, buf_ref, h0_ref, cw_ref, cb_ref, wa_ref, ba_ref, wx_ref, bx_ref, lam_ref,
                         o_ref, hl_ref, nb_ref, xs, a_s, u_s, tail, hc, *, tc):
    t = pl.program_id(1)
    width = cw_ref.shape[0]

    @pl.when(t == 0)
    def _load_state():
        tail[...] = jnp.zeros(tail.shape, F32)
        tail[8 - (width - 1):8, :] = buf_ref[0]
        hc[...] = h0_ref[0]

    xs[0:8, :] = tail[...]
    xs[8:8 + tc, :] = xr_ref[0]
    tail[...] = xs[tc:tc + 8, :]
    xc = cb_ref[...]
    for jj in range(width):
        off = 8 - (width - 1) + jj
        xc = xc + cw_ref[jj:jj + 1, :] * xs[off:off + tc, :]
    a, u = _rglru_gates(xc, wa_ref[...], ba_ref[...], wx_ref[...], bx_ref[...], lam_ref[...])
    a_s[...] = a
    u_s[...] = u

    def scan_body(i, h):
        h = a_s[pl.ds(i, 1), :] * h + u_s[pl.ds(i, 1), :]
        u_s[pl.ds(i, 1), :] = h
        return h
    h = lax.fori_loop(0, tc, scan_body, hc[...], unroll=8)
    hc[...] = h
    o_ref[0] = (_gelu(g_ref[0]) * u_s[...]).astype(BF16)
    hl_ref[0] = h
    nb_ref[0] = xs[tc + 8 - (width - 1):tc + 8, :]


def _rglru_prompt(gate, xr, buf, h0, rg, *, batch, seq, tc=512):
    d = gate.shape[-1]
    width = rg["cw"].shape[0]
    g3, x3 = gate.reshape(batch, seq, d), xr.reshape(batch, seq, d)
    blk = pl.BlockSpec((1, tc, d), lambda b, t: (b, t, 0))
    per_b = lambda n: pl.BlockSpec((1, n, d), lambda b, t: (b, 0, 0))
    vec = _full_spec((1, d))
    return pl.pallas_call(
        functools.partial(_rglru_prompt_kernel, tc=tc),
        grid=(batch, seq // tc),
        in_specs=[blk, blk, per_b(width - 1), per_b(1), _full_spec((width, d)), vec,
                  _full_spec((d, d)), vec, _full_spec((d, d)), vec, vec],
        out_specs=[blk, per_b(1), per_b(width - 1)],
        out_shape=[jax.ShapeDtypeStruct((batch, seq, d), BF16), jax.ShapeDtypeStruct((batch, 1, d), F32),
                   jax.ShapeDtypeStruct((batch, width - 1, d), F32)],
        scratch_shapes=[pltpu.VMEM((tc + 8, d), F32), pltpu.VMEM((tc, d), F32), pltpu.VMEM((tc, d), F32),
                        pltpu.VMEM((8, d), F32), pltpu.VMEM((1, d), F32)],
        compiler_params=_cparams(2),
        name="rglru_prompt",
    )(g3, x3, buf, h0.reshape(batch, 1, d), rg["cw"], rg["cb"], rg["wa"], rg["ba"], rg["wx"], rg["bx"], rg["lam"])


def _rglru_dec_kernel(g_ref, xr_ref, buf_ref, h0_ref, cw_ref, cb_ref, wa_ref, ba_ref, wx_ref, bx_ref, lam_ref,
                      o_ref, hl_ref):
    width = cw_ref.shape[0]
    xc = cb_ref[...]
    for jj in range(width - 1):
        xc = xc + cw_ref[jj:jj + 1, :] * buf_ref[jj]
    xc = xc + cw_ref[width - 1:width, :] * xr_ref[...]
    a, u = _rglru_gates(xc, wa_ref[...], ba_ref[...], wx_ref[...], bx_ref[...], lam_ref[...])
    h = a * h0_ref[...] + u
    hl_ref[...] = h
    o_ref[...] = (_gelu(g_ref[...]) * h).astype(BF16)


def _rglru_dec(gate, xr, buf_t, h0, rg):
    m, d = gate.shape
    return pl.pallas_call(
        _rglru_dec_kernel,
        out_shape=[jax.ShapeDtypeStruct((m, d), BF16), jax.ShapeDtypeStruct((m, d), F32)],
        name="rglru_dec",
    )(gate, xr, buf_t, h0, rg["cw"], rg["cb"], rg["wa"], rg["ba"], rg["wx"], rg["bx"], rg["lam"])


def _ffn_tile(y1, gf_ref, wup_ref, cw_ref, cb_ref, wdn_ref, conv_prev, n_split):
    d_ff = wdn_ref.shape[0]
    cf = d_ff // n_split
    hn = _rms(y1, gf_ref[...]).astype(BF16)
    out = jnp.zeros(y1.shape, F32)
    gates = []
    for k in range(n_split):
        c0 = k * cf
        g = jnp.dot(hn, wup_ref[:, c0:c0 + cf], preferred_element_type=F32)
        u = jnp.dot(hn, wup_ref[:, d_ff + c0:d_ff + c0 + cf], preferred_element_type=F32)
        g1, g2 = conv_prev(k, g)
        gc = cb_ref[:, c0:c0 + cf] + cw_ref[0:1, c0:c0 + cf] * g2 + cw_ref[1:2, c0:c0 + cf] * g1 \
            + cw_ref[2:3, c0:c0 + cf] * g
        act = (_gelu(gc) * u).astype(BF16)
        out = out + jnp.dot(act, wdn_ref[c0:c0 + cf, :], preferred_element_type=F32)
        gates.append(g)
    return out, gates


def _prompt_conv_prev(gs, carry, fb_ref, nb_ref, tm, cf):
    t = pl.program_id(1)

    @pl.when(t == 0)
    def _load_state():
        carry[...] = jnp.zeros(carry.shape, F32)
        for k in range(carry.shape[0]):
            carry[k, 6:8, :] = fb_ref[0, :, k * cf:(k + 1) * cf]

    def conv_prev(k, g):
        gs[0:8, :] = carry[k]
        gs[8:8 + tm, :] = g
        carry[k] = g[tm - 8:tm, :]
        nb_ref[0, :, k * cf:(k + 1) * cf] = g[tm - 2:tm, :]
        return gs[7:7 + tm, :], gs[6:6 + tm, :]
    return conv_prev


def _mix_ab_tile(y_ref, a_ref, r_ref, woa_ref, wob_ref):
    return y_ref[0] + jnp.dot(a_ref[0], woa_ref[...], preferred_element_type=F32) \
        + jnp.dot(r_ref[0], wob_ref[...], preferred_element_type=F32)


def _post_ab_prompt_kernel(y_ref, a_ref, r_ref, fb_ref, woa_ref, wob_ref, gf_ref, wup_ref, cw_ref, cb_ref, wdn_ref,
                           o_ref, nb_ref, gs, carry, *, tm, n_split):
    y1 = _mix_ab_tile(y_ref, a_ref, r_ref, woa_ref, wob_ref)
    cf = wdn_ref.shape[0] // n_split
    out, _ = _ffn_tile(y1, gf_ref, wup_ref, cw_ref, cb_ref, wdn_ref,
                       _prompt_conv_prev(gs, carry, fb_ref, nb_ref, tm, cf), n_split)
    o_ref[0] = y1 + out


def _ffn_specs(d, d_ff):
    return [_full_spec((1, d)), _full_spec((d, 2 * d_ff)), _full_spec((3, d_ff)), _full_spec((1, d_ff)),
            _full_spec((d_ff, d))]


def _post_ab_prompt(y, attn, rgo, fbuf, wo_a, wo_b, ffn, *, tm=512, n_split=2):
    batch, seq, d = y.shape
    d_ff = ffn["wdn"].shape[0]
    cf = d_ff // n_split
    blk = lambda n: pl.BlockSpec((1, tm, n), lambda b, t: (b, t, 0))
    fb = pl.BlockSpec((1, 2, d_ff), lambda b, t: (b, 0, 0))
    return pl.pallas_call(
        functools.partial(_post_ab_prompt_kernel, tm=tm, n_split=n_split),
        grid=(batch, seq // tm),
        in_specs=[blk(d), blk(attn.shape[-1]), blk(rgo.shape[-1]), fb, _full_spec(wo_a.shape), _full_spec(wo_b.shape)]
        + _ffn_specs(d, d_ff),
        out_specs=[blk(d), fb],
        out_shape=[jax.ShapeDtypeStruct((batch, seq, d), F32), jax.ShapeDtypeStruct((batch, 2, d_ff), F32)],
        scratch_shapes=[pltpu.VMEM((tm + 8, cf), F32), pltpu.VMEM((n_split, 8, cf), F32)],
        compiler_params=_cparams(2),
        name="post_ab_prompt",
    )(y, attn, rgo, fbuf, wo_a, wo_b, ffn["g"], ffn["wup"], ffn["cw"], ffn["cb"], ffn["wdn"])


def _dec_conv_prev(fb_ref, cf):
    def conv_prev(k, g):
        return fb_ref[1, :, k * cf:(k + 1) * cf], fb_ref[0, :, k * cf:(k + 1) * cf]
    return conv_prev


def _post_ab_dec_kernel(y_ref, a_ref, r_ref, fb_ref, woa_ref, wob_ref, gf_ref, wup_ref, cw_ref, cb_ref, wdn_ref,
                        o_ref, g_ref, *, n_split):
    y1 = y_ref[...] + jnp.dot(a_ref[...], woa_ref[...], preferred_element_type=F32) \
        + jnp.dot(r_ref[...], wob_ref[...], preferred_element_type=F32)
    cf = wdn_ref.shape[0] // n_split
    out, gates = _ffn_tile(y1, gf_ref, wup_ref, cw_ref, cb_ref, wdn_ref, _dec_conv_prev(fb_ref, cf), n_split)
    o_ref[...] = y1 + out
    for k, g in enumerate(gates):
        g_ref[:, k * cf:(k + 1) * cf] = g


def _post_ab_dec(y, attn, rgo, fbuf_t, wo_a, wo_b, ffn, *, n_split=2):
    m, d = y.shape
    d_ff = ffn["wdn"].shape[0]
    return pl.pallas_call(
        functools.partial(_post_ab_dec_kernel, n_split=n_split),
        out_shape=[jax.ShapeDtypeStruct((m, d), F32), jax.ShapeDtypeStruct((m, d_ff), F32)],
        compiler_params=pltpu.CompilerParams(vmem_limit_bytes=VMEM_LIMIT),
        name="post_ab_dec",
    )(y, attn, rgo, fbuf_t, wo_a, wo_b, ffn["g"], ffn["wup"], ffn["cw"], ffn["cb"], ffn["wdn"])


def _gmlp_in(y, gm_ref, win_ref, bin_ref, sn_ref):
    d_c = win_ref.shape[1] // 2
    z = _gelu(_dot(_rms(y, gm_ref[...]), win_ref[...]) + bin_ref[...])
    return z[:, :d_c], _rms(z[:, d_c:], sn_ref[...])


def _layer_c_prompt_kernel(y_ref, fb_ref, gm_ref, win_ref, bin_ref, sn_ref, sw_ref, sbt_ref, woc_ref,
                           gf_ref, wup_ref, cw_ref, cb_ref, wdn_ref, gfin_ref,
                           o_ref, nb_ref, gs, carry, *, tm, n_split):
    y = y_ref[0]
    u, v = _gmlp_in(y, gm_ref, win_ref, bin_ref, sn_ref)
    vb = v.astype(BF16)
    n_groups = sw_ref.shape[0]
    tril = lax.broadcasted_iota(I32, (CHUNK, CHUNK), 0) >= lax.broadcasted_iota(I32, (CHUNK, CHUNK), 1)
    wm = [jnp.where(tril, sw_ref[gi], 0.0).astype(BF16) for gi in range(n_groups)]
    rows = []
    for r in range(tm // CHUNK):
        cols = []
        for gi in range(n_groups):
            mixed = jnp.dot(wm[gi], vb[r * CHUNK:(r + 1) * CHUNK, gi * LANES:(gi + 1) * LANES],
                            preferred_element_type=F32)
            cols.append(mixed + sbt_ref[:, gi:gi + 1])
        rows.append(jnp.concatenate(cols, axis=1))
    gated = u * jnp.concatenate(rows, axis=0)
    y1 = y + _dot(gated, woc_ref[...])
    cf = wdn_ref.shape[0] // n_split
    out, _ = _ffn_tile(y1, gf_ref, wup_ref, cw_ref, cb_ref, wdn_ref,
                       _prompt_conv_prev(gs, carry, fb_ref, nb_ref, tm, cf), n_split)
    o_ref[0] = _rms(y1 + out, gfin_ref[...])


def _layer_c_prompt(y, fbuf, cp, ffn, g_final, *, tm=512, n_split=2):
    batch, seq, d = y.shape
    d_ff = ffn["wdn"].shape[0]
    cf = d_ff // n_split
    blk = pl.BlockSpec((1, tm, d), lambda b, t: (b, t, 0))
    fb = pl.BlockSpec((1, 2, d_ff), lambda b, t: (b, 0, 0))
    consts = [cp["g"], cp["win"], cp["bin"], cp["sn"], cp["sw"], cp["sbt"], cp["woc"],
              ffn["g"], ffn["wup"], ffn["cw"], ffn["cb"], ffn["wdn"], g_final]
    return pl.pallas_call(
        functools.partial(_layer_c_prompt_kernel, tm=tm, n_split=n_split),
        grid=(batch, seq // tm),
        in_specs=[blk, fb] + [_full_spec(c.shape) for c in consts],
        out_specs=[blk, fb],
        out_shape=[jax.ShapeDtypeStruct((batch, seq, d), F32), jax.ShapeDtypeStruct((batch, 2, d_ff), F32)],
        scratch_shapes=[pltpu.VMEM((tm + 8, cf), F32), pltpu.VMEM((n_split, 8, cf), F32)],
        compiler_params=_cparams(2),
        name="layer_c_prompt",
    )(y, fbuf, *consts)


def _layer_c_dec_kernel(y_ref, fb_ref, gm_ref, win_ref, bin_ref, sn_ref, sw0_ref, sb0_ref, woc_ref,
                        gf_ref, wup_ref, cw_ref, cb_ref, wdn_ref, gfin_ref, o_ref, g_ref, v_ref, *, n_split):
    y = y_ref[...]
    u, v = _gmlp_in(y, gm_ref, win_ref, bin_ref, sn_ref)
    v_ref[...] = v
    y1 = y + _dot(u * (sw0_ref[...] * v + sb0_ref[...]), woc_ref[...])
    cf = wdn_ref.shape[0] // n_split
    out, gates = _ffn_tile(y1, gf_ref, wup_ref, cw_ref, cb_ref, wdn_ref, _dec_conv_prev(fb_ref, cf), n_split)
    o_ref[...] = _rms(y1 + out, gfin_ref[...])
    for k, g in enumerate(gates):
        g_ref[:, k * cf:(k + 1) * cf] = g


def _layer_c_dec(y, fbuf_t, cp, ffn, g_final, *, n_split=2):
    m, d = y.shape
    d_ff = ffn["wdn"].shape[0]
    d_c = cp["woc"].shape[0]
    return pl.pallas_call(
        functools.partial(_layer_c_dec_kernel, n_split=n_split),
        out_shape=[jax.ShapeDtypeStruct((m, d), F32), jax.ShapeDtypeStruct((m, d_ff), F32),
                   jax.ShapeDtypeStruct((m, d_c), F32)],
        compiler_params=pltpu.CompilerParams(vmem_limit_bytes=VMEM_LIMIT),
        name="layer_c_dec",
    )(y, fbuf_t, cp["g"], cp["win"], cp["bin"], cp["sn"], cp["sw0"], cp["sb0"], cp["woc"],
      ffn["g"], ffn["wup"], ffn["cw"], ffn["cb"], ffn["wdn"], g_final)


def _page_copies(src_ref, pt_ref, b, dst_ref, sem, n_pages, page, fn):
    def body(pg, carry):
        col = pl.multiple_of(pg * page, page)
        fn(pltpu.make_async_copy(src_ref.at[pt_ref[b, pg]], dst_ref.at[:, pl.ds(col, page)], sem))
        return carry
    lax.fori_loop(0, n_pages, body, 0)


def _dec_score_kernel(pt_ref, qi_ref, wi_ref, ixn_ref, cik_ref, keys_ref, knew_ref, ibuf, sems, *, n_pages, page):
    b = pl.program_id(0)
    nb = pl.num_programs(0)
    slot = lax.rem(b, 2)

    def copies(bb, sl, fn):
        _page_copies(cik_ref, pt_ref, bb, ibuf.at[sl], sems.at[sl], n_pages, page, fn)

    @pl.when(b == 0)
    def _first():
        copies(0, 0, lambda c: c.start())

    @pl.when(b + 1 < nb)
    def _prefetch_next():
        copies(b + 1, 1 - slot, lambda c: c.start())

    copies(b, slot, lambda c: c.wait())
    qi = qi_ref[0].astype(BF16)
    wi = wi_ref[0]
    s = jnp.dot(qi, ibuf[slot].astype(BF16), preferred_element_type=F32)
    sc = jnp.sum(jnp.maximum(s, 0.0) * wi, axis=0, keepdims=True) * IDX_HEADS ** -0.5
    keys_ref[0] = _float_key(sc)
    kin = ixn_ref[0][:, 0:IDX_DIM]
    sn = jnp.sum(qi_ref[0] * kin, axis=1, keepdims=True)
    scn = jnp.sum(jnp.maximum(sn, 0.0) * wi, axis=0, keepdims=True) * IDX_HEADS ** -0.5
    knew_ref[0] = jnp.broadcast_to(_float_key(scn), (1, LANES))


def _dec_scores(page_table, qi3, wi3, ix3, cik_t, *, page):
    db, n_pages = page_table.shape
    past = n_pages * page
    return pl.pallas_call(
        functools.partial(_dec_score_kernel, n_pages=n_pages, page=page),
        grid_spec=pltpu.PrefetchScalarGridSpec(
            num_scalar_prefetch=1,
            grid=(db,),
            in_specs=[pl.BlockSpec((1, IDX_HEADS, IDX_DIM), lambda b, pt: (b, 0, 0)),
                      pl.BlockSpec((1, IDX_HEADS, 1), lambda b, pt: (b, 0, 0)),
                      pl.BlockSpec((1, 1, LANES), lambda b, pt: (b, 0, 0)),
                      pl.BlockSpec(memory_space=pl.ANY)],
            out_specs=[pl.BlockSpec((1, 1, past), lambda b, pt: (b, 0, 0)),
                       pl.BlockSpec((1, 1, LANES), lambda b, pt: (b, 0, 0))],
            scratch_shapes=[pltpu.VMEM((2, IDX_DIM, past), F32), pltpu.SemaphoreType.DMA((2,))]),
        out_shape=[jax.ShapeDtypeStruct((db, 1, past), I32), jax.ShapeDtypeStruct((db, 1, LANES), I32)],
        compiler_params=_cparams(1),
        name="dec_scores",
    )(page_table, qi3, wi3, ix3, cik_t)


def _dec_select_kernel(keys_ref, knew_ref, ko_ref, kno_ref, thr_ref, *, topk):
    kk = keys_ref[...]
    kn = knew_ref[...][:, 0:1]
    kf = jnp.float32(topk)

    def count(pred, pred_new):
        return jnp.sum(jnp.where(pred, 1.0, 0.0), axis=1, keepdims=True) + jnp.where(pred_new, 1.0, 0.0)

    def search_body(i, ans):
        cand = ans | jnp.left_shift(jnp.int32(1), 31 - i)
        cs = cand ^ jnp.int32(INT_MIN)
        return jnp.where(count(kk >= cs, kn >= cs) >= kf, cand, ans)
    ans = lax.fori_loop(0, 32, search_body, jnp.zeros((kk.shape[0], 1), I32))
    thr = jnp.maximum(ans ^ jnp.int32(INT_MIN), jnp.int32(INT_MIN + 1))
    need = kf - count(kk > thr, kn > thr)
    past = kk.shape[1]
    big = jnp.int32(4 * past)
    eqcol = jnp.where(kk == thr, lax.broadcasted_iota(I32, kk.shape, 1), big)
    eqnew = jnp.where(kn == thr, jnp.int32(past), big)
    nbits = int(math.log2(past)) + 1

    def tie_body(i, best):
        cand = best | jnp.left_shift(jnp.int32(1), nbits - 1 - i)
        return jnp.where(count(eqcol < cand, eqnew < cand) < need, cand, best)
    last = lax.fori_loop(0, nbits, tie_body, jnp.zeros((kk.shape[0], 1), I32))
    ko_ref[...] = jnp.where((eqcol > last) & (eqcol < big), thr - 1, kk)
    kno_ref[...] = jnp.broadcast_to(jnp.where((eqnew > last) & (eqnew < big), thr - 1, kn), kno_ref.shape)
    thr_ref[...] = jnp.broadcast_to(thr, thr_ref.shape)


def _dec_select(keys, knew, *, topk):
    db, past = keys.shape
    return pl.pallas_call(
        functools.partial(_dec_select_kernel, topk=topk),
        out_shape=[jax.ShapeDtypeStruct((db, past), I32), jax.ShapeDtypeStruct((db, LANES), I32),
                   jax.ShapeDtypeStruct((db, LANES), I32)],
        name="dec_select",
    )(keys, knew)


def _dec_attn_kernel(pt_ref, q_ref, kvn_ref, keys_ref, knew_ref, thr_ref, bias_ref, ck_ref, cv_ref, o_ref,
                     kbuf, vbuf, sems, *, n_pages, page):
    b = pl.program_id(0)
    nb = pl.num_programs(0)
    slot = lax.rem(b, 2)
    past = n_pages * page

    def copies(bb, sl, fn):
        _page_copies(ck_ref, pt_ref, bb, kbuf.at[sl], sems.at[0, sl], n_pages, page, fn)
        _page_copies(cv_ref, pt_ref, bb, vbuf.at[sl], sems.at[1, sl], n_pages, page, fn)

    @pl.when(b == 0)
    def _first():
        copies(0, 0, lambda c: c.start())

    @pl.when(b + 1 < nb)
    def _prefetch_next():
        copies(b + 1, 1 - slot, lambda c: c.start())

    copies(b, slot, lambda c: c.wait())
    qm = q_ref[0]
    thr = thr_ref[0][:, 0:1]
    sel = keys_ref[0] >= thr
    sel_new = knew_ref[0][:, 0:1] >= thr
    bias = bias_ref[...]
    far, last, bnew = bias[:, LANES:LANES + 1], bias[:, 0:page], bias[:, LANES + 1:LANES + 2]
    lg = jnp.dot(qm.astype(BF16), kbuf[slot].astype(BF16), preferred_element_type=F32)
    lane = lax.broadcasted_iota(I32, lg.shape, 1)
    lastp = jnp.concatenate([jnp.zeros((N_HEADS, past - page), F32), last], axis=1)
    lg = jnp.where(sel, lg + jnp.where(lane >= past - page, lastp, far), -jnp.inf)
    kvn = kvn_ref[0]
    lgn = jnp.sum(qm * kvn[:, 0:LANES], axis=1, keepdims=True) + bnew
    lgn = jnp.where(sel_new, lgn, -jnp.inf)
    m = jnp.maximum(jnp.max(lg, axis=1, keepdims=True), lgn)
    e = jnp.exp(lg - m)
    en = jnp.exp(lgn - m)
    den = jnp.sum(e, axis=1, keepdims=True) + en
    pv = _dot_nt(e.astype(BF16), vbuf[slot].astype(BF16)) + en * kvn[:, LANES:2 * LANES]
    pv = pv / den
    lo = lax.broadcasted_iota(I32, (1, LANES), 1) < HEAD_DIM
    o_ref[0] = jnp.concatenate([jnp.where(lo, pv[2 * p:2 * p + 1], pv[2 * p + 1:2 * p + 2]) for p in range(4)],
                               axis=1).astype(BF16)


def _dec_attn(page_table, qm, kvn, keys, knew, thr, bias_dec, ck_t, cv_t, *, page):
    db, n_pages = page_table.shape
    past = n_pages * page
    row = lambda n: pl.BlockSpec((1, 1, n), lambda b, pt: (b, 0, 0))
    return pl.pallas_call(
        functools.partial(_dec_attn_kernel, n_pages=n_pages, page=page),
        grid_spec=pltpu.PrefetchScalarGridSpec(
            num_scalar_prefetch=1,
            grid=(db,),
            in_specs=[pl.BlockSpec((1, N_HEADS, LANES), lambda b, pt: (b, 0, 0)), row(2 * LANES), row(past),
                      row(LANES), row(LANES), pl.BlockSpec((N_HEADS, 2 * LANES), lambda b, pt: (0, 0)),
                      pl.BlockSpec(memory_space=pl.ANY), pl.BlockSpec(memory_space=pl.ANY)],
            out_specs=row(4 * LANES),
            scratch_shapes=[pltpu.VMEM((2, 2 * HEAD_DIM, past), F32), pltpu.VMEM((2, 2 * HEAD_DIM, past), F32),
                            pltpu.SemaphoreType.DMA((2, 2))]),
        out_shape=jax.ShapeDtypeStruct((db, 1, 4 * LANES), BF16),
        compiler_params=_cparams(1),
        name="dec_attn",
    )(page_table, qm, kvn, keys, knew, thr, bias_dec, ck_t, cv_t)


def _prep_in_ab(w):
    d = w.shape[0]
    nq, nkv = N_HEADS * HEAD_DIM, N_KV_HEADS * HEAD_DIM
    offs = np.cumsum([nq, nkv, nkv, IDX_HEADS * IDX_DIM, IDX_DIM, IDX_HEADS, 512])
    q, k, v, qi, ki, wi, g, xr = jnp.split(w, offs.tolist(), axis=1)
    q = q.reshape(d, N_HEADS, HEAD_DIM)[:, np.array(HEAD_PERM), :].reshape(d, nq)
    pad = jnp.zeros((d, _C_G - _C_IX - IDX_DIM - IDX_HEADS), w.dtype)
    return jnp.concatenate([q, k, v, qi, ki, wi, pad, g, xr], axis=1).astype(BF16)


def _block_diag(w):
    n, c, _ = w.shape
    return (jnp.eye(n, dtype=w.dtype)[:, None, :, None] * w[:, :, None, :]).reshape(n * c, n * c).astype(BF16)


def _ffn_params(layer, norm_ffn, w_up, cw, cb, w_down):
    return {"g": norm_ffn[layer][None], "wup": w_up[layer].astype(BF16), "cw": cw[layer], "cb": cb[layer][None],
            "wdn": w_down[layer].astype(BF16)}


def kernel(x_prompt, x_sample, cache_k, cache_v, cache_idx_k, state_rglru_h, state_rglru_conv, state_ffn_conv,
           page_table, norm_mix, norm_ffn, norm_final, rel_bias, w_in_ab, w_out_ab, rg_conv_w, rg_conv_b,
           rg_wa, rg_ba, rg_wx, rg_bx, rg_lambda, w_in_c, b_in_c, sgu_norm, sgu_w, sgu_b, w_out_c,
           ffn_w_up, ffn_conv_w, ffn_conv_b, ffn_w_down):
    batch, seq, d = x_prompt.shape
    db = x_sample.shape[0]
    page = cache_k.shape[2]
    d_a = N_HEADS * HEAD_DIM
    d_b = rg_conv_w.shape[-1]
    d_ff = ffn_w_down.shape[1]
    assert x_sample.shape[1] == 1 and seq % 512 == 0 and page == LANES and w_in_ab.shape[0] == 1

    w_in0 = _prep_in_ab(w_in_ab[0])
    wo = w_out_ab[0]
    wo_a = wo[:d_a].reshape(N_HEADS, HEAD_DIM, d)[np.array(HEAD_PERM)].reshape(d_a, d).astype(BF16)
    wo_b = wo[d_a:].astype(BF16)
    rg = {"cw": rg_conv_w[0], "cb": rg_conv_b[0][None], "wa": _block_diag(rg_wa[0]), "ba": rg_ba[0][None],
          "wx": _block_diag(rg_wx[0]), "bx": rg_bx[0][None], "lam": rg_lambda[0][None]}
    ffn0 = _ffn_params(0, norm_ffn, ffn_w_up, ffn_conv_w, ffn_conv_b, ffn_w_down)
    ffn1 = _ffn_params(1, norm_ffn, ffn_w_up, ffn_conv_w, ffn_conv_b, ffn_w_down)
    cp = {"g": norm_mix[1][None], "win": w_in_c[0].astype(BF16), "bin": b_in_c[0][None], "sn": sgu_norm[0][None],
          "sw": sgu_w[0], "sbt": sgu_b[0].T, "woc": w_out_c[0].astype(BF16),
          "sw0": jnp.repeat(sgu_w[0][:, 0, 0], d // sgu_w.shape[1])[None],
          "sb0": jnp.repeat(sgu_b[0][:, 0], d // sgu_w.shape[1])[None]}
    g_mix0 = norm_mix[0][None]
    g_final = norm_final[None]
    bias_st, bias_dec = _bias_tables(rel_bias, page)

    xp = x_prompt.reshape(batch * seq, d)
    q_st, qi_st, kv_p, ix_p, gate_p, xr_p = _inproj(xp, g_mix0, w_in0, stack=True, tm=512)
    attn_p = _attn_prompt(q_st, qi_st, ix_p, kv_p, bias_st, batch=batch, seq=seq)
    rg_p, h_p, cbuf_p = _rglru_prompt(gate_p, xr_p, jnp.zeros((batch, rg["cw"].shape[0] - 1, d_b), F32),
                                      jnp.zeros((batch, d_b), F32), rg, batch=batch, seq=seq)
    zero_fb = jnp.zeros((batch, 2, d_ff), F32)
    y1_p, fb0_p = _post_ab_prompt(x_prompt, attn_p, rg_p, zero_fb, wo_a, wo_b, ffn0)
    y_p, fb1_p = _layer_c_prompt(y1_p, zero_fb, cp, ffn1, g_final)

    xs = x_sample.reshape(db, d)
    qm_s, qi_s, kv_s, ix_s, gate_s, xr_s = _inproj(xs, g_mix0, w_in0, stack=False, tm=db)
    cik_t = jnp.transpose(cache_idx_k[0], (0, 2, 1))
    ck_t = jnp.transpose(cache_k[0], (0, 2, 3, 1)).reshape(-1, 2 * HEAD_DIM, page)
    cv_t = jnp.transpose(cache_v[0], (0, 2, 3, 1)).reshape(-1, 2 * HEAD_DIM, page)
    topk_s = min(TOPK_MAX, (page_table.shape[1] * page + 1) // 4)
    keys_s, knew_s = _dec_scores(page_table, qi_s.reshape(db, IDX_HEADS, IDX_DIM),
                                 ix_s[:, IDX_DIM:IDX_DIM + IDX_HEADS].reshape(db, IDX_HEADS, 1),
                                 ix_s.reshape(db, 1, LANES), cik_t, page=page)
    keys_s, knew_s, thr_s = _dec_select(keys_s.reshape(db, -1), knew_s.reshape(db, LANES), topk=topk_s)
    attn_s = _dec_attn(page_table, jnp.transpose(qm_s, (1, 0, 2)), kv_s.reshape(db, 1, 2 * LANES),
                       keys_s.reshape(db, 1, -1), knew_s.reshape(db, 1, LANES), thr_s.reshape(db, 1, LANES),
                       bias_dec, ck_t, cv_t, page=page).reshape(db, d_a)
    cbuf_s_in = state_rglru_conv[0]
    rg_s, h_s = _rglru_dec(gate_s, xr_s, jnp.transpose(cbuf_s_in, (1, 0, 2)), state_rglru_h[0], rg)
    y1_s, g0_s = _post_ab_dec(xs, attn_s, rg_s, jnp.transpose(state_ffn_conv[0], (1, 0, 2)), wo_a, wo_b, ffn0)
    y_s, g1_s, v_s = _layer_c_dec(y1_s, jnp.transpose(state_ffn_conv[1], (1, 0, 2)), cp, ffn1, g_final)

    kv4 = kv_p.reshape(batch, seq, 2, N_KV_HEADS, HEAD_DIM)
    kvs = kv_s.reshape(db, 1, 2, N_KV_HEADS, HEAD_DIM)
    fbuf_s = lambda layer, g: jnp.concatenate([state_ffn_conv[layer][:, 1:], g[:, None]], axis=1)
    return (y_p, y_s.reshape(db, 1, d),
            kv4[None, :, :, 0], kv4[None, :, :, 1], ix_p.reshape(batch, seq, LANES)[None, :, :, :IDX_DIM],
            kvs[None, :, :, 0], kvs[None, :, :, 1], ix_s.reshape(db, 1, LANES)[None, :, :, :IDX_DIM],
            h_p.reshape(batch, d_b)[None], cbuf_p[None],
            h_s[None], jnp.concatenate([cbuf_s_in[:, 1:], xr_s[:, None]], axis=1)[None],
            v_s.reshape(db, 1, -1)[None],
            jnp.stack([fb0_p, fb1_p]), jnp.stack([fbuf_s(0, g0_s), fbuf_s(1, g1_s)]))
```

```python
import functools
import math

import numpy as np
import jax
import jax.numpy as jnp
from jax import lax
from jax.experimental import pallas as pl
from jax.experimental.pallas import tpu as pltpu

F32 = jnp.float32
BF16 = jnp.bfloat16
I32 = jnp.int32

N_HEADS = 8
HEAD_DIM = 64
N_KV_HEADS = 2
Q_PER_KV = N_HEADS // N_KV_HEADS
IDX_HEADS = 8
IDX_DIM = 64
TOPK_MAX = 256
N_BUCKETS = 32
REL_MAX_EXACT = N_BUCKETS // 2
REL_MAX_DIST = 128
RG_C = 8.0
CHUNK = 128
EPS = 1e-6

LANES = 128
QB = 128
INT_MIN = -(2 ** 31)
HEAD_PERM = (0, 4, 1, 5, 2, 6, 3, 7)
VMEM_LIMIT = 56 * 1024 * 1024


def _cparams(n_grid):
    return pltpu.CompilerParams(dimension_semantics=("arbitrary",) * n_grid, vmem_limit_bytes=VMEM_LIMIT)


def _full_spec(shape):
    nd = len(shape)
    return pl.BlockSpec(shape, lambda *_: (0,) * nd, pipeline_mode=pl.Buffered(1))


def _rms(x, g):
    return x * lax.rsqrt(jnp.mean(x * x, axis=-1, keepdims=True) + EPS) * g


def _gelu(x):
    return x * (0.5 * (1.0 + jnp.tanh(math.sqrt(2.0 / math.pi) * (x + 0.044715 * (x * x * x)))))


def _sigmoid(x):
    return 1.0 / (1.0 + jnp.exp(-x))


def _softplus(x):
    return jnp.maximum(x, 0.0) + jnp.log(1.0 + jnp.exp(-jnp.abs(x)))


def _dot(a, b):
    return jnp.dot(a.astype(BF16), b, preferred_element_type=F32)


def _dot_nt(a, b):
    return lax.dot_general(a, b, (((1,), (1,)), ((), ())), preferred_element_type=F32)


def _float_key(x):
    bits = pltpu.bitcast(x, I32)
    key = jnp.where(bits < 0, bits ^ jnp.int32(0x7FFFFFFF), bits)
    return jnp.where(bits == jnp.int32(INT_MIN), jnp.int32(0), key)


def _t5_bucket_np(n):
    n = np.maximum(n, 0)
    nf = np.maximum(n, 1).astype(np.float32)
    large = REL_MAX_EXACT + (np.log(nf / np.float32(REL_MAX_EXACT)) / np.float32(math.log(REL_MAX_DIST / REL_MAX_EXACT))
                             * np.float32(N_BUCKETS - REL_MAX_EXACT)).astype(np.int32)
    large = np.minimum(large, N_BUCKETS - 1)
    return np.where(n < REL_MAX_EXACT, n, large).astype(np.int32)


_C_Q, _C_KV, _C_QI, _C_IX, _C_G, _C_X, _C_END = 0, 512, 768, 1280, 1408, 1920, 2432


def _inproj_kernel(x_ref, g_ref, w_ref, q_ref, qi_ref, kv_ref, ix_ref, gate_ref, xr_ref, *, stack):
    hn = _rms(x_ref[...], g_ref[...])
    z = _dot(hn, w_ref[...])
    q = z[:, _C_Q:_C_KV] * HEAD_DIM ** -0.5
    qi = z[:, _C_QI:_C_IX] * IDX_DIM ** -0.5
    kv_ref[...] = z[:, _C_KV:_C_QI]
    ix_ref[...] = z[:, _C_IX:_C_G]
    gate_ref[...] = z[:, _C_G:_C_X]
    xr_ref[...] = z[:, _C_X:_C_END]
    if stack:
        qb, qib = q.astype(BF16), qi.astype(BF16)
        for r in range(q.shape[0] // QB):
            for p in range(4):
                q_ref[r, p * QB:(p + 1) * QB, :] = qb[r * QB:(r + 1) * QB, p * LANES:(p + 1) * LANES]
                qi_ref[r, p * QB:(p + 1) * QB, :] = qib[r * QB:(r + 1) * QB, p * LANES:(p + 1) * LANES]
    else:
        lo = lax.broadcasted_iota(I32, (q.shape[0], LANES), 1) < HEAD_DIM
        for p in range(4):
            qp = q[:, p * LANES:(p + 1) * LANES]
            q_ref[2 * p] = jnp.where(lo, qp, 0.0)
            q_ref[2 * p + 1] = jnp.where(lo, 0.0, qp)
        qi_ref[...] = qi


def _inproj(x2d, g, w, *, stack, tm):
    m, d = x2d.shape
    if stack:
        q_shape, q_spec = (m // QB, 4 * QB, LANES), pl.BlockSpec((tm // QB, 4 * QB, LANES), lambda i: (i, 0, 0))
        qi_shape, qi_spec, qdt = q_shape, q_spec, BF16
    else:
        q_shape, q_spec = (N_HEADS, m, LANES), pl.BlockSpec((N_HEADS, tm, LANES), lambda i: (0, i, 0))
        qi_shape, qi_spec, qdt = (m, 512), pl.BlockSpec((tm, 512), lambda i: (i, 0)), F32
    row = lambda n: pl.BlockSpec((tm, n), lambda i: (i, 0))
    return pl.pallas_call(
        functools.partial(_inproj_kernel, stack=stack),
        grid=(m // tm,),
        in_specs=[row(d), _full_spec((1, d)), _full_spec(w.shape)],
        out_specs=[q_spec, qi_spec, row(256), row(128), row(512), row(512)],
        out_shape=[jax.ShapeDtypeStruct(q_shape, qdt), jax.ShapeDtypeStruct(qi_shape, qdt),
                   jax.ShapeDtypeStruct((m, 256), F32), jax.ShapeDtypeStruct((m, 128), F32),
                   jax.ShapeDtypeStruct((m, 512), F32), jax.ShapeDtypeStruct((m, 512), F32)],
        compiler_params=_cparams(1),
        name="inproj_stack" if stack else "inproj_dec",
    )(x2d, g, w)


def _bias_kernel(rb_ref, bk_ref, bkd_ref, o_ref, od_ref):
    for d in range(3):
        bk = bk_ref[d]
        for p in range(4):
            for a in range(2):
                h = p + 4 * a
                acc = jnp.zeros((QB, LANES), F32)
                for b in range(N_BUCKETS):
                    acc = jnp.where(bk == b, rb_ref[b, h], acc)
                o_ref[d, a * QB:(a + 1) * QB, p * LANES:(p + 1) * LANES] = acc
    bkd = bkd_ref[...]
    rowi = lax.broadcasted_iota(I32, (N_HEADS, 2 * LANES), 0)
    acc = jnp.zeros((N_HEADS, 2 * LANES), F32)
    for r in range(N_HEADS):
        h = r // 2 + 4 * (r % 2)
        for b in range(N_BUCKETS):
            acc = jnp.where((rowi == r) & (bkd == b), rb_ref[b, h], acc)
    od_ref[...] = acc


def _bias_tables(rel_bias, page):
    key = np.arange(QB)[:, None]
    qry = np.arange(LANES)[None, :]
    bk = np.stack([_t5_bucket_np(d * QB + qry - key) for d in range(3)])
    assert (_t5_bucket_np(np.arange(2 * QB + 1 - LANES, 4 * QB)) == N_BUCKETS - 1).all()
    assert (_t5_bucket_np(np.arange(page, 8 * page)) == N_BUCKETS - 1).all()
    dec = np.zeros((2 * LANES,), np.int64)
    dec[:page] = page - np.arange(page)
    dec[LANES] = 2 * REL_MAX_DIST
    dec[LANES + 1] = 0
    bkd = np.broadcast_to(_t5_bucket_np(dec)[None, :], (N_HEADS, 2 * LANES))
    return pl.pallas_call(
        _bias_kernel,
        in_specs=[pl.BlockSpec(memory_space=pltpu.SMEM), pl.BlockSpec(memory_space=pltpu.VMEM),
                  pl.BlockSpec(memory_space=pltpu.VMEM)],
        out_shape=[jax.ShapeDtypeStruct((3, 2 * QB, 4 * LANES), F32), jax.ShapeDtypeStruct((N_HEADS, 2 * LANES), F32)],
        name="bias_tables",
    )(rel_bias, jnp.asarray(bk, I32), jnp.asarray(bkd, I32))


def _search_widths(n_chunks):
    cuts = sorted({min(c, n_chunks) for c in (2, 4, 8, 12, 16)} | {n_chunks})
    return [c for c in cuts if c <= n_chunks]


def _attn_prompt_kernel(q_ref, qi_ref, ixq_ref, ixk_ref, kv_ref, bias_ref, o_ref,
                        kblk, vblk_t, kiblk, keys, logits, acc, thr_ref, cge_ref, *, n_chunks, topk):
    j = pl.program_id(1)
    lane = lax.broadcasted_iota(I32, (QB, LANES), 1)
    row = lax.broadcasted_iota(I32, (QB, LANES), 0)
    lo = lane < HEAD_DIM
    blocks = [(a, p) for a in range(2) for p in range(4)]
    rs = lambda a: slice(a * QB, (a + 1) * QB)
    cs = lambda p: slice(p * LANES, (p + 1) * LANES)
    chunk = lambda c: pl.ds(pl.multiple_of(c * QB, QB), QB)

    @pl.when(j == 0)
    def _build_block_diagonal_keys():
        def body(c, carry):
            kc = kv_ref[0, chunk(c), 0:LANES]
            vt = kv_ref[0, chunk(c), LANES:2 * LANES].T
            kia = jnp.where(lo, ixk_ref[0, chunk(c), :], 0.0)
            kblk[c, 0:QB, :] = jnp.where(lo, kc, 0.0).astype(BF16)
            kblk[c, QB:2 * QB, :] = jnp.where(lo, 0.0, kc).astype(BF16)
            vblk_t[c, :, 0:QB] = jnp.where(row < HEAD_DIM, vt, 0.0).astype(BF16)
            vblk_t[c, :, QB:2 * QB] = jnp.where(row < HEAD_DIM, 0.0, vt).astype(BF16)
            kiblk[c, 0:QB, :] = kia.astype(BF16)
            kiblk[c, QB:2 * QB, :] = pltpu.roll(kia, HEAD_DIM, 1).astype(BF16)
            return carry
        lax.fori_loop(0, n_chunks, body, 0)

    qi = qi_ref[0]
    wt = ixq_ref[0].T
    w_row = {(a, p): wt[IDX_DIM + 2 * p + a:IDX_DIM + 2 * p + a + 1, :] for a, p in blocks}
    qpos = j * QB + lane

    n_pairs = (j + 2) // 2

    def chunk_loop(body, carry):
        n_quads = n_pairs // 2
        carry = lax.fori_loop(0, n_quads, lambda i, cr: body([4 * i + u for u in range(4)], cr), carry)
        return lax.fori_loop(2 * n_quads, n_pairs, lambda i, cr: body([2 * i, 2 * i + 1], cr), carry)

    def score_body(cs_, carry):
        for c in cs_:
            s = _dot_nt(kiblk[c], qi)
            sc = jnp.zeros((QB, LANES), F32)
            for a, p in blocks:
                sc = sc + jnp.maximum(s[rs(a), cs(p)], 0.0) * w_row[(a, p)]
            key = _float_key(sc * IDX_HEADS ** -0.5)
            keys[chunk(c), :] = jnp.where(c * QB + row <= qpos, key, jnp.int32(INT_MIN))
        return carry
    chunk_loop(score_body, 0)

    def fill_body(c, carry):
        keys[chunk(c), :] = jnp.full((QB, LANES), INT_MIN, I32)
        return carry
    lax.fori_loop(2 * n_pairs, n_chunks, fill_body, 0)

    kf = jnp.float32(topk)

    def count(pred):
        ones = jnp.where(pred, 1.0, 0.0)
        part = ones.reshape(ones.shape[0] // 64, 64, LANES).sum(axis=0)
        return jnp.sum(part, axis=0, keepdims=True)

    def search(width):
        def search_body(i, ans):
            cand = ans | jnp.left_shift(jnp.int32(1), 31 - i)
            cnt = count(keys[0:width, :] >= (cand ^ jnp.int32(INT_MIN)))
            return jnp.where(cnt >= kf, cand, ans)
        ans = lax.fori_loop(0, 32, search_body, jnp.zeros((1, LANES), I32))
        t = jnp.maximum(ans ^ jnp.int32(INT_MIN), jnp.int32(INT_MIN + 1))
        thr_ref[...] = t
        cge_ref[...] = count(keys[0:width, :] >= t)

    prev = 0
    for n in _search_widths(n_chunks):
        pl.when((j >= prev) & (j < n))(functools.partial(search, n * QB))
        prev = n
    thr = thr_ref[...]

    @pl.when(jnp.max(cge_ref[...]) > kf)
    def _break_ties_by_position():
        kk = keys[...]
        s_len = kk.shape[0]
        need = kf - count(kk > thr)
        big = jnp.int32(2 * s_len)
        eqrow = jnp.where(kk == thr, lax.broadcasted_iota(I32, kk.shape, 0), big)
        nbits = int(math.log2(s_len))

        def tie_body(i, best):
            cand = best | jnp.left_shift(jnp.int32(1), nbits - 1 - i)
            return jnp.where(count(eqrow < cand) < need, cand, best)
        last = lax.fori_loop(0, nbits, tie_body, jnp.zeros((1, LANES), I32))
        keys[...] = jnp.where((eqrow > last) & (eqrow < big), thr - 1, kk)

    q = q_ref[0]

    def logits_body(cs_, mx):
        mx = list(mx)
        for c in cs_:
            lg = _dot_nt(kblk[c], q)
            bias = bias_ref[jnp.clip(j - c, 0, 2)]
            sel = keys[chunk(c), :] >= thr
            for n, (a, p) in enumerate(blocks):
                blk = jnp.where(sel, lg[rs(a), cs(p)] + bias[rs(a), cs(p)], -jnp.inf)
                logits[c, rs(a), cs(p)] = blk
                mx[n] = jnp.maximum(mx[n], jnp.max(blk, axis=0, keepdims=True))
        return tuple(mx)
    mx = chunk_loop(logits_body, tuple(jnp.full((1, LANES), -jnp.inf, F32) for _ in blocks))

    acc[...] = jnp.zeros(acc.shape, F32)

    def pv_body(cs_, ls):
        ls = list(ls)
        pv = jnp.zeros(acc.shape, F32)
        for c in cs_:
            rows = []
            for a in range(2):
                cols = []
                for p in range(4):
                    e = jnp.exp(logits[c, rs(a), cs(p)] - mx[a * 4 + p])
                    ls[a * 4 + p] = ls[a * 4 + p] + jnp.sum(e, axis=0, keepdims=True)
                    cols.append(e.astype(BF16))
                rows.append(jnp.concatenate(cols, axis=1))
            pmat = jnp.concatenate(rows, axis=0)
            pv = pv + jnp.dot(vblk_t[c], pmat, preferred_element_type=F32)
        acc[...] = acc[...] + pv
        return tuple(ls)
    ls = chunk_loop(pv_body, tuple(jnp.zeros((1, LANES), F32) for _ in blocks))

    for p in range(4):
        inv = jnp.where(row < HEAD_DIM, 1.0 / ls[p], 1.0 / ls[4 + p])
        o_ref[0, :, cs(p)] = (acc[:, cs(p)] * inv).T.astype(BF16)


def _attn_prompt(q_st, qi_st, ix, kv, bias_st, *, batch, seq):
    nq = seq // QB
    assert nq % 2 == 0
    topk = min(TOPK_MAX, seq // 4)
    ix3 = ix.reshape(batch, seq, LANES)
    kv3 = kv.reshape(batch, seq, 2 * LANES)
    return pl.pallas_call(
        functools.partial(_attn_prompt_kernel, n_chunks=nq, topk=topk),
        grid=(batch, nq),
        in_specs=[pl.BlockSpec((1, 4 * QB, LANES), lambda b, j: (b * nq + j, 0, 0)),
                  pl.BlockSpec((1, 4 * QB, LANES), lambda b, j: (b * nq + j, 0, 0)),
                  pl.BlockSpec((1, QB, LANES), lambda b, j: (b, j, 0)),
                  pl.BlockSpec((1, seq, LANES), lambda b, j: (b, 0, 0)),
                  pl.BlockSpec((1, seq, 2 * LANES), lambda b, j: (b, 0, 0)),
                  _full_spec(bias_st.shape)],
        out_specs=pl.BlockSpec((1, QB, 4 * LANES), lambda b, j: (b, j, 0)),
        out_shape=jax.ShapeDtypeStruct((batch, seq, 4 * LANES), BF16),
        scratch_shapes=[pltpu.VMEM((nq, 2 * QB, LANES), BF16), pltpu.VMEM((nq, LANES, 2 * QB), BF16),
                        pltpu.VMEM((nq, 2 * QB, LANES), BF16), pltpu.VMEM((seq, LANES), I32),
                        pltpu.VMEM((nq, 2 * QB, 4 * LANES), F32), pltpu.VMEM((LANES, 4 * LANES), F32),
                        pltpu.VMEM((1, LANES), I32), pltpu.VMEM((1, LANES), F32)],
        compiler_params=_cparams(2),
        name="attn_prompt",
    )(q_st, qi_st, ix3, ix3, kv3, bias_st)


def _rglru_gates(xc, wa, ba, wx, bx, lam):
    r = _sigmoid(_dot(xc, wa) + ba)
    i = _sigmoid(_dot(xc, wx) + bx)
    log_a = -RG_C * r * _softplus(-lam)
    a = jnp.exp(log_a)
    u = jnp.sqrt(1.0 - jnp.exp(2.0 * log_a)) * (i * xc)
    return a, u


def _rglru_prompt_kernel(g_ref, xr_ref, buf_ref, h0_ref, cw_ref, cb_ref, wa_ref, ba_ref, wx_ref, bx_ref, lam_ref,
                         o_ref, hl_ref, nb_ref, xs, a_s, u_s, tail, hc, *, tc):
    t = pl.program_id(0)
    width = cw_ref.shape[0]
    nb, _, d = g_ref.shape

    @pl.when(t == 0)
    def _load_state():
        tail[...] = jnp.zeros(tail.shape, F32)
        tail[:, 8 - (width - 1):8, :] = buf_ref[...]
        hc[...] = h0_ref[...]

    xs[:, 0:8, :] = tail[...]
    xs[:, 8:8 + tc, :] = xr_ref[...]
    tail[...] = xs[:, tc:tc + 8, :]
    xc = cb_ref[...]
    for jj in range(width):
        off = 8 - (width - 1) + jj
        xc = xc + cw_ref[jj:jj + 1, :] * xs[:, off:off + tc, :]
    a, u = _rglru_gates(xc.reshape(nb * tc, d), wa_ref[...], ba_ref[...], wx_ref[...], bx_ref[...], lam_ref[...])
    n_lb = d // LANES
    pitch = tc + 8
    for k in range(n_lb):
        for b in range(nb):
            a_s[k, b * pitch:b * pitch + tc, :] = a[b * tc:(b + 1) * tc, k * LANES:(k + 1) * LANES]
            u_s[k, b * pitch:b * pitch + tc, :] = u[b * tc:(b + 1) * tc, k * LANES:(k + 1) * LANES]

    def scan_body(i, hs):
        rows = pl.ds(i, nb, stride=pitch)
        out = []
        for k in range(n_lb):
            h = a_s[k, rows, :] * hs[k] + u_s[k, rows, :]
            u_s[k, rows, :] = h
            out.append(h)
        return tuple(out)
    h0 = hc[...]
    hs = lax.fori_loop(0, tc, scan_body, tuple(h0[:, k * LANES:(k + 1) * LANES] for k in range(n_lb)), unroll=8)
    h = jnp.concatenate(hs, axis=1)
    hc[...] = h
    hseq = jnp.concatenate([jnp.concatenate([u_s[k, b * pitch:b * pitch + tc, :] for b in range(nb)], axis=0)
                            for k in range(n_lb)], axis=1)
    o_ref[...] = (_gelu(g_ref[...].reshape(nb * tc, d)) * hseq).reshape(nb, tc, d).astype(BF16)
    hl_ref[...] = h
    nb_ref[...] = xs[:, tc + 8 - (width - 1):tc + 8, :]


def _rglru_prompt(gate, xr, buf, h0, rg, *, batch, seq, tc=256):
    d = gate.shape[-1]
    width = rg["cw"].shape[0]
    g3, x3 = gate.reshape(batch, seq, d), xr.reshape(batch, seq, d)
    blk = pl.BlockSpec((batch, tc, d), lambda t: (0, t, 0))
    vec = _full_spec((1, d))
    return pl.pallas_call(
        functools.partial(_rglru_prompt_kernel, tc=tc),
        grid=(seq // tc,),
        in_specs=[blk, blk, _full_spec((batch, width - 1, d)), _full_spec((batch, d)), _full_spec((width, d)), vec,
                  _full_spec((d, d)), vec, _full_spec((d, d)), vec, vec],
        out_specs=[blk, pl.BlockSpec((batch, d), lambda t: (0, 0)), pl.BlockSpec((batch, width - 1, d), lambda t: (0, 0, 0))],
        out_shape=[jax.ShapeDtypeStruct((batch, seq, d), BF16), jax.ShapeDtypeStruct((batch, d), F32),
                   jax.ShapeDtypeStruct((batch, width - 1, d), F32)],
        scratch_shapes=[pltpu.VMEM((batch, tc + 8, d), F32), pltpu.VMEM((d // LANES, batch * (tc + 8), LANES), F32),
                        pltpu.VMEM((d // LANES, batch * (tc + 8), LANES), F32), pltpu.VMEM((batch, 8, d), F32),
                        pltpu.VMEM((batch, d), F32)],
        compiler_params=_cparams(1),
        name="rglru_prompt",
    )(g3, x3, buf, h0, rg["cw"], rg["cb"], rg["wa"], rg["ba"], rg["wx"], rg["bx"], rg["lam"])


def _rglru_dec_kernel(g_ref, xr_ref, buf_ref, h0_ref, cw_ref, cb_ref, wa_ref, ba_ref, wx_ref, bx_ref, lam_ref,
                      o_ref, hl_ref):
    width = cw_ref.shape[0]
    xc = cb_ref[...]
    for jj in range(width - 1):
        xc = xc + cw_ref[jj:jj + 1, :] * buf_ref[jj]
    xc = xc + cw_ref[width - 1:width, :] * xr_ref[...]
    a, u = _rglru_gates(xc, wa_ref[...], ba_ref[...], wx_ref[...], bx_ref[...], lam_ref[...])
    h = a * h0_ref[...] + u
    hl_ref[...] = h
    o_ref[...] = (_gelu(g_ref[...]) * h).astype(BF16)


def _rglru_dec(gate, xr, buf_t, h0, rg):
    m, d = gate.shape
    return pl.pallas_call(
        _rglru_dec_kernel,
        out_shape=[jax.ShapeDtypeStruct((m, d), BF16), jax.ShapeDtypeStruct((m, d), F32)],
        name="rglru_dec",
    )(gate, xr, buf_t, h0, rg["cw"], rg["cb"], rg["wa"], rg["ba"], rg["wx"], rg["bx"], rg["lam"])


def _ffn_tile(y1, gf_ref, wup_ref, cw_ref, cb_ref, wdn_ref, conv_prev, n_split):
    d_ff = wdn_ref.shape[0]
    cf = d_ff // n_split
    hn = _rms(y1, gf_ref[...]).astype(BF16)
    out = jnp.zeros(y1.shape, F32)
    gates = []
    for k in range(n_split):
        c0 = k * cf
        g = jnp.dot(hn, wup_ref[:, c0:c0 + cf], preferred_element_type=F32)
        u = jnp.dot(hn, wup_ref[:, d_ff + c0:d_ff + c0 + cf], preferred_element_type=F32)
        g1, g2 = conv_prev(k, g)
        gc = cb_ref[:, c0:c0 + cf] + cw_ref[0:1, c0:c0 + cf] * g2 + cw_ref[1:2, c0:c0 + cf] * g1 \
            + cw_ref[2:3, c0:c0 + cf] * g
        act = (_gelu(gc) * u).astype(BF16)
        out = out + jnp.dot(act, wdn_ref[c0:c0 + cf, :], preferred_element_type=F32)
        gates.append(g)
    return out, gates


def _prompt_conv_prev(gs, carry, fb_ref, nb_ref, tm, cf):
    t = pl.program_id(1)

    @pl.when(t == 0)
    def _load_state():
        carry[...] = jnp.zeros(carry.shape, F32)
        for k in range(carry.shape[0]):
            carry[k, 6:8, :] = fb_ref[0, :, k * cf:(k + 1) * cf]

    def conv_prev(k, g):
        gs[0:8, :] = carry[k]
        gs[8:8 + tm, :] = g
        carry[k] = g[tm - 8:tm, :]
        nb_ref[0, :, k * cf:(k + 1) * cf] = g[tm - 2:tm, :]
        return gs[7:7 + tm, :], gs[6:6 + tm, :]
    return conv_prev


def _mix_ab_tile(y_ref, a_ref, r_ref, woa_ref, wob_ref):
    return y_ref[0] + jnp.dot(a_ref[0], woa_ref[...], preferred_element_type=F32) \
        + jnp.dot(r_ref[0], wob_ref[...], preferred_element_type=F32)


def _post_ab_prompt_kernel(y_ref, a_ref, r_ref, fb_ref, woa_ref, wob_ref, gf_ref, wup_ref, cw_ref, cb_ref, wdn_ref,
                           o_ref, nb_ref, gs, carry, *, tm, n_split):
    y1 = _mix_ab_tile(y_ref, a_ref, r_ref, woa_ref, wob_ref)
    cf = wdn_ref.shape[0] // n_split
    out, _ = _ffn_tile(y1, gf_ref, wup_ref, cw_ref, cb_ref, wdn_ref,
                       _prompt_conv_prev(gs, carry, fb_ref, nb_ref, tm, cf), n_split)
    o_ref[0] = y1 + out


def _ffn_specs(d, d_ff):
    return [_full_spec((1, d)), _full_spec((d, 2 * d_ff)), _full_spec((3, d_ff)), _full_spec((1, d_ff)),
            _full_spec((d_ff, d))]


def _post_ab_prompt(y, attn, rgo, fbuf, wo_a, wo_b, ffn, *, tm=512, n_split=2):
    batch, seq, d = y.shape
    d_ff = ffn["wdn"].shape[0]
    cf = d_ff // n_split
    blk = lambda n: pl.BlockSpec((1, tm, n), lambda b, t: (b, t, 0))
    fb = pl.BlockSpec((1, 2, d_ff), lambda b, t: (b, 0, 0))
    return pl.pallas_call(
        functools.partial(_post_ab_prompt_kernel, tm=tm, n_split=n_split),
        grid=(batch, seq // tm),
        in_specs=[blk(d), blk(attn.shape[-1]), blk(rgo.shape[-1]), fb, _full_spec(wo_a.shape), _full_spec(wo_b.shape)]
        + _ffn_specs(d, d_ff),
        out_specs=[blk(d), fb],
        out_shape=[jax.ShapeDtypeStruct((batch, seq, d), F32), jax.ShapeDtypeStruct((batch, 2, d_ff), F32)],
        scratch_shapes=[pltpu.VMEM((tm + 8, cf), F32), pltpu.VMEM((n_split, 8, cf), F32)],
        compiler_params=_cparams(2),
        name="post_ab_prompt",
    )(y, attn, rgo, fbuf, wo_a, wo_b, ffn["g"], ffn["wup"], ffn["cw"], ffn["cb"], ffn["wdn"])


def _dec_conv_prev(fb_ref, cf):
    def conv_prev(k, g):
        return fb_ref[1, :, k * cf:(k + 1) * cf], fb_ref[0, :, k * cf:(k + 1) * cf]
    return conv_prev


def _post_ab_dec_kernel(y_ref, a_ref, r_ref, fb_ref, woa_ref, wob_ref, gf_ref, wup_ref, cw_ref, cb_ref, wdn_ref,
                        o_ref, g_ref, *, n_split):
    y1 = y_ref[...] + jnp.dot(a_ref[...], woa_ref[...], preferred_element_type=F32) \
        + jnp.dot(r_ref[...], wob_ref[...], preferred_element_type=F32)
    cf = wdn_ref.shape[0] // n_split
    out, gates = _ffn_tile(y1, gf_ref, wup_ref, cw_ref, cb_ref, wdn_ref, _dec_conv_prev(fb_ref, cf), n_split)
    o_ref[...] = y1 + out
    for k, g in enumerate(gates):
        g_ref[:, k * cf:(k + 1) * cf] = g


def _post_ab_dec(y, attn, rgo, fbuf_t, wo_a, wo_b, ffn, *, n_split=2):
    m, d = y.shape
    d_ff = ffn["wdn"].shape[0]
    return pl.pallas_call(
        functools.partial(_post_ab_dec_kernel, n_split=n_split),
        out_shape=[jax.ShapeDtypeStruct((m, d), F32), jax.ShapeDtypeStruct((m, d_ff), F32)],
        compiler_params=pltpu.CompilerParams(vmem_limit_bytes=VMEM_LIMIT),
        name="post_ab_dec",
    )(y, attn, rgo, fbuf_t, wo_a, wo_b, ffn["g"], ffn["wup"], ffn["cw"], ffn["cb"], ffn["wdn"])


def _gmlp_in(y, gm_ref, win_ref, bin_ref, sn_ref):
    d_c = win_ref.shape[1] // 2
    z = _gelu(_dot(_rms(y, gm_ref[...]), win_ref[...]) + bin_ref[...])
    return z[:, :d_c], _rms(z[:, d_c:], sn_ref[...])


def _layer_c_prompt_kernel(y_ref, fb_ref, gm_ref, win_ref, bin_ref, sn_ref, sw_ref, sbt_ref, woc_ref,
                           gf_ref, wup_ref, cw_ref, cb_ref, wdn_ref, gfin_ref,
                           o_ref, nb_ref, gs, carry, *, tm, n_split):
    y = y_ref[0]
    u, v = _gmlp_in(y, gm_ref, win_ref, bin_ref, sn_ref)
    vb = v.astype(BF16)
    n_groups = sw_ref.shape[0]
    tril = lax.broadcasted_iota(I32, (CHUNK, CHUNK), 0) >= lax.broadcasted_iota(I32, (CHUNK, CHUNK), 1)
    wm = [jnp.where(tril, sw_ref[gi], 0.0).astype(BF16) for gi in range(n_groups)]
    rows = []
    for r in range(tm // CHUNK):
        cols = []
        for gi in range(n_groups):
            mixed = jnp.dot(wm[gi], vb[r * CHUNK:(r + 1) * CHUNK, gi * LANES:(gi + 1) * LANES],
                            preferred_element_type=F32)
            cols.append(mixed + sbt_ref[:, gi:gi + 1])
        rows.append(jnp.concatenate(cols, axis=1))
    gated = u * jnp.concatenate(rows, axis=0)
    y1 = y + _dot(gated, woc_ref[...])
    cf = wdn_ref.shape[0] // n_split
    out, _ = _ffn_tile(y1, gf_ref, wup_ref, cw_ref, cb_ref, wdn_ref,
                       _prompt_conv_prev(gs, carry, fb_ref, nb_ref, tm, cf), n_split)
    o_ref[0] = _rms(y1 + out, gfin_ref[...])


def _layer_c_prompt(y, fbuf, cp, ffn, g_final, *, tm=512, n_split=2):
    batch, seq, d = y.shape
    d_ff = ffn["wdn"].shape[0]
    cf = d_ff // n_split
    blk = pl.BlockSpec((1, tm, d), lambda b, t: (b, t, 0))
    fb = pl.BlockSpec((1, 2, d_ff), lambda b, t: (b, 0, 0))
    consts = [cp["g"], cp["win"], cp["bin"], cp["sn"], cp["sw"], cp["sbt"], cp["woc"],
              ffn["g"], ffn["wup"], ffn["cw"], ffn["cb"], ffn["wdn"], g_final]
    return pl.pallas_call(
        functools.partial(_layer_c_prompt_kernel, tm=tm, n_split=n_split),
        grid=(batch, seq // tm),
        in_specs=[blk, fb] + [_full_spec(c.shape) for c in consts],
        out_specs=[blk, fb],
        out_shape=[jax.ShapeDtypeStruct((batch, seq, d), F32), jax.ShapeDtypeStruct((batch, 2, d_ff), F32)],
        scratch_shapes=[pltpu.VMEM((tm + 8, cf), F32), pltpu.VMEM((n_split, 8, cf), F32)],
        compiler_params=_cparams(2),
        name="layer_c_prompt",
    )(y, fbuf, *consts)


def _layer_c_dec_kernel(y_ref, fb_ref, gm_ref, win_ref, bin_ref, sn_ref, sw0_ref, sb0_ref, woc_ref,
                        gf_ref, wup_ref, cw_ref, cb_ref, wdn_ref, gfin_ref, o_ref, g_ref, v_ref, *, n_split):
    y = y_ref[...]
    u, v = _gmlp_in(y, gm_ref, win_ref, bin_ref, sn_ref)
    v_ref[...] = v
    y1 = y + _dot(u * (sw0_ref[...] * v + sb0_ref[...]), woc_ref[...])
    cf = wdn_ref.shape[0] // n_split
    out, gates = _ffn_tile(y1, gf_ref, wup_ref, cw_ref, cb_ref, wdn_ref, _dec_conv_prev(fb_ref, cf), n_split)
    o_ref[...] = _rms(y1 + out, gfin_ref[...])
    for k, g in enumerate(gates):
        g_ref[:, k * cf:(k + 1) * cf] = g


def _layer_c_dec(y, fbuf_t, cp, ffn, g_final, *, n_split=2):
    m, d = y.shape
    d_ff = ffn["wdn"].shape[0]
    d_c = cp["woc"].shape[0]
    return pl.pallas_call(
        functools.partial(_layer_c_dec_kernel, n_split=n_split),
        out_shape=[jax.ShapeDtypeStruct((m, d), F32), jax.ShapeDtypeStruct((m, d_ff), F32),
                   jax.ShapeDtypeStruct((m, d_c), F32)],
        compiler_params=pltpu.CompilerParams(vmem_limit_bytes=VMEM_LIMIT),
        name="layer_c_dec",
    )(y, fbuf_t, cp["g"], cp["win"], cp["bin"], cp["sn"], cp["sw0"], cp["sb0"], cp["woc"],
      ffn["g"], ffn["wup"], ffn["cw"], ffn["cb"], ffn["wdn"], g_final)


def _start_page_copies(src_ref, pt_ref, b, dst_ref, sem, n_pages, page):
    def body(pg, carry):
        col = pl.multiple_of(pg * page, page)
        pltpu.make_async_copy(src_ref.at[pt_ref[b, pg]], dst_ref.at[:, pl.ds(col, page)], sem).start()
        return carry
    lax.fori_loop(0, n_pages, body, 0, unroll=8)


def _wait_page_copies(dst_ref, sem):
    pltpu.make_async_copy(dst_ref, dst_ref, sem).wait()


def _dec_score_kernel(pt_ref, qi_ref, wi_ref, ixn_ref, cik_ref, keys_ref, knew_ref, ibuf, sems, *, n_pages, page):
    b = pl.program_id(0)
    nb = pl.num_programs(0)
    slot = lax.rem(b, 2)

    def start(bb, sl):
        _start_page_copies(cik_ref, pt_ref, bb, ibuf.at[sl], sems.at[sl], n_pages, page)

    @pl.when(b == 0)
    def _first():
        start(0, 0)

    @pl.when(b + 1 < nb)
    def _prefetch_next():
        start(b + 1, 1 - slot)

    _wait_page_copies(ibuf.at[slot], sems.at[slot])
    qi = qi_ref[0].astype(BF16)
    wi = wi_ref[0]
    s = jnp.dot(qi, ibuf[slot].astype(BF16), preferred_element_type=F32)
    sc = jnp.sum(jnp.maximum(s, 0.0) * wi, axis=0, keepdims=True) * IDX_HEADS ** -0.5
    keys_ref[0] = _float_key(sc)
    kin = ixn_ref[0][:, 0:IDX_DIM]
    sn = jnp.sum(qi_ref[0] * kin, axis=1, keepdims=True)
    scn = jnp.sum(jnp.maximum(sn, 0.0) * wi, axis=0, keepdims=True) * IDX_HEADS ** -0.5
    knew_ref[0] = jnp.broadcast_to(_float_key(scn), (1, LANES))


def _dec_scores(page_table, qi3, wi3, ix3, cik_t, *, page):
    db, n_pages = page_table.shape
    past = n_pages * page
    return pl.pallas_call(
        functools.partial(_dec_score_kernel, n_pages=n_pages, page=page),
        grid_spec=pltpu.PrefetchScalarGridSpec(
            num_scalar_prefetch=1,
            grid=(db,),
            in_specs=[pl.BlockSpec((1, IDX_HEADS, IDX_DIM), lambda b, pt: (b, 0, 0)),
                      pl.BlockSpec((1, IDX_HEADS, 1), lambda b, pt: (b, 0, 0)),
                      pl.BlockSpec((1, 1, LANES), lambda b, pt: (b, 0, 0)),
                      pl.BlockSpec(memory_space=pl.ANY)],
            out_specs=[pl.BlockSpec((1, 1, past), lambda b, pt: (b, 0, 0)),
                       pl.BlockSpec((1, 1, LANES), lambda b, pt: (b, 0, 0))],
            scratch_shapes=[pltpu.VMEM((2, IDX_DIM, past), F32), pltpu.SemaphoreType.DMA((2,))]),
        out_shape=[jax.ShapeDtypeStruct((db, 1, past), I32), jax.ShapeDtypeStruct((db, 1, LANES), I32)],
        compiler_params=_cparams(1),
        name="dec_scores",
    )(page_table, qi3, wi3, ix3, cik_t)


def _dec_select_kernel(keys_ref, knew_ref, ko_ref, kno_ref, thr_ref, *, topk):
    kk = keys_ref[...]
    kn = knew_ref[...][:, 0:1]
    kf = jnp.float32(topk)

    def count(pred, pred_new):
        return jnp.sum(jnp.where(pred, 1.0, 0.0), axis=1, keepdims=True) + jnp.where(pred_new, 1.0, 0.0)

    def search_body(i, ans):
        cand = ans | jnp.left_shift(jnp.int32(1), 31 - i)
        cs = cand ^ jnp.int32(INT_MIN)
        return jnp.where(count(kk >= cs, kn >= cs) >= kf, cand, ans)
    ans = lax.fori_loop(0, 32, search_body, jnp.zeros((kk.shape[0], 1), I32))
    thr = jnp.maximum(ans ^ jnp.int32(INT_MIN), jnp.int32(INT_MIN + 1))
    need = kf - count(kk > thr, kn > thr)
    past = kk.shape[1]
    big = jnp.int32(4 * past)
    eqcol = jnp.where(kk == thr, lax.broadcasted_iota(I32, kk.shape, 1), big)
    eqnew = jnp.where(kn == thr, jnp.int32(past), big)
    nbits = int(math.log2(past)) + 1

    def tie_body(i, best):
        cand = best | jnp.left_shift(jnp.int32(1), nbits - 1 - i)
        return jnp.where(count(eqcol < cand, eqnew < cand) < need, cand, best)
    last = lax.fori_loop(0, nbits, tie_body, jnp.zeros((kk.shape[0], 1), I32))
    ko_ref[...] = jnp.where((eqcol > last) & (eqcol < big), thr - 1, kk)
    kno_ref[...] = jnp.broadcast_to(jnp.where((eqnew > last) & (eqnew < big), thr - 1, kn), kno_ref.shape)
    thr_ref[...] = jnp.broadcast_to(thr, thr_ref.shape)


def _dec_select(keys, knew, *, topk):
    db, past = keys.shape
    return pl.pallas_call(
        functools.partial(_dec_select_kernel, topk=topk),
        out_shape=[jax.ShapeDtypeStruct((db, past), I32), jax.ShapeDtypeStruct((db, LANES), I32),
                   jax.ShapeDtypeStruct((db, LANES), I32)],
        name="dec_select",
    )(keys, knew)


def _dec_attn_kernel(pt_ref, q_ref, kvn_ref, keys_ref, knew_ref, thr_ref, bias_ref, ck_ref, cv_ref, o_ref,
                     kbuf, vbuf, sems, *, n_pages, page):
    b = pl.program_id(0)
    nb = pl.num_programs(0)
    slot = lax.rem(b, 2)
    past = n_pages * page

    def start(bb, sl):
        _start_page_copies(ck_ref, pt_ref, bb, kbuf.at[sl], sems.at[0, sl], n_pages, page)
        _start_page_copies(cv_ref, pt_ref, bb, vbuf.at[sl], sems.at[1, sl], n_pages, page)

    @pl.when(b == 0)
    def _first():
        start(0, 0)

    @pl.when(b + 1 < nb)
    def _prefetch_next():
        start(b + 1, 1 - slot)

    _wait_page_copies(kbuf.at[slot], sems.at[0, slot])
    _wait_page_copies(vbuf.at[slot], sems.at[1, slot])
    qm = q_ref[0]
    thr = thr_ref[0][:, 0:1]
    sel = keys_ref[0] >= thr
    sel_new = knew_ref[0][:, 0:1] >= thr
    bias = bias_ref[...]
    far, last, bnew = bias[:, LANES:LANES + 1], bias[:, 0:page], bias[:, LANES + 1:LANES + 2]
    lg = jnp.dot(qm.astype(BF16), kbuf[slot].astype(BF16), preferred_element_type=F32)
    lane = lax.broadcasted_iota(I32, lg.shape, 1)
    lastp = jnp.concatenate([jnp.zeros((N_HEADS, past - page), F32), last], axis=1)
    lg = jnp.where(sel, lg + jnp.where(lane >= past - page, lastp, far), -jnp.inf)
    kvn = kvn_ref[0]
    lgn = jnp.sum(qm * kvn[:, 0:LANES], axis=1, keepdims=True) + bnew
    lgn = jnp.where(sel_new, lgn, -jnp.inf)
    m = jnp.maximum(jnp.max(lg, axis=1, keepdims=True), lgn)
    e = jnp.exp(lg - m)
    en = jnp.exp(lgn - m)
    den = jnp.sum(e, axis=1, keepdims=True) + en
    pv = _dot_nt(e.astype(BF16), vbuf[slot].astype(BF16)) + en * kvn[:, LANES:2 * LANES]
    pv = pv / den
    lo = lax.broadcasted_iota(I32, (1, LANES), 1) < HEAD_DIM
    o_ref[0] = jnp.concatenate([jnp.where(lo, pv[2 * p:2 * p + 1], pv[2 * p + 1:2 * p + 2]) for p in range(4)],
                               axis=1).astype(BF16)


def _dec_attn(page_table, qm, kvn, keys, knew, thr, bias_dec, ck_t, cv_t, *, page):
    db, n_pages = page_table.shape
    past = n_pages * page
    row = lambda n: pl.BlockSpec((1, 1, n), lambda b, pt: (b, 0, 0))
    return pl.pallas_call(
        functools.partial(_dec_attn_kernel, n_pages=n_pages, page=page),
        grid_spec=pltpu.PrefetchScalarGridSpec(
            num_scalar_prefetch=1,
            grid=(db,),
            in_specs=[pl.BlockSpec((1, N_HEADS, LANES), lambda b, pt: (b, 0, 0)), row(2 * LANES), row(past),
                      row(LANES), row(LANES), pl.BlockSpec((N_HEADS, 2 * LANES), lambda b, pt: (0, 0)),
                      pl.BlockSpec(memory_space=pl.ANY), pl.BlockSpec(memory_space=pl.ANY)],
            out_specs=row(4 * LANES),
            scratch_shapes=[pltpu.VMEM((2, 2 * HEAD_DIM, past), F32), pltpu.VMEM((2, 2 * HEAD_DIM, past), F32),
                            pltpu.SemaphoreType.DMA((2, 2))]),
        out_shape=jax.ShapeDtypeStruct((db, 1, 4 * LANES), BF16),
        compiler_params=_cparams(1),
        name="dec_attn",
    )(page_table, qm, kvn, keys, knew, thr, bias_dec, ck_t, cv_t)


def _prep_in_ab(w):
    d = w.shape[0]
    nq, nkv = N_HEADS * HEAD_DIM, N_KV_HEADS * HEAD_DIM
    offs = np.cumsum([nq, nkv, nkv, IDX_HEADS * IDX_DIM, IDX_DIM, IDX_HEADS, 512])
    q, k, v, qi, ki, wi, g, xr = jnp.split(w, offs.tolist(), axis=1)
    q = q.reshape(d, N_HEADS, HEAD_DIM)[:, np.array(HEAD_PERM), :].reshape(d, nq)
    pad = jnp.zeros((d, _C_G - _C_IX - IDX_DIM - IDX_HEADS), w.dtype)
    return jnp.concatenate([q, k, v, qi, ki, wi, pad, g, xr], axis=1).astype(BF16)


def _block_diag(w):
    n, c, _ = w.shape
    return (jnp.eye(n, dtype=w.dtype)[:, None, :, None] * w[:, :, None, :]).reshape(n * c, n * c).astype(BF16)


def _ffn_params(layer, norm_ffn, w_up, cw, cb, w_down):
    return {"g": norm_ffn[layer][None], "wup": w_up[layer].astype(BF16), "cw": cw[layer], "cb": cb[layer][None],
            "wdn": w_down[layer].astype(BF16)}


def kernel(x_prompt, x_sample, cache_k, cache_v, cache_idx_k, state_rglru_h, state_rglru_conv, state_ffn_conv,
           page_table, norm_mix, norm_ffn, norm_final, rel_bias, w_in_ab, w_out_ab, rg_conv_w, rg_conv_b,
           rg_wa, rg_ba, rg_wx, rg_bx, rg_lambda, w_in_c, b_in_c, sgu_norm, sgu_w, sgu_b, w_out_c,
           ffn_w_up, ffn_conv_w, ffn_conv_b, ffn_w_down):
    batch, seq, d = x_prompt.shape
    db = x_sample.shape[0]
    page = cache_k.shape[2]
    d_a = N_HEADS * HEAD_DIM
    d_b = rg_conv_w.shape[-1]
    d_ff = ffn_w_down.shape[1]
    assert x_sample.shape[1] == 1 and seq % 512 == 0 and page == LANES and w_in_ab.shape[0] == 1

    w_in0 = _prep_in_ab(w_in_ab[0])
    wo = w_out_ab[0]
    wo_a = wo[:d_a].reshape(N_HEADS, HEAD_DIM, d)[np.array(HEAD_PERM)].reshape(d_a, d).astype(BF16)
    wo_b = wo[d_a:].astype(BF16)
    rg = {"cw": rg_conv_w[0], "cb": rg_conv_b[0][None], "wa": _block_diag(rg_wa[0]), "ba": rg_ba[0][None],
          "wx": _block_diag(rg_wx[0]), "bx": rg_bx[0][None], "lam": rg_lambda[0][None]}
    ffn0 = _ffn_params(0, norm_ffn, ffn_w_up, ffn_conv_w, ffn_conv_b, ffn_w_down)
    ffn1 = _ffn_params(1, norm_ffn, ffn_w_up, ffn_conv_w, ffn_conv_b, ffn_w_down)
    cp = {"g": norm_mix[1][None], "win": w_in_c[0].astype(BF16), "bin": b_in_c[0][None], "sn": sgu_norm[0][None],
          "sw": sgu_w[0], "sbt": sgu_b[0].T, "woc": w_out_c[0].astype(BF16),
          "sw0": jnp.repeat(sgu_w[0][:, 0, 0], d // sgu_w.shape[1])[None],
          "sb0": jnp.repeat(sgu_b[0][:, 0], d // sgu_w.shape[1])[None]}
    g_mix0 = norm_mix[0][None]
    g_final = norm_final[None]
    bias_st, bias_dec = _bias_tables(rel_bias, page)

    xp = x_prompt.reshape(batch * seq, d)
    q_st, qi_st, kv_p, ix_p, gate_p, xr_p = _inproj(xp, g_mix0, w_in0, stack=True, tm=512)
    attn_p = _attn_prompt(q_st, qi_st, ix_p, kv_p, bias_st, batch=batch, seq=seq)
    rg_p, h_p, cbuf_p = _rglru_prompt(gate_p, xr_p, jnp.zeros((batch, rg["cw"].shape[0] - 1, d_b), F32),
                                      jnp.zeros((batch, d_b), F32), rg, batch=batch, seq=seq)
    zero_fb = jnp.zeros((batch, 2, d_ff), F32)
    y1_p, fb0_p = _post_ab_prompt(x_prompt, attn_p, rg_p, zero_fb, wo_a, wo_b, ffn0)
    y_p, fb1_p = _layer_c_prompt(y1_p, zero_fb, cp, ffn1, g_final)

    xs = x_sample.reshape(db, d)
    qm_s, qi_s, kv_s, ix_s, gate_s, xr_s = _inproj(xs, g_mix0, w_in0, stack=False, tm=db)
    cik_t = jnp.transpose(cache_idx_k[0], (0, 2, 1))
    ck_t = jnp.transpose(cache_k[0], (0, 2, 3, 1)).reshape(-1, 2 * HEAD_DIM, page)
    cv_t = jnp.transpose(cache_v[0], (0, 2, 3, 1)).reshape(-1, 2 * HEAD_DIM, page)
    topk_s = min(TOPK_MAX, (page_table.shape[1] * page + 1) // 4)
    keys_s, knew_s = _dec_scores(page_table, qi_s.reshape(db, IDX_HEADS, IDX_DIM),
                                 ix_s[:, IDX_DIM:IDX_DIM + IDX_HEADS].reshape(db, IDX_HEADS, 1),
                                 ix_s.reshape(db, 1, LANES), cik_t, page=page)
    keys_s, knew_s, thr_s = _dec_select(keys_s.reshape(db, -1), knew_s.reshape(db, LANES), topk=topk_s)
    attn_s = _dec_attn(page_table, jnp.transpose(qm_s, (1, 0, 2)), kv_s.reshape(db, 1, 2 * LANES),
                       keys_s.reshape(db, 1, -1), knew_s.reshape(db, 1, LANES), thr_s.reshape(db, 1, LANES),
                       bias_dec, ck_t, cv_t, page=page).reshape(db, d_a)
    cbuf_s_in = state_rglru_conv[0]
    rg_s, h_s = _rglru_dec(gate_s, xr_s, jnp.transpose(cbuf_s_in, (1, 0, 2)), state_rglru_h[0], rg)
    y1_s, g0_s = _post_ab_dec(xs, attn_s, rg_s, jnp.transpose(state_ffn_conv[0], (1, 0, 2)), wo_a, wo_b, ffn0)
    y_s, g1_s, v_s = _layer_c_dec(y1_s, jnp.transpose(state_ffn_conv[1], (1, 0, 2)), cp, ffn1, g_final)

    kv4 = kv_p.reshape(batch, seq, 2, N_KV_HEADS, HEAD_DIM)
    kvs = kv_s.reshape(db, 1, 2, N_KV_HEADS, HEAD_DIM)
    fbuf_s = lambda layer, g: jnp.concatenate([state_ffn_conv[layer][:, 1:], g[:, None]], axis=1)
    return (y_p, y_s.reshape(db, 1, d),
            kv4[None, :, :, 0], kv4[None, :, :, 1], ix_p.reshape(batch, seq, LANES)[None, :, :, :IDX_DIM],
            kvs[None, :, :, 0], kvs[None, :, :, 1], ix_s.reshape(db, 1, LANES)[None, :, :, :IDX_DIM],
            h_p.reshape(batch, d_b)[None], cbuf_p[None],
            h_s[None], jnp.concatenate([cbuf_s_in[:, 1:], xr_s[:, None]], axis=1)[None],
            v_s.reshape(db, 1, -1)[None],
            jnp.stack([fb0_p, fb1_p]), jnp.stack([fbuf_s(0, g0_s), fbuf_s(1, g1_s)]))
```

```python
import functools
import math

import numpy as np
import jax
import jax.numpy as jnp
from jax import lax
from jax.experimental import pallas as pl
from jax.experimental.pallas import tpu as pltpu

F32 = jnp.float32
BF16 = jnp.bfloat16
I32 = jnp.int32
I16 = jnp.int16

N_HEADS = 8
HEAD_DIM = 64
N_KV_HEADS = 2
Q_PER_KV = N_HEADS // N_KV_HEADS
IDX_HEADS = 8
IDX_DIM = 64
TOPK_MAX = 256
N_BUCKETS = 32
REL_MAX_EXACT = N_BUCKETS // 2
REL_MAX_DIST = 128
RG_C = 8.0
CHUNK = 128
EPS = 1e-6

LANES = 128
QB = 128
INT_MIN = -(2 ** 31)
I16_MIN = -(2 ** 15)
HEAD_PERM = (0, 4, 1, 5, 2, 6, 3, 7)
VMEM_LIMIT = 56 * 1024 * 1024


def _cparams(n_grid):
    return pltpu.CompilerParams(dimension_semantics=("arbitrary",) * n_grid, vmem_limit_bytes=VMEM_LIMIT)


def _full_spec(shape):
    nd = len(shape)
    return pl.BlockSpec(shape, lambda *_: (0,) * nd, pipeline_mode=pl.Buffered(1))


def _rms(x, g):
    return x * lax.rsqrt(jnp.mean(x * x, axis=-1, keepdims=True) + EPS) * g


def _gelu(x):
    return x * (0.5 * (1.0 + jnp.tanh(math.sqrt(2.0 / math.pi) * (x + 0.044715 * (x * x * x)))))


def _sigmoid(x):
    return 1.0 / (1.0 + jnp.exp(-x))


def _softplus(x):
    return jnp.maximum(x, 0.0) + jnp.log(1.0 + jnp.exp(-jnp.abs(x)))


def _dot(a, b):
    return jnp.dot(a.astype(BF16), b, preferred_element_type=F32)


def _dot_nt(a, b):
    return lax.dot_general(a, b, (((1,), (1,)), ((), ())), preferred_element_type=F32)


def _float_key(x):
    bits = pltpu.bitcast(x, I32)
    key = jnp.where(bits < 0, bits ^ jnp.int32(0x7FFFFFFF), bits)
    return jnp.where(bits == jnp.int32(INT_MIN), jnp.int32(0), key)


def _t5_bucket_np(n):
    n = np.maximum(n, 0)
    nf = np.maximum(n, 1).astype(np.float32)
    large = REL_MAX_EXACT + (np.log(nf / np.float32(REL_MAX_EXACT)) / np.float32(math.log(REL_MAX_DIST / REL_MAX_EXACT))
                             * np.float32(N_BUCKETS - REL_MAX_EXACT)).astype(np.int32)
    large = np.minimum(large, N_BUCKETS - 1)
    return np.where(n < REL_MAX_EXACT, n, large).astype(np.int32)


_C_Q, _C_KV, _C_QI, _C_IX, _C_G, _C_X, _C_END = 0, 512, 768, 1280, 1408, 1920, 2432


def _inproj_kernel(x_ref, g_ref, w_ref, q_ref, qi_ref, kv_ref, ix_ref, gate_ref, xr_ref, *, stack):
    hn = _rms(x_ref[...], g_ref[...])
    z = _dot(hn, w_ref[...])
    q = z[:, _C_Q:_C_KV] * HEAD_DIM ** -0.5
    qi = z[:, _C_QI:_C_IX] * IDX_DIM ** -0.5
    kv_ref[...] = z[:, _C_KV:_C_QI]
    ix_ref[...] = z[:, _C_IX:_C_G]
    gate_ref[...] = z[:, _C_G:_C_X]
    xr_ref[...] = z[:, _C_X:_C_END]
    if stack:
        qb, qib = q.astype(BF16), qi.astype(BF16)
        for r in range(q.shape[0] // QB):
            for p in range(4):
                q_ref[r, p * QB:(p + 1) * QB, :] = qb[r * QB:(r + 1) * QB, p * LANES:(p + 1) * LANES]
                qi_ref[r, p * QB:(p + 1) * QB, :] = qib[r * QB:(r + 1) * QB, p * LANES:(p + 1) * LANES]
    else:
        lo = lax.broadcasted_iota(I32, (q.shape[0], LANES), 1) < HEAD_DIM
        for p in range(4):
            qp = q[:, p * LANES:(p + 1) * LANES]
            q_ref[2 * p] = jnp.where(lo, qp, 0.0)
            q_ref[2 * p + 1] = jnp.where(lo, 0.0, qp)
        qi_ref[...] = qi


def _inproj(x2d, g, w, *, stack, tm):
    m, d = x2d.shape
    if stack:
        q_shape, q_spec = (m // QB, 4 * QB, LANES), pl.BlockSpec((tm // QB, 4 * QB, LANES), lambda i: (i, 0, 0))
        qi_shape, qi_spec, qdt = q_shape, q_spec, BF16
    else:
        q_shape, q_spec = (N_HEADS, m, LANES), pl.BlockSpec((N_HEADS, tm, LANES), lambda i: (0, i, 0))
        qi_shape, qi_spec, qdt = (m, 512), pl.BlockSpec((tm, 512), lambda i: (i, 0)), F32
    row = lambda n: pl.BlockSpec((tm, n), lambda i: (i, 0))
    return pl.pallas_call(
        functools.partial(_inproj_kernel, stack=stack),
        grid=(m // tm,),
        in_specs=[row(d), _full_spec((1, d)), _full_spec(w.shape)],
        out_specs=[q_spec, qi_spec, row(256), row(128), row(512), row(512)],
        out_shape=[jax.ShapeDtypeStruct(q_shape, qdt), jax.ShapeDtypeStruct(qi_shape, qdt),
                   jax.ShapeDtypeStruct((m, 256), F32), jax.ShapeDtypeStruct((m, 128), F32),
                   jax.ShapeDtypeStruct((m, 512), F32), jax.ShapeDtypeStruct((m, 512), F32)],
        compiler_params=_cparams(1),
        name="inproj_stack" if stack else "inproj_dec",
    )(x2d, g, w)


def _bias_kernel(rb_ref, bk_ref, bkd_ref, o_ref, od_ref):
    for d in range(3):
        bk = bk_ref[d]
        for p in range(4):
            for a in range(2):
                h = p + 4 * a
                acc = jnp.zeros((QB, LANES), F32)
                for b in range(N_BUCKETS):
                    acc = jnp.where(bk == b, rb_ref[b, h], acc)
                o_ref[d, a * QB:(a + 1) * QB, p * LANES:(p + 1) * LANES] = acc
    bkd = bkd_ref[...]
    rowi = lax.broadcasted_iota(I32, (N_HEADS, 2 * LANES), 0)
    acc = jnp.zeros((N_HEADS, 2 * LANES), F32)
    for r in range(N_HEADS):
        h = r // 2 + 4 * (r % 2)
        for b in range(N_BUCKETS):
            acc = jnp.where((rowi == r) & (bkd == b), rb_ref[b, h], acc)
    od_ref[...] = acc


def _bias_tables(rel_bias, page):
    key = np.arange(QB)[:, None]
    qry = np.arange(LANES)[None, :]
    bk = np.stack([_t5_bucket_np(d * QB + qry - key) for d in range(3)])
    assert (_t5_bucket_np(np.arange(2 * QB + 1 - LANES, 4 * QB)) == N_BUCKETS - 1).all()
    assert (_t5_bucket_np(np.arange(page, 8 * page)) == N_BUCKETS - 1).all()
    dec = np.zeros((2 * LANES,), np.int64)
    dec[:page] = page - np.arange(page)
    dec[LANES] = 2 * REL_MAX_DIST
    dec[LANES + 1] = 0
    bkd = np.broadcast_to(_t5_bucket_np(dec)[None, :], (N_HEADS, 2 * LANES))
    return pl.pallas_call(
        _bias_kernel,
        in_specs=[pl.BlockSpec(memory_space=pltpu.SMEM), pl.BlockSpec(memory_space=pltpu.VMEM),
                  pl.BlockSpec(memory_space=pltpu.VMEM)],
        out_shape=[jax.ShapeDtypeStruct((3, 2 * QB, 4 * LANES), F32), jax.ShapeDtypeStruct((N_HEADS, 2 * LANES), F32)],
        name="bias_tables",
    )(rel_bias, jnp.asarray(bk, I32), jnp.asarray(bkd, I32))


def _search_widths(n_chunks):
    cuts = sorted({min(c, n_chunks) for c in (2, 4, 8, 12, 16)} | {n_chunks})
    return [c for c in cuts if c <= n_chunks]


def _attn_prompt_kernel(q_ref, qi_ref, ixq_ref, ixk_ref, kv_ref, bias_ref, o_ref,
                        kblk, vblk_t, kiblk, keys, k_hi, k_lo, k_band, logits, acc, thr_ref, cge_ref, *, n_chunks, topk):
    j = pl.program_id(1)
    lane = lax.broadcasted_iota(I32, (QB, LANES), 1)
    row = lax.broadcasted_iota(I32, (QB, LANES), 0)
    lo = lane < HEAD_DIM
    blocks = [(a, p) for a in range(2) for p in range(4)]
    rs = lambda a: slice(a * QB, (a + 1) * QB)
    cs = lambda p: slice(p * LANES, (p + 1) * LANES)
    chunk = lambda c: pl.ds(pl.multiple_of(c * QB, QB), QB)

    @pl.when(j == 0)
    def _build_block_diagonal_keys():
        def body(c, carry):
            kc = kv_ref[0, chunk(c), 0:LANES]
            vt = kv_ref[0, chunk(c), LANES:2 * LANES].T
            kia = jnp.where(lo, ixk_ref[0, chunk(c), :], 0.0)
            kblk[c, 0:QB, :] = jnp.where(lo, kc, 0.0).astype(BF16)
            kblk[c, QB:2 * QB, :] = jnp.where(lo, 0.0, kc).astype(BF16)
            vblk_t[c, :, 0:QB] = jnp.where(row < HEAD_DIM, vt, 0.0).astype(BF16)
            vblk_t[c, :, QB:2 * QB] = jnp.where(row < HEAD_DIM, 0.0, vt).astype(BF16)
            kiblk[c, 0:QB, :] = kia.astype(BF16)
            kiblk[c, QB:2 * QB, :] = pltpu.roll(kia, HEAD_DIM, 1).astype(BF16)
            return carry
        lax.fori_loop(0, n_chunks, body, 0)

    qi = qi_ref[0]
    wt = ixq_ref[0].T
    w_row = {(a, p): wt[IDX_DIM + 2 * p + a:IDX_DIM + 2 * p + a + 1, :] for a, p in blocks}
    qpos = j * QB + lane

    n_pairs = (j + 2) // 2

    def chunk_loop(body, carry):
        n_quads = n_pairs // 2
        carry = lax.fori_loop(0, n_quads, lambda i, cr: body([4 * i + u for u in range(4)], cr), carry)
        return lax.fori_loop(2 * n_quads, n_pairs, lambda i, cr: body([2 * i, 2 * i + 1], cr), carry)

    def score_body(cs_, carry):
        for c in cs_:
            s = _dot_nt(kiblk[c], qi)
            sc = jnp.zeros((QB, LANES), F32)
            for a, p in blocks:
                sc = sc + jnp.maximum(s[rs(a), cs(p)], 0.0) * w_row[(a, p)]
            key = _float_key(sc * IDX_HEADS ** -0.5)
            key = jnp.where(c * QB + row <= qpos, key, jnp.int32(INT_MIN))
            keys[chunk(c), :] = key
            k_hi[chunk(c), :] = lax.shift_right_arithmetic(key, 16).astype(I16)
            k_lo[chunk(c), :] = ((key & 0xFFFF) + I16_MIN).astype(I16)
        return carry
    chunk_loop(score_body, 0)

    def fill_body(c, carry):
        keys[chunk(c), :] = jnp.full((QB, LANES), INT_MIN, I32)
        k_hi[chunk(c), :] = jnp.full((QB, LANES), I16_MIN, I16)
        k_lo[chunk(c), :] = jnp.full((QB, LANES), I16_MIN, I16)
        return carry
    lax.fori_loop(2 * n_pairs, n_chunks, fill_body, 0)

    kf = jnp.float32(topk)

    def count(pred):
        ones = jnp.where(pred, 1.0, 0.0)
        part = ones.reshape(ones.shape[0] // 64, 64, LANES).sum(axis=0)
        return jnp.sum(part, axis=0, keepdims=True)

    def count16(pred):
        ones = jnp.where(pred, jnp.bfloat16(1), jnp.bfloat16(0))
        groups = [ones[g * QB:(g + 1) * QB] for g in range(ones.shape[0] // QB)]
        return jnp.sum(functools.reduce(lambda x, y: x + y, groups).astype(F32), axis=0, keepdims=True)

    def search(width):
        assert width // QB <= 256

        def half_search(src, target):
            def body(i, ans):
                cand = ans | jnp.left_shift(jnp.int32(1), 15 - i)
                cnt = count16(src[0:width, :] >= (cand + I16_MIN).astype(I16))
                return jnp.where(cnt >= target, cand, ans)
            return lax.fori_loop(0, 16, body, jnp.zeros((1, LANES), I32))

        hi_u = half_search(k_hi, kf)
        hi16 = (hi_u + I16_MIN).astype(I16)
        kh = k_hi[0:width, :]
        above = count16(kh > hi16)
        k_band[0:width, :] = jnp.where(kh == hi16, k_lo[0:width, :], jnp.int16(I16_MIN))
        lo_u = half_search(k_band, kf - above)
        t = jnp.maximum(jnp.left_shift(hi_u + I16_MIN, 16) | lo_u, jnp.int32(INT_MIN + 1))
        thr_ref[...] = t
        cge_ref[...] = count(keys[0:width, :] >= t)

    prev = 0
    for n in _search_widths(n_chunks):
        pl.when((j >= prev) & (j < n))(functools.partial(search, n * QB))
        prev = n
    thr = thr_ref[...]

    @pl.when(jnp.max(cge_ref[...]) > kf)
    def _break_ties_by_position():
        kk = keys[...]
        s_len = kk.shape[0]
        need = kf - count(kk > thr)
        big = jnp.int32(2 * s_len)
        eqrow = jnp.where(kk == thr, lax.broadcasted_iota(I32, kk.shape, 0), big)
        nbits = int(math.log2(s_len))

        def tie_body(i, best):
            cand = best | jnp.left_shift(jnp.int32(1), nbits - 1 - i)
            return jnp.where(count(eqrow < cand) < need, cand, best)
        last = lax.fori_loop(0, nbits, tie_body, jnp.zeros((1, LANES), I32))
        keys[...] = jnp.where((eqrow > last) & (eqrow < big), thr - 1, kk)

    q = q_ref[0]

    def logits_body(cs_, mx):
        mx = list(mx)
        for c in cs_:
            lg = _dot_nt(kblk[c], q)
            bias = bias_ref[jnp.clip(j - c, 0, 2)]
            sel = keys[chunk(c), :] >= thr
            for n, (a, p) in enumerate(blocks):
                blk = jnp.where(sel, lg[rs(a), cs(p)] + bias[rs(a), cs(p)], -jnp.inf)
                logits[c, rs(a), cs(p)] = blk
                mx[n] = jnp.maximum(mx[n], jnp.max(blk, axis=0, keepdims=True))
        return tuple(mx)
    mx = chunk_loop(logits_body, tuple(jnp.full((1, LANES), -jnp.inf, F32) for _ in blocks))

    acc[...] = jnp.zeros(acc.shape, F32)

    def pv_body(cs_, ls):
        ls = list(ls)
        pv = jnp.zeros(acc.shape, F32)
        for c in cs_:
            rows = []
            for a in range(2):
                cols = []
                for p in range(4):
                    e = jnp.exp(logits[c, rs(a), cs(p)] - mx[a * 4 + p])
                    ls[a * 4 + p] = ls[a * 4 + p] + jnp.sum(e, axis=0, keepdims=True)
                    cols.append(e.astype(BF16))
                rows.append(jnp.concatenate(cols, axis=1))
            pmat = jnp.concatenate(rows, axis=0)
            pv = pv + jnp.dot(vblk_t[c], pmat, preferred_element_type=F32)
        acc[...] = acc[...] + pv
        return tuple(ls)
    ls = chunk_loop(pv_body, tuple(jnp.zeros((1, LANES), F32) for _ in blocks))

    for p in range(4):
        inv = jnp.where(row < HEAD_DIM, 1.0 / ls[p], 1.0 / ls[4 + p])
        o_ref[0, :, cs(p)] = (acc[:, cs(p)] * inv).T.astype(BF16)


def _attn_prompt(q_st, qi_st, ix, kv, bias_st, *, batch, seq):
    nq = seq // QB
    assert nq % 2 == 0
    topk = min(TOPK_MAX, seq // 4)
    ix3 = ix.reshape(batch, seq, LANES)
    kv3 = kv.reshape(batch, seq, 2 * LANES)
    return pl.pallas_call(
        functools.partial(_attn_prompt_kernel, n_chunks=nq, topk=topk),
        grid=(batch, nq),
        in_specs=[pl.BlockSpec((1, 4 * QB, LANES), lambda b, j: (b * nq + j, 0, 0)),
                  pl.BlockSpec((1, 4 * QB, LANES), lambda b, j: (b * nq + j, 0, 0)),
                  pl.BlockSpec((1, QB, LANES), lambda b, j: (b, j, 0)),
                  pl.BlockSpec((1, seq, LANES), lambda b, j: (b, 0, 0)),
                  pl.BlockSpec((1, seq, 2 * LANES), lambda b, j: (b, 0, 0)),
                  _full_spec(bias_st.shape)],
        out_specs=pl.BlockSpec((1, QB, 4 * LANES), lambda b, j: (b, j, 0)),
        out_shape=jax.ShapeDtypeStruct((batch, seq, 4 * LANES), BF16),
        scratch_shapes=[pltpu.VMEM((nq, 2 * QB, LANES), BF16), pltpu.VMEM((nq, LANES, 2 * QB), BF16),
                        pltpu.VMEM((nq, 2 * QB, LANES), BF16), pltpu.VMEM((seq, LANES), I32),
                        pltpu.VMEM((seq, LANES), I16), pltpu.VMEM((seq, LANES), I16), pltpu.VMEM((seq, LANES), I16),
                        pltpu.VMEM((nq, 2 * QB, 4 * LANES), F32), pltpu.VMEM((LANES, 4 * LANES), F32),
                        pltpu.VMEM((1, LANES), I32), pltpu.VMEM((1, LANES), F32)],
        compiler_params=_cparams(2),
        name="attn_prompt",
    )(q_st, qi_st, ix3, ix3, kv3, bias_st)


def _rglru_gates(xc, wa, ba, wx, bx, lam):
    r = _sigmoid(_dot(xc, wa) + ba)
    i = _sigmoid(_dot(xc, wx) + bx)
    log_a = -RG_C * r * _softplus(-lam)
    a = jnp.exp(log_a)
    u = jnp.sqrt(1.0 - jnp.exp(2.0 * log_a)) * (i * xc)
    return a, u


def _rglru_prompt_kernel(g_ref, xr_ref, buf_ref, h0_ref, cw_ref, cb_ref, wa_ref, ba_ref, wx_ref, bx_ref, lam_ref,
                         o_ref, hl_ref, nb_ref, xs, a_s, u_s, tail, hc, *, tc):
    t = pl.program_id(0)
    width = cw_ref.shape[0]
    nb, _, d = g_ref.shape

    @pl.when(t == 0)
    def _load_state():
        tail[...] = jnp.zeros(tail.shape, F32)
        tail[:, 8 - (width - 1):8, :] = buf_ref[...]
        hc[...] = h0_ref[...]

    xs[:, 0:8, :] = tail[...]
    xs[:, 8:8 + tc, :] = xr_ref[...]
    tail[...] = xs[:, tc:tc + 8, :]
    xc = cb_ref[...]
    for jj in range(width):
        off = 8 - (width - 1) + jj
        xc = xc + cw_ref[jj:jj + 1, :] * xs[:, off:off + tc, :]
    a, u = _rglru_gates(xc.reshape(nb * tc, d), wa_ref[...], ba_ref[...], wx_ref[...], bx_ref[...], lam_ref[...])
    n_lb = d // LANES
    pitch = tc + 8
    for k in range(n_lb):
        for b in range(nb):
            a_s[k, b * pitch:b * pitch + tc, :] = a[b * tc:(b + 1) * tc, k * LANES:(k + 1) * LANES]
            u_s[k, b * pitch:b * pitch + tc, :] = u[b * tc:(b + 1) * tc, k * LANES:(k + 1) * LANES]

    def scan_body(i, hs):
        rows = pl.ds(i, nb, stride=pitch)
        out = []
        for k in range(n_lb):
            h = a_s[k, rows, :] * hs[k] + u_s[k, rows, :]
            u_s[k, rows, :] = h
            out.append(h)
        return tuple(out)
    h0 = hc[...]
    hs = lax.fori_loop(0, tc, scan_body, tuple(h0[:, k * LANES:(k + 1) * LANES] for k in range(n_lb)), unroll=8)
    h = jnp.concatenate(hs, axis=1)
    hc[...] = h
    hseq = jnp.concatenate([jnp.concatenate([u_s[k, b * pitch:b * pitch + tc, :] for b in range(nb)], axis=0)
                            for k in range(n_lb)], axis=1)
    o_ref[...] = (_gelu(g_ref[...].reshape(nb * tc, d)) * hseq).reshape(nb, tc, d).astype(BF16)
    hl_ref[...] = h
    nb_ref[...] = xs[:, tc + 8 - (width - 1):tc + 8, :]


def _rglru_prompt(gate, xr, buf, h0, rg, *, batch, seq, tc=256):
    d = gate.shape[-1]
    width = rg["cw"].shape[0]
    g3, x3 = gate.reshape(batch, seq, d), xr.reshape(batch, seq, d)
    blk = pl.BlockSpec((batch, tc, d), lambda t: (0, t, 0))
    vec = _full_spec((1, d))
    return pl.pallas_call(
        functools.partial(_rglru_prompt_kernel, tc=tc),
        grid=(seq // tc,),
        in_specs=[blk, blk, _full_spec((batch, width - 1, d)), _full_spec((batch, d)), _full_spec((width, d)), vec,
                  _full_spec((d, d)), vec, _full_spec((d, d)), vec, vec],
        out_specs=[blk, pl.BlockSpec((batch, d), lambda t: (0, 0)), pl.BlockSpec((batch, width - 1, d), lambda t: (0, 0, 0))],
        out_shape=[jax.ShapeDtypeStruct((batch, seq, d), BF16), jax.ShapeDtypeStruct((batch, d), F32),
                   jax.ShapeDtypeStruct((batch, width - 1, d), F32)],
        scratch_shapes=[pltpu.VMEM((batch, tc + 8, d), F32), pltpu.VMEM((d // LANES, batch * (tc + 8), LANES), F32),
                        pltpu.VMEM((d // LANES, batch * (tc + 8), LANES), F32), pltpu.VMEM((batch, 8, d), F32),
                        pltpu.VMEM((batch, d), F32)],
        compiler_params=_cparams(1),
        name="rglru_prompt",
    )(g3, x3, buf, h0, rg["cw"], rg["cb"], rg["wa"], rg["ba"], rg["wx"], rg["bx"], rg["lam"])


def _rglru_dec_kernel(g_ref, xr_ref, buf_ref, h0_ref, cw_ref, cb_ref, wa_ref, ba_ref, wx_ref, bx_ref, lam_ref,
                      o_ref, hl_ref):
    width = cw_ref.shape[0]
    xc = cb_ref[...]
    for jj in range(width - 1):
        xc = xc + cw_ref[jj:jj + 1, :] * buf_ref[jj]
    xc = xc + cw_ref[width - 1:width, :] * xr_ref[...]
    a, u = _rglru_gates(xc, wa_ref[...], ba_ref[...], wx_ref[...], bx_ref[...], lam_ref[...])
    h = a * h0_ref[...] + u
    hl_ref[...] = h
    o_ref[...] = (_gelu(g_ref[...]) * h).astype(BF16)


def _rglru_dec(gate, xr, buf_t, h0, rg):
    m, d = gate.shape
    return pl.pallas_call(
        _rglru_dec_kernel,
        out_shape=[jax.ShapeDtypeStruct((m, d), BF16), jax.ShapeDtypeStruct((m, d), F32)],
        name="rglru_dec",
    )(gate, xr, buf_t, h0, rg["cw"], rg["cb"], rg["wa"], rg["ba"], rg["wx"], rg["bx"], rg["lam"])


def _ffn_tile(y1, gf_ref, wup_ref, cw_ref, cb_ref, wdn_ref, conv_prev, n_split):
    d_ff = wdn_ref.shape[0]
    cf = d_ff // n_split
    hn = _rms(y1, gf_ref[...]).astype(BF16)
    out = jnp.zeros(y1.shape, F32)
    gates = []
    for k in range(n_split):
        c0 = k * cf
        g = jnp.dot(hn, wup_ref[:, c0:c0 + cf], preferred_element_type=F32)
        u = jnp.dot(hn, wup_ref[:, d_ff + c0:d_ff + c0 + cf], preferred_element_type=F32)
        g1, g2 = conv_prev(k, g)
        gc = cb_ref[:, c0:c0 + cf] + cw_ref[0:1, c0:c0 + cf] * g2 + cw_ref[1:2, c0:c0 + cf] * g1 \
            + cw_ref[2:3, c0:c0 + cf] * g
        act = (_gelu(gc) * u).astype(BF16)
        out = out + jnp.dot(act, wdn_ref[c0:c0 + cf, :], preferred_element_type=F32)
        gates.append(g)
    return out, gates


def _prompt_conv_prev(gs, carry, fb_ref, nb_ref, tm, cf):
    t = pl.program_id(1)

    @pl.when(t == 0)
    def _load_state():
        carry[...] = jnp.zeros(carry.shape, F32)
        for k in range(carry.shape[0]):
            carry[k, 6:8, :] = fb_ref[0, :, k * cf:(k + 1) * cf]

    def conv_prev(k, g):
        gs[0:8, :] = carry[k]
        gs[8:8 + tm, :] = g
        carry[k] = g[tm - 8:tm, :]
        nb_ref[0, :, k * cf:(k + 1) * cf] = g[tm - 2:tm, :]
        return gs[7:7 + tm, :], gs[6:6 + tm, :]
    return conv_prev


def _mix_ab_tile(y_ref, a_ref, r_ref, woa_ref, wob_ref):
    return y_ref[0] + jnp.dot(a_ref[0], woa_ref[...], preferred_element_type=F32) \
        + jnp.dot(r_ref[0], wob_ref[...], preferred_element_type=F32)


def _post_ab_prompt_kernel(y_ref, a_ref, r_ref, fb_ref, woa_ref, wob_ref, gf_ref, wup_ref, cw_ref, cb_ref, wdn_ref,
                           o_ref, nb_ref, gs, carry, *, tm, n_split):
    y1 = _mix_ab_tile(y_ref, a_ref, r_ref, woa_ref, wob_ref)
    cf = wdn_ref.shape[0] // n_split
    out, _ = _ffn_tile(y1, gf_ref, wup_ref, cw_ref, cb_ref, wdn_ref,
                       _prompt_conv_prev(gs, carry, fb_ref, nb_ref, tm, cf), n_split)
    o_ref[0] = y1 + out


def _ffn_specs(d, d_ff):
    return [_full_spec((1, d)), _full_spec((d, 2 * d_ff)), _full_spec((3, d_ff)), _full_spec((1, d_ff)),
            _full_spec((d_ff, d))]


def _post_ab_prompt(y, attn, rgo, fbuf, wo_a, wo_b, ffn, *, tm=512, n_split=2):
    batch, seq, d = y.shape
    d_ff = ffn["wdn"].shape[0]
    cf = d_ff // n_split
    blk = lambda n: pl.BlockSpec((1, tm, n), lambda b, t: (b, t, 0))
    fb = pl.BlockSpec((1, 2, d_ff), lambda b, t: (b, 0, 0))
    return pl.pallas_call(
        functools.partial(_post_ab_prompt_kernel, tm=tm, n_split=n_split),
        grid=(batch, seq // tm),
        in_specs=[blk(d), blk(attn.shape[-1]), blk(rgo.shape[-1]), fb, _full_spec(wo_a.shape), _full_spec(wo_b.shape)]
        + _ffn_specs(d, d_ff),
        out_specs=[blk(d), fb],
        out_shape=[jax.ShapeDtypeStruct((batch, seq, d), F32), jax.ShapeDtypeStruct((batch, 2, d_ff), F32)],
        scratch_shapes=[pltpu.VMEM((tm + 8, cf), F32), pltpu.VMEM((n_split, 8, cf), F32)],
        compiler_params=_cparams(2),
        name="post_ab_prompt",
    )(y, attn, rgo, fbuf, wo_a, wo_b, ffn["g"], ffn["wup"], ffn["cw"], ffn["cb"], ffn["wdn"])


def _dec_conv_prev(fb_ref, cf):
    def conv_prev(k, g):
        return fb_ref[1, :, k * cf:(k + 1) * cf], fb_ref[0, :, k * cf:(k + 1) * cf]
    return conv_prev


def _post_ab_dec_kernel(y_ref, a_ref, r_ref, fb_ref, woa_ref, wob_ref, gf_ref, wup_ref, cw_ref, cb_ref, wdn_ref,
                        o_ref, g_ref, *, n_split):
    y1 = y_ref[...] + jnp.dot(a_ref[...], woa_ref[...], preferred_element_type=F32) \
        + jnp.dot(r_ref[...], wob_ref[...], preferred_element_type=F32)
    cf = wdn_ref.shape[0] // n_split
    out, gates = _ffn_tile(y1, gf_ref, wup_ref, cw_ref, cb_ref, wdn_ref, _dec_conv_prev(fb_ref, cf), n_split)
    o_ref[...] = y1 + out
    for k, g in enumerate(gates):
        g_ref[:, k * cf:(k + 1) * cf] = g


def _post_ab_dec(y, attn, rgo, fbuf_t, wo_a, wo_b, ffn, *, n_split=2):
    m, d = y.shape
    d_ff = ffn["wdn"].shape[0]
    return pl.pallas_call(
        functools.partial(_post_ab_dec_kernel, n_split=n_split),
        out_shape=[jax.ShapeDtypeStruct((m, d), F32), jax.ShapeDtypeStruct((m, d_ff), F32)],
        compiler_params=pltpu.CompilerParams(vmem_limit_bytes=VMEM_LIMIT),
        name="post_ab_dec",
    )(y, attn, rgo, fbuf_t, wo_a, wo_b, ffn["g"], ffn["wup"], ffn["cw"], ffn["cb"], ffn["wdn"])


def _gmlp_in(y, gm_ref, win_ref, bin_ref, sn_ref):
    d_c = win_ref.shape[1] // 2
    z = _gelu(_dot(_rms(y, gm_ref[...]), win_ref[...]) + bin_ref[...])
    return z[:, :d_c], _rms(z[:, d_c:], sn_ref[...])


def _layer_c_prompt_kernel(y_ref, fb_ref, gm_ref, win_ref, bin_ref, sn_ref, sw_ref, sbt_ref, woc_ref,
                           gf_ref, wup_ref, cw_ref, cb_ref, wdn_ref, gfin_ref,
                           o_ref, nb_ref, gs, carry, *, tm, n_split):
    y = y_ref[0]
    u, v = _gmlp_in(y, gm_ref, win_ref, bin_ref, sn_ref)
    vb = v.astype(BF16)
    n_groups = sw_ref.shape[0]
    tril = lax.broadcasted_iota(I32, (CHUNK, CHUNK), 0) >= lax.broadcasted_iota(I32, (CHUNK, CHUNK), 1)
    wm = [jnp.where(tril, sw_ref[gi], 0.0).astype(BF16) for gi in range(n_groups)]
    rows = []
    for r in range(tm // CHUNK):
        cols = []
        for gi in range(n_groups):
            mixed = jnp.dot(wm[gi], vb[r * CHUNK:(r + 1) * CHUNK, gi * LANES:(gi + 1) * LANES],
                            preferred_element_type=F32)
            cols.append(mixed + sbt_ref[:, gi:gi + 1])
        rows.append(jnp.concatenate(cols, axis=1))
    gated = u * jnp.concatenate(rows, axis=0)
    y1 = y + _dot(gated, woc_ref[...])
    cf = wdn_ref.shape[0] // n_split
    out, _ = _ffn_tile(y1, gf_ref, wup_ref, cw_ref, cb_ref, wdn_ref,
                       _prompt_conv_prev(gs, carry, fb_ref, nb_ref, tm, cf), n_split)
    o_ref[0] = _rms(y1 + out, gfin_ref[...])


def _layer_c_prompt(y, fbuf, cp, ffn, g_final, *, tm=512, n_split=2):
    batch, seq, d = y.shape
    d_ff = ffn["wdn"].shape[0]
    cf = d_ff // n_split
    blk = pl.BlockSpec((1, tm, d), lambda b, t: (b, t, 0))
    fb = pl.BlockSpec((1, 2, d_ff), lambda b, t: (b, 0, 0))
    consts = [cp["g"], cp["win"], cp["bin"], cp["sn"], cp["sw"], cp["sbt"], cp["woc"],
              ffn["g"], ffn["wup"], ffn["cw"], ffn["cb"], ffn["wdn"], g_final]
    return pl.pallas_call(
        functools.partial(_layer_c_prompt_kernel, tm=tm, n_split=n_split),
        grid=(batch, seq // tm),
        in_specs=[blk, fb] + [_full_spec(c.shape) for c in consts],
        out_specs=[blk, fb],
        out_shape=[jax.ShapeDtypeStruct((batch, seq, d), F32), jax.ShapeDtypeStruct((batch, 2, d_ff), F32)],
        scratch_shapes=[pltpu.VMEM((tm + 8, cf), F32), pltpu.VMEM((n_split, 8, cf), F32)],
        compiler_params=_cparams(2),
        name="layer_c_prompt",
    )(y, fbuf, *consts)


def _layer_c_dec_kernel(y_ref, fb_ref, gm_ref, win_ref, bin_ref, sn_ref, sw0_ref, sb0_ref, woc_ref,
                        gf_ref, wup_ref, cw_ref, cb_ref, wdn_ref, gfin_ref, o_ref, g_ref, v_ref, *, n_split):
    y = y_ref[...]
    u, v = _gmlp_in(y, gm_ref, win_ref, bin_ref, sn_ref)
    v_ref[...] = v
    y1 = y + _dot(u * (sw0_ref[...] * v + sb0_ref[...]), woc_ref[...])
    cf = wdn_ref.shape[0] // n_split
    out, gates = _ffn_tile(y1, gf_ref, wup_ref, cw_ref, cb_ref, wdn_ref, _dec_conv_prev(fb_ref, cf), n_split)
    o_ref[...] = _rms(y1 + out, gfin_ref[...])
    for k, g in enumerate(gates):
        g_ref[:, k * cf:(k + 1) * cf] = g


def _layer_c_dec(y, fbuf_t, cp, ffn, g_final, *, n_split=2):
    m, d = y.shape
    d_ff = ffn["wdn"].shape[0]
    d_c = cp["woc"].shape[0]
    return pl.pallas_call(
        functools.partial(_layer_c_dec_kernel, n_split=n_split),
        out_shape=[jax.ShapeDtypeStruct((m, d), F32), jax.ShapeDtypeStruct((m, d_ff), F32),
                   jax.ShapeDtypeStruct((m, d_c), F32)],
        compiler_params=pltpu.CompilerParams(vmem_limit_bytes=VMEM_LIMIT),
        name="layer_c_dec",
    )(y, fbuf_t, cp["g"], cp["win"], cp["bin"], cp["sn"], cp["sw0"], cp["sb0"], cp["woc"],
      ffn["g"], ffn["wup"], ffn["cw"], ffn["cb"], ffn["wdn"], g_final)


def _start_page_copies(src_ref, pt_ref, b, dst_ref, sem, n_pages, page):
    def body(pg, carry):
        col = pl.multiple_of(pg * page, page)
        pltpu.make_async_copy(src_ref.at[pt_ref[b, pg]], dst_ref.at[:, pl.ds(col, page)], sem).start()
        return carry
    lax.fori_loop(0, n_pages, body, 0, unroll=8)


def _wait_page_copies(dst_ref, sem):
    pltpu.make_async_copy(dst_ref, dst_ref, sem).wait()


def _dec_score_kernel(pt_ref, qi_ref, wi_ref, ixn_ref, cik_ref, keys_ref, knew_ref, ibuf, sems, *, n_pages, page):
    b = pl.program_id(0)
    nb = pl.num_programs(0)
    slot = lax.rem(b, 2)

    def start(bb, sl):
        _start_page_copies(cik_ref, pt_ref, bb, ibuf.at[sl], sems.at[sl], n_pages, page)

    @pl.when(b == 0)
    def _first():
        start(0, 0)

    @pl.when(b + 1 < nb)
    def _prefetch_next():
        start(b + 1, 1 - slot)

    _wait_page_copies(ibuf.at[slot], sems.at[slot])
    qi = qi_ref[0].astype(BF16)
    wi = wi_ref[0]
    s = jnp.dot(qi, ibuf[slot].astype(BF16), preferred_element_type=F32)
    sc = jnp.sum(jnp.maximum(s, 0.0) * wi, axis=0, keepdims=True) * IDX_HEADS ** -0.5
    keys_ref[0] = _float_key(sc)
    kin = ixn_ref[0][:, 0:IDX_DIM]
    sn = jnp.sum(qi_ref[0] * kin, axis=1, keepdims=True)
    scn = jnp.sum(jnp.maximum(sn, 0.0) * wi, axis=0, keepdims=True) * IDX_HEADS ** -0.5
    knew_ref[0] = jnp.broadcast_to(_float_key(scn), (1, LANES))


def _dec_scores(page_table, qi3, wi3, ix3, cik_t, *, page):
    db, n_pages = page_table.shape
    past = n_pages * page
    return pl.pallas_call(
        functools.partial(_dec_score_kernel, n_pages=n_pages, page=page),
        grid_spec=pltpu.PrefetchScalarGridSpec(
            num_scalar_prefetch=1,
            grid=(db,),
            in_specs=[pl.BlockSpec((1, IDX_HEADS, IDX_DIM), lambda b, pt: (b, 0, 0)),
                      pl.BlockSpec((1, IDX_HEADS, 1), lambda b, pt: (b, 0, 0)),
                      pl.BlockSpec((1, 1, LANES), lambda b, pt: (b, 0, 0)),
                      pl.BlockSpec(memory_space=pl.ANY)],
            out_specs=[pl.BlockSpec((1, 1, past), lambda b, pt: (b, 0, 0)),
                       pl.BlockSpec((1, 1, LANES), lambda b, pt: (b, 0, 0))],
            scratch_shapes=[pltpu.VMEM((2, IDX_DIM, past), F32), pltpu.SemaphoreType.DMA((2,))]),
        out_shape=[jax.ShapeDtypeStruct((db, 1, past), I32), jax.ShapeDtypeStruct((db, 1, LANES), I32)],
        compiler_params=_cparams(1),
        name="dec_scores",
    )(page_table, qi3, wi3, ix3, cik_t)


def _dec_select_kernel(keys_ref, knew_ref, ko_ref, kno_ref, thr_ref, *, topk):
    kk = keys_ref[...]
    kn = knew_ref[...][:, 0:1]
    kf = jnp.float32(topk)

    def count(pred, pred_new):
        return jnp.sum(jnp.where(pred, 1.0, 0.0), axis=1, keepdims=True) + jnp.where(pred_new, 1.0, 0.0)

    def search_body(i, ans):
        cand = ans | jnp.left_shift(jnp.int32(1), 31 - i)
        cs = cand ^ jnp.int32(INT_MIN)
        return jnp.where(count(kk >= cs, kn >= cs) >= kf, cand, ans)
    ans = lax.fori_loop(0, 32, search_body, jnp.zeros((kk.shape[0], 1), I32))
    thr = jnp.maximum(ans ^ jnp.int32(INT_MIN), jnp.int32(INT_MIN + 1))
    need = kf - count(kk > thr, kn > thr)
    past = kk.shape[1]
    big = jnp.int32(4 * past)
    eqcol = jnp.where(kk == thr, lax.broadcasted_iota(I32, kk.shape, 1), big)
    eqnew = jnp.where(kn == thr, jnp.int32(past), big)
    nbits = int(math.log2(past)) + 1

    def tie_body(i, best):
        cand = best | jnp.left_shift(jnp.int32(1), nbits - 1 - i)
        return jnp.where(count(eqcol < cand, eqnew < cand) < need, cand, best)
    last = lax.fori_loop(0, nbits, tie_body, jnp.zeros((kk.shape[0], 1), I32))
    ko_ref[...] = jnp.where((eqcol > last) & (eqcol < big), thr - 1, kk)
    kno_ref[...] = jnp.broadcast_to(jnp.where((eqnew > last) & (eqnew < big), thr - 1, kn), kno_ref.shape)
    thr_ref[...] = jnp.broadcast_to(thr, thr_ref.shape)


def _dec_select(keys, knew, *, topk):
    db, past = keys.shape
    return pl.pallas_call(
        functools.partial(_dec_select_kernel, topk=topk),
        out_shape=[jax.ShapeDtypeStruct((db, past), I32), jax.ShapeDtypeStruct((db, LANES), I32),
                   jax.ShapeDtypeStruct((db, LANES), I32)],
        name="dec_select",
    )(keys, knew)


def _dec_attn_kernel(pt_ref, q_ref, kvn_ref, keys_ref, knew_ref, thr_ref, bias_ref, ck_ref, cv_ref, o_ref,
                     kbuf, vbuf, sems, *, n_pages, page):
    b = pl.program_id(0)
    nb = pl.num_programs(0)
    slot = lax.rem(b, 2)
    past = n_pages * page

    def start(bb, sl):
        _start_page_copies(ck_ref, pt_ref, bb, kbuf.at[sl], sems.at[0, sl], n_pages, page)
        _start_page_copies(cv_ref, pt_ref, bb, vbuf.at[sl], sems.at[1, sl], n_pages, page)

    @pl.when(b == 0)
    def _first():
        start(0, 0)

    @pl.when(b + 1 < nb)
    def _prefetch_next():
        start(b + 1, 1 - slot)

    _wait_page_copies(kbuf.at[slot], sems.at[0, slot])
    _wait_page_copies(vbuf.at[slot], sems.at[1, slot])
    qm = q_ref[0]
    thr = thr_ref[0][:, 0:1]
    sel = keys_ref[0] >= thr
    sel_new = knew_ref[0][:, 0:1] >= thr
    bias = bias_ref[...]
    far, last, bnew = bias[:, LANES:LANES + 1], bias[:, 0:page], bias[:, LANES + 1:LANES + 2]
    lg = jnp.dot(qm.astype(BF16), kbuf[slot].astype(BF16), preferred_element_type=F32)
    lane = lax.broadcasted_iota(I32, lg.shape, 1)
    lastp = jnp.concatenate([jnp.zeros((N_HEADS, past - page), F32), last], axis=1)
    lg = jnp.where(sel, lg + jnp.where(lane >= past - page, lastp, far), -jnp.inf)
    kvn = kvn_ref[0]
    lgn = jnp.sum(qm * kvn[:, 0:LANES], axis=1, keepdims=True) + bnew
    lgn = jnp.where(sel_new, lgn, -jnp.inf)
    m = jnp.maximum(jnp.max(lg, axis=1, keepdims=True), lgn)
    e = jnp.exp(lg - m)
    en = jnp.exp(lgn - m)
    den = jnp.sum(e, axis=1, keepdims=True) + en
    pv = _dot_nt(e.astype(BF16), vbuf[slot].astype(BF16)) + en * kvn[:, LANES:2 * LANES]
    pv = pv / den
    lo = lax.broadcasted_iota(I32, (1, LANES), 1) < HEAD_DIM
    o_ref[0] = jnp.concatenate([jnp.where(lo, pv[2 * p:2 * p + 1], pv[2 * p + 1:2 * p + 2]) for p in range(4)],
                               axis=1).astype(BF16)


def _dec_attn(page_table, qm, kvn, keys, knew, thr, bias_dec, ck_t, cv_t, *, page):
    db, n_pages = page_table.shape
    past = n_pages * page
    row = lambda n: pl.BlockSpec((1, 1, n), lambda b, pt: (b, 0, 0))
    return pl.pallas_call(
        functools.partial(_dec_attn_kernel, n_pages=n_pages, page=page),
        grid_spec=pltpu.PrefetchScalarGridSpec(
            num_scalar_prefetch=1,
            grid=(db,),
            in_specs=[pl.BlockSpec((1, N_HEADS, LANES), lambda b, pt: (b, 0, 0)), row(2 * LANES), row(past),
                      row(LANES), row(LANES), pl.BlockSpec((N_HEADS, 2 * LANES), lambda b, pt: (0, 0)),
                      pl.BlockSpec(memory_space=pl.ANY), pl.BlockSpec(memory_space=pl.ANY)],
            out_specs=row(4 * LANES),
            scratch_shapes=[pltpu.VMEM((2, 2 * HEAD_DIM, past), F32), pltpu.VMEM((2, 2 * HEAD_DIM, past), F32),
                            pltpu.SemaphoreType.DMA((2, 2))]),
        out_shape=jax.ShapeDtypeStruct((db, 1, 4 * LANES), BF16),
        compiler_params=_cparams(1),
        name="dec_attn",
    )(page_table, qm, kvn, keys, knew, thr, bias_dec, ck_t, cv_t)


def _prep_in_ab(w):
    d = w.shape[0]
    nq, nkv = N_HEADS * HEAD_DIM, N_KV_HEADS * HEAD_DIM
    offs = np.cumsum([nq, nkv, nkv, IDX_HEADS * IDX_DIM, IDX_DIM, IDX_HEADS, 512])
    q, k, v, qi, ki, wi, g, xr = jnp.split(w, offs.tolist(), axis=1)
    q = q.reshape(d, N_HEADS, HEAD_DIM)[:, np.array(HEAD_PERM), :].reshape(d, nq)
    pad = jnp.zeros((d, _C_G - _C_IX - IDX_DIM - IDX_HEADS), w.dtype)
    return jnp.concatenate([q, k, v, qi, ki, wi, pad, g, xr], axis=1).astype(BF16)


def _block_diag(w):
    n, c, _ = w.shape
    return (jnp.eye(n, dtype=w.dtype)[:, None, :, None] * w[:, :, None, :]).reshape(n * c, n * c).astype(BF16)


def _ffn_params(layer, norm_ffn, w_up, cw, cb, w_down):
    return {"g": norm_ffn[layer][None], "wup": w_up[layer].astype(BF16), "cw": cw[layer], "cb": cb[layer][None],
            "wdn": w_down[layer].astype(BF16)}


def kernel(x_prompt, x_sample, cache_k, cache_v, cache_idx_k, state_rglru_h, state_rglru_conv, state_ffn_conv,
           page_table, norm_mix, norm_ffn, norm_final, rel_bias, w_in_ab, w_out_ab, rg_conv_w, rg_conv_b,
           rg_wa, rg_ba, rg_wx, rg_bx, rg_lambda, w_in_c, b_in_c, sgu_norm, sgu_w, sgu_b, w_out_c,
           ffn_w_up, ffn_conv_w, ffn_conv_b, ffn_w_down):
    batch, seq, d = x_prompt.shape
    db = x_sample.shape[0]
    page = cache_k.shape[2]
    d_a = N_HEADS * HEAD_DIM
    d_b = rg_conv_w.shape[-1]
    d_ff = ffn_w_down.shape[1]
    assert x_sample.shape[1] == 1 and seq % 512 == 0 and page == LANES and w_in_ab.shape[0] == 1

    w_in0 = _prep_in_ab(w_in_ab[0])
    wo = w_out_ab[0]
    wo_a = wo[:d_a].reshape(N_HEADS, HEAD_DIM, d)[np.array(HEAD_PERM)].reshape(d_a, d).astype(BF16)
    wo_b = wo[d_a:].astype(BF16)
    rg = {"cw": rg_conv_w[0], "cb": rg_conv_b[0][None], "wa": _block_diag(rg_wa[0]), "ba": rg_ba[0][None],
          "wx": _block_diag(rg_wx[0]), "bx": rg_bx[0][None], "lam": rg_lambda[0][None]}
    ffn0 = _ffn_params(0, norm_ffn, ffn_w_up, ffn_conv_w, ffn_conv_b, ffn_w_down)
    ffn1 = _ffn_params(1, norm_ffn, ffn_w_up, ffn_conv_w, ffn_conv_b, ffn_w_down)
    cp = {"g": norm_mix[1][None], "win": w_in_c[0].astype(BF16), "bin": b_in_c[0][None], "sn": sgu_norm[0][None],
          "sw": sgu_w[0], "sbt": sgu_b[0].T, "woc": w_out_c[0].astype(BF16),
          "sw0": jnp.repeat(sgu_w[0][:, 0, 0], d // sgu_w.shape[1])[None],
          "sb0": jnp.repeat(sgu_b[0][:, 0], d // sgu_w.shape[1])[None]}
    g_mix0 = norm_mix[0][None]
    g_final = norm_final[None]
    bias_st, bias_dec = _bias_tables(rel_bias, page)

    xp = x_prompt.reshape(batch * seq, d)
    q_st, qi_st, kv_p, ix_p, gate_p, xr_p = _inproj(xp, g_mix0, w_in0, stack=True, tm=512)
    attn_p = _attn_prompt(q_st, qi_st, ix_p, kv_p, bias_st, batch=batch, seq=seq)
    rg_p, h_p, cbuf_p = _rglru_prompt(gate_p, xr_p, jnp.zeros((batch, rg["cw"].shape[0] - 1, d_b), F32),
                                      jnp.zeros((batch, d_b), F32), rg, batch=batch, seq=seq)
    zero_fb = jnp.zeros((batch, 2, d_ff), F32)
    y1_p, fb0_p = _post_ab_prompt(x_prompt, attn_p, rg_p, zero_fb, wo_a, wo_b, ffn0)
    y_p, fb1_p = _layer_c_prompt(y1_p, zero_fb, cp, ffn1, g_final)

    xs = x_sample.reshape(db, d)
    qm_s, qi_s, kv_s, ix_s, gate_s, xr_s = _inproj(xs, g_mix0, w_in0, stack=False, tm=db)
    cik_t = jnp.transpose(cache_idx_k[0], (0, 2, 1))
    ck_t = jnp.transpose(cache_k[0], (0, 2, 3, 1)).reshape(-1, 2 * HEAD_DIM, page)
    cv_t = jnp.transpose(cache_v[0], (0, 2, 3, 1)).reshape(-1, 2 * HEAD_DIM, page)
    topk_s = min(TOPK_MAX, (page_table.shape[1] * page + 1) // 4)
    keys_s, knew_s = _dec_scores(page_table, qi_s.reshape(db, IDX_HEADS, IDX_DIM),
                                 ix_s[:, IDX_DIM:IDX_DIM + IDX_HEADS].reshape(db, IDX_HEADS, 1),
                                 ix_s.reshape(db, 1, LANES), cik_t, page=page)
    keys_s, knew_s, thr_s = _dec_select(keys_s.reshape(db, -1), knew_s.reshape(db, LANES), topk=topk_s)
    attn_s = _dec_attn(page_table, jnp.transpose(qm_s, (1, 0, 2)), kv_s.reshape(db, 1, 2 * LANES),
                       keys_s.reshape(db, 1, -1), knew_s.reshape(db, 1, LANES), thr_s.reshape(db, 1, LANES),
                       bias_dec, ck_t, cv_t, page=page).reshape(db, d_a)
    cbuf_s_in = state_rglru_conv[0]
    rg_s, h_s = _rglru_dec(gate_s, xr_s, jnp.transpose(cbuf_s_in, (1, 0, 2)), state_rglru_h[0], rg)
    y1_s, g0_s = _post_ab_dec(xs, attn_s, rg_s, jnp.transpose(state_ffn_conv[0], (1, 0, 2)), wo_a, wo_b, ffn0)
    y_s, g1_s, v_s = _layer_c_dec(y1_s, jnp.transpose(state_ffn_conv[1], (1, 0, 2)), cp, ffn1, g_final)

    kv4 = kv_p.reshape(batch, seq, 2, N_KV_HEADS, HEAD_DIM)
    kvs = kv_s.reshape(db, 1, 2, N_KV_HEADS, HEAD_DIM)
    fbuf_s = lambda layer, g: jnp.concatenate([state_ffn_conv[layer][:, 1:], g[:, None]], axis=1)
    return (y_p, y_s.reshape(db, 1, d),
            kv4[None, :, :, 0], kv4[None, :, :, 1], ix_p.reshape(batch, seq, LANES)[None, :, :, :IDX_DIM],
            kvs[None, :, :, 0], kvs[None, :, :, 1], ix_s.reshape(db, 1, LANES)[None, :, :, :IDX_DIM],
            h_p.reshape(batch, d_b)[None], cbuf_p[None],
            h_s[None], jnp.concatenate([cbuf_s_in[:, 1:], xr_s[:, None]], axis=1)[None],
            v_s.reshape(db, 1, -1)[None],
            jnp.stack([fb0_p, fb1_p]), jnp.stack([fbuf_s(0, g0_s), fbuf_s(1, g1_s)]))
```

```python
import functools
import math

import numpy as np
import jax
import jax.numpy as jnp
from jax import lax
from jax.experimental import pallas as pl
from jax.experimental.pallas import tpu as pltpu

F32 = jnp.float32
BF16 = jnp.bfloat16
I32 = jnp.int32

N_HEADS = 8
HEAD_DIM = 64
N_KV_HEADS = 2
Q_PER_KV = N_HEADS // N_KV_HEADS
IDX_HEADS = 8
IDX_DIM = 64
TOPK_MAX = 256
N_BUCKETS = 32
REL_MAX_EXACT = N_BUCKETS // 2
REL_MAX_DIST = 128
RG_C = 8.0
CHUNK = 128
EPS = 1e-6

LANES = 128
QB = 128
INT_MIN = -(2 ** 31)
KEY_MIN_FINITE = INT_MIN + 0x800000
NEG_MAX = float(np.finfo(np.float32).min)
HEAD_PERM = (0, 4, 1, 5, 2, 6, 3, 7)
VMEM_LIMIT = 56 * 1024 * 1024


def _cparams(n_grid):
    return pltpu.CompilerParams(dimension_semantics=("arbitrary",) * n_grid, vmem_limit_bytes=VMEM_LIMIT)


def _full_spec(shape):
    nd = len(shape)
    return pl.BlockSpec(shape, lambda *_: (0,) * nd, pipeline_mode=pl.Buffered(1))


def _rms(x, g):
    return x * lax.rsqrt(jnp.mean(x * x, axis=-1, keepdims=True) + EPS) * g


def _gelu(x):
    return x * (0.5 * (1.0 + jnp.tanh(math.sqrt(2.0 / math.pi) * (x + 0.044715 * (x * x * x)))))


def _sigmoid(x):
    return 1.0 / (1.0 + jnp.exp(-x))


def _softplus(x):
    return jnp.maximum(x, 0.0) + jnp.log(1.0 + jnp.exp(-jnp.abs(x)))


def _dot(a, b):
    return jnp.dot(a.astype(BF16), b, preferred_element_type=F32)


def _dot_nt(a, b):
    return lax.dot_general(a, b, (((1,), (1,)), ((), ())), preferred_element_type=F32)


def _float_key(x):
    bits = pltpu.bitcast(x, I32)
    key = jnp.where(bits < 0, bits ^ jnp.int32(0x7FFFFFFF), bits)
    return jnp.where(bits == jnp.int32(INT_MIN), jnp.int32(0), key)


def _key_float(key):
    key = jnp.maximum(key, jnp.int32(KEY_MIN_FINITE))
    return pltpu.bitcast(jnp.where(key < 0, key ^ jnp.int32(0x7FFFFFFF), key), F32)


def _walk_to_kth(scores, t, cge, cgt, kf, count_ge_gt, max_steps):
    axis = 0 if t.shape[0] == 1 else 1

    def settled(t, cge, cgt):
        return (cgt < kf) & ((cge >= kf) | (t <= NEG_MAX))

    def unsettled(t, cge, cgt):
        return jnp.max(jnp.where(settled(t, cge, cgt), 0.0, 1.0)) > 0.0

    def body(st):
        t, cge, cgt, it = st
        sc = scores()
        below = jnp.max(jnp.where(sc < t, sc, -jnp.inf), axis=axis, keepdims=True)
        above = jnp.min(jnp.where(sc > t, sc, jnp.inf), axis=axis, keepdims=True)
        t = jnp.where(cgt >= kf, above, jnp.where((cge < kf) & (t > NEG_MAX), jnp.maximum(below, NEG_MAX), t))
        cge, cgt = count_ge_gt(t)
        return t, cge, cgt, it + 1

    t, cge, cgt, _ = lax.while_loop(lambda st: unsettled(st[0], st[1], st[2]) & (st[3] < max_steps), body,
                                    (t, cge, cgt, jnp.int32(0)))
    return t, cge, cgt


def _t5_bucket_np(n):
    n = np.maximum(n, 0)
    nf = np.maximum(n, 1).astype(np.float32)
    large = REL_MAX_EXACT + (np.log(nf / np.float32(REL_MAX_EXACT)) / np.float32(math.log(REL_MAX_DIST / REL_MAX_EXACT))
                             * np.float32(N_BUCKETS - REL_MAX_EXACT)).astype(np.int32)
    large = np.minimum(large, N_BUCKETS - 1)
    return np.where(n < REL_MAX_EXACT, n, large).astype(np.int32)


_C_Q, _C_KV, _C_QI, _C_IX, _C_G, _C_X, _C_END = 0, 512, 768, 1280, 1408, 1920, 2432


def _inproj_kernel(x_ref, g_ref, w_ref, q_ref, qi_ref, kv_ref, ix_ref, gate_ref, xr_ref, *, stack):
    hn = _rms(x_ref[...], g_ref[...])
    z = _dot(hn, w_ref[...])
    q = z[:, _C_Q:_C_KV] * HEAD_DIM ** -0.5
    qi = z[:, _C_QI:_C_IX] * IDX_DIM ** -0.5
    kv_ref[...] = z[:, _C_KV:_C_QI]
    ix_ref[...] = z[:, _C_IX:_C_G]
    gate_ref[...] = z[:, _C_G:_C_X]
    xr_ref[...] = z[:, _C_X:_C_END]
    if stack:
        qb, qib = q.astype(BF16), qi.astype(BF16)
        for r in range(q.shape[0] // QB):
            for p in range(4):
                q_ref[r, p * QB:(p + 1) * QB, :] = qb[r * QB:(r + 1) * QB, p * LANES:(p + 1) * LANES]
                qi_ref[r, p * QB:(p + 1) * QB, :] = qib[r * QB:(r + 1) * QB, p * LANES:(p + 1) * LANES]
    else:
        lo = lax.broadcasted_iota(I32, (q.shape[0], LANES), 1) < HEAD_DIM
        for p in range(4):
            qp = q[:, p * LANES:(p + 1) * LANES]
            q_ref[2 * p] = jnp.where(lo, qp, 0.0)
            q_ref[2 * p + 1] = jnp.where(lo, 0.0, qp)
        qi_ref[...] = qi


def _inproj(x2d, g, w, *, stack, tm):
    m, d = x2d.shape
    if stack:
        q_shape, q_spec = (m // QB, 4 * QB, LANES), pl.BlockSpec((tm // QB, 4 * QB, LANES), lambda i: (i, 0, 0))
        qi_shape, qi_spec, qdt = q_shape, q_spec, BF16
    else:
        q_shape, q_spec = (N_HEADS, m, LANES), pl.BlockSpec((N_HEADS, tm, LANES), lambda i: (0, i, 0))
        qi_shape, qi_spec, qdt = (m, 512), pl.BlockSpec((tm, 512), lambda i: (i, 0)), F32
    row = lambda n: pl.BlockSpec((tm, n), lambda i: (i, 0))
    return pl.pallas_call(
        functools.partial(_inproj_kernel, stack=stack),
        grid=(m // tm,),
        in_specs=[row(d), _full_spec((1, d)), _full_spec(w.shape)],
        out_specs=[q_spec, qi_spec, row(256), row(128), row(512), row(512)],
        out_shape=[jax.ShapeDtypeStruct(q_shape, qdt), jax.ShapeDtypeStruct(qi_shape, qdt),
                   jax.ShapeDtypeStruct((m, 256), F32), jax.ShapeDtypeStruct((m, 128), F32),
                   jax.ShapeDtypeStruct((m, 512), F32), jax.ShapeDtypeStruct((m, 512), F32)],
        compiler_params=_cparams(1),
        name="inproj_stack" if stack else "inproj_dec",
    )(x2d, g, w)


def _bias_kernel(rb_ref, bk_ref, bkd_ref, o_ref, od_ref):
    for d in range(3):
        bk = bk_ref[d]
        for p in range(4):
            for a in range(2):
                h = p + 4 * a
                acc = jnp.zeros((QB, LANES), F32)
                for b in range(N_BUCKETS):
                    acc = jnp.where(bk == b, rb_ref[b, h], acc)
                o_ref[d, a * QB:(a + 1) * QB, p * LANES:(p + 1) * LANES] = acc
    bkd = bkd_ref[...]
    rowi = lax.broadcasted_iota(I32, (N_HEADS, 2 * LANES), 0)
    acc = jnp.zeros((N_HEADS, 2 * LANES), F32)
    for r in range(N_HEADS):
        h = r // 2 + 4 * (r % 2)
        for b in range(N_BUCKETS):
            acc = jnp.where((rowi == r) & (bkd == b), rb_ref[b, h], acc)
    od_ref[...] = acc


def _bias_tables(rel_bias, page):
    key = np.arange(QB)[:, None]
    qry = np.arange(LANES)[None, :]
    bk = np.stack([_t5_bucket_np(d * QB + qry - key) for d in range(3)])
    assert (_t5_bucket_np(np.arange(2 * QB + 1 - LANES, 4 * QB)) == N_BUCKETS - 1).all()
    assert (_t5_bucket_np(np.arange(page, 8 * page)) == N_BUCKETS - 1).all()
    dec = np.zeros((2 * LANES,), np.int64)
    dec[:page] = page - np.arange(page)
    dec[LANES] = 2 * REL_MAX_DIST
    dec[LANES + 1] = 0
    bkd = np.broadcast_to(_t5_bucket_np(dec)[None, :], (N_HEADS, 2 * LANES))
    return pl.pallas_call(
        _bias_kernel,
        in_specs=[pl.BlockSpec(memory_space=pltpu.SMEM), pl.BlockSpec(memory_space=pltpu.VMEM),
                  pl.BlockSpec(memory_space=pltpu.VMEM)],
        out_shape=[jax.ShapeDtypeStruct((3, 2 * QB, 4 * LANES), F32), jax.ShapeDtypeStruct((N_HEADS, 2 * LANES), F32)],
        name="bias_tables",
    )(rel_bias, jnp.asarray(bk, I32), jnp.asarray(bkd, I32))


def _search_widths(n_chunks):
    cuts = sorted({min(c, n_chunks) for c in (2, 4, 8, 12, 16)} | {n_chunks})
    return [c for c in cuts if c <= n_chunks]


def _attn_prompt_kernel(q_ref, qi_ref, ixq_ref, ixk_ref, kv_ref, bias_ref, o_ref,
                        kblk, vblk_t, kiblk, keys, scores, logits, acc, thr_ref, cge_ref, cgt_ref, *, n_chunks, topk):
    j = pl.program_id(1)
    lane = lax.broadcasted_iota(I32, (QB, LANES), 1)
    row = lax.broadcasted_iota(I32, (QB, LANES), 0)
    lo = lane < HEAD_DIM
    blocks = [(a, p) for a in range(2) for p in range(4)]
    rs = lambda a: slice(a * QB, (a + 1) * QB)
    cs = lambda p: slice(p * LANES, (p + 1) * LANES)
    chunk = lambda c: pl.ds(pl.multiple_of(c * QB, QB), QB)

    @pl.when(j == 0)
    def _build_block_diagonal_keys():
        def body(c, carry):
            kc = kv_ref[0, chunk(c), 0:LANES]
            vt = kv_ref[0, chunk(c), LANES:2 * LANES].T
            kia = jnp.where(lo, ixk_ref[0, chunk(c), :], 0.0)
            kblk[c, 0:QB, :] = jnp.where(lo, kc, 0.0).astype(BF16)
            kblk[c, QB:2 * QB, :] = jnp.where(lo, 0.0, kc).astype(BF16)
            vblk_t[c, :, 0:QB] = jnp.where(row < HEAD_DIM, vt, 0.0).astype(BF16)
            vblk_t[c, :, QB:2 * QB] = jnp.where(row < HEAD_DIM, 0.0, vt).astype(BF16)
            kiblk[c, 0:QB, :] = kia.astype(BF16)
            kiblk[c, QB:2 * QB, :] = pltpu.roll(kia, HEAD_DIM, 1).astype(BF16)
            return carry
        lax.fori_loop(0, n_chunks, body, 0)

    qi = qi_ref[0]
    q = q_ref[0]
    wt = ixq_ref[0].T
    w_row = {(a, p): wt[IDX_DIM + 2 * p + a:IDX_DIM + 2 * p + a + 1, :] for a, p in blocks}
    qpos = j * QB + lane

    n_pairs = (j + 2) // 2

    def chunk_loop(body, carry):
        n_quads = n_pairs // 2
        carry = lax.fori_loop(0, n_quads, lambda i, cr: body([4 * i + u for u in range(4)], cr), carry)
        return lax.fori_loop(2 * n_quads, n_pairs, lambda i, cr: body([2 * i, 2 * i + 1], cr), carry)

    def score_body(cs_, carry):
        for c in cs_:
            s = _dot_nt(kiblk[c], qi)
            lg = _dot_nt(kblk[c], q)
            sc = jnp.zeros((QB, LANES), F32)
            for a, p in blocks:
                sc = sc + jnp.maximum(s[rs(a), cs(p)], 0.0) * w_row[(a, p)]
            sc = sc * IDX_HEADS ** -0.5
            admissible = c * QB + row <= qpos
            scores[chunk(c), :] = jnp.where(admissible, sc, -jnp.inf)
            keys[chunk(c), :] = jnp.where(admissible, _float_key(sc), jnp.int32(INT_MIN))
            logits[c] = lg + bias_ref[jnp.clip(j - c, 0, 2)]
        return carry
    chunk_loop(score_body, 0)

    def fill_body(c, carry):
        scores[chunk(c), :] = jnp.full((QB, LANES), -jnp.inf, F32)
        keys[chunk(c), :] = jnp.full((QB, LANES), INT_MIN, I32)
        return carry
    lax.fori_loop(2 * n_pairs, n_chunks, fill_body, 0)

    kf = jnp.float32(topk)

    def count(src, width, pred):
        accs = [jnp.zeros((8, LANES), F32) for _ in range(8)]
        for g in range(width // 8):
            accs[g % 8] = accs[g % 8] + jnp.where(pred(src[g * 8:(g + 1) * 8, :]), 1.0, 0.0)
        return jnp.sum(functools.reduce(lambda x, y: x + y, accs), axis=0, keepdims=True)

    def search(width):
        def search_body(i, ans):
            cand = ans | jnp.left_shift(jnp.int32(1), 31 - i)
            cs_ = cand ^ jnp.int32(INT_MIN)
            return jnp.where(count(keys, width, lambda k: k >= cs_) >= kf, cand, ans)
        ans = lax.fori_loop(0, 32, search_body, jnp.zeros((1, LANES), I32))
        t = _key_float(ans ^ jnp.int32(INT_MIN))
        thr_ref[...] = t
        cge_ref[...] = count(scores, width, lambda s: s >= t)
        cgt_ref[...] = count(scores, width, lambda s: s > t)

    prev = 0
    for n in _search_widths(n_chunks):
        pl.when((j >= prev) & (j < n))(functools.partial(search, n * QB))
        prev = n

    s_len = scores.shape[0]
    thr, cge, cgt = _walk_to_kth(
        lambda: scores[...], thr_ref[...], cge_ref[...], cgt_ref[...], kf,
        lambda t: (count(scores, s_len, lambda s: s >= t), count(scores, s_len, lambda s: s > t)), 4 * topk)

    @pl.when(jnp.max(cge) > kf)
    def _break_ties_by_position():
        need = kf - cgt
        big = jnp.int32(2 * s_len)
        eqrow_ref = keys
        eqrow_ref[...] = jnp.where(scores[...] == thr, lax.broadcasted_iota(I32, scores.shape, 0), big)
        nbits = int(math.log2(s_len))

        def tie_body(i, best):
            cand = best | jnp.left_shift(jnp.int32(1), nbits - 1 - i)
            return jnp.where(count(eqrow_ref, s_len, lambda e: e < cand) < need, cand, best)
        last = lax.fori_loop(0, nbits, tie_body, jnp.zeros((1, LANES), I32))
        eqrow = eqrow_ref[...]
        scores[...] = jnp.where((eqrow > last) & (eqrow < big), -jnp.inf, scores[...])

    def mask_body(cs_, mx):
        mx = list(mx)
        for c in cs_:
            sel = scores[chunk(c), :] >= thr
            for n, (a, p) in enumerate(blocks):
                blk = jnp.where(sel, logits[c, rs(a), cs(p)], -jnp.inf)
                logits[c, rs(a), cs(p)] = blk
                mx[n] = jnp.maximum(mx[n], jnp.max(blk, axis=0, keepdims=True))
        return tuple(mx)
    mx = chunk_loop(mask_body, tuple(jnp.full((1, LANES), -jnp.inf, F32) for _ in blocks))

    acc[...] = jnp.zeros(acc.shape, F32)

    def pv_body(cs_, ls):
        ls = list(ls)
        pv = jnp.zeros(acc.shape, F32)
        for c in cs_:
            rows = []
            for a in range(2):
                cols = []
                for p in range(4):
                    e = jnp.exp(logits[c, rs(a), cs(p)] - mx[a * 4 + p])
                    ls[a * 4 + p] = ls[a * 4 + p] + jnp.sum(e, axis=0, keepdims=True)
                    cols.append(e.astype(BF16))
                rows.append(jnp.concatenate(cols, axis=1))
            pmat = jnp.concatenate(rows, axis=0)
            pv = pv + jnp.dot(vblk_t[c], pmat, preferred_element_type=F32)
        acc[...] = acc[...] + pv
        return tuple(ls)
    ls = chunk_loop(pv_body, tuple(jnp.zeros((1, LANES), F32) for _ in blocks))

    for p in range(4):
        inv = jnp.where(row < HEAD_DIM, 1.0 / ls[p], 1.0 / ls[4 + p])
        o_ref[0, :, cs(p)] = (acc[:, cs(p)] * inv).T.astype(BF16)


def _attn_prompt(q_st, qi_st, ix, kv, bias_st, *, batch, seq):
    nq = seq // QB
    assert nq % 2 == 0
    topk = min(TOPK_MAX, seq // 4)
    ix3 = ix.reshape(batch, seq, LANES)
    kv3 = kv.reshape(batch, seq, 2 * LANES)
    return pl.pallas_call(
        functools.partial(_attn_prompt_kernel, n_chunks=nq, topk=topk),
        grid=(batch, nq),
        in_specs=[pl.BlockSpec((1, 4 * QB, LANES), lambda b, j: (b * nq + j, 0, 0)),
                  pl.BlockSpec((1, 4 * QB, LANES), lambda b, j: (b * nq + j, 0, 0)),
                  pl.BlockSpec((1, QB, LANES), lambda b, j: (b, j, 0)),
                  pl.BlockSpec((1, seq, LANES), lambda b, j: (b, 0, 0)),
                  pl.BlockSpec((1, seq, 2 * LANES), lambda b, j: (b, 0, 0)),
                  _full_spec(bias_st.shape)],
        out_specs=pl.BlockSpec((1, QB, 4 * LANES), lambda b, j: (b, j, 0)),
        out_shape=jax.ShapeDtypeStruct((batch, seq, 4 * LANES), BF16),
        scratch_shapes=[pltpu.VMEM((nq, 2 * QB, LANES), BF16), pltpu.VMEM((nq, LANES, 2 * QB), BF16),
                        pltpu.VMEM((nq, 2 * QB, LANES), BF16), pltpu.VMEM((seq, LANES), I32),
                        pltpu.VMEM((seq, LANES), F32),
                        pltpu.VMEM((nq, 2 * QB, 4 * LANES), F32), pltpu.VMEM((LANES, 4 * LANES), F32),
                        pltpu.VMEM((1, LANES), F32), pltpu.VMEM((1, LANES), F32), pltpu.VMEM((1, LANES), F32)],
        compiler_params=_cparams(2),
        name="attn_prompt",
    )(q_st, qi_st, ix3, ix3, kv3, bias_st)


def _rglru_gates(xc, wa, ba, wx, bx, lam):
    r = _sigmoid(_dot(xc, wa) + ba)
    i = _sigmoid(_dot(xc, wx) + bx)
    log_a = -RG_C * r * _softplus(-lam)
    a = jnp.exp(log_a)
    u = jnp.sqrt(1.0 - jnp.exp(2.0 * log_a)) * (i * xc)
    return a, u


def _rglru_prompt_kernel(g_ref, xr_ref, buf_ref, h0_ref, cw_ref, cb_ref, wa_ref, ba_ref, wx_ref, bx_ref, lam_ref,
                         o_ref, hl_ref, nb_ref, xs, a_s, u_s, tail, hc, *, tc):
    t = pl.program_id(0)
    width = cw_ref.shape[0]
    nb, _, d = g_ref.shape

    @pl.when(t == 0)
    def _load_state():
        tail[...] = jnp.zeros(tail.shape, F32)
        tail[:, 8 - (width - 1):8, :] = buf_ref[...]
        hc[...] = h0_ref[...]

    xs[:, 0:8, :] = tail[...]
    xs[:, 8:8 + tc, :] = xr_ref[...]
    tail[...] = xs[:, tc:tc + 8, :]
    xc = cb_ref[...]
    for jj in range(width):
        off = 8 - (width - 1) + jj
        xc = xc + cw_ref[jj:jj + 1, :] * xs[:, off:off + tc, :]
    a, u = _rglru_gates(xc.reshape(nb * tc, d), wa_ref[...], ba_ref[...], wx_ref[...], bx_ref[...], lam_ref[...])
    n_lb = d // LANES
    pitch = tc + 8
    for k in range(n_lb):
        for b in range(nb):
            a_s[k, b * pitch:b * pitch + tc, :] = a[b * tc:(b + 1) * tc, k * LANES:(k + 1) * LANES]
            u_s[k, b * pitch:b * pitch + tc, :] = u[b * tc:(b + 1) * tc, k * LANES:(k + 1) * LANES]

    def scan_body(i, hs):
        rows = pl.ds(i, nb, stride=pitch)
        out = []
        for k in range(n_lb):
            h = a_s[k, rows, :] * hs[k] + u_s[k, rows, :]
            u_s[k, rows, :] = h
            out.append(h)
        return tuple(out)
    h0 = hc[...]
    hs = lax.fori_loop(0, tc, scan_body, tuple(h0[:, k * LANES:(k + 1) * LANES] for k in range(n_lb)), unroll=8)
    h = jnp.concatenate(hs, axis=1)
    hc[...] = h
    hseq = jnp.concatenate([jnp.concatenate([u_s[k, b * pitch:b * pitch + tc, :] for b in range(nb)], axis=0)
                            for k in range(n_lb)], axis=1)
    o_ref[...] = (_gelu(g_ref[...].reshape(nb * tc, d)) * hseq).reshape(nb, tc, d).astype(BF16)
    hl_ref[...] = h
    nb_ref[...] = xs[:, tc + 8 - (width - 1):tc + 8, :]


def _rglru_prompt(gate, xr, buf, h0, rg, *, batch, seq, tc=256):
    d = gate.shape[-1]
    width = rg["cw"].shape[0]
    g3, x3 = gate.reshape(batch, seq, d), xr.reshape(batch, seq, d)
    blk = pl.BlockSpec((batch, tc, d), lambda t: (0, t, 0))
    vec = _full_spec((1, d))
    return pl.pallas_call(
        functools.partial(_rglru_prompt_kernel, tc=tc),
        grid=(seq // tc,),
        in_specs=[blk, blk, _full_spec((batch, width - 1, d)), _full_spec((batch, d)), _full_spec((width, d)), vec,
                  _full_spec((d, d)), vec, _full_spec((d, d)), vec, vec],
        out_specs=[blk, pl.BlockSpec((batch, d), lambda t: (0, 0)), pl.BlockSpec((batch, width - 1, d), lambda t: (0, 0, 0))],
        out_shape=[jax.ShapeDtypeStruct((batch, seq, d), BF16), jax.ShapeDtypeStruct((batch, d), F32),
                   jax.ShapeDtypeStruct((batch, width - 1, d), F32)],
        scratch_shapes=[pltpu.VMEM((batch, tc + 8, d), F32), pltpu.VMEM((d // LANES, batch * (tc + 8), LANES), F32),
                        pltpu.VMEM((d // LANES, batch * (tc + 8), LANES), F32), pltpu.VMEM((batch, 8, d), F32),
                        pltpu.VMEM((batch, d), F32)],
        compiler_params=_cparams(1),
        name="rglru_prompt",
    )(g3, x3, buf, h0, rg["cw"], rg["cb"], rg["wa"], rg["ba"], rg["wx"], rg["bx"], rg["lam"])


def _rglru_dec_kernel(g_ref, xr_ref, buf_ref, h0_ref, cw_ref, cb_ref, wa_ref, ba_ref, wx_ref, bx_ref, lam_ref,
                      o_ref, hl_ref):
    width = cw_ref.shape[0]
    xc = cb_ref[...]
    for jj in range(width - 1):
        xc = xc + cw_ref[jj:jj + 1, :] * buf_ref[jj]
    xc = xc + cw_ref[width - 1:width, :] * xr_ref[...]
    a, u = _rglru_gates(xc, wa_ref[...], ba_ref[...], wx_ref[...], bx_ref[...], lam_ref[...])
    h = a * h0_ref[...] + u
    hl_ref[...] = h
    o_ref[...] = (_gelu(g_ref[...]) * h).astype(BF16)


def _rglru_dec(gate, xr, buf_t, h0, rg):
    m, d = gate.shape
    return pl.pallas_call(
        _rglru_dec_kernel,
        out_shape=[jax.ShapeDtypeStruct((m, d), BF16), jax.ShapeDtypeStruct((m, d), F32)],
        name="rglru_dec",
    )(gate, xr, buf_t, h0, rg["cw"], rg["cb"], rg["wa"], rg["ba"], rg["wx"], rg["bx"], rg["lam"])


def _ffn_tile(y1, gf_ref, wup_ref, cw_ref, cb_ref, wdn_ref, conv_prev, n_split):
    d_ff = wdn_ref.shape[0]
    cf = d_ff // n_split
    hn = _rms(y1, gf_ref[...]).astype(BF16)
    out = jnp.zeros(y1.shape, F32)
    gates = []
    for k in range(n_split):
        c0 = k * cf
        g = jnp.dot(hn, wup_ref[:, c0:c0 + cf], preferred_element_type=F32)
        u = jnp.dot(hn, wup_ref[:, d_ff + c0:d_ff + c0 + cf], preferred_element_type=F32)
        g1, g2 = conv_prev(k, g)
        gc = cb_ref[:, c0:c0 + cf] + cw_ref[0:1, c0:c0 + cf] * g2 + cw_ref[1:2, c0:c0 + cf] * g1 \
            + cw_ref[2:3, c0:c0 + cf] * g
        act = (_gelu(gc) * u).astype(BF16)
        out = out + jnp.dot(act, wdn_ref[c0:c0 + cf, :], preferred_element_type=F32)
        gates.append(g)
    return out, gates


def _prompt_conv_prev(gs, carry, fb_ref, nb_ref, tm, cf):
    t = pl.program_id(1)

    @pl.when(t == 0)
    def _load_state():
        carry[...] = jnp.zeros(carry.shape, F32)
        for k in range(carry.shape[0]):
            carry[k, 6:8, :] = fb_ref[0, :, k * cf:(k + 1) * cf]

    def conv_prev(k, g):
        gs[0:8, :] = carry[k]
        gs[8:8 + tm, :] = g
        carry[k] = g[tm - 8:tm, :]
        nb_ref[0, :, k * cf:(k + 1) * cf] = g[tm - 2:tm, :]
        return gs[7:7 + tm, :], gs[6:6 + tm, :]
    return conv_prev


def _mix_ab_tile(y_ref, a_ref, r_ref, woa_ref, wob_ref):
    return y_ref[0] + jnp.dot(a_ref[0], woa_ref[...], preferred_element_type=F32) \
        + jnp.dot(r_ref[0], wob_ref[...], preferred_element_type=F32)


def _post_ab_prompt_kernel(y_ref, a_ref, r_ref, fb_ref, woa_ref, wob_ref, gf_ref, wup_ref, cw_ref, cb_ref, wdn_ref,
                           o_ref, nb_ref, gs, carry, *, tm, n_split):
    y1 = _mix_ab_tile(y_ref, a_ref, r_ref, woa_ref, wob_ref)
    cf = wdn_ref.shape[0] // n_split
    out, _ = _ffn_tile(y1, gf_ref, wup_ref, cw_ref, cb_ref, wdn_ref,
                       _prompt_conv_prev(gs, carry, fb_ref, nb_ref, tm, cf), n_split)
    o_ref[0] = y1 + out


def _ffn_specs(d, d_ff):
    return [_full_spec((1, d)), _full_spec((d, 2 * d_ff)), _full_spec((3, d_ff)), _full_spec((1, d_ff)),
            _full_spec((d_ff, d))]


def _post_ab_prompt(y, attn, rgo, fbuf, wo_a, wo_b, ffn, *, tm=512, n_split=2):
    batch, seq, d = y.shape
    d_ff = ffn["wdn"].shape[0]
    cf = d_ff // n_split
    blk = lambda n: pl.BlockSpec((1, tm, n), lambda b, t: (b, t, 0))
    fb = pl.BlockSpec((1, 2, d_ff), lambda b, t: (b, 0, 0))
    return pl.pallas_call(
        functools.partial(_post_ab_prompt_kernel, tm=tm, n_split=n_split),
        grid=(batch, seq // tm),
        in_specs=[blk(d), blk(attn.shape[-1]), blk(rgo.shape[-1]), fb, _full_spec(wo_a.shape), _full_spec(wo_b.shape)]
        + _ffn_specs(d, d_ff),
        out_specs=[blk(d), fb],
        out_shape=[jax.ShapeDtypeStruct((batch, seq, d), F32), jax.ShapeDtypeStruct((batch, 2, d_ff), F32)],
        scratch_shapes=[pltpu.VMEM((tm + 8, cf), F32), pltpu.VMEM((n_split, 8, cf), F32)],
        compiler_params=_cparams(2),
        name="post_ab_prompt",
    )(y, attn, rgo, fbuf, wo_a, wo_b, ffn["g"], ffn["wup"], ffn["cw"], ffn["cb"], ffn["wdn"])


def _dec_conv_prev(fb_ref, cf):
    def conv_prev(k, g):
        return fb_ref[1, :, k * cf:(k + 1) * cf], fb_ref[0, :, k * cf:(k + 1) * cf]
    return conv_prev


def _post_ab_dec_kernel(y_ref, a_ref, r_ref, fb_ref, woa_ref, wob_ref, gf_ref, wup_ref, cw_ref, cb_ref, wdn_ref,
                        o_ref, g_ref, *, n_split):
    y1 = y_ref[...] + jnp.dot(a_ref[...], woa_ref[...], preferred_element_type=F32) \
        + jnp.dot(r_ref[...], wob_ref[...], preferred_element_type=F32)
    cf = wdn_ref.shape[0] // n_split
    out, gates = _ffn_tile(y1, gf_ref, wup_ref, cw_ref, cb_ref, wdn_ref, _dec_conv_prev(fb_ref, cf), n_split)
    o_ref[...] = y1 + out
    for k, g in enumerate(gates):
        g_ref[:, k * cf:(k + 1) * cf] = g


def _post_ab_dec(y, attn, rgo, fbuf_t, wo_a, wo_b, ffn, *, n_split=2):
    m, d = y.shape
    d_ff = ffn["wdn"].shape[0]
    return pl.pallas_call(
        functools.partial(_post_ab_dec_kernel, n_split=n_split),
        out_shape=[jax.ShapeDtypeStruct((m, d), F32), jax.ShapeDtypeStruct((m, d_ff), F32)],
        compiler_params=pltpu.CompilerParams(vmem_limit_bytes=VMEM_LIMIT),
        name="post_ab_dec",
    )(y, attn, rgo, fbuf_t, wo_a, wo_b, ffn["g"], ffn["wup"], ffn["cw"], ffn["cb"], ffn["wdn"])


def _gmlp_in(y, gm_ref, win_ref, bin_ref, sn_ref):
    d_c = win_ref.shape[1] // 2
    z = _gelu(_dot(_rms(y, gm_ref[...]), win_ref[...]) + bin_ref[...])
    return z[:, :d_c], _rms(z[:, d_c:], sn_ref[...])


def _layer_c_prompt_kernel(y_ref, fb_ref, gm_ref, win_ref, bin_ref, sn_ref, sw_ref, sbt_ref, woc_ref,
                           gf_ref, wup_ref, cw_ref, cb_ref, wdn_ref, gfin_ref,
                           o_ref, nb_ref, gs, carry, *, tm, n_split):
    y = y_ref[0]
    u, v = _gmlp_in(y, gm_ref, win_ref, bin_ref, sn_ref)
    vb = v.astype(BF16)
    n_groups = sw_ref.shape[0]
    tril = lax.broadcasted_iota(I32, (CHUNK, CHUNK), 0) >= lax.broadcasted_iota(I32, (CHUNK, CHUNK), 1)
    wm = [jnp.where(tril, sw_ref[gi], 0.0).astype(BF16) for gi in range(n_groups)]
    rows = []
    for r in range(tm // CHUNK):
        cols = []
        for gi in range(n_groups):
            mixed = jnp.dot(wm[gi], vb[r * CHUNK:(r + 1) * CHUNK, gi * LANES:(gi + 1) * LANES],
                            preferred_element_type=F32)
            cols.append(mixed + sbt_ref[:, gi:gi + 1])
        rows.append(jnp.concatenate(cols, axis=1))
    gated = u * jnp.concatenate(rows, axis=0)
    y1 = y + _dot(gated, woc_ref[...])
    cf = wdn_ref.shape[0] // n_split
    out, _ = _ffn_tile(y1, gf_ref, wup_ref, cw_ref, cb_ref, wdn_ref,
                       _prompt_conv_prev(gs, carry, fb_ref, nb_ref, tm, cf), n_split)
    o_ref[0] = _rms(y1 + out, gfin_ref[...])


def _layer_c_prompt(y, fbuf, cp, ffn, g_final, *, tm=512, n_split=2):
    batch, seq, d = y.shape
    d_ff = ffn["wdn"].shape[0]
    cf = d_ff // n_split
    blk = pl.BlockSpec((1, tm, d), lambda b, t: (b, t, 0))
    fb = pl.BlockSpec((1, 2, d_ff), lambda b, t: (b, 0, 0))
    consts = [cp["g"], cp["win"], cp["bin"], cp["sn"], cp["sw"], cp["sbt"], cp["woc"],
              ffn["g"], ffn["wup"], ffn["cw"], ffn["cb"], ffn["wdn"], g_final]
    return pl.pallas_call(
        functools.partial(_layer_c_prompt_kernel, tm=tm, n_split=n_split),
        grid=(batch, seq // tm),
        in_specs=[blk, fb] + [_full_spec(c.shape) for c in consts],
        out_specs=[blk, fb],
        out_shape=[jax.ShapeDtypeStruct((batch, seq, d), F32), jax.ShapeDtypeStruct((batch, 2, d_ff), F32)],
        scratch_shapes=[pltpu.VMEM((tm + 8, cf), F32), pltpu.VMEM((n_split, 8, cf), F32)],
        compiler_params=_cparams(2),
        name="layer_c_prompt",
    )(y, fbuf, *consts)


def _layer_c_dec_kernel(y_ref, fb_ref, gm_ref, win_ref, bin_ref, sn_ref, sw0_ref, sb0_ref, woc_ref,
                        gf_ref, wup_ref, cw_ref, cb_ref, wdn_ref, gfin_ref, o_ref, g_ref, v_ref, *, n_split):
    y = y_ref[...]
    u, v = _gmlp_in(y, gm_ref, win_ref, bin_ref, sn_ref)
    v_ref[...] = v
    y1 = y + _dot(u * (sw0_ref[...] * v + sb0_ref[...]), woc_ref[...])
    cf = wdn_ref.shape[0] // n_split
    out, gates = _ffn_tile(y1, gf_ref, wup_ref, cw_ref, cb_ref, wdn_ref, _dec_conv_prev(fb_ref, cf), n_split)
    o_ref[...] = _rms(y1 + out, gfin_ref[...])
    for k, g in enumerate(gates):
        g_ref[:, k * cf:(k + 1) * cf] = g


def _layer_c_dec(y, fbuf_t, cp, ffn, g_final, *, n_split=2):
    m, d = y.shape
    d_ff = ffn["wdn"].shape[0]
    d_c = cp["woc"].shape[0]
    return pl.pallas_call(
        functools.partial(_layer_c_dec_kernel, n_split=n_split),
        out_shape=[jax.ShapeDtypeStruct((m, d), F32), jax.ShapeDtypeStruct((m, d_ff), F32),
                   jax.ShapeDtypeStruct((m, d_c), F32)],
        compiler_params=pltpu.CompilerParams(vmem_limit_bytes=VMEM_LIMIT),
        name="layer_c_dec",
    )(y, fbuf_t, cp["g"], cp["win"], cp["bin"], cp["sn"], cp["sw0"], cp["sb0"], cp["woc"],
      ffn["g"], ffn["wup"], ffn["cw"], ffn["cb"], ffn["wdn"], g_final)


def _start_page_copies(src_ref, pt_ref, b, dst_ref, sem, n_pages, page):
    def body(pg, carry):
        col = pl.multiple_of(pg * page, page)
        pltpu.make_async_copy(src_ref.at[pt_ref[b, pg]], dst_ref.at[:, pl.ds(col, page)], sem).start()
        return carry
    lax.fori_loop(0, n_pages, body, 0, unroll=8)


def _wait_page_copies(dst_ref, sem):
    pltpu.make_async_copy(dst_ref, dst_ref, sem).wait()


def _dec_score_kernel(pt_ref, qi_ref, wi_ref, ixn_ref, cik_ref, keys_ref, knew_ref, ibuf, sems, *, n_pages, page):
    b = pl.program_id(0)
    nb = pl.num_programs(0)
    slot = lax.rem(b, 2)

    def start(bb, sl):
        _start_page_copies(cik_ref, pt_ref, bb, ibuf.at[sl], sems.at[sl], n_pages, page)

    @pl.when(b == 0)
    def _first():
        start(0, 0)

    @pl.when(b + 1 < nb)
    def _prefetch_next():
        start(b + 1, 1 - slot)

    _wait_page_copies(ibuf.at[slot], sems.at[slot])
    qi = qi_ref[0].astype(BF16)
    wi = wi_ref[0]
    s = jnp.dot(qi, ibuf[slot].astype(BF16), preferred_element_type=F32)
    sc = jnp.sum(jnp.maximum(s, 0.0) * wi, axis=0, keepdims=True) * IDX_HEADS ** -0.5
    keys_ref[0] = sc
    kin = ixn_ref[0][:, 0:IDX_DIM]
    sn = jnp.sum(qi_ref[0] * kin, axis=1, keepdims=True)
    scn = jnp.sum(jnp.maximum(sn, 0.0) * wi, axis=0, keepdims=True) * IDX_HEADS ** -0.5
    knew_ref[0] = jnp.broadcast_to(scn, (1, LANES))


def _dec_scores(page_table, qi3, wi3, ix3, cik_t, *, page):
    db, n_pages = page_table.shape
    past = n_pages * page
    return pl.pallas_call(
        functools.partial(_dec_score_kernel, n_pages=n_pages, page=page),
        grid_spec=pltpu.PrefetchScalarGridSpec(
            num_scalar_prefetch=1,
            grid=(db,),
            in_specs=[pl.BlockSpec((1, IDX_HEADS, IDX_DIM), lambda b, pt: (b, 0, 0)),
                      pl.BlockSpec((1, IDX_HEADS, 1), lambda b, pt: (b, 0, 0)),
                      pl.BlockSpec((1, 1, LANES), lambda b, pt: (b, 0, 0)),
                      pl.BlockSpec(memory_space=pl.ANY)],
            out_specs=[pl.BlockSpec((1, 1, past), lambda b, pt: (b, 0, 0)),
                       pl.BlockSpec((1, 1, LANES), lambda b, pt: (b, 0, 0))],
            scratch_shapes=[pltpu.VMEM((2, IDX_DIM, past), F32), pltpu.SemaphoreType.DMA((2,))]),
        out_shape=[jax.ShapeDtypeStruct((db, 1, past), F32), jax.ShapeDtypeStruct((db, 1, LANES), F32)],
        compiler_params=_cparams(1),
        name="dec_scores",
    )(page_table, qi3, wi3, ix3, cik_t)


def _dec_select_kernel(sc_ref, scn_ref, so_ref, sno_ref, thr_ref, *, topk):
    past = sc_ref.shape[1]
    lane0 = lax.broadcasted_iota(I32, scn_ref.shape, 1) == 0
    sc = jnp.concatenate([sc_ref[...], jnp.where(lane0, scn_ref[...], -jnp.inf)], axis=1)
    kk = _float_key(sc)
    kf = jnp.float32(topk)

    def count(pred):
        return jnp.sum(jnp.where(pred, 1.0, 0.0), axis=1, keepdims=True)

    def search_body(i, ans):
        cand = ans | jnp.left_shift(jnp.int32(1), 31 - i)
        return jnp.where(count(kk >= (cand ^ jnp.int32(INT_MIN))) >= kf, cand, ans)
    ans = lax.fori_loop(0, 32, search_body, jnp.zeros((sc.shape[0], 1), I32))
    thr = _key_float(ans ^ jnp.int32(INT_MIN))
    counts = lambda t: (count(sc >= t), count(sc > t))
    thr, cge, cgt = _walk_to_kth(lambda: sc, thr, *counts(thr), kf, counts, 4 * topk)
    need = kf - cgt
    big = jnp.int32(4 * past)
    eqcol = jnp.where(sc == thr, lax.broadcasted_iota(I32, sc.shape, 1), big)
    nbits = int(math.log2(past)) + 1

    def tie_body(i, best):
        cand = best | jnp.left_shift(jnp.int32(1), nbits - 1 - i)
        return jnp.where(count(eqcol < cand) < need, cand, best)
    last = lax.fori_loop(0, nbits, tie_body, jnp.zeros((sc.shape[0], 1), I32))
    sc = jnp.where((eqcol > last) & (eqcol < big), -jnp.inf, sc)
    so_ref[...] = sc[:, :past]
    sno_ref[...] = sc[:, past:]
    thr_ref[...] = jnp.broadcast_to(thr, thr_ref.shape)


def _dec_select(scores, snew, *, topk):
    db, past = scores.shape
    assert past + 1 >= topk
    return pl.pallas_call(
        functools.partial(_dec_select_kernel, topk=topk),
        out_shape=[jax.ShapeDtypeStruct((db, past), F32), jax.ShapeDtypeStruct((db, LANES), F32),
                   jax.ShapeDtypeStruct((db, LANES), F32)],
        name="dec_select",
    )(scores, snew)


def _dec_attn_kernel(pt_ref, q_ref, kvn_ref, keys_ref, knew_ref, thr_ref, bias_ref, ck_ref, cv_ref, o_ref,
                     kbuf, vbuf, sems, *, n_pages, page):
    b = pl.program_id(0)
    nb = pl.num_programs(0)
    slot = lax.rem(b, 2)
    past = n_pages * page

    def start(bb, sl):
        _start_page_copies(ck_ref, pt_ref, bb, kbuf.at[sl], sems.at[0, sl], n_pages, page)
        _start_page_copies(cv_ref, pt_ref, bb, vbuf.at[sl], sems.at[1, sl], n_pages, page)

    @pl.when(b == 0)
    def _first():
        start(0, 0)

    @pl.when(b + 1 < nb)
    def _prefetch_next():
        start(b + 1, 1 - slot)

    _wait_page_copies(kbuf.at[slot], sems.at[0, slot])
    _wait_page_copies(vbuf.at[slot], sems.at[1, slot])
    qm = q_ref[0]
    thr = thr_ref[0][:, 0:1]
    sel = keys_ref[0] >= thr
    sel_new = knew_ref[0][:, 0:1] >= thr
    bias = bias_ref[...]
    far, last, bnew = bias[:, LANES:LANES + 1], bias[:, 0:page], bias[:, LANES + 1:LANES + 2]
    lg = jnp.dot(qm.astype(BF16), kbuf[slot].astype(BF16), preferred_element_type=F32)
    lane = lax.broadcasted_iota(I32, lg.shape, 1)
    lastp = jnp.concatenate([jnp.zeros((N_HEADS, past - page), F32), last], axis=1)
    lg = jnp.where(sel, lg + jnp.where(lane >= past - page, lastp, far), -jnp.inf)
    kvn = kvn_ref[0]
    lgn = jnp.sum(qm * kvn[:, 0:LANES], axis=1, keepdims=True) + bnew
    lgn = jnp.where(sel_new, lgn, -jnp.inf)
    m = jnp.maximum(jnp.max(lg, axis=1, keepdims=True), lgn)
    e = jnp.exp(lg - m)
    en = jnp.exp(lgn - m)
    den = jnp.sum(e, axis=1, keepdims=True) + en
    pv = _dot_nt(e.astype(BF16), vbuf[slot].astype(BF16)) + en * kvn[:, LANES:2 * LANES]
    pv = pv / den
    lo = lax.broadcasted_iota(I32, (1, LANES), 1) < HEAD_DIM
    o_ref[0] = jnp.concatenate([jnp.where(lo, pv[2 * p:2 * p + 1], pv[2 * p + 1:2 * p + 2]) for p in range(4)],
                               axis=1).astype(BF16)


def _dec_attn(page_table, qm, kvn, keys, knew, thr, bias_dec, ck_t, cv_t, *, page):
    db, n_pages = page_table.shape
    past = n_pages * page
    row = lambda n: pl.BlockSpec((1, 1, n), lambda b, pt: (b, 0, 0))
    return pl.pallas_call(
        functools.partial(_dec_attn_kernel, n_pages=n_pages, page=page),
        grid_spec=pltpu.PrefetchScalarGridSpec(
            num_scalar_prefetch=1,
            grid=(db,),
            in_specs=[pl.BlockSpec((1, N_HEADS, LANES), lambda b, pt: (b, 0, 0)), row(2 * LANES), row(past),
                      row(LANES), row(LANES), pl.BlockSpec((N_HEADS, 2 * LANES), lambda b, pt: (0, 0)),
                      pl.BlockSpec(memory_space=pl.ANY), pl.BlockSpec(memory_space=pl.ANY)],
            out_specs=row(4 * LANES),
            scratch_shapes=[pltpu.VMEM((2, 2 * HEAD_DIM, past), F32), pltpu.VMEM((2, 2 * HEAD_DIM, past), F32),
                            pltpu.SemaphoreType.DMA((2, 2))]),
        out_shape=jax.ShapeDtypeStruct((db, 1, 4 * LANES), BF16),
        compiler_params=_cparams(1),
        name="dec_attn",
    )(page_table, qm, kvn, keys, knew, thr, bias_dec, ck_t, cv_t)


def _prep_in_ab(w):
    d = w.shape[0]
    nq, nkv = N_HEADS * HEAD_DIM, N_KV_HEADS * HEAD_DIM
    offs = np.cumsum([nq, nkv, nkv, IDX_HEADS * IDX_DIM, IDX_DIM, IDX_HEADS, 512])
    q, k, v, qi, ki, wi, g, xr = jnp.split(w, offs.tolist(), axis=1)
    q = q.reshape(d, N_HEADS, HEAD_DIM)[:, np.array(HEAD_PERM), :].reshape(d, nq)
    pad = jnp.zeros((d, _C_G - _C_IX - IDX_DIM - IDX_HEADS), w.dtype)
    return jnp.concatenate([q, k, v, qi, ki, wi, pad, g, xr], axis=1).astype(BF16)


def _block_diag(w):
    n, c, _ = w.shape
    return (jnp.eye(n, dtype=w.dtype)[:, None, :, None] * w[:, :, None, :]).reshape(n * c, n * c).astype(BF16)


def _ffn_params(layer, norm_ffn, w_up, cw, cb, w_down):
    return {"g": norm_ffn[layer][None], "wup": w_up[layer].astype(BF16), "cw": cw[layer], "cb": cb[layer][None],
            "wdn": w_down[layer].astype(BF16)}


def kernel(x_prompt, x_sample, cache_k, cache_v, cache_idx_k, state_rglru_h, state_rglru_conv, state_ffn_conv,
           page_table, norm_mix, norm_ffn, norm_final, rel_bias, w_in_ab, w_out_ab, rg_conv_w, rg_conv_b,
           rg_wa, rg_ba, rg_wx, rg_bx, rg_lambda, w_in_c, b_in_c, sgu_norm, sgu_w, sgu_b, w_out_c,
           ffn_w_up, ffn_conv_w, ffn_conv_b, ffn_w_down):
    batch, seq, d = x_prompt.shape
    db = x_sample.shape[0]
    page = cache_k.shape[2]
    d_a = N_HEADS * HEAD_DIM
    d_b = rg_conv_w.shape[-1]
    d_ff = ffn_w_down.shape[1]
    assert x_sample.shape[1] == 1 and seq % 512 == 0 and page == LANES and w_in_ab.shape[0] == 1

    w_in0 = _prep_in_ab(w_in_ab[0])
    wo = w_out_ab[0]
    wo_a = wo[:d_a].reshape(N_HEADS, HEAD_DIM, d)[np.array(HEAD_PERM)].reshape(d_a, d).astype(BF16)
    wo_b = wo[d_a:].astype(BF16)
    rg = {"cw": rg_conv_w[0], "cb": rg_conv_b[0][None], "wa": _block_diag(rg_wa[0]), "ba": rg_ba[0][None],
          "wx": _block_diag(rg_wx[0]), "bx": rg_bx[0][None], "lam": rg_lambda[0][None]}
    ffn0 = _ffn_params(0, norm_ffn, ffn_w_up, ffn_conv_w, ffn_conv_b, ffn_w_down)
    ffn1 = _ffn_params(1, norm_ffn, ffn_w_up, ffn_conv_w, ffn_conv_b, ffn_w_down)
    cp = {"g": norm_mix[1][None], "win": w_in_c[0].astype(BF16), "bin": b_in_c[0][None], "sn": sgu_norm[0][None],
          "sw": sgu_w[0], "sbt": sgu_b[0].T, "woc": w_out_c[0].astype(BF16),
          "sw0": jnp.repeat(sgu_w[0][:, 0, 0], d // sgu_w.shape[1])[None],
          "sb0": jnp.repeat(sgu_b[0][:, 0], d // sgu_w.shape[1])[None]}
    g_mix0 = norm_mix[0][None]
    g_final = norm_final[None]
    bias_st, bias_dec = _bias_tables(rel_bias, page)

    xp = x_prompt.reshape(batch * seq, d)
    q_st, qi_st, kv_p, ix_p, gate_p, xr_p = _inproj(xp, g_mix0, w_in0, stack=True, tm=512)
    attn_p = _attn_prompt(q_st, qi_st, ix_p, kv_p, bias_st, batch=batch, seq=seq)
    rg_p, h_p, cbuf_p = _rglru_prompt(gate_p, xr_p, jnp.zeros((batch, rg["cw"].shape[0] - 1, d_b), F32),
                                      jnp.zeros((batch, d_b), F32), rg, batch=batch, seq=seq)
    zero_fb = jnp.zeros((batch, 2, d_ff), F32)
    y1_p, fb0_p = _post_ab_prompt(x_prompt, attn_p, rg_p, zero_fb, wo_a, wo_b, ffn0)
    y_p, fb1_p = _layer_c_prompt(y1_p, zero_fb, cp, ffn1, g_final)

    xs = x_sample.reshape(db, d)
    qm_s, qi_s, kv_s, ix_s, gate_s, xr_s = _inproj(xs, g_mix0, w_in0, stack=False, tm=db)
    cik_t = jnp.transpose(cache_idx_k[0], (0, 2, 1))
    ck_t = jnp.transpose(cache_k[0], (0, 2, 3, 1)).reshape(-1, 2 * HEAD_DIM, page)
    cv_t = jnp.transpose(cache_v[0], (0, 2, 3, 1)).reshape(-1, 2 * HEAD_DIM, page)
    topk_s = min(TOPK_MAX, (page_table.shape[1] * page + 1) // 4)
    keys_s, knew_s = _dec_scores(page_table, qi_s.reshape(db, IDX_HEADS, IDX_DIM),
                                 ix_s[:, IDX_DIM:IDX_DIM + IDX_HEADS].reshape(db, IDX_HEADS, 1),
                                 ix_s.reshape(db, 1, LANES), cik_t, page=page)
    keys_s, knew_s, thr_s = _dec_select(keys_s.reshape(db, -1), knew_s.reshape(db, LANES), topk=topk_s)
    attn_s = _dec_attn(page_table, jnp.transpose(qm_s, (1, 0, 2)), kv_s.reshape(db, 1, 2 * LANES),
                       keys_s.reshape(db, 1, -1), knew_s.reshape(db, 1, LANES), thr_s.reshape(db, 1, LANES),
                       bias_dec, ck_t, cv_t, page=page).reshape(db, d_a)
    cbuf_s_in = state_rglru_conv[0]
    rg_s, h_s = _rglru_dec(gate_s, xr_s, jnp.transpose(cbuf_s_in, (1, 0, 2)), state_rglru_h[0], rg)
    y1_s, g0_s = _post_ab_dec(xs, attn_s, rg_s, jnp.transpose(state_ffn_conv[0], (1, 0, 2)), wo_a, wo_b, ffn0)
    y_s, g1_s, v_s = _layer_c_dec(y1_s, jnp.transpose(state_ffn_conv[1], (1, 0, 2)), cp, ffn1, g_final)

    kv4 = kv_p.reshape(batch, seq, 2, N_KV_HEADS, HEAD_DIM)
    kvs = kv_s.reshape(db, 1, 2, N_KV_HEADS, HEAD_DIM)
    fbuf_s = lambda layer, g: jnp.concatenate([state_ffn_conv[layer][:, 1:], g[:, None]], axis=1)
    return (y_p, y_s.reshape(db, 1, d),
            kv4[None, :, :, 0], kv4[None, :, :, 1], ix_p.reshape(batch, seq, LANES)[None, :, :, :IDX_DIM],
            kvs[None, :, :, 0], kvs[None, :, :, 1], ix_s.reshape(db, 1, LANES)[None, :, :, :IDX_DIM],
            h_p.reshape(batch, d_b)[None], cbuf_p[None],
            h_s[None], jnp.concatenate([cbuf_s_in[:, 1:], xr_s[:, None]], axis=1)[None],
            v_s.reshape(db, 1, -1)[None],
            jnp.stack([fb0_p, fb1_p]), jnp.stack([fbuf_s(0, g0_s), fbuf_s(1, g1_s)]))
```

```python
import functools
import math

import numpy as np
import jax
import jax.numpy as jnp
from jax import lax
from jax.experimental import pallas as pl
from jax.experimental.pallas import tpu as pltpu

F32 = jnp.float32
BF16 = jnp.bfloat16
I32 = jnp.int32

N_HEADS = 8
HEAD_DIM = 64
N_KV_HEADS = 2
Q_PER_KV = N_HEADS // N_KV_HEADS
IDX_HEADS = 8
IDX_DIM = 64
TOPK_MAX = 256
N_BUCKETS = 32
REL_MAX_EXACT = N_BUCKETS // 2
REL_MAX_DIST = 128
RG_C = 8.0
CHUNK = 128
EPS = 1e-6

LANES = 128
QB = 128
INT_MIN = -(2 ** 31)
KEY_MIN_FINITE = INT_MIN + 0x800000
NEG_MAX = float(np.finfo(np.float32).min)
HEAD_PERM = (0, 4, 1, 5, 2, 6, 3, 7)
VMEM_LIMIT = 56 * 1024 * 1024


def _cparams(n_grid):
    return pltpu.CompilerParams(dimension_semantics=("arbitrary",) * n_grid, vmem_limit_bytes=VMEM_LIMIT)


def _full_spec(shape):
    nd = len(shape)
    return pl.BlockSpec(shape, lambda *_: (0,) * nd, pipeline_mode=pl.Buffered(1))


def _whole_spec(shape):
    nd = len(shape)
    return pl.BlockSpec(shape, lambda *_: (0,) * nd)


def _rms(x, g):
    return x * lax.rsqrt(jnp.mean(x * x, axis=-1, keepdims=True) + EPS) * g


def _gelu(x):
    return x * (0.5 * (1.0 + jnp.tanh(math.sqrt(2.0 / math.pi) * (x + 0.044715 * (x * x * x)))))


def _sigmoid(x):
    return 1.0 / (1.0 + jnp.exp(-x))


def _softplus(x):
    return jnp.maximum(x, 0.0) + jnp.log(1.0 + jnp.exp(-jnp.abs(x)))


def _dot(a, b):
    return jnp.dot(a.astype(BF16), b, preferred_element_type=F32)


def _dot_nt(a, b):
    return lax.dot_general(a, b, (((1,), (1,)), ((), ())), preferred_element_type=F32)


def _float_key(x):
    bits = pltpu.bitcast(x, I32)
    key = jnp.where(bits < 0, bits ^ jnp.int32(0x7FFFFFFF), bits)
    return jnp.where(bits == jnp.int32(INT_MIN), jnp.int32(0), key)


def _key_float(key):
    key = jnp.maximum(key, jnp.int32(KEY_MIN_FINITE))
    return pltpu.bitcast(jnp.where(key < 0, key ^ jnp.int32(0x7FFFFFFF), key), F32)


def _walk_to_kth(scores, t, cge, cgt, kf, count_ge_gt, max_steps):
    axis = 0 if t.shape[0] == 1 else 1

    def settled(t, cge, cgt):
        return (cgt < kf) & ((cge >= kf) | (t <= NEG_MAX))

    def unsettled(t, cge, cgt):
        return jnp.max(jnp.where(settled(t, cge, cgt), 0.0, 1.0)) > 0.0

    def body(st):
        t, cge, cgt, it = st
        sc = scores()
        below = jnp.max(jnp.where(sc < t, sc, -jnp.inf), axis=axis, keepdims=True)
        above = jnp.min(jnp.where(sc > t, sc, jnp.inf), axis=axis, keepdims=True)
        t = jnp.where(cgt >= kf, above, jnp.where((cge < kf) & (t > NEG_MAX), jnp.maximum(below, NEG_MAX), t))
        cge, cgt = count_ge_gt(t)
        return t, cge, cgt, it + 1

    t, cge, cgt, _ = lax.while_loop(lambda st: unsettled(st[0], st[1], st[2]) & (st[3] < max_steps), body,
                                    (t, cge, cgt, jnp.int32(0)))
    return t, cge, cgt


def _t5_bucket_np(n):
    n = np.maximum(n, 0)
    nf = np.maximum(n, 1).astype(np.float32)
    large = REL_MAX_EXACT + (np.log(nf / np.float32(REL_MAX_EXACT)) / np.float32(math.log(REL_MAX_DIST / REL_MAX_EXACT))
                             * np.float32(N_BUCKETS - REL_MAX_EXACT)).astype(np.int32)
    large = np.minimum(large, N_BUCKETS - 1)
    return np.where(n < REL_MAX_EXACT, n, large).astype(np.int32)


_C_Q, _C_KV, _C_QI, _C_IX, _C_G, _C_X, _C_END = 0, 512, 768, 1280, 1408, 1920, 2432


def _inproj_kernel(x_ref, g_ref, w_ref, q_ref, qi_ref, kv_ref, ix_ref, gate_ref, xr_ref, *, stack):
    hn = _rms(x_ref[...], g_ref[...])
    z = _dot(hn, w_ref[...])
    q = z[:, _C_Q:_C_KV] * HEAD_DIM ** -0.5
    qi = z[:, _C_QI:_C_IX] * IDX_DIM ** -0.5
    kv_ref[...] = z[:, _C_KV:_C_QI]
    ix_ref[...] = z[:, _C_IX:_C_G]
    gate_ref[...] = z[:, _C_G:_C_X]
    xr_ref[...] = z[:, _C_X:_C_END]
    if stack:
        qb, qib = q.astype(BF16), qi.astype(BF16)
        for r in range(q.shape[0] // QB):
            for p in range(4):
                q_ref[r, p * QB:(p + 1) * QB, :] = qb[r * QB:(r + 1) * QB, p * LANES:(p + 1) * LANES]
                qi_ref[r, p * QB:(p + 1) * QB, :] = qib[r * QB:(r + 1) * QB, p * LANES:(p + 1) * LANES]
    else:
        lo = lax.broadcasted_iota(I32, (q.shape[0], LANES), 1) < HEAD_DIM
        for p in range(4):
            qp = q[:, p * LANES:(p + 1) * LANES]
            q_ref[2 * p] = jnp.where(lo, qp, 0.0)
            q_ref[2 * p + 1] = jnp.where(lo, 0.0, qp)
        qi_ref[...] = qi


def _inproj(x2d, g, w, *, stack, tm):
    m, d = x2d.shape
    if stack:
        q_shape, q_spec = (m // QB, 4 * QB, LANES), pl.BlockSpec((tm // QB, 4 * QB, LANES), lambda i: (i, 0, 0))
        qi_shape, qi_spec, qdt = q_shape, q_spec, BF16
    else:
        q_shape, q_spec = (N_HEADS, m, LANES), pl.BlockSpec((N_HEADS, tm, LANES), lambda i: (0, i, 0))
        qi_shape, qi_spec, qdt = (m, 512), pl.BlockSpec((tm, 512), lambda i: (i, 0)), F32
    row = lambda n: pl.BlockSpec((tm, n), lambda i: (i, 0))
    return pl.pallas_call(
        functools.partial(_inproj_kernel, stack=stack),
        grid=(m // tm,),
        in_specs=[row(d), _full_spec((1, d)), _full_spec(w.shape)],
        out_specs=[q_spec, qi_spec, row(256), row(128), row(512), row(512)],
        out_shape=[jax.ShapeDtypeStruct(q_shape, qdt), jax.ShapeDtypeStruct(qi_shape, qdt),
                   jax.ShapeDtypeStruct((m, 256), F32), jax.ShapeDtypeStruct((m, 128), F32),
                   jax.ShapeDtypeStruct((m, 512), F32), jax.ShapeDtypeStruct((m, 512), F32)],
        compiler_params=_cparams(1),
        name="inproj_stack" if stack else "inproj_dec",
    )(x2d, g, w)


def _bias_kernel(rb_ref, bk_ref, bkd_ref, o_ref, od_ref):
    for d in range(3):
        bk = bk_ref[d]
        for p in range(4):
            for a in range(2):
                h = p + 4 * a
                acc = jnp.zeros((QB, LANES), F32)
                for b in range(N_BUCKETS):
                    acc = jnp.where(bk == b, rb_ref[b, h], acc)
                o_ref[d, a * QB:(a + 1) * QB, p * LANES:(p + 1) * LANES] = acc
    bkd = bkd_ref[...]
    rowi = lax.broadcasted_iota(I32, (N_HEADS, 2 * LANES), 0)
    acc = jnp.zeros((N_HEADS, 2 * LANES), F32)
    for r in range(N_HEADS):
        h = r // 2 + 4 * (r % 2)
        for b in range(N_BUCKETS):
            acc = jnp.where((rowi == r) & (bkd == b), rb_ref[b, h], acc)
    od_ref[...] = acc


def _bias_tables(rel_bias, page):
    key = np.arange(QB)[:, None]
    qry = np.arange(LANES)[None, :]
    bk = np.stack([_t5_bucket_np(d * QB + qry - key) for d in range(3)])
    assert (_t5_bucket_np(np.arange(2 * QB + 1 - LANES, 4 * QB)) == N_BUCKETS - 1).all()
    assert (_t5_bucket_np(np.arange(page, 8 * page)) == N_BUCKETS - 1).all()
    dec = np.zeros((2 * LANES,), np.int64)
    dec[:page] = page - np.arange(page)
    dec[LANES] = 2 * REL_MAX_DIST
    dec[LANES + 1] = 0
    bkd = np.broadcast_to(_t5_bucket_np(dec)[None, :], (N_HEADS, 2 * LANES))
    return pl.pallas_call(
        _bias_kernel,
        in_specs=[pl.BlockSpec(memory_space=pltpu.SMEM), pl.BlockSpec(memory_space=pltpu.VMEM),
                  pl.BlockSpec(memory_space=pltpu.VMEM)],
        out_shape=[jax.ShapeDtypeStruct((3, 2 * QB, 4 * LANES), F32), jax.ShapeDtypeStruct((N_HEADS, 2 * LANES), F32)],
        name="bias_tables",
    )(rel_bias, jnp.asarray(bk, I32), jnp.asarray(bkd, I32))


def _search_widths(n_chunks):
    cuts = sorted({min(c, n_chunks) for c in (2, 4, 8, 12, 16)} | {n_chunks})
    return [c for c in cuts if c <= n_chunks]


def _attn_prompt_kernel(q_ref, qi_ref, ixq_ref, ixk_ref, kv_ref, bias_ref, o_ref,
                        kblk, vblk_t, kiblk, keys, scores, logits, acc, thr_ref, cge_ref, cgt_ref, *, n_chunks, topk):
    j = pl.program_id(1)
    lane = lax.broadcasted_iota(I32, (QB, LANES), 1)
    row = lax.broadcasted_iota(I32, (QB, LANES), 0)
    lo = lane < HEAD_DIM
    blocks = [(a, p) for a in range(2) for p in range(4)]
    rs = lambda a: slice(a * QB, (a + 1) * QB)
    cs = lambda p: slice(p * LANES, (p + 1) * LANES)
    chunk = lambda c: pl.ds(pl.multiple_of(c * QB, QB), QB)

    @pl.when(j == 0)
    def _build_block_diagonal_keys():
        def body(c, carry):
            kc = kv_ref[0, chunk(c), 0:LANES]
            vt = kv_ref[0, chunk(c), LANES:2 * LANES].T
            kia = jnp.where(lo, ixk_ref[0, chunk(c), :], 0.0)
            kblk[c, 0:QB, :] = jnp.where(lo, kc, 0.0).astype(BF16)
            kblk[c, QB:2 * QB, :] = jnp.where(lo, 0.0, kc).astype(BF16)
            vblk_t[c, :, 0:QB] = jnp.where(row < HEAD_DIM, vt, 0.0).astype(BF16)
            vblk_t[c, :, QB:2 * QB] = jnp.where(row < HEAD_DIM, 0.0, vt).astype(BF16)
            kiblk[c, 0:QB, :] = kia.astype(BF16)
            kiblk[c, QB:2 * QB, :] = pltpu.roll(kia, HEAD_DIM, 1).astype(BF16)
            return carry
        lax.fori_loop(0, n_chunks, body, 0)

    qi = qi_ref[0]
    q = q_ref[0]
    wt = ixq_ref[0].T
    w_row = {(a, p): wt[IDX_DIM + 2 * p + a:IDX_DIM + 2 * p + a + 1, :] for a, p in blocks}
    qpos = j * QB + lane

    n_pairs = (j + 2) // 2

    def chunk_loop(body, carry):
        n_quads = n_pairs // 2
        carry = lax.fori_loop(0, n_quads, lambda i, cr: body([4 * i + u for u in range(4)], cr), carry)
        return lax.fori_loop(2 * n_quads, n_pairs, lambda i, cr: body([2 * i, 2 * i + 1], cr), carry)

    def score_body(cs_, carry):
        for c in cs_:
            s = _dot_nt(kiblk[c], qi)
            lg = _dot_nt(kblk[c], q)
            sc = jnp.zeros((QB, LANES), F32)
            for a, p in blocks:
                sc = sc + jnp.maximum(s[rs(a), cs(p)], 0.0) * w_row[(a, p)]
            sc = sc * IDX_HEADS ** -0.5
            admissible = c * QB + row <= qpos
            scores[chunk(c), :] = jnp.where(admissible, sc, -jnp.inf)
            keys[chunk(c), :] = jnp.where(admissible, _float_key(sc), jnp.int32(INT_MIN))
            logits[c] = lg + bias_ref[jnp.clip(j - c, 0, 2)]
        return carry
    chunk_loop(score_body, 0)

    def fill_body(c, carry):
        scores[chunk(c), :] = jnp.full((QB, LANES), -jnp.inf, F32)
        keys[chunk(c), :] = jnp.full((QB, LANES), INT_MIN, I32)
        return carry
    lax.fori_loop(2 * n_pairs, n_chunks, fill_body, 0)

    kf = jnp.float32(topk)

    def count(src, width, pred):
        accs = [jnp.zeros((8, LANES), F32) for _ in range(8)]
        for g in range(width // 8):
            accs[g % 8] = accs[g % 8] + jnp.where(pred(src[g * 8:(g + 1) * 8, :]), 1.0, 0.0)
        return jnp.sum(functools.reduce(lambda x, y: x + y, accs), axis=0, keepdims=True)

    def search(width):
        def search_body(i, ans):
            cand = ans | jnp.left_shift(jnp.int32(1), 31 - i)
            cs_ = cand ^ jnp.int32(INT_MIN)
            return jnp.where(count(keys, width, lambda k: k >= cs_) >= kf, cand, ans)
        ans = lax.fori_loop(0, 32, search_body, jnp.zeros((1, LANES), I32))
        t = _key_float(ans ^ jnp.int32(INT_MIN))
        thr_ref[...] = t
        cge_ref[...] = count(scores, width, lambda s: s >= t)
        cgt_ref[...] = count(scores, width, lambda s: s > t)

    prev = 0
    for n in _search_widths(n_chunks):
        pl.when((j >= prev) & (j < n))(functools.partial(search, n * QB))
        prev = n

    s_len = scores.shape[0]
    thr, cge, cgt = _walk_to_kth(
        lambda: scores[...], thr_ref[...], cge_ref[...], cgt_ref[...], kf,
        lambda t: (count(scores, s_len, lambda s: s >= t), count(scores, s_len, lambda s: s > t)), 4 * topk)

    @pl.when(jnp.max(cge) > kf)
    def _break_ties_by_position():
        need = kf - cgt
        big = jnp.int32(2 * s_len)
        eqrow_ref = keys
        eqrow_ref[...] = jnp.where(scores[...] == thr, lax.broadcasted_iota(I32, scores.shape, 0), big)
        nbits = int(math.log2(s_len))

        def tie_body(i, best):
            cand = best | jnp.left_shift(jnp.int32(1), nbits - 1 - i)
            return jnp.where(count(eqrow_ref, s_len, lambda e: e < cand) < need, cand, best)
        last = lax.fori_loop(0, nbits, tie_body, jnp.zeros((1, LANES), I32))
        eqrow = eqrow_ref[...]
        scores[...] = jnp.where((eqrow > last) & (eqrow < big), -jnp.inf, scores[...])

    def mask_body(cs_, mx):
        mx = list(mx)
        for c in cs_:
            sel = scores[chunk(c), :] >= thr
            for n, (a, p) in enumerate(blocks):
                blk = jnp.where(sel, logits[c, rs(a), cs(p)], -jnp.inf)
                logits[c, rs(a), cs(p)] = blk
                mx[n] = jnp.maximum(mx[n], jnp.max(blk, axis=0, keepdims=True))
        return tuple(mx)
    mx = chunk_loop(mask_body, tuple(jnp.full((1, LANES), -jnp.inf, F32) for _ in blocks))

    acc[...] = jnp.zeros(acc.shape, F32)

    def pv_body(cs_, ls):
        ls = list(ls)
        pv = jnp.zeros(acc.shape, F32)
        for c in cs_:
            rows = []
            for a in range(2):
                cols = []
                for p in range(4):
                    e = jnp.exp(logits[c, rs(a), cs(p)] - mx[a * 4 + p])
                    ls[a * 4 + p] = ls[a * 4 + p] + jnp.sum(e, axis=0, keepdims=True)
                    cols.append(e.astype(BF16))
                rows.append(jnp.concatenate(cols, axis=1))
            pmat = jnp.concatenate(rows, axis=0)
            pv = pv + jnp.dot(vblk_t[c], pmat, preferred_element_type=F32)
        acc[...] = acc[...] + pv
        return tuple(ls)
    ls = chunk_loop(pv_body, tuple(jnp.zeros((1, LANES), F32) for _ in blocks))

    for p in range(4):
        inv = jnp.where(row < HEAD_DIM, 1.0 / ls[p], 1.0 / ls[4 + p])
        o_ref[0, :, cs(p)] = (acc[:, cs(p)] * inv).T.astype(BF16)


def _attn_prompt(q_st, qi_st, ix, kv, bias_st, *, batch, seq):
    nq = seq // QB
    assert nq % 2 == 0
    topk = min(TOPK_MAX, seq // 4)
    ix3 = ix.reshape(batch, seq, LANES)
    kv3 = kv.reshape(batch, seq, 2 * LANES)
    return pl.pallas_call(
        functools.partial(_attn_prompt_kernel, n_chunks=nq, topk=topk),
        grid=(batch, nq),
        in_specs=[pl.BlockSpec((1, 4 * QB, LANES), lambda b, j: (b * nq + j, 0, 0)),
                  pl.BlockSpec((1, 4 * QB, LANES), lambda b, j: (b * nq + j, 0, 0)),
                  pl.BlockSpec((1, QB, LANES), lambda b, j: (b, j, 0)),
                  pl.BlockSpec((1, seq, LANES), lambda b, j: (b, 0, 0)),
                  pl.BlockSpec((1, seq, 2 * LANES), lambda b, j: (b, 0, 0)),
                  _full_spec(bias_st.shape)],
        out_specs=pl.BlockSpec((1, QB, 4 * LANES), lambda b, j: (b, j, 0)),
        out_shape=jax.ShapeDtypeStruct((batch, seq, 4 * LANES), BF16),
        scratch_shapes=[pltpu.VMEM((nq, 2 * QB, LANES), BF16), pltpu.VMEM((nq, LANES, 2 * QB), BF16),
                        pltpu.VMEM((nq, 2 * QB, LANES), BF16), pltpu.VMEM((seq, LANES), I32),
                        pltpu.VMEM((seq, LANES), F32),
                        pltpu.VMEM((nq, 2 * QB, 4 * LANES), F32), pltpu.VMEM((LANES, 4 * LANES), F32),
                        pltpu.VMEM((1, LANES), F32), pltpu.VMEM((1, LANES), F32), pltpu.VMEM((1, LANES), F32)],
        compiler_params=_cparams(2),
        name="attn_prompt",
    )(q_st, qi_st, ix3, ix3, kv3, bias_st)


def _rglru_gates(xc, wa, ba, wx, bx, lam):
    r = _sigmoid(_dot(xc, wa) + ba)
    i = _sigmoid(_dot(xc, wx) + bx)
    log_a = -RG_C * r * _softplus(-lam)
    a = jnp.exp(log_a)
    u = jnp.sqrt(1.0 - jnp.exp(2.0 * log_a)) * (i * xc)
    return a, u


def _rglru_prompt_kernel(g_ref, xr_ref, buf_ref, h0_ref, cw_ref, cb_ref, wa_ref, ba_ref, wx_ref, bx_ref, lam_ref,
                         o_ref, hl_ref, nb_ref, xs, a_s, u_s, tail, hc, *, tc):
    t = pl.program_id(0)
    width = cw_ref.shape[0]
    nb, _, d = g_ref.shape

    @pl.when(t == 0)
    def _load_state():
        tail[...] = jnp.zeros(tail.shape, F32)
        tail[:, 8 - (width - 1):8, :] = buf_ref[...]
        hc[...] = h0_ref[...]

    xs[:, 0:8, :] = tail[...]
    xs[:, 8:8 + tc, :] = xr_ref[...]
    tail[...] = xs[:, tc:tc + 8, :]
    xc = cb_ref[...]
    for jj in range(width):
        off = 8 - (width - 1) + jj
        xc = xc + cw_ref[jj:jj + 1, :] * xs[:, off:off + tc, :]
    a, u = _rglru_gates(xc.reshape(nb * tc, d), wa_ref[...], ba_ref[...], wx_ref[...], bx_ref[...], lam_ref[...])
    n_lb = d // LANES
    pitch = tc + 8
    for k in range(n_lb):
        for b in range(nb):
            a_s[k, b * pitch:b * pitch + tc, :] = a[b * tc:(b + 1) * tc, k * LANES:(k + 1) * LANES]
            u_s[k, b * pitch:b * pitch + tc, :] = u[b * tc:(b + 1) * tc, k * LANES:(k + 1) * LANES]

    def scan_body(i, hs):
        rows = pl.ds(i, nb, stride=pitch)
        out = []
        for k in range(n_lb):
            h = a_s[k, rows, :] * hs[k] + u_s[k, rows, :]
            u_s[k, rows, :] = h
            out.append(h)
        return tuple(out)
    h0 = hc[...]
    hs = lax.fori_loop(0, tc, scan_body, tuple(h0[:, k * LANES:(k + 1) * LANES] for k in range(n_lb)), unroll=8)
    h = jnp.concatenate(hs, axis=1)
    hc[...] = h
    hseq = jnp.concatenate([jnp.concatenate([u_s[k, b * pitch:b * pitch + tc, :] for b in range(nb)], axis=0)
                            for k in range(n_lb)], axis=1)
    o_ref[...] = (_gelu(g_ref[...].reshape(nb * tc, d)) * hseq).reshape(nb, tc, d).astype(BF16)
    hl_ref[...] = h
    nb_ref[...] = xs[:, tc + 8 - (width - 1):tc + 8, :]


def _rglru_prompt(gate, xr, buf, h0, rg, *, batch, seq, tc=256):
    d = gate.shape[-1]
    width = rg["cw"].shape[0]
    g3, x3 = gate.reshape(batch, seq, d), xr.reshape(batch, seq, d)
    blk = pl.BlockSpec((batch, tc, d), lambda t: (0, t, 0))
    vec = _full_spec((1, d))
    return pl.pallas_call(
        functools.partial(_rglru_prompt_kernel, tc=tc),
        grid=(seq // tc,),
        in_specs=[blk, blk, _full_spec((batch, width - 1, d)), _full_spec((batch, d)), _full_spec((width, d)), vec,
                  _full_spec((d, d)), vec, _full_spec((d, d)), vec, vec],
        out_specs=[blk, pl.BlockSpec((batch, d), lambda t: (0, 0)), pl.BlockSpec((batch, width - 1, d), lambda t: (0, 0, 0))],
        out_shape=[jax.ShapeDtypeStruct((batch, seq, d), BF16), jax.ShapeDtypeStruct((batch, d), F32),
                   jax.ShapeDtypeStruct((batch, width - 1, d), F32)],
        scratch_shapes=[pltpu.VMEM((batch, tc + 8, d), F32), pltpu.VMEM((d // LANES, batch * (tc + 8), LANES), F32),
                        pltpu.VMEM((d // LANES, batch * (tc + 8), LANES), F32), pltpu.VMEM((batch, 8, d), F32),
                        pltpu.VMEM((batch, d), F32)],
        compiler_params=_cparams(1),
        name="rglru_prompt",
    )(g3, x3, buf, h0, rg["cw"], rg["cb"], rg["wa"], rg["ba"], rg["wx"], rg["bx"], rg["lam"])


def _rglru_dec_kernel(g_ref, xr_ref, buf_ref, h0_ref, cw_ref, cb_ref, wa_ref, ba_ref, wx_ref, bx_ref, lam_ref,
                      o_ref, hl_ref):
    width = cw_ref.shape[0]
    xc = cb_ref[...]
    for jj in range(width - 1):
        xc = xc + cw_ref[jj:jj + 1, :] * buf_ref[jj]
    xc = xc + cw_ref[width - 1:width, :] * xr_ref[...]
    a, u = _rglru_gates(xc, wa_ref[...], ba_ref[...], wx_ref[...], bx_ref[...], lam_ref[...])
    h = a * h0_ref[...] + u
    hl_ref[...] = h
    o_ref[...] = (_gelu(g_ref[...]) * h).astype(BF16)


def _rglru_dec(gate, xr, buf_t, h0, rg):
    m, d = gate.shape
    return pl.pallas_call(
        _rglru_dec_kernel,
        out_shape=[jax.ShapeDtypeStruct((m, d), BF16), jax.ShapeDtypeStruct((m, d), F32)],
        name="rglru_dec",
    )(gate, xr, buf_t, h0, rg["cw"], rg["cb"], rg["wa"], rg["ba"], rg["wx"], rg["bx"], rg["lam"])


def _ffn_tile(y1, gf_ref, wup_ref, cw_ref, cb_ref, wdn_ref, conv_prev, n_split):
    d_ff = wdn_ref.shape[0]
    cf = d_ff // n_split
    hn = _rms(y1, gf_ref[...]).astype(BF16)
    out = jnp.zeros(y1.shape, F32)
    gates = []
    for k in range(n_split):
        c0 = k * cf
        g = jnp.dot(hn, wup_ref[:, c0:c0 + cf], preferred_element_type=F32)
        u = jnp.dot(hn, wup_ref[:, d_ff + c0:d_ff + c0 + cf], preferred_element_type=F32)
        g1, g2 = conv_prev(k, g)
        gc = cb_ref[:, c0:c0 + cf] + cw_ref[0:1, c0:c0 + cf] * g2 + cw_ref[1:2, c0:c0 + cf] * g1 \
            + cw_ref[2:3, c0:c0 + cf] * g
        act = (_gelu(gc) * u).astype(BF16)
        out = out + jnp.dot(act, wdn_ref[c0:c0 + cf, :], preferred_element_type=F32)
        gates.append(g)
    return out, gates


def _prompt_conv_prev(gs, carry, fb_ref, nb_ref, tm, cf):
    t = pl.program_id(1)

    @pl.when(t == 0)
    def _load_state():
        carry[...] = jnp.zeros(carry.shape, F32)
        for k in range(carry.shape[0]):
            carry[k, 6:8, :] = fb_ref[0, :, k * cf:(k + 1) * cf]

    def conv_prev(k, g):
        gs[0:8, :] = carry[k]
        gs[8:8 + tm, :] = g
        carry[k] = g[tm - 8:tm, :]
        nb_ref[0, :, k * cf:(k + 1) * cf] = g[tm - 2:tm, :]
        return gs[7:7 + tm, :], gs[6:6 + tm, :]
    return conv_prev


def _mix_ab_tile(y_ref, a_ref, r_ref, woa_ref, wob_ref):
    return y_ref[0] + jnp.dot(a_ref[0], woa_ref[...], preferred_element_type=F32) \
        + jnp.dot(r_ref[0], wob_ref[...], preferred_element_type=F32)


def _post_ab_prompt_kernel(y_ref, a_ref, r_ref, fb_ref, woa_ref, wob_ref, gf_ref, wup_ref, cw_ref, cb_ref, wdn_ref,
                           o_ref, nb_ref, gs, carry, *, tm, n_split):
    y1 = _mix_ab_tile(y_ref, a_ref, r_ref, woa_ref, wob_ref)
    cf = wdn_ref.shape[0] // n_split
    out, _ = _ffn_tile(y1, gf_ref, wup_ref, cw_ref, cb_ref, wdn_ref,
                       _prompt_conv_prev(gs, carry, fb_ref, nb_ref, tm, cf), n_split)
    o_ref[0] = y1 + out


_FFN_KEYS = ("g", "wup", "cw", "cb", "wdn")


def _ffn_specs(ffn):
    layer = ffn["layer"]
    return [pl.BlockSpec((None,) + ffn[k].shape[1:], lambda *_: (layer, 0, 0), pipeline_mode=pl.Buffered(1))
            for k in _FFN_KEYS]


def _ffn_args(ffn):
    return [ffn[k] for k in _FFN_KEYS]


def _post_ab_prompt(y, attn, rgo, fbuf, wo_a, wo_b, ffn, *, tm=512, n_split=2):
    batch, seq, d = y.shape
    d_ff = ffn["wdn"].shape[1]
    cf = d_ff // n_split
    blk = lambda n: pl.BlockSpec((1, tm, n), lambda b, t: (b, t, 0))
    fb = pl.BlockSpec((1, 2, d_ff), lambda b, t: (b, 0, 0))
    return pl.pallas_call(
        functools.partial(_post_ab_prompt_kernel, tm=tm, n_split=n_split),
        grid=(batch, seq // tm),
        in_specs=[blk(d), blk(attn.shape[-1]), blk(rgo.shape[-1]), fb, _full_spec(wo_a.shape), _full_spec(wo_b.shape)]
        + _ffn_specs(ffn),
        out_specs=[blk(d), fb],
        out_shape=[jax.ShapeDtypeStruct((batch, seq, d), F32), jax.ShapeDtypeStruct((batch, 2, d_ff), F32)],
        scratch_shapes=[pltpu.VMEM((tm + 8, cf), F32), pltpu.VMEM((n_split, 8, cf), F32)],
        compiler_params=_cparams(2),
        name="post_ab_prompt",
    )(y, attn, rgo, fbuf, wo_a, wo_b, *_ffn_args(ffn))


def _dec_conv_prev(fb_ref, cf):
    def conv_prev(k, g):
        return fb_ref[1, :, k * cf:(k + 1) * cf], fb_ref[0, :, k * cf:(k + 1) * cf]
    return conv_prev


def _post_ab_dec_kernel(y_ref, a_ref, r_ref, fb_ref, woa_ref, wob_ref, gf_ref, wup_ref, cw_ref, cb_ref, wdn_ref,
                        o_ref, g_ref, *, n_split):
    y1 = y_ref[...] + jnp.dot(a_ref[...], woa_ref[...], preferred_element_type=F32) \
        + jnp.dot(r_ref[...], wob_ref[...], preferred_element_type=F32)
    cf = wdn_ref.shape[0] // n_split
    out, gates = _ffn_tile(y1, gf_ref, wup_ref, cw_ref, cb_ref, wdn_ref, _dec_conv_prev(fb_ref, cf), n_split)
    o_ref[...] = y1 + out
    for k, g in enumerate(gates):
        g_ref[:, k * cf:(k + 1) * cf] = g


def _post_ab_dec(y, attn, rgo, fbuf_t, wo_a, wo_b, ffn, *, n_split=2):
    m, d = y.shape
    d_ff = ffn["wdn"].shape[1]
    args = (y, attn, rgo, fbuf_t, wo_a, wo_b)
    return pl.pallas_call(
        functools.partial(_post_ab_dec_kernel, n_split=n_split),
        grid=(1,),
        in_specs=[_full_spec(a.shape) for a in args] + _ffn_specs(ffn),
        out_specs=[_whole_spec((m, d)), _whole_spec((m, d_ff))],
        out_shape=[jax.ShapeDtypeStruct((m, d), F32), jax.ShapeDtypeStruct((m, d_ff), F32)],
        compiler_params=_cparams(1),
        name="post_ab_dec",
    )(*args, *_ffn_args(ffn))


def _gmlp_in(y, gm_ref, win_ref, bin_ref, sn_ref):
    d_c = win_ref.shape[1] // 2
    z = _gelu(_dot(_rms(y, gm_ref[...]), win_ref[...]) + bin_ref[...])
    return z[:, :d_c], _rms(z[:, d_c:], sn_ref[...])


def _layer_c_prompt_kernel(y_ref, fb_ref, gm_ref, win_ref, bin_ref, sn_ref, sw_ref, sbt_ref, woc_ref,
                           gf_ref, wup_ref, cw_ref, cb_ref, wdn_ref, gfin_ref,
                           o_ref, nb_ref, gs, carry, *, tm, n_split):
    y = y_ref[0]
    u, v = _gmlp_in(y, gm_ref, win_ref, bin_ref, sn_ref)
    vb = v.astype(BF16)
    n_groups = sw_ref.shape[0]
    tril = lax.broadcasted_iota(I32, (CHUNK, CHUNK), 0) >= lax.broadcasted_iota(I32, (CHUNK, CHUNK), 1)
    wm = [jnp.where(tril, sw_ref[gi], 0.0).astype(BF16) for gi in range(n_groups)]
    rows = []
    for r in range(tm // CHUNK):
        cols = []
        for gi in range(n_groups):
            mixed = jnp.dot(wm[gi], vb[r * CHUNK:(r + 1) * CHUNK, gi * LANES:(gi + 1) * LANES],
                            preferred_element_type=F32)
            cols.append(mixed + sbt_ref[:, gi:gi + 1])
        rows.append(jnp.concatenate(cols, axis=1))
    gated = u * jnp.concatenate(rows, axis=0)
    y1 = y + _dot(gated, woc_ref[...])
    cf = wdn_ref.shape[0] // n_split
    out, _ = _ffn_tile(y1, gf_ref, wup_ref, cw_ref, cb_ref, wdn_ref,
                       _prompt_conv_prev(gs, carry, fb_ref, nb_ref, tm, cf), n_split)
    o_ref[0] = _rms(y1 + out, gfin_ref[...])


def _layer_c_prompt(y, fbuf, cp, ffn, g_final, *, tm=512, n_split=2):
    batch, seq, d = y.shape
    d_ff = ffn["wdn"].shape[1]
    cf = d_ff // n_split
    blk = pl.BlockSpec((1, tm, d), lambda b, t: (b, t, 0))
    fb = pl.BlockSpec((1, 2, d_ff), lambda b, t: (b, 0, 0))
    consts = [cp["g"], cp["win"], cp["bin"], cp["sn"], cp["sw"], cp["sbt"], cp["woc"]]
    return pl.pallas_call(
        functools.partial(_layer_c_prompt_kernel, tm=tm, n_split=n_split),
        grid=(batch, seq // tm),
        in_specs=[blk, fb] + [_full_spec(c.shape) for c in consts] + _ffn_specs(ffn) + [_full_spec(g_final.shape)],
        out_specs=[blk, fb],
        out_shape=[jax.ShapeDtypeStruct((batch, seq, d), F32), jax.ShapeDtypeStruct((batch, 2, d_ff), F32)],
        scratch_shapes=[pltpu.VMEM((tm + 8, cf), F32), pltpu.VMEM((n_split, 8, cf), F32)],
        compiler_params=_cparams(2),
        name="layer_c_prompt",
    )(y, fbuf, *consts, *_ffn_args(ffn), g_final)


def _layer_c_dec_kernel(y_ref, fb_ref, gm_ref, win_ref, bin_ref, sn_ref, sw0_ref, sb0_ref, woc_ref,
                        gf_ref, wup_ref, cw_ref, cb_ref, wdn_ref, gfin_ref, o_ref, g_ref, v_ref, *, n_split):
    y = y_ref[...]
    u, v = _gmlp_in(y, gm_ref, win_ref, bin_ref, sn_ref)
    v_ref[...] = v
    y1 = y + _dot(u * (sw0_ref[...] * v + sb0_ref[...]), woc_ref[...])
    cf = wdn_ref.shape[0] // n_split
    out, gates = _ffn_tile(y1, gf_ref, wup_ref, cw_ref, cb_ref, wdn_ref, _dec_conv_prev(fb_ref, cf), n_split)
    o_ref[...] = _rms(y1 + out, gfin_ref[...])
    for k, g in enumerate(gates):
        g_ref[:, k * cf:(k + 1) * cf] = g


def _layer_c_dec(y, fbuf_t, cp, ffn, g_final, *, n_split=2):
    m, d = y.shape
    d_ff = ffn["wdn"].shape[1]
    d_c = cp["woc"].shape[0]
    args = (y, fbuf_t, cp["g"], cp["win"], cp["bin"], cp["sn"], cp["sw0"], cp["sb0"], cp["woc"])
    return pl.pallas_call(
        functools.partial(_layer_c_dec_kernel, n_split=n_split),
        grid=(1,),
        in_specs=[_full_spec(a.shape) for a in args] + _ffn_specs(ffn) + [_full_spec(g_final.shape)],
        out_specs=[_whole_spec((m, d)), _whole_spec((m, d_ff)), _whole_spec((m, d_c))],
        out_shape=[jax.ShapeDtypeStruct((m, d), F32), jax.ShapeDtypeStruct((m, d_ff), F32),
                   jax.ShapeDtypeStruct((m, d_c), F32)],
        compiler_params=_cparams(1),
        name="layer_c_dec",
    )(*args, *_ffn_args(ffn), g_final)


def _start_page_copies(src_ref, pt_ref, b, dst_ref, sem, n_pages, page):
    def body(pg, carry):
        col = pl.multiple_of(pg * page, page)
        pltpu.make_async_copy(src_ref.at[pt_ref[b, pg]], dst_ref.at[:, pl.ds(col, page)], sem).start()
        return carry
    lax.fori_loop(0, n_pages, body, 0, unroll=8)


def _wait_page_copies(dst_ref, sem):
    pltpu.make_async_copy(dst_ref, dst_ref, sem).wait()


def _dec_score_kernel(pt_ref, qi_ref, wi_ref, ixn_ref, cik_ref, keys_ref, knew_ref, ibuf, sems, *, n_pages, page):
    b = pl.program_id(0)
    nb = pl.num_programs(0)
    slot = lax.rem(b, 2)

    def start(bb, sl):
        _start_page_copies(cik_ref, pt_ref, bb, ibuf.at[sl], sems.at[sl], n_pages, page)

    @pl.when(b == 0)
    def _first():
        start(0, 0)

    @pl.when(b + 1 < nb)
    def _prefetch_next():
        start(b + 1, 1 - slot)

    _wait_page_copies(ibuf.at[slot], sems.at[slot])
    qi = qi_ref[0].astype(BF16)
    wi = wi_ref[0]
    s = jnp.dot(qi, ibuf[slot].astype(BF16), preferred_element_type=F32)
    sc = jnp.sum(jnp.maximum(s, 0.0) * wi, axis=0, keepdims=True) * IDX_HEADS ** -0.5
    keys_ref[0] = sc
    kin = ixn_ref[0][:, 0:IDX_DIM]
    sn = jnp.sum(qi_ref[0] * kin, axis=1, keepdims=True)
    scn = jnp.sum(jnp.maximum(sn, 0.0) * wi, axis=0, keepdims=True) * IDX_HEADS ** -0.5
    knew_ref[0] = jnp.broadcast_to(scn, (1, LANES))


def _dec_scores(page_table, qi3, wi3, ix3, cik_t, *, page):
    db, n_pages = page_table.shape
    past = n_pages * page
    return pl.pallas_call(
        functools.partial(_dec_score_kernel, n_pages=n_pages, page=page),
        grid_spec=pltpu.PrefetchScalarGridSpec(
            num_scalar_prefetch=1,
            grid=(db,),
            in_specs=[pl.BlockSpec((1, IDX_HEADS, IDX_DIM), lambda b, pt: (b, 0, 0)),
                      pl.BlockSpec((1, IDX_HEADS, 1), lambda b, pt: (b, 0, 0)),
                      pl.BlockSpec((1, 1, LANES), lambda b, pt: (b, 0, 0)),
                      pl.BlockSpec(memory_space=pl.ANY)],
            out_specs=[pl.BlockSpec((1, 1, past), lambda b, pt: (b, 0, 0)),
                       pl.BlockSpec((1, 1, LANES), lambda b, pt: (b, 0, 0))],
            scratch_shapes=[pltpu.VMEM((2, IDX_DIM, past), F32), pltpu.SemaphoreType.DMA((2,))]),
        out_shape=[jax.ShapeDtypeStruct((db, 1, past), F32), jax.ShapeDtypeStruct((db, 1, LANES), F32)],
        compiler_params=_cparams(1),
        name="dec_scores",
    )(page_table, qi3, wi3, ix3, cik_t)


def _dec_select_kernel(sc_ref, scn_ref, so_ref, sno_ref, thr_ref, *, topk):
    past = sc_ref.shape[1]
    lane0 = lax.broadcasted_iota(I32, scn_ref.shape, 1) == 0
    sc = jnp.concatenate([sc_ref[...], jnp.where(lane0, scn_ref[...], -jnp.inf)], axis=1)
    kk = _float_key(sc)
    kf = jnp.float32(topk)

    def count(pred):
        return jnp.sum(jnp.where(pred, 1.0, 0.0), axis=1, keepdims=True)

    def search_body(i, ans):
        cand = ans | jnp.left_shift(jnp.int32(1), 31 - i)
        return jnp.where(count(kk >= (cand ^ jnp.int32(INT_MIN))) >= kf, cand, ans)
    ans = lax.fori_loop(0, 32, search_body, jnp.zeros((sc.shape[0], 1), I32))
    thr = _key_float(ans ^ jnp.int32(INT_MIN))
    counts = lambda t: (count(sc >= t), count(sc > t))
    thr, cge, cgt = _walk_to_kth(lambda: sc, thr, *counts(thr), kf, counts, 4 * topk)
    need = kf - cgt
    big = jnp.int32(4 * past)
    eqcol = jnp.where(sc == thr, lax.broadcasted_iota(I32, sc.shape, 1), big)
    nbits = int(math.log2(past)) + 1

    def tie_body(i, best):
        cand = best | jnp.left_shift(jnp.int32(1), nbits - 1 - i)
        return jnp.where(count(eqcol < cand) < need, cand, best)
    last = lax.fori_loop(0, nbits, tie_body, jnp.zeros((sc.shape[0], 1), I32))
    sc = jnp.where((eqcol > last) & (eqcol < big), -jnp.inf, sc)
    so_ref[...] = sc[:, :past]
    sno_ref[...] = sc[:, past:]
    thr_ref[...] = jnp.broadcast_to(thr, thr_ref.shape)


def _dec_select(scores, snew, *, topk):
    db, past = scores.shape
    assert past + 1 >= topk
    return pl.pallas_call(
        functools.partial(_dec_select_kernel, topk=topk),
        out_shape=[jax.ShapeDtypeStruct((db, past), F32), jax.ShapeDtypeStruct((db, LANES), F32),
                   jax.ShapeDtypeStruct((db, LANES), F32)],
        name="dec_select",
    )(scores, snew)


def _dec_attn_kernel(pt_ref, q_ref, kvn_ref, keys_ref, knew_ref, thr_ref, bias_ref, ck_ref, cv_ref, o_ref,
                     kbuf, vbuf, sems, *, n_pages, page):
    b = pl.program_id(0)
    nb = pl.num_programs(0)
    slot = lax.rem(b, 2)
    past = n_pages * page

    def start(bb, sl):
        _start_page_copies(ck_ref, pt_ref, bb, kbuf.at[sl], sems.at[0, sl], n_pages, page)
        _start_page_copies(cv_ref, pt_ref, bb, vbuf.at[sl], sems.at[1, sl], n_pages, page)

    @pl.when(b == 0)
    def _first():
        start(0, 0)

    @pl.when(b + 1 < nb)
    def _prefetch_next():
        start(b + 1, 1 - slot)

    _wait_page_copies(kbuf.at[slot], sems.at[0, slot])
    _wait_page_copies(vbuf.at[slot], sems.at[1, slot])
    qm = q_ref[0]
    thr = thr_ref[0][:, 0:1]
    sel = keys_ref[0] >= thr
    sel_new = knew_ref[0][:, 0:1] >= thr
    bias = bias_ref[...]
    far, last, bnew = bias[:, LANES:LANES + 1], bias[:, 0:page], bias[:, LANES + 1:LANES + 2]
    lg = jnp.dot(qm.astype(BF16), kbuf[slot].astype(BF16), preferred_element_type=F32)
    lane = lax.broadcasted_iota(I32, lg.shape, 1)
    lastp = jnp.concatenate([jnp.zeros((N_HEADS, past - page), F32), last], axis=1)
    lg = jnp.where(sel, lg + jnp.where(lane >= past - page, lastp, far), -jnp.inf)
    kvn = kvn_ref[0]
    lgn = jnp.sum(qm * kvn[:, 0:LANES], axis=1, keepdims=True) + bnew
    lgn = jnp.where(sel_new, lgn, -jnp.inf)
    m = jnp.maximum(jnp.max(lg, axis=1, keepdims=True), lgn)
    e = jnp.exp(lg - m)
    en = jnp.exp(lgn - m)
    den = jnp.sum(e, axis=1, keepdims=True) + en
    pv = _dot_nt(e.astype(BF16), vbuf[slot].astype(BF16)) + en * kvn[:, LANES:2 * LANES]
    pv = pv / den
    lo = lax.broadcasted_iota(I32, (1, LANES), 1) < HEAD_DIM
    o_ref[0] = jnp.concatenate([jnp.where(lo, pv[2 * p:2 * p + 1], pv[2 * p + 1:2 * p + 2]) for p in range(4)],
                               axis=1).astype(BF16)


def _dec_attn(page_table, qm, kvn, keys, knew, thr, bias_dec, ck_t, cv_t, *, page):
    db, n_pages = page_table.shape
    past = n_pages * page
    row = lambda n: pl.BlockSpec((1, 1, n), lambda b, pt: (b, 0, 0))
    return pl.pallas_call(
        functools.partial(_dec_attn_kernel, n_pages=n_pages, page=page),
        grid_spec=pltpu.PrefetchScalarGridSpec(
            num_scalar_prefetch=1,
            grid=(db,),
            in_specs=[pl.BlockSpec((1, N_HEADS, LANES), lambda b, pt: (b, 0, 0)), row(2 * LANES), row(past),
                      row(LANES), row(LANES), pl.BlockSpec((N_HEADS, 2 * LANES), lambda b, pt: (0, 0)),
                      pl.BlockSpec(memory_space=pl.ANY), pl.BlockSpec(memory_space=pl.ANY)],
            out_specs=row(4 * LANES),
            scratch_shapes=[pltpu.VMEM((2, 2 * HEAD_DIM, past), F32), pltpu.VMEM((2, 2 * HEAD_DIM, past), F32),
                            pltpu.SemaphoreType.DMA((2, 2))]),
        out_shape=jax.ShapeDtypeStruct((db, 1, 4 * LANES), BF16),
        compiler_params=_cparams(1),
        name="dec_attn",
    )(page_table, qm, kvn, keys, knew, thr, bias_dec, ck_t, cv_t)


def _prep_in_ab(w):
    d = w.shape[0]
    nq, nkv = N_HEADS * HEAD_DIM, N_KV_HEADS * HEAD_DIM
    offs = np.cumsum([nq, nkv, nkv, IDX_HEADS * IDX_DIM, IDX_DIM, IDX_HEADS, 512])
    q, k, v, qi, ki, wi, g, xr = jnp.split(w, offs.tolist(), axis=1)
    q = q.reshape(d, N_HEADS, HEAD_DIM)[:, np.array(HEAD_PERM), :].reshape(d, nq)
    pad = jnp.zeros((d, _C_G - _C_IX - IDX_DIM - IDX_HEADS), w.dtype)
    return jnp.concatenate([q, k, v, qi, ki, wi, pad, g, xr], axis=1).astype(BF16)


def _block_diag(w):
    n, c, _ = w.shape
    return (jnp.eye(n, dtype=w.dtype)[:, None, :, None] * w[:, :, None, :]).reshape(n * c, n * c).astype(BF16)


def _ffn_params(layer, stacked):
    return dict(stacked, layer=layer)


def kernel(x_prompt, x_sample, cache_k, cache_v, cache_idx_k, state_rglru_h, state_rglru_conv, state_ffn_conv,
           page_table, norm_mix, norm_ffn, norm_final, rel_bias, w_in_ab, w_out_ab, rg_conv_w, rg_conv_b,
           rg_wa, rg_ba, rg_wx, rg_bx, rg_lambda, w_in_c, b_in_c, sgu_norm, sgu_w, sgu_b, w_out_c,
           ffn_w_up, ffn_conv_w, ffn_conv_b, ffn_w_down):
    batch, seq, d = x_prompt.shape
    db = x_sample.shape[0]
    page = cache_k.shape[2]
    d_a = N_HEADS * HEAD_DIM
    d_b = rg_conv_w.shape[-1]
    d_ff = ffn_w_down.shape[1]
    assert x_sample.shape[1] == 1 and seq % 512 == 0 and page == LANES and w_in_ab.shape[0] == 1

    w_in0 = _prep_in_ab(w_in_ab[0])
    wo = w_out_ab[0]
    wo_a = wo[:d_a].reshape(N_HEADS, HEAD_DIM, d)[np.array(HEAD_PERM)].reshape(d_a, d).astype(BF16)
    wo_b = wo[d_a:].astype(BF16)
    rg = {"cw": rg_conv_w[0], "cb": rg_conv_b[0][None], "wa": _block_diag(rg_wa[0]), "ba": rg_ba[0][None],
          "wx": _block_diag(rg_wx[0]), "bx": rg_bx[0][None], "lam": rg_lambda[0][None]}
    ffn_all = {"g": norm_ffn[:, None, :], "wup": ffn_w_up.astype(BF16), "cw": ffn_conv_w, "cb": ffn_conv_b[:, None, :],
               "wdn": ffn_w_down.astype(BF16)}
    ffn0, ffn1 = _ffn_params(0, ffn_all), _ffn_params(1, ffn_all)
    cp = {"g": norm_mix[1][None], "win": w_in_c[0].astype(BF16), "bin": b_in_c[0][None], "sn": sgu_norm[0][None],
          "sw": sgu_w[0], "sbt": sgu_b[0].T, "woc": w_out_c[0].astype(BF16),
          "sw0": jnp.repeat(sgu_w[0][:, 0, 0], d // sgu_w.shape[1])[None],
          "sb0": jnp.repeat(sgu_b[0][:, 0], d // sgu_w.shape[1])[None]}
    g_mix0 = norm_mix[0][None]
    g_final = norm_final[None]
    bias_st, bias_dec = _bias_tables(rel_bias, page)

    xp = x_prompt.reshape(batch * seq, d)
    q_st, qi_st, kv_p, ix_p, gate_p, xr_p = _inproj(xp, g_mix0, w_in0, stack=True, tm=512)
    attn_p = _attn_prompt(q_st, qi_st, ix_p, kv_p, bias_st, batch=batch, seq=seq)
    rg_p, h_p, cbuf_p = _rglru_prompt(gate_p, xr_p, jnp.zeros((batch, rg["cw"].shape[0] - 1, d_b), F32),
                                      jnp.zeros((batch, d_b), F32), rg, batch=batch, seq=seq)
    zero_fb = jnp.zeros((batch, 2, d_ff), F32)
    y1_p, fb0_p = _post_ab_prompt(x_prompt, attn_p, rg_p, zero_fb, wo_a, wo_b, ffn0)
    y_p, fb1_p = _layer_c_prompt(y1_p, zero_fb, cp, ffn1, g_final)

    xs = x_sample.reshape(db, d)
    qm_s, qi_s, kv_s, ix_s, gate_s, xr_s = _inproj(xs, g_mix0, w_in0, stack=False, tm=db)
    cik_t = jnp.transpose(cache_idx_k[0], (0, 2, 1))
    ck_t = jnp.transpose(cache_k[0], (0, 2, 3, 1)).reshape(-1, 2 * HEAD_DIM, page)
    cv_t = jnp.transpose(cache_v[0], (0, 2, 3, 1)).reshape(-1, 2 * HEAD_DIM, page)
    topk_s = min(TOPK_MAX, (page_table.shape[1] * page + 1) // 4)
    keys_s, knew_s = _dec_scores(page_table, qi_s.reshape(db, IDX_HEADS, IDX_DIM),
                                 ix_s[:, IDX_DIM:IDX_DIM + IDX_HEADS].reshape(db, IDX_HEADS, 1),
                                 ix_s.reshape(db, 1, LANES), cik_t, page=page)
    keys_s, knew_s, thr_s = _dec_select(keys_s.reshape(db, -1), knew_s.reshape(db, LANES), topk=topk_s)
    attn_s = _dec_attn(page_table, jnp.transpose(qm_s, (1, 0, 2)), kv_s.reshape(db, 1, 2 * LANES),
                       keys_s.reshape(db, 1, -1), knew_s.reshape(db, 1, LANES), thr_s.reshape(db, 1, LANES),
                       bias_dec, ck_t, cv_t, page=page).reshape(db, d_a)
    cbuf_s_in = state_rglru_conv[0]
    rg_s, h_s = _rglru_dec(gate_s, xr_s, jnp.transpose(cbuf_s_in, (1, 0, 2)), state_rglru_h[0], rg)
    y1_s, g0_s = _post_ab_dec(xs, attn_s, rg_s, jnp.transpose(state_ffn_conv[0], (1, 0, 2)), wo_a, wo_b, ffn0)
    y_s, g1_s, v_s = _layer_c_dec(y1_s, jnp.transpose(state_ffn_conv[1], (1, 0, 2)), cp, ffn1, g_final)

    kv4 = kv_p.reshape(batch, seq, 2, N_KV_HEADS, HEAD_DIM)
    kvs = kv_s.reshape(db, 1, 2, N_KV_HEADS, HEAD_DIM)
    fbuf_s = lambda layer, g: jnp.concatenate([state_ffn_conv[layer][:, 1:], g[:, None]], axis=1)
    return (y_p, y_s.reshape(db, 1, d),
            kv4[None, :, :, 0], kv4[None, :, :, 1], ix_p.reshape(batch, seq, LANES)[None, :, :, :IDX_DIM],
            kvs[None, :, :, 0], kvs[None, :, :, 1], ix_s.reshape(db, 1, LANES)[None, :, :, :IDX_DIM],
            h_p.reshape(batch, d_b)[None], cbuf_p[None],
            h_s[None], jnp.concatenate([cbuf_s_in[:, 1:], xr_s[:, None]], axis=1)[None],
            v_s.reshape(db, 1, -1)[None],
            jnp.stack([fb0_p, fb1_p]), jnp.stack([fbuf_s(0, g0_s), fbuf_s(1, g1_s)]))
```

```python
import functools
import math

import numpy as np
import jax
import jax.numpy as jnp
from jax import lax
from jax.experimental import pallas as pl
from jax.experimental.pallas import tpu as pltpu

F32 = jnp.float32
BF16 = jnp.bfloat16
I32 = jnp.int32

N_HEADS = 8
HEAD_DIM = 64
N_KV_HEADS = 2
Q_PER_KV = N_HEADS // N_KV_HEADS
IDX_HEADS = 8
IDX_DIM = 64
TOPK_MAX = 256
N_BUCKETS = 32
REL_MAX_EXACT = N_BUCKETS // 2
REL_MAX_DIST = 128
RG_C = 8.0
CHUNK = 128
EPS = 1e-6

LANES = 128
QB = 128
INT_MIN = -(2 ** 31)
KEY_MIN_FINITE = INT_MIN + 0x800000
NEG_MAX = float(np.finfo(np.float32).min)
HEAD_PERM = (0, 4, 1, 5, 2, 6, 3, 7)
VMEM_LIMIT = 56 * 1024 * 1024


def _cparams(n_grid):
    return pltpu.CompilerParams(dimension_semantics=("arbitrary",) * n_grid, vmem_limit_bytes=VMEM_LIMIT)


def _full_spec(shape):
    nd = len(shape)
    return pl.BlockSpec(shape, lambda *_: (0,) * nd, pipeline_mode=pl.Buffered(1))


def _whole_spec(shape):
    nd = len(shape)
    return pl.BlockSpec(shape, lambda *_: (0,) * nd)


def _rms(x, g):
    return x * lax.rsqrt(jnp.mean(x * x, axis=-1, keepdims=True) + EPS) * g


def _gelu(x):
    return x * (0.5 * (1.0 + jnp.tanh(math.sqrt(2.0 / math.pi) * (x + 0.044715 * (x * x * x)))))


def _sigmoid(x):
    return 1.0 / (1.0 + jnp.exp(-x))


def _softplus(x):
    return jnp.maximum(x, 0.0) + jnp.log(1.0 + jnp.exp(-jnp.abs(x)))


def _dot(a, b):
    return jnp.dot(a.astype(BF16), b, preferred_element_type=F32)


def _dot_nt(a, b):
    return lax.dot_general(a, b, (((1,), (1,)), ((), ())), preferred_element_type=F32)


def _float_key(x):
    bits = pltpu.bitcast(x, I32)
    key = jnp.where(bits < 0, bits ^ jnp.int32(0x7FFFFFFF), bits)
    return jnp.where(bits == jnp.int32(INT_MIN), jnp.int32(0), key)


def _key_float(key):
    key = jnp.maximum(key, jnp.int32(KEY_MIN_FINITE))
    return pltpu.bitcast(jnp.where(key < 0, key ^ jnp.int32(0x7FFFFFFF), key), F32)


def _walk_to_kth(scores, t, cge, cgt, kf, count_ge_gt, max_steps):
    axis = 0 if t.shape[0] == 1 else 1

    def settled(t, cge, cgt):
        return (cgt < kf) & ((cge >= kf) | (t <= NEG_MAX))

    def unsettled(t, cge, cgt):
        return jnp.max(jnp.where(settled(t, cge, cgt), 0.0, 1.0)) > 0.0

    def body(st):
        t, cge, cgt, it = st
        sc = scores()
        below = jnp.max(jnp.where(sc < t, sc, -jnp.inf), axis=axis, keepdims=True)
        above = jnp.min(jnp.where(sc > t, sc, jnp.inf), axis=axis, keepdims=True)
        t = jnp.where(cgt >= kf, above, jnp.where((cge < kf) & (t > NEG_MAX), jnp.maximum(below, NEG_MAX), t))
        cge, cgt = count_ge_gt(t)
        return t, cge, cgt, it + 1

    t, cge, cgt, _ = lax.while_loop(lambda st: unsettled(st[0], st[1], st[2]) & (st[3] < max_steps), body,
                                    (t, cge, cgt, jnp.int32(0)))
    return t, cge, cgt


def _t5_bucket_np(n):
    n = np.maximum(n, 0)
    nf = np.maximum(n, 1).astype(np.float32)
    large = REL_MAX_EXACT + (np.log(nf / np.float32(REL_MAX_EXACT)) / np.float32(math.log(REL_MAX_DIST / REL_MAX_EXACT))
                             * np.float32(N_BUCKETS - REL_MAX_EXACT)).astype(np.int32)
    large = np.minimum(large, N_BUCKETS - 1)
    return np.where(n < REL_MAX_EXACT, n, large).astype(np.int32)


_C_Q, _C_KV, _C_QI, _C_IX, _C_G, _C_X, _C_END = 0, 512, 768, 1280, 1408, 1920, 2432


def _inproj_kernel(x_ref, g_ref, w_ref, q_ref, qi_ref, kv_ref, ix_ref, gate_ref, xr_ref, *, stack):
    hn = _rms(x_ref[...], g_ref[...])
    z = _dot(hn, w_ref[...])
    q = z[:, _C_Q:_C_KV] * HEAD_DIM ** -0.5
    qi = z[:, _C_QI:_C_IX] * IDX_DIM ** -0.5
    kv_ref[...] = z[:, _C_KV:_C_QI]
    ix_ref[...] = z[:, _C_IX:_C_G]
    gate_ref[...] = z[:, _C_G:_C_X]
    xr_ref[...] = z[:, _C_X:_C_END]
    if stack:
        qb, qib = q.astype(BF16), qi.astype(BF16)
        for r in range(q.shape[0] // QB):
            for p in range(4):
                q_ref[r, p * QB:(p + 1) * QB, :] = qb[r * QB:(r + 1) * QB, p * LANES:(p + 1) * LANES]
                qi_ref[r, p * QB:(p + 1) * QB, :] = qib[r * QB:(r + 1) * QB, p * LANES:(p + 1) * LANES]
    else:
        lo = lax.broadcasted_iota(I32, (q.shape[0], LANES), 1) < HEAD_DIM
        for p in range(4):
            qp = q[:, p * LANES:(p + 1) * LANES]
            q_ref[2 * p] = jnp.where(lo, qp, 0.0)
            q_ref[2 * p + 1] = jnp.where(lo, 0.0, qp)
        qi_ref[...] = qi


def _inproj(x2d, g, w, *, stack, tm):
    m, d = x2d.shape
    if stack:
        q_shape, q_spec = (m // QB, 4 * QB, LANES), pl.BlockSpec((tm // QB, 4 * QB, LANES), lambda i: (i, 0, 0))
        qi_shape, qi_spec, qdt = q_shape, q_spec, BF16
    else:
        q_shape, q_spec = (N_HEADS, m, LANES), pl.BlockSpec((N_HEADS, tm, LANES), lambda i: (0, i, 0))
        qi_shape, qi_spec, qdt = (m, 512), pl.BlockSpec((tm, 512), lambda i: (i, 0)), F32
    row = lambda n: pl.BlockSpec((tm, n), lambda i: (i, 0))
    return pl.pallas_call(
        functools.partial(_inproj_kernel, stack=stack),
        grid=(m // tm,),
        in_specs=[row(d), _full_spec((1, d)), _full_spec(w.shape)],
        out_specs=[q_spec, qi_spec, row(256), row(128), row(512), row(512)],
        out_shape=[jax.ShapeDtypeStruct(q_shape, qdt), jax.ShapeDtypeStruct(qi_shape, qdt),
                   jax.ShapeDtypeStruct((m, 256), F32), jax.ShapeDtypeStruct((m, 128), F32),
                   jax.ShapeDtypeStruct((m, 512), F32), jax.ShapeDtypeStruct((m, 512), F32)],
        compiler_params=_cparams(1),
        name="inproj_stack" if stack else "inproj_dec",
    )(x2d, g, w)


def _bias_kernel(rb_ref, bk_ref, bkd_ref, o_ref, od_ref):
    for d in range(3):
        bk = bk_ref[d]
        for p in range(4):
            for a in range(2):
                h = p + 4 * a
                acc = jnp.zeros((QB, LANES), F32)
                for b in range(N_BUCKETS):
                    acc = jnp.where(bk == b, rb_ref[b, h], acc)
                o_ref[d, a * QB:(a + 1) * QB, p * LANES:(p + 1) * LANES] = acc
    bkd = bkd_ref[...]
    rowi = lax.broadcasted_iota(I32, (N_HEADS, 2 * LANES), 0)
    acc = jnp.zeros((N_HEADS, 2 * LANES), F32)
    for r in range(N_HEADS):
        h = r // 2 + 4 * (r % 2)
        for b in range(N_BUCKETS):
            acc = jnp.where((rowi == r) & (bkd == b), rb_ref[b, h], acc)
    od_ref[...] = acc


def _bias_tables(rel_bias, page):
    key = np.arange(QB)[:, None]
    qry = np.arange(LANES)[None, :]
    bk = np.stack([_t5_bucket_np(d * QB + qry - key) for d in range(3)])
    assert (_t5_bucket_np(np.arange(2 * QB + 1 - LANES, 4 * QB)) == N_BUCKETS - 1).all()
    assert (_t5_bucket_np(np.arange(page, 8 * page)) == N_BUCKETS - 1).all()
    dec = np.zeros((2 * LANES,), np.int64)
    dec[:page] = page - np.arange(page)
    dec[LANES] = 2 * REL_MAX_DIST
    dec[LANES + 1] = 0
    bkd = np.broadcast_to(_t5_bucket_np(dec)[None, :], (N_HEADS, 2 * LANES))
    return pl.pallas_call(
        _bias_kernel,
        in_specs=[pl.BlockSpec(memory_space=pltpu.SMEM), pl.BlockSpec(memory_space=pltpu.VMEM),
                  pl.BlockSpec(memory_space=pltpu.VMEM)],
        out_shape=[jax.ShapeDtypeStruct((3, 2 * QB, 4 * LANES), F32), jax.ShapeDtypeStruct((N_HEADS, 2 * LANES), F32)],
        name="bias_tables",
    )(rel_bias, jnp.asarray(bk, I32), jnp.asarray(bkd, I32))


def _search_widths(n_chunks):
    cuts = sorted({min(c, n_chunks) for c in (2, 4, 8, 12, 16)} | {n_chunks})
    return [c for c in cuts if c <= n_chunks]


def _attn_kernel(pt_ref, q_ref, qi_ref, ixq_ref, ixk_ref, kv_ref, bias_ref,
                 dq_ref, dkvn_ref, dsc_ref, dsn_ref, dthr_ref, dbias_ref, ck_ref, cv_ref,
                 o_ref, od_ref,
                 kblk, vblk_t, kiblk, keys, scores, logits, acc, thr_ref, cge_ref, cgt_ref, kbuf, vbuf, sems,
                 *, n_chunks, topk, n_pages, page, steps_per_seq):
    j = pl.program_id(1)
    step = pl.program_id(0) * n_chunks + j
    seq_s = lax.div(step, jnp.int32(steps_per_seq))
    phase = lax.rem(step, jnp.int32(steps_per_seq))

    @pl.when(phase == 0)
    def _fetch_sample_pages():
        _start_page_copies(ck_ref, pt_ref, seq_s, kbuf, sems.at[0], n_pages, page)
        _start_page_copies(cv_ref, pt_ref, seq_s, vbuf, sems.at[1], n_pages, page)

    lane = lax.broadcasted_iota(I32, (QB, LANES), 1)
    row = lax.broadcasted_iota(I32, (QB, LANES), 0)
    lo = lane < HEAD_DIM
    blocks = [(a, p) for a in range(2) for p in range(4)]
    rs = lambda a: slice(a * QB, (a + 1) * QB)
    cs = lambda p: slice(p * LANES, (p + 1) * LANES)
    chunk = lambda c: pl.ds(pl.multiple_of(c * QB, QB), QB)

    @pl.when(j == 0)
    def _build_block_diagonal_keys():
        def body(c, carry):
            kc = kv_ref[0, chunk(c), 0:LANES]
            vt = kv_ref[0, chunk(c), LANES:2 * LANES].T
            kia = jnp.where(lo, ixk_ref[0, chunk(c), :], 0.0)
            kblk[c, 0:QB, :] = jnp.where(lo, kc, 0.0).astype(BF16)
            kblk[c, QB:2 * QB, :] = jnp.where(lo, 0.0, kc).astype(BF16)
            vblk_t[c, :, 0:QB] = jnp.where(row < HEAD_DIM, vt, 0.0).astype(BF16)
            vblk_t[c, :, QB:2 * QB] = jnp.where(row < HEAD_DIM, 0.0, vt).astype(BF16)
            kiblk[c, 0:QB, :] = kia.astype(BF16)
            kiblk[c, QB:2 * QB, :] = pltpu.roll(kia, HEAD_DIM, 1).astype(BF16)
            return carry
        lax.fori_loop(0, n_chunks, body, 0)

    qi = qi_ref[0]
    q = q_ref[0]
    wt = ixq_ref[0].T
    w_row = {(a, p): wt[IDX_DIM + 2 * p + a:IDX_DIM + 2 * p + a + 1, :] for a, p in blocks}
    qpos = j * QB + lane

    n_pairs = (j + 2) // 2

    def chunk_loop(body, carry):
        n_quads = n_pairs // 2
        carry = lax.fori_loop(0, n_quads, lambda i, cr: body([4 * i + u for u in range(4)], cr), carry)
        return lax.fori_loop(2 * n_quads, n_pairs, lambda i, cr: body([2 * i, 2 * i + 1], cr), carry)

    def score_body(cs_, carry):
        for c in cs_:
            s = _dot_nt(kiblk[c], qi)
            lg = _dot_nt(kblk[c], q)
            sc = jnp.zeros((QB, LANES), F32)
            for a, p in blocks:
                sc = sc + jnp.maximum(s[rs(a), cs(p)], 0.0) * w_row[(a, p)]
            sc = sc * IDX_HEADS ** -0.5
            admissible = c * QB + row <= qpos
            scores[chunk(c), :] = jnp.where(admissible, sc, -jnp.inf)
            keys[chunk(c), :] = jnp.where(admissible, _float_key(sc), jnp.int32(INT_MIN))
            logits[c] = lg + bias_ref[jnp.clip(j - c, 0, 2)]
        return carry
    chunk_loop(score_body, 0)

    def fill_body(c, carry):
        scores[chunk(c), :] = jnp.full((QB, LANES), -jnp.inf, F32)
        keys[chunk(c), :] = jnp.full((QB, LANES), INT_MIN, I32)
        return carry
    lax.fori_loop(2 * n_pairs, n_chunks, fill_body, 0)

    kf = jnp.float32(topk)

    def count(src, width, pred):
        accs = [jnp.zeros((8, LANES), F32) for _ in range(8)]
        for g in range(width // 8):
            accs[g % 8] = accs[g % 8] + jnp.where(pred(src[g * 8:(g + 1) * 8, :]), 1.0, 0.0)
        return jnp.sum(functools.reduce(lambda x, y: x + y, accs), axis=0, keepdims=True)

    def search(width):
        def search_body(i, ans):
            cand = ans | jnp.left_shift(jnp.int32(1), 31 - i)
            cs_ = cand ^ jnp.int32(INT_MIN)
            return jnp.where(count(keys, width, lambda k: k >= cs_) >= kf, cand, ans)
        ans = lax.fori_loop(0, 32, search_body, jnp.zeros((1, LANES), I32))
        t = _key_float(ans ^ jnp.int32(INT_MIN))
        thr_ref[...] = t
        cge_ref[...] = count(scores, width, lambda s: s >= t)
        cgt_ref[...] = count(scores, width, lambda s: s > t)

    prev = 0
    for n in _search_widths(n_chunks):
        pl.when((j >= prev) & (j < n))(functools.partial(search, n * QB))
        prev = n

    s_len = scores.shape[0]
    thr, cge, cgt = _walk_to_kth(
        lambda: scores[...], thr_ref[...], cge_ref[...], cgt_ref[...], kf,
        lambda t: (count(scores, s_len, lambda s: s >= t), count(scores, s_len, lambda s: s > t)), 4 * topk)

    @pl.when(jnp.max(cge) > kf)
    def _break_ties_by_position():
        need = kf - cgt
        big = jnp.int32(2 * s_len)
        eqrow_ref = keys
        eqrow_ref[...] = jnp.where(scores[...] == thr, lax.broadcasted_iota(I32, scores.shape, 0), big)
        nbits = int(math.log2(s_len))

        def tie_body(i, best):
            cand = best | jnp.left_shift(jnp.int32(1), nbits - 1 - i)
            return jnp.where(count(eqrow_ref, s_len, lambda e: e < cand) < need, cand, best)
        last = lax.fori_loop(0, nbits, tie_body, jnp.zeros((1, LANES), I32))
        eqrow = eqrow_ref[...]
        scores[...] = jnp.where((eqrow > last) & (eqrow < big), -jnp.inf, scores[...])

    def mask_body(cs_, mx):
        mx = list(mx)
        for c in cs_:
            sel = scores[chunk(c), :] >= thr
            for n, (a, p) in enumerate(blocks):
                blk = jnp.where(sel, logits[c, rs(a), cs(p)], -jnp.inf)
                logits[c, rs(a), cs(p)] = blk
                mx[n] = jnp.maximum(mx[n], jnp.max(blk, axis=0, keepdims=True))
        return tuple(mx)
    mx = chunk_loop(mask_body, tuple(jnp.full((1, LANES), -jnp.inf, F32) for _ in blocks))

    acc[...] = jnp.zeros(acc.shape, F32)

    def pv_body(cs_, ls):
        ls = list(ls)
        pv = jnp.zeros(acc.shape, F32)
        for c in cs_:
            rows = []
            for a in range(2):
                cols = []
                for p in range(4):
                    e = jnp.exp(logits[c, rs(a), cs(p)] - mx[a * 4 + p])
                    ls[a * 4 + p] = ls[a * 4 + p] + jnp.sum(e, axis=0, keepdims=True)
                    cols.append(e.astype(BF16))
                rows.append(jnp.concatenate(cols, axis=1))
            pmat = jnp.concatenate(rows, axis=0)
            pv = pv + jnp.dot(vblk_t[c], pmat, preferred_element_type=F32)
        acc[...] = acc[...] + pv
        return tuple(ls)
    ls = chunk_loop(pv_body, tuple(jnp.zeros((1, LANES), F32) for _ in blocks))

    for p in range(4):
        inv = jnp.where(row < HEAD_DIM, 1.0 / ls[p], 1.0 / ls[4 + p])
        o_ref[0, :, cs(p)] = (acc[:, cs(p)] * inv).T.astype(BF16)

    @pl.when(phase == steps_per_seq // 2)
    def _sample_attention():
        _wait_page_copies(kbuf, sems.at[0])
        _wait_page_copies(vbuf, sems.at[1])
        od_ref[0] = _dec_attn_row(dq_ref[0], dkvn_ref[0], dsc_ref[0], dsn_ref[0], dthr_ref[0], dbias_ref[...],
                                  kbuf, vbuf, page)


def _attn(q_st, qi_st, ix, kv, bias_st, page_table, qm, kvn, sc_s, sn_s, thr_s, bias_dec, ck_t, cv_t,
          *, batch, seq, page):
    nq = seq // QB
    assert nq % 2 == 0
    topk = min(TOPK_MAX, seq // 4)
    ix3 = ix.reshape(batch, seq, LANES)
    kv3 = kv.reshape(batch, seq, 2 * LANES)
    db, n_pages = page_table.shape
    past = n_pages * page
    steps_per_seq = (batch * nq) // db
    assert steps_per_seq * db == batch * nq
    drow = lambda n: pl.BlockSpec((1, 1, n), lambda b, j, pt: ((b * nq + j) // steps_per_seq, 0, 0))
    return pl.pallas_call(
        functools.partial(_attn_kernel, n_chunks=nq, topk=topk, n_pages=n_pages, page=page,
                          steps_per_seq=steps_per_seq),
        grid_spec=pltpu.PrefetchScalarGridSpec(
            num_scalar_prefetch=1,
            grid=(batch, nq),
            in_specs=[pl.BlockSpec((1, 4 * QB, LANES), lambda b, j, pt: (b * nq + j, 0, 0)),
                      pl.BlockSpec((1, 4 * QB, LANES), lambda b, j, pt: (b * nq + j, 0, 0)),
                      pl.BlockSpec((1, QB, LANES), lambda b, j, pt: (b, j, 0)),
                      pl.BlockSpec((1, seq, LANES), lambda b, j, pt: (b, 0, 0)),
                      pl.BlockSpec((1, seq, 2 * LANES), lambda b, j, pt: (b, 0, 0)),
                      _full_spec(bias_st.shape),
                      pl.BlockSpec((1, N_HEADS, LANES), lambda b, j, pt: ((b * nq + j) // steps_per_seq, 0, 0)),
                      drow(2 * LANES), drow(past), drow(LANES), drow(LANES), _full_spec(bias_dec.shape),
                      pl.BlockSpec(memory_space=pl.ANY), pl.BlockSpec(memory_space=pl.ANY)],
            out_specs=[pl.BlockSpec((1, QB, 4 * LANES), lambda b, j, pt: (b, j, 0)), drow(4 * LANES)],
            scratch_shapes=[pltpu.VMEM((nq, 2 * QB, LANES), BF16), pltpu.VMEM((nq, LANES, 2 * QB), BF16),
                            pltpu.VMEM((nq, 2 * QB, LANES), BF16), pltpu.VMEM((seq, LANES), I32),
                            pltpu.VMEM((seq, LANES), F32),
                            pltpu.VMEM((nq, 2 * QB, 4 * LANES), F32), pltpu.VMEM((LANES, 4 * LANES), F32),
                            pltpu.VMEM((1, LANES), F32), pltpu.VMEM((1, LANES), F32), pltpu.VMEM((1, LANES), F32),
                            pltpu.VMEM((2 * HEAD_DIM, past), F32), pltpu.VMEM((2 * HEAD_DIM, past), F32),
                            pltpu.SemaphoreType.DMA((2,))]),
        out_shape=[jax.ShapeDtypeStruct((batch, seq, 4 * LANES), BF16),
                   jax.ShapeDtypeStruct((db, 1, 4 * LANES), BF16)],
        compiler_params=_cparams(2),
        name="attn",
    )(page_table, q_st, qi_st, ix3, ix3, kv3, bias_st, qm, kvn, sc_s, sn_s, thr_s, bias_dec, ck_t, cv_t)


def _rglru_gates(xc, wa, ba, wx, bx, lam):
    r = _sigmoid(_dot(xc, wa) + ba)
    i = _sigmoid(_dot(xc, wx) + bx)
    log_a = -RG_C * r * _softplus(-lam)
    a = jnp.exp(log_a)
    u = jnp.sqrt(1.0 - jnp.exp(2.0 * log_a)) * (i * xc)
    return a, u


def _rglru_prompt_kernel(g_ref, xr_ref, buf_ref, h0_ref, cw_ref, cb_ref, wa_ref, ba_ref, wx_ref, bx_ref, lam_ref,
                         o_ref, hl_ref, nb_ref, xs, a_s, u_s, tail, hc, *, tc):
    t = pl.program_id(0)
    width = cw_ref.shape[0]
    nb, _, d = g_ref.shape

    @pl.when(t == 0)
    def _load_state():
        tail[...] = jnp.zeros(tail.shape, F32)
        tail[:, 8 - (width - 1):8, :] = buf_ref[...]
        hc[...] = h0_ref[...]

    xs[:, 0:8, :] = tail[...]
    xs[:, 8:8 + tc, :] = xr_ref[...]
    tail[...] = xs[:, tc:tc + 8, :]
    xc = cb_ref[...]
    for jj in range(width):
        off = 8 - (width - 1) + jj
        xc = xc + cw_ref[jj:jj + 1, :] * xs[:, off:off + tc, :]
    a, u = _rglru_gates(xc.reshape(nb * tc, d), wa_ref[...], ba_ref[...], wx_ref[...], bx_ref[...], lam_ref[...])
    n_lb = d // LANES
    pitch = tc + 8
    for k in range(n_lb):
        for b in range(nb):
            a_s[k, b * pitch:b * pitch + tc, :] = a[b * tc:(b + 1) * tc, k * LANES:(k + 1) * LANES]
            u_s[k, b * pitch:b * pitch + tc, :] = u[b * tc:(b + 1) * tc, k * LANES:(k + 1) * LANES]

    def scan_body(i, hs):
        rows = pl.ds(i, nb, stride=pitch)
        out = []
        for k in range(n_lb):
            h = a_s[k, rows, :] * hs[k] + u_s[k, rows, :]
            u_s[k, rows, :] = h
            out.append(h)
        return tuple(out)
    h0 = hc[...]
    hs = lax.fori_loop(0, tc, scan_body, tuple(h0[:, k * LANES:(k + 1) * LANES] for k in range(n_lb)), unroll=8)
    h = jnp.concatenate(hs, axis=1)
    hc[...] = h
    hseq = jnp.concatenate([jnp.concatenate([u_s[k, b * pitch:b * pitch + tc, :] for b in range(nb)], axis=0)
                            for k in range(n_lb)], axis=1)
    o_ref[...] = (_gelu(g_ref[...].reshape(nb * tc, d)) * hseq).reshape(nb, tc, d).astype(BF16)
    hl_ref[...] = h
    nb_ref[...] = xs[:, tc + 8 - (width - 1):tc + 8, :]


def _rglru_prompt(gate, xr, buf, h0, rg, *, batch, seq, tc=256):
    d = gate.shape[-1]
    width = rg["cw"].shape[0]
    g3, x3 = gate.reshape(batch, seq, d), xr.reshape(batch, seq, d)
    blk = pl.BlockSpec((batch, tc, d), lambda t: (0, t, 0))
    vec = _full_spec((1, d))
    return pl.pallas_call(
        functools.partial(_rglru_prompt_kernel, tc=tc),
        grid=(seq // tc,),
        in_specs=[blk, blk, _full_spec((batch, width - 1, d)), _full_spec((batch, d)), _full_spec((width, d)), vec,
                  _full_spec((d, d)), vec, _full_spec((d, d)), vec, vec],
        out_specs=[blk, pl.BlockSpec((batch, d), lambda t: (0, 0)), pl.BlockSpec((batch, width - 1, d), lambda t: (0, 0, 0))],
        out_shape=[jax.ShapeDtypeStruct((batch, seq, d), BF16), jax.ShapeDtypeStruct((batch, d), F32),
                   jax.ShapeDtypeStruct((batch, width - 1, d), F32)],
        scratch_shapes=[pltpu.VMEM((batch, tc + 8, d), F32), pltpu.VMEM((d // LANES, batch * (tc + 8), LANES), F32),
                        pltpu.VMEM((d // LANES, batch * (tc + 8), LANES), F32), pltpu.VMEM((batch, 8, d), F32),
                        pltpu.VMEM((batch, d), F32)],
        compiler_params=_cparams(1),
        name="rglru_prompt",
    )(g3, x3, buf, h0, rg["cw"], rg["cb"], rg["wa"], rg["ba"], rg["wx"], rg["bx"], rg["lam"])


def _rglru_dec_kernel(g_ref, xr_ref, buf_ref, h0_ref, cw_ref, cb_ref, wa_ref, ba_ref, wx_ref, bx_ref, lam_ref,
                      o_ref, hl_ref):
    width = cw_ref.shape[0]
    xc = cb_ref[...]
    for jj in range(width - 1):
        xc = xc + cw_ref[jj:jj + 1, :] * buf_ref[jj]
    xc = xc + cw_ref[width - 1:width, :] * xr_ref[...]
    a, u = _rglru_gates(xc, wa_ref[...], ba_ref[...], wx_ref[...], bx_ref[...], lam_ref[...])
    h = a * h0_ref[...] + u
    hl_ref[...] = h
    o_ref[...] = (_gelu(g_ref[...]) * h).astype(BF16)


def _rglru_dec(gate, xr, buf_t, h0, rg):
    m, d = gate.shape
    return pl.pallas_call(
        _rglru_dec_kernel,
        out_shape=[jax.ShapeDtypeStruct((m, d), BF16), jax.ShapeDtypeStruct((m, d), F32)],
        name="rglru_dec",
    )(gate, xr, buf_t, h0, rg["cw"], rg["cb"], rg["wa"], rg["ba"], rg["wx"], rg["bx"], rg["lam"])


def _ffn_tile(y1, gf_ref, wup_ref, cw_ref, cb_ref, wdn_ref, conv_prev, n_split):
    d_ff = wdn_ref.shape[0]
    cf = d_ff // n_split
    hn = _rms(y1, gf_ref[...]).astype(BF16)
    out = jnp.zeros(y1.shape, F32)
    gates = []
    for k in range(n_split):
        c0 = k * cf
        g = jnp.dot(hn, wup_ref[:, c0:c0 + cf], preferred_element_type=F32)
        u = jnp.dot(hn, wup_ref[:, d_ff + c0:d_ff + c0 + cf], preferred_element_type=F32)
        g1, g2 = conv_prev(k, g)
        gc = cb_ref[:, c0:c0 + cf] + cw_ref[0:1, c0:c0 + cf] * g2 + cw_ref[1:2, c0:c0 + cf] * g1 \
            + cw_ref[2:3, c0:c0 + cf] * g
        act = (_gelu(gc) * u).astype(BF16)
        out = out + jnp.dot(act, wdn_ref[c0:c0 + cf, :], preferred_element_type=F32)
        gates.append(g)
    return out, gates


def _prompt_conv_prev(gs, carry, fb_ref, nb_ref, tm, cf):
    t = pl.program_id(1)

    @pl.when(t == 0)
    def _load_state():
        carry[...] = jnp.zeros(carry.shape, F32)
        for k in range(carry.shape[0]):
            carry[k, 6:8, :] = fb_ref[0, :, k * cf:(k + 1) * cf]

    def conv_prev(k, g):
        gs[0:8, :] = carry[k]
        gs[8:8 + tm, :] = g
        carry[k] = g[tm - 8:tm, :]
        nb_ref[0, :, k * cf:(k + 1) * cf] = g[tm - 2:tm, :]
        return gs[7:7 + tm, :], gs[6:6 + tm, :]
    return conv_prev


def _mix_ab_tile(y_ref, a_ref, r_ref, woa_ref, wob_ref):
    return y_ref[0] + jnp.dot(a_ref[0], woa_ref[...], preferred_element_type=F32) \
        + jnp.dot(r_ref[0], wob_ref[...], preferred_element_type=F32)


def _post_ab_prompt_kernel(y_ref, a_ref, r_ref, fb_ref, woa_ref, wob_ref, gf_ref, wup_ref, cw_ref, cb_ref, wdn_ref,
                           o_ref, nb_ref, gs, carry, *, tm, n_split):
    y1 = _mix_ab_tile(y_ref, a_ref, r_ref, woa_ref, wob_ref)
    cf = wdn_ref.shape[0] // n_split
    out, _ = _ffn_tile(y1, gf_ref, wup_ref, cw_ref, cb_ref, wdn_ref,
                       _prompt_conv_prev(gs, carry, fb_ref, nb_ref, tm, cf), n_split)
    o_ref[0] = y1 + out


_FFN_KEYS = ("g", "wup", "cw", "cb", "wdn")


def _ffn_specs(ffn):
    layer = ffn["layer"]
    return [pl.BlockSpec((None,) + ffn[k].shape[1:], lambda *_: (layer, 0, 0), pipeline_mode=pl.Buffered(1))
            for k in _FFN_KEYS]


def _ffn_args(ffn):
    return [ffn[k] for k in _FFN_KEYS]


def _post_ab_prompt(y, attn, rgo, fbuf, wo_a, wo_b, ffn, *, tm=512, n_split=2):
    batch, seq, d = y.shape
    d_ff = ffn["wdn"].shape[1]
    cf = d_ff // n_split
    blk = lambda n: pl.BlockSpec((1, tm, n), lambda b, t: (b, t, 0))
    fb = pl.BlockSpec((1, 2, d_ff), lambda b, t: (b, 0, 0))
    return pl.pallas_call(
        functools.partial(_post_ab_prompt_kernel, tm=tm, n_split=n_split),
        grid=(batch, seq // tm),
        in_specs=[blk(d), blk(attn.shape[-1]), blk(rgo.shape[-1]), fb, _full_spec(wo_a.shape), _full_spec(wo_b.shape)]
        + _ffn_specs(ffn),
        out_specs=[blk(d), fb],
        out_shape=[jax.ShapeDtypeStruct((batch, seq, d), F32), jax.ShapeDtypeStruct((batch, 2, d_ff), F32)],
        scratch_shapes=[pltpu.VMEM((tm + 8, cf), F32), pltpu.VMEM((n_split, 8, cf), F32)],
        compiler_params=_cparams(2),
        name="post_ab_prompt",
    )(y, attn, rgo, fbuf, wo_a, wo_b, *_ffn_args(ffn))


def _dec_conv_prev(fb_ref, cf):
    def conv_prev(k, g):
        return fb_ref[1, :, k * cf:(k + 1) * cf], fb_ref[0, :, k * cf:(k + 1) * cf]
    return conv_prev


def _post_ab_dec_kernel(y_ref, a_ref, r_ref, fb_ref, woa_ref, wob_ref, gf_ref, wup_ref, cw_ref, cb_ref, wdn_ref,
                        o_ref, g_ref, *, n_split):
    y1 = y_ref[...] + jnp.dot(a_ref[...], woa_ref[...], preferred_element_type=F32) \
        + jnp.dot(r_ref[...], wob_ref[...], preferred_element_type=F32)
    cf = wdn_ref.shape[0] // n_split
    out, gates = _ffn_tile(y1, gf_ref, wup_ref, cw_ref, cb_ref, wdn_ref, _dec_conv_prev(fb_ref, cf), n_split)
    o_ref[...] = y1 + out
    for k, g in enumerate(gates):
        g_ref[:, k * cf:(k + 1) * cf] = g


def _post_ab_dec(y, attn, rgo, fbuf_t, wo_a, wo_b, ffn, *, n_split=2):
    m, d = y.shape
    d_ff = ffn["wdn"].shape[1]
    args = (y, attn, rgo, fbuf_t, wo_a, wo_b)
    return pl.pallas_call(
        functools.partial(_post_ab_dec_kernel, n_split=n_split),
        grid=(1,),
        in_specs=[_full_spec(a.shape) for a in args] + _ffn_specs(ffn),
        out_specs=[_whole_spec((m, d)), _whole_spec((m, d_ff))],
        out_shape=[jax.ShapeDtypeStruct((m, d), F32), jax.ShapeDtypeStruct((m, d_ff), F32)],
        compiler_params=_cparams(1),
        name="post_ab_dec",
    )(*args, *_ffn_args(ffn))


def _gmlp_in(y, gm_ref, win_ref, bin_ref, sn_ref):
    d_c = win_ref.shape[1] // 2
    z = _gelu(_dot(_rms(y, gm_ref[...]), win_ref[...]) + bin_ref[...])
    return z[:, :d_c], _rms(z[:, d_c:], sn_ref[...])


def _layer_c_prompt_kernel(y_ref, fb_ref, gm_ref, win_ref, bin_ref, sn_ref, sw_ref, sbt_ref, woc_ref,
                           gf_ref, wup_ref, cw_ref, cb_ref, wdn_ref, gfin_ref,
                           o_ref, nb_ref, gs, carry, *, tm, n_split):
    y = y_ref[0]
    u, v = _gmlp_in(y, gm_ref, win_ref, bin_ref, sn_ref)
    vb = v.astype(BF16)
    n_groups = sw_ref.shape[0]
    tril = lax.broadcasted_iota(I32, (CHUNK, CHUNK), 0) >= lax.broadcasted_iota(I32, (CHUNK, CHUNK), 1)
    wm = [jnp.where(tril, sw_ref[gi], 0.0).astype(BF16) for gi in range(n_groups)]
    rows = []
    for r in range(tm // CHUNK):
        cols = []
        for gi in range(n_groups):
            mixed = jnp.dot(wm[gi], vb[r * CHUNK:(r + 1) * CHUNK, gi * LANES:(gi + 1) * LANES],
                            preferred_element_type=F32)
            cols.append(mixed + sbt_ref[:, gi:gi + 1])
        rows.append(jnp.concatenate(cols, axis=1))
    gated = u * jnp.concatenate(rows, axis=0)
    y1 = y + _dot(gated, woc_ref[...])
    cf = wdn_ref.shape[0] // n_split
    out, _ = _ffn_tile(y1, gf_ref, wup_ref, cw_ref, cb_ref, wdn_ref,
                       _prompt_conv_prev(gs, carry, fb_ref, nb_ref, tm, cf), n_split)
    o_ref[0] = _rms(y1 + out, gfin_ref[...])


def _layer_c_prompt(y, fbuf, cp, ffn, g_final, *, tm=512, n_split=2):
    batch, seq, d = y.shape
    d_ff = ffn["wdn"].shape[1]
    cf = d_ff // n_split
    blk = pl.BlockSpec((1, tm, d), lambda b, t: (b, t, 0))
    fb = pl.BlockSpec((1, 2, d_ff), lambda b, t: (b, 0, 0))
    consts = [cp["g"], cp["win"], cp["bin"], cp["sn"], cp["sw"], cp["sbt"], cp["woc"]]
    return pl.pallas_call(
        functools.partial(_layer_c_prompt_kernel, tm=tm, n_split=n_split),
        grid=(batch, seq // tm),
        in_specs=[blk, fb] + [_full_spec(c.shape) for c in consts] + _ffn_specs(ffn) + [_full_spec(g_final.shape)],
        out_specs=[blk, fb],
        out_shape=[jax.ShapeDtypeStruct((batch, seq, d), F32), jax.ShapeDtypeStruct((batch, 2, d_ff), F32)],
        scratch_shapes=[pltpu.VMEM((tm + 8, cf), F32), pltpu.VMEM((n_split, 8, cf), F32)],
        compiler_params=_cparams(2),
        name="layer_c_prompt",
    )(y, fbuf, *consts, *_ffn_args(ffn), g_final)


def _layer_c_dec_kernel(y_ref, fb_ref, gm_ref, win_ref, bin_ref, sn_ref, sw0_ref, sb0_ref, woc_ref,
                        gf_ref, wup_ref, cw_ref, cb_ref, wdn_ref, gfin_ref, o_ref, g_ref, v_ref, *, n_split):
    y = y_ref[...]
    u, v = _gmlp_in(y, gm_ref, win_ref, bin_ref, sn_ref)
    v_ref[...] = v
    y1 = y + _dot(u * (sw0_ref[...] * v + sb0_ref[...]), woc_ref[...])
    cf = wdn_ref.shape[0] // n_split
    out, gates = _ffn_tile(y1, gf_ref, wup_ref, cw_ref, cb_ref, wdn_ref, _dec_conv_prev(fb_ref, cf), n_split)
    o_ref[...] = _rms(y1 + out, gfin_ref[...])
    for k, g in enumerate(gates):
        g_ref[:, k * cf:(k + 1) * cf] = g


def _layer_c_dec(y, fbuf_t, cp, ffn, g_final, *, n_split=2):
    m, d = y.shape
    d_ff = ffn["wdn"].shape[1]
    d_c = cp["woc"].shape[0]
    args = (y, fbuf_t, cp["g"], cp["win"], cp["bin"], cp["sn"], cp["sw0"], cp["sb0"], cp["woc"])
    return pl.pallas_call(
        functools.partial(_layer_c_dec_kernel, n_split=n_split),
        grid=(1,),
        in_specs=[_full_spec(a.shape) for a in args] + _ffn_specs(ffn) + [_full_spec(g_final.shape)],
        out_specs=[_whole_spec((m, d)), _whole_spec((m, d_ff)), _whole_spec((m, d_c))],
        out_shape=[jax.ShapeDtypeStruct((m, d), F32), jax.ShapeDtypeStruct((m, d_ff), F32),
                   jax.ShapeDtypeStruct((m, d_c), F32)],
        compiler_params=_cparams(1),
        name="layer_c_dec",
    )(*args, *_ffn_args(ffn), g_final)


def _start_page_copies(src_ref, pt_ref, b, dst_ref, sem, n_pages, page):
    def body(pg, carry):
        col = pl.multiple_of(pg * page, page)
        pltpu.make_async_copy(src_ref.at[pt_ref[b, pg]], dst_ref.at[:, pl.ds(col, page)], sem).start()
        return carry
    lax.fori_loop(0, n_pages, body, 0, unroll=8)


def _wait_page_copies(dst_ref, sem):
    pltpu.make_async_copy(dst_ref, dst_ref, sem).wait()


def _dec_score_kernel(pt_ref, qi_ref, wi_ref, ixn_ref, cik_ref, keys_ref, knew_ref, ibuf, sems, *, n_pages, page):
    b = pl.program_id(0)
    nb = pl.num_programs(0)
    slot = lax.rem(b, 2)

    def start(bb, sl):
        _start_page_copies(cik_ref, pt_ref, bb, ibuf.at[sl], sems.at[sl], n_pages, page)

    @pl.when(b == 0)
    def _first():
        start(0, 0)

    @pl.when(b + 1 < nb)
    def _prefetch_next():
        start(b + 1, 1 - slot)

    _wait_page_copies(ibuf.at[slot], sems.at[slot])
    qi = qi_ref[0].astype(BF16)
    wi = wi_ref[0]
    s = jnp.dot(qi, ibuf[slot].astype(BF16), preferred_element_type=F32)
    sc = jnp.sum(jnp.maximum(s, 0.0) * wi, axis=0, keepdims=True) * IDX_HEADS ** -0.5
    keys_ref[0] = sc
    kin = ixn_ref[0][:, 0:IDX_DIM]
    sn = jnp.sum(qi_ref[0] * kin, axis=1, keepdims=True)
    scn = jnp.sum(jnp.maximum(sn, 0.0) * wi, axis=0, keepdims=True) * IDX_HEADS ** -0.5
    knew_ref[0] = jnp.broadcast_to(scn, (1, LANES))


def _dec_scores(page_table, qi3, wi3, ix3, cik_t, *, page):
    db, n_pages = page_table.shape
    past = n_pages * page
    return pl.pallas_call(
        functools.partial(_dec_score_kernel, n_pages=n_pages, page=page),
        grid_spec=pltpu.PrefetchScalarGridSpec(
            num_scalar_prefetch=1,
            grid=(db,),
            in_specs=[pl.BlockSpec((1, IDX_HEADS, IDX_DIM), lambda b, pt: (b, 0, 0)),
                      pl.BlockSpec((1, IDX_HEADS, 1), lambda b, pt: (b, 0, 0)),
                      pl.BlockSpec((1, 1, LANES), lambda b, pt: (b, 0, 0)),
                      pl.BlockSpec(memory_space=pl.ANY)],
            out_specs=[pl.BlockSpec((1, 1, past), lambda b, pt: (b, 0, 0)),
                       pl.BlockSpec((1, 1, LANES), lambda b, pt: (b, 0, 0))],
            scratch_shapes=[pltpu.VMEM((2, IDX_DIM, past), F32), pltpu.SemaphoreType.DMA((2,))]),
        out_shape=[jax.ShapeDtypeStruct((db, 1, past), F32), jax.ShapeDtypeStruct((db, 1, LANES), F32)],
        compiler_params=_cparams(1),
        name="dec_scores",
    )(page_table, qi3, wi3, ix3, cik_t)


def _dec_select_kernel(sc_ref, scn_ref, so_ref, sno_ref, thr_ref, *, topk):
    past = sc_ref.shape[1]
    lane0 = lax.broadcasted_iota(I32, scn_ref.shape, 1) == 0
    sc = jnp.concatenate([sc_ref[...], jnp.where(lane0, scn_ref[...], -jnp.inf)], axis=1)
    kk = _float_key(sc)
    kf = jnp.float32(topk)

    def count(pred):
        return jnp.sum(jnp.where(pred, 1.0, 0.0), axis=1, keepdims=True)

    def search_body(i, ans):
        cand = ans | jnp.left_shift(jnp.int32(1), 31 - i)
        return jnp.where(count(kk >= (cand ^ jnp.int32(INT_MIN))) >= kf, cand, ans)
    ans = lax.fori_loop(0, 32, search_body, jnp.zeros((sc.shape[0], 1), I32))
    thr = _key_float(ans ^ jnp.int32(INT_MIN))
    counts = lambda t: (count(sc >= t), count(sc > t))
    thr, cge, cgt = _walk_to_kth(lambda: sc, thr, *counts(thr), kf, counts, 4 * topk)
    need = kf - cgt
    big = jnp.int32(4 * past)
    eqcol = jnp.where(sc == thr, lax.broadcasted_iota(I32, sc.shape, 1), big)
    nbits = int(math.log2(past)) + 1

    def tie_body(i, best):
        cand = best | jnp.left_shift(jnp.int32(1), nbits - 1 - i)
        return jnp.where(count(eqcol < cand) < need, cand, best)
    last = lax.fori_loop(0, nbits, tie_body, jnp.zeros((sc.shape[0], 1), I32))
    sc = jnp.where((eqcol > last) & (eqcol < big), -jnp.inf, sc)
    so_ref[...] = sc[:, :past]
    sno_ref[...] = sc[:, past:]
    thr_ref[...] = jnp.broadcast_to(thr, thr_ref.shape)


def _dec_select(scores, snew, *, topk):
    db, past = scores.shape
    assert past + 1 >= topk
    return pl.pallas_call(
        functools.partial(_dec_select_kernel, topk=topk),
        out_shape=[jax.ShapeDtypeStruct((db, past), F32), jax.ShapeDtypeStruct((db, LANES), F32),
                   jax.ShapeDtypeStruct((db, LANES), F32)],
        name="dec_select",
    )(scores, snew)


def _dec_attn_row(qm, kvn, sc, sn, thr_row, bias, kbuf, vbuf, page):
    past = kbuf.shape[1]
    thr = thr_row[:, 0:1]
    sel = sc >= thr
    sel_new = sn[:, 0:1] >= thr
    far, last, bnew = bias[:, LANES:LANES + 1], bias[:, 0:page], bias[:, LANES + 1:LANES + 2]
    lg = jnp.dot(qm.astype(BF16), kbuf[...].astype(BF16), preferred_element_type=F32)
    lane = lax.broadcasted_iota(I32, lg.shape, 1)
    lastp = jnp.concatenate([jnp.zeros((N_HEADS, past - page), F32), last], axis=1)
    lg = jnp.where(sel, lg + jnp.where(lane >= past - page, lastp, far), -jnp.inf)
    lgn = jnp.sum(qm * kvn[:, 0:LANES], axis=1, keepdims=True) + bnew
    lgn = jnp.where(sel_new, lgn, -jnp.inf)
    m = jnp.maximum(jnp.max(lg, axis=1, keepdims=True), lgn)
    e = jnp.exp(lg - m)
    en = jnp.exp(lgn - m)
    den = jnp.sum(e, axis=1, keepdims=True) + en
    pv = _dot_nt(e.astype(BF16), vbuf[...].astype(BF16)) + en * kvn[:, LANES:2 * LANES]
    pv = pv / den
    lo = lax.broadcasted_iota(I32, (1, LANES), 1) < HEAD_DIM
    return jnp.concatenate([jnp.where(lo, pv[2 * p:2 * p + 1], pv[2 * p + 1:2 * p + 2]) for p in range(4)],
                           axis=1).astype(BF16)


def _prep_in_ab(w):
    d = w.shape[0]
    nq, nkv = N_HEADS * HEAD_DIM, N_KV_HEADS * HEAD_DIM
    offs = np.cumsum([nq, nkv, nkv, IDX_HEADS * IDX_DIM, IDX_DIM, IDX_HEADS, 512])
    q, k, v, qi, ki, wi, g, xr = jnp.split(w, offs.tolist(), axis=1)
    q = q.reshape(d, N_HEADS, HEAD_DIM)[:, np.array(HEAD_PERM), :].reshape(d, nq)
    pad = jnp.zeros((d, _C_G - _C_IX - IDX_DIM - IDX_HEADS), w.dtype)
    return jnp.concatenate([q, k, v, qi, ki, wi, pad, g, xr], axis=1).astype(BF16)


def _block_diag(w):
    n, c, _ = w.shape
    return (jnp.eye(n, dtype=w.dtype)[:, None, :, None] * w[:, :, None, :]).reshape(n * c, n * c).astype(BF16)


def _ffn_params(layer, stacked):
    return dict(stacked, layer=layer)


def kernel(x_prompt, x_sample, cache_k, cache_v, cache_idx_k, state_rglru_h, state_rglru_conv, state_ffn_conv,
           page_table, norm_mix, norm_ffn, norm_final, rel_bias, w_in_ab, w_out_ab, rg_conv_w, rg_conv_b,
           rg_wa, rg_ba, rg_wx, rg_bx, rg_lambda, w_in_c, b_in_c, sgu_norm, sgu_w, sgu_b, w_out_c,
           ffn_w_up, ffn_conv_w, ffn_conv_b, ffn_w_down):
    batch, seq, d = x_prompt.shape
    db = x_sample.shape[0]
    page = cache_k.shape[2]
    d_a = N_HEADS * HEAD_DIM
    d_b = rg_conv_w.shape[-1]
    d_ff = ffn_w_down.shape[1]
    assert x_sample.shape[1] == 1 and seq % 512 == 0 and page == LANES and w_in_ab.shape[0] == 1

    w_in0 = _prep_in_ab(w_in_ab[0])
    wo = w_out_ab[0]
    wo_a = wo[:d_a].reshape(N_HEADS, HEAD_DIM, d)[np.array(HEAD_PERM)].reshape(d_a, d).astype(BF16)
    wo_b = wo[d_a:].astype(BF16)
    rg = {"cw": rg_conv_w[0], "cb": rg_conv_b[0][None], "wa": _block_diag(rg_wa[0]), "ba": rg_ba[0][None],
          "wx": _block_diag(rg_wx[0]), "bx": rg_bx[0][None], "lam": rg_lambda[0][None]}
    ffn_all = {"g": norm_ffn[:, None, :], "wup": ffn_w_up.astype(BF16), "cw": ffn_conv_w, "cb": ffn_conv_b[:, None, :],
               "wdn": ffn_w_down.astype(BF16)}
    ffn0, ffn1 = _ffn_params(0, ffn_all), _ffn_params(1, ffn_all)
    cp = {"g": norm_mix[1][None], "win": w_in_c[0].astype(BF16), "bin": b_in_c[0][None], "sn": sgu_norm[0][None],
          "sw": sgu_w[0], "sbt": sgu_b[0].T, "woc": w_out_c[0].astype(BF16),
          "sw0": jnp.repeat(sgu_w[0][:, 0, 0], d // sgu_w.shape[1])[None],
          "sb0": jnp.repeat(sgu_b[0][:, 0], d // sgu_w.shape[1])[None]}
    g_mix0 = norm_mix[0][None]
    g_final = norm_final[None]
    bias_st, bias_dec = _bias_tables(rel_bias, page)

    xp = x_prompt.reshape(batch * seq, d)
    xs = x_sample.reshape(db, d)
    q_st, qi_st, kv_p, ix_p, gate_p, xr_p = _inproj(xp, g_mix0, w_in0, stack=True, tm=512)
    qm_s, qi_s, kv_s, ix_s, gate_s, xr_s = _inproj(xs, g_mix0, w_in0, stack=False, tm=db)
    cik_t = jnp.transpose(cache_idx_k[0], (0, 2, 1))
    ck_t = jnp.transpose(cache_k[0], (0, 2, 3, 1)).reshape(-1, 2 * HEAD_DIM, page)
    cv_t = jnp.transpose(cache_v[0], (0, 2, 3, 1)).reshape(-1, 2 * HEAD_DIM, page)
    topk_s = min(TOPK_MAX, (page_table.shape[1] * page + 1) // 4)
    sc_s, sn_s = _dec_scores(page_table, qi_s.reshape(db, IDX_HEADS, IDX_DIM),
                             ix_s[:, IDX_DIM:IDX_DIM + IDX_HEADS].reshape(db, IDX_HEADS, 1),
                             ix_s.reshape(db, 1, LANES), cik_t, page=page)
    sc_s, sn_s, thr_s = _dec_select(sc_s.reshape(db, -1), sn_s.reshape(db, LANES), topk=topk_s)

    attn_p, attn_s = _attn(q_st, qi_st, ix_p, kv_p, bias_st, page_table, jnp.transpose(qm_s, (1, 0, 2)),
                           kv_s.reshape(db, 1, 2 * LANES), sc_s.reshape(db, 1, -1), sn_s.reshape(db, 1, LANES),
                           thr_s.reshape(db, 1, LANES), bias_dec, ck_t, cv_t, batch=batch, seq=seq, page=page)
    attn_s = attn_s.reshape(db, d_a)

    rg_p, h_p, cbuf_p = _rglru_prompt(gate_p, xr_p, jnp.zeros((batch, rg["cw"].shape[0] - 1, d_b), F32),
                                      jnp.zeros((batch, d_b), F32), rg, batch=batch, seq=seq)
    zero_fb = jnp.zeros((batch, 2, d_ff), F32)
    y1_p, fb0_p = _post_ab_prompt(x_prompt, attn_p, rg_p, zero_fb, wo_a, wo_b, ffn0)
    y_p, fb1_p = _layer_c_prompt(y1_p, zero_fb, cp, ffn1, g_final)

    cbuf_s_in = state_rglru_conv[0]
    rg_s, h_s = _rglru_dec(gate_s, xr_s, jnp.transpose(cbuf_s_in, (1, 0, 2)), state_rglru_h[0], rg)
    y1_s, g0_s = _post_ab_dec(xs, attn_s, rg_s, jnp.transpose(state_ffn_conv[0], (1, 0, 2)), wo_a, wo_b, ffn0)
    y_s, g1_s, v_s = _layer_c_dec(y1_s, jnp.transpose(state_ffn_conv[1], (1, 0, 2)), cp, ffn1, g_final)

    kv4 = kv_p.reshape(batch, seq, 2, N_KV_HEADS, HEAD_DIM)
    kvs = kv_s.reshape(db, 1, 2, N_KV_HEADS, HEAD_DIM)
    fbuf_s = lambda layer, g: jnp.concatenate([state_ffn_conv[layer][:, 1:], g[:, None]], axis=1)
    return (y_p, y_s.reshape(db, 1, d),
            kv4[None, :, :, 0], kv4[None, :, :, 1], ix_p.reshape(batch, seq, LANES)[None, :, :, :IDX_DIM],
            kvs[None, :, :, 0], kvs[None, :, :, 1], ix_s.reshape(db, 1, LANES)[None, :, :, :IDX_DIM],
            h_p.reshape(batch, d_b)[None], cbuf_p[None],
            h_s[None], jnp.concatenate([cbuf_s_in[:, 1:], xr_s[:, None]], axis=1)[None],
            v_s.reshape(db, 1, -1)[None],
            jnp.stack([fb0_p, fb1_p]), jnp.stack([fbuf_s(0, g0_s), fbuf_s(1, g1_s)]))
```

```python
import functools
import math

import numpy as np
import jax
import jax.numpy as jnp
from jax import lax
from jax.experimental import pallas as pl
from jax.experimental.pallas import tpu as pltpu

F32 = jnp.float32
BF16 = jnp.bfloat16
I32 = jnp.int32

N_HEADS = 8
HEAD_DIM = 64
N_KV_HEADS = 2
Q_PER_KV = N_HEADS // N_KV_HEADS
IDX_HEADS = 8
IDX_DIM = 64
TOPK_MAX = 256
N_BUCKETS = 32
REL_MAX_EXACT = N_BUCKETS // 2
REL_MAX_DIST = 128
RG_C = 8.0
CHUNK = 128
EPS = 1e-6

LANES = 128
QB = 128
SEARCH_FIXED_BITS = 20
INT_MIN = -(2 ** 31)
KEY_MIN_FINITE = INT_MIN + 0x800000
NEG_MAX = float(np.finfo(np.float32).min)
HEAD_PERM = (0, 4, 1, 5, 2, 6, 3, 7)
VMEM_LIMIT = 56 * 1024 * 1024


def _cparams(n_grid):
    return pltpu.CompilerParams(dimension_semantics=("arbitrary",) * n_grid, vmem_limit_bytes=VMEM_LIMIT)


def _full_spec(shape):
    nd = len(shape)
    return pl.BlockSpec(shape, lambda *_: (0,) * nd, pipeline_mode=pl.Buffered(1))


def _whole_spec(shape):
    nd = len(shape)
    return pl.BlockSpec(shape, lambda *_: (0,) * nd)


def _rms(x, g):
    return x * lax.rsqrt(jnp.mean(x * x, axis=-1, keepdims=True) + EPS) * g


def _gelu(x):
    return x * (0.5 * (1.0 + jnp.tanh(math.sqrt(2.0 / math.pi) * (x + 0.044715 * (x * x * x)))))


def _sigmoid(x):
    return 1.0 / (1.0 + jnp.exp(-x))


def _softplus(x):
    return jnp.maximum(x, 0.0) + jnp.log(1.0 + jnp.exp(-jnp.abs(x)))


def _dot(a, b):
    return jnp.dot(a.astype(BF16), b, preferred_element_type=F32)


def _dot_nt(a, b):
    return lax.dot_general(a, b, (((1,), (1,)), ((), ())), preferred_element_type=F32)


def _float_key(x):
    bits = pltpu.bitcast(x, I32)
    key = jnp.where(bits < 0, bits ^ jnp.int32(0x7FFFFFFF), bits)
    return jnp.where(bits == jnp.int32(INT_MIN), jnp.int32(0), key)


def _key_float(key):
    key = jnp.maximum(key, jnp.int32(KEY_MIN_FINITE))
    return pltpu.bitcast(jnp.where(key < 0, key ^ jnp.int32(0x7FFFFFFF), key), F32)


def _walk_to_kth(scores, t, cge, cgt, kf, count_ge_gt, max_steps):
    axis = 0 if t.shape[0] == 1 else 1

    def settled(t, cge, cgt):
        return (cge == kf) | ((cgt < kf) & ((cge >= kf) | (t <= NEG_MAX)))

    def unsettled(t, cge, cgt):
        return jnp.max(jnp.where(settled(t, cge, cgt), 0.0, 1.0)) > 0.0

    def body(st):
        t, cge, cgt, it = st
        sc = scores()
        below = jnp.max(jnp.where(sc < t, sc, -jnp.inf), axis=axis, keepdims=True)
        above = jnp.min(jnp.where(sc > t, sc, jnp.inf), axis=axis, keepdims=True)
        t = jnp.where(cgt >= kf, above, jnp.where((cge < kf) & (t > NEG_MAX), jnp.maximum(below, NEG_MAX), t))
        cge, cgt = count_ge_gt(t)
        return t, cge, cgt, it + 1

    t, cge, cgt, _ = lax.while_loop(lambda st: unsettled(st[0], st[1], st[2]) & (st[3] < max_steps), body,
                                    (t, cge, cgt, jnp.int32(0)))
    return t, cge, cgt


def _t5_bucket_np(n):
    n = np.maximum(n, 0)
    nf = np.maximum(n, 1).astype(np.float32)
    large = REL_MAX_EXACT + (np.log(nf / np.float32(REL_MAX_EXACT)) / np.float32(math.log(REL_MAX_DIST / REL_MAX_EXACT))
                             * np.float32(N_BUCKETS - REL_MAX_EXACT)).astype(np.int32)
    large = np.minimum(large, N_BUCKETS - 1)
    return np.where(n < REL_MAX_EXACT, n, large).astype(np.int32)


_C_Q, _C_KV, _C_QI, _C_IX, _C_G, _C_X, _C_END = 0, 512, 768, 1280, 1408, 1920, 2432


def _inproj_kernel(x_ref, g_ref, w_ref, q_ref, qi_ref, kv_ref, ix_ref, gate_ref, xr_ref, *, stack):
    hn = _rms(x_ref[...], g_ref[...])
    z = _dot(hn, w_ref[...])
    q = z[:, _C_Q:_C_KV] * HEAD_DIM ** -0.5
    qi = z[:, _C_QI:_C_IX] * IDX_DIM ** -0.5
    kv_ref[...] = z[:, _C_KV:_C_QI]
    ix_ref[...] = z[:, _C_IX:_C_G]
    gate_ref[...] = z[:, _C_G:_C_X]
    xr_ref[...] = z[:, _C_X:_C_END]
    if stack:
        qb, qib = q.astype(BF16), qi.astype(BF16)
        for r in range(q.shape[0] // QB):
            for p in range(4):
                q_ref[r, p * QB:(p + 1) * QB, :] = qb[r * QB:(r + 1) * QB, p * LANES:(p + 1) * LANES]
                qi_ref[r, p * QB:(p + 1) * QB, :] = qib[r * QB:(r + 1) * QB, p * LANES:(p + 1) * LANES]
    else:
        lo = lax.broadcasted_iota(I32, (q.shape[0], LANES), 1) < HEAD_DIM
        for p in range(4):
            qp = q[:, p * LANES:(p + 1) * LANES]
            q_ref[2 * p] = jnp.where(lo, qp, 0.0)
            q_ref[2 * p + 1] = jnp.where(lo, 0.0, qp)
        qi_ref[...] = qi


def _inproj(x2d, g, w, *, stack, tm):
    m, d = x2d.shape
    if stack:
        q_shape, q_spec = (m // QB, 4 * QB, LANES), pl.BlockSpec((tm // QB, 4 * QB, LANES), lambda i: (i, 0, 0))
        qi_shape, qi_spec, qdt = q_shape, q_spec, BF16
    else:
        q_shape, q_spec = (N_HEADS, m, LANES), pl.BlockSpec((N_HEADS, tm, LANES), lambda i: (0, i, 0))
        qi_shape, qi_spec, qdt = (m, 512), pl.BlockSpec((tm, 512), lambda i: (i, 0)), F32
    row = lambda n: pl.BlockSpec((tm, n), lambda i: (i, 0))
    return pl.pallas_call(
        functools.partial(_inproj_kernel, stack=stack),
        grid=(m // tm,),
        in_specs=[row(d), _full_spec((1, d)), _full_spec(w.shape)],
        out_specs=[q_spec, qi_spec, row(256), row(128), row(512), row(512)],
        out_shape=[jax.ShapeDtypeStruct(q_shape, qdt), jax.ShapeDtypeStruct(qi_shape, qdt),
                   jax.ShapeDtypeStruct((m, 256), F32), jax.ShapeDtypeStruct((m, 128), F32),
                   jax.ShapeDtypeStruct((m, 512), F32), jax.ShapeDtypeStruct((m, 512), F32)],
        compiler_params=_cparams(1),
        name="inproj_stack" if stack else "inproj_dec",
    )(x2d, g, w)


def _bias_kernel(rb_ref, bk_ref, bkd_ref, o_ref, od_ref):
    for d in range(3):
        bk = bk_ref[d]
        for p in range(4):
            for a in range(2):
                h = p + 4 * a
                acc = jnp.zeros((QB, LANES), F32)
                for b in range(N_BUCKETS):
                    acc = jnp.where(bk == b, rb_ref[b, h], acc)
                o_ref[d, a * QB:(a + 1) * QB, p * LANES:(p + 1) * LANES] = acc
    bkd = bkd_ref[...]
    rowi = lax.broadcasted_iota(I32, (N_HEADS, 2 * LANES), 0)
    acc = jnp.zeros((N_HEADS, 2 * LANES), F32)
    for r in range(N_HEADS):
        h = r // 2 + 4 * (r % 2)
        for b in range(N_BUCKETS):
            acc = jnp.where((rowi == r) & (bkd == b), rb_ref[b, h], acc)
    od_ref[...] = acc


def _bias_tables(rel_bias, page):
    key = np.arange(QB)[:, None]
    qry = np.arange(LANES)[None, :]
    bk = np.stack([_t5_bucket_np(d * QB + qry - key) for d in range(3)])
    assert (_t5_bucket_np(np.arange(2 * QB + 1 - LANES, 4 * QB)) == N_BUCKETS - 1).all()
    assert (_t5_bucket_np(np.arange(page, 8 * page)) == N_BUCKETS - 1).all()
    dec = np.zeros((2 * LANES,), np.int64)
    dec[:page] = page - np.arange(page)
    dec[LANES] = 2 * REL_MAX_DIST
    dec[LANES + 1] = 0
    bkd = np.broadcast_to(_t5_bucket_np(dec)[None, :], (N_HEADS, 2 * LANES))
    return pl.pallas_call(
        _bias_kernel,
        in_specs=[pl.BlockSpec(memory_space=pltpu.SMEM), pl.BlockSpec(memory_space=pltpu.VMEM),
                  pl.BlockSpec(memory_space=pltpu.VMEM)],
        out_shape=[jax.ShapeDtypeStruct((3, 2 * QB, 4 * LANES), F32), jax.ShapeDtypeStruct((N_HEADS, 2 * LANES), F32)],
        name="bias_tables",
    )(rel_bias, jnp.asarray(bk, I32), jnp.asarray(bkd, I32))


def _search_widths(n_chunks):
    cuts = sorted({min(c, n_chunks) for c in (2, 4, 8, 12, 16)} | {n_chunks})
    return [c for c in cuts if c <= n_chunks]


def _attn_kernel(pt_ref, q_ref, qi_ref, ixq_ref, ixk_ref, kv_ref, bias_ref,
                 dq_ref, dkvn_ref, dsc_ref, dsn_ref, dthr_ref, dbias_ref, ck_ref, cv_ref,
                 o_ref, od_ref,
                 kblk, vblk_t, kiblk, keys, scores, logits, acc, thr_ref, cge_ref, cgt_ref, kbuf, vbuf, sems,
                 *, n_chunks, topk, n_pages, page, steps_per_seq):
    j = pl.program_id(1)
    step = pl.program_id(0) * n_chunks + j
    seq_s = lax.div(step, jnp.int32(steps_per_seq))
    phase = lax.rem(step, jnp.int32(steps_per_seq))

    @pl.when(phase == 0)
    def _fetch_sample_pages():
        _start_page_copies(ck_ref, pt_ref, seq_s, kbuf, sems.at[0], n_pages, page)
        _start_page_copies(cv_ref, pt_ref, seq_s, vbuf, sems.at[1], n_pages, page)

    lane = lax.broadcasted_iota(I32, (QB, LANES), 1)
    row = lax.broadcasted_iota(I32, (QB, LANES), 0)
    lo = lane < HEAD_DIM
    blocks = [(a, p) for a in range(2) for p in range(4)]
    rs = lambda a: slice(a * QB, (a + 1) * QB)
    cs = lambda p: slice(p * LANES, (p + 1) * LANES)
    chunk = lambda c: pl.ds(pl.multiple_of(c * QB, QB), QB)

    @pl.when(j == 0)
    def _build_block_diagonal_keys():
        def body(c, carry):
            kc = kv_ref[0, chunk(c), 0:LANES]
            vt = kv_ref[0, chunk(c), LANES:2 * LANES].T
            kia = jnp.where(lo, ixk_ref[0, chunk(c), :], 0.0)
            kblk[c, 0:QB, :] = jnp.where(lo, kc, 0.0).astype(BF16)
            kblk[c, QB:2 * QB, :] = jnp.where(lo, 0.0, kc).astype(BF16)
            vblk_t[c, :, 0:QB] = jnp.where(row < HEAD_DIM, vt, 0.0).astype(BF16)
            vblk_t[c, :, QB:2 * QB] = jnp.where(row < HEAD_DIM, 0.0, vt).astype(BF16)
            kiblk[c, 0:QB, :] = kia.astype(BF16)
            kiblk[c, QB:2 * QB, :] = pltpu.roll(kia, HEAD_DIM, 1).astype(BF16)
            return carry
        lax.fori_loop(0, n_chunks, body, 0)

    qi = qi_ref[0]
    q = q_ref[0]
    wt = ixq_ref[0].T
    w_row = {(a, p): wt[IDX_DIM + 2 * p + a:IDX_DIM + 2 * p + a + 1, :] for a, p in blocks}
    qpos = j * QB + lane

    n_pairs = (j + 2) // 2

    def chunk_loop(body, carry):
        n_quads = n_pairs // 2
        carry = lax.fori_loop(0, n_quads, lambda i, cr: body([4 * i + u for u in range(4)], cr), carry)
        return lax.fori_loop(2 * n_quads, n_pairs, lambda i, cr: body([2 * i, 2 * i + 1], cr), carry)

    def score_body(cs_, carry):
        for c in cs_:
            s = _dot_nt(kiblk[c], qi)
            lg = _dot_nt(kblk[c], q)
            sc = jnp.zeros((QB, LANES), F32)
            for a, p in blocks:
                sc = sc + jnp.maximum(s[rs(a), cs(p)], 0.0) * w_row[(a, p)]
            sc = sc * IDX_HEADS ** -0.5
            admissible = c * QB + row <= qpos
            scores[chunk(c), :] = jnp.where(admissible, sc, -jnp.inf)
            keys[chunk(c), :] = jnp.where(admissible, _float_key(sc), jnp.int32(INT_MIN))
            logits[c] = lg + bias_ref[jnp.clip(j - c, 0, 2)]
        return carry
    chunk_loop(score_body, 0)

    def fill_body(c, carry):
        scores[chunk(c), :] = jnp.full((QB, LANES), -jnp.inf, F32)
        keys[chunk(c), :] = jnp.full((QB, LANES), INT_MIN, I32)
        return carry
    lax.fori_loop(2 * n_pairs, n_chunks, fill_body, 0)

    kf = jnp.float32(topk)

    def count(src, width, pred):
        accs = [jnp.zeros((8, LANES), F32) for _ in range(8)]
        for g in range(width // 8):
            accs[g % 8] = accs[g % 8] + jnp.where(pred(src[g * 8:(g + 1) * 8, :]), 1.0, 0.0)
        return jnp.sum(functools.reduce(lambda x, y: x + y, accs), axis=0, keepdims=True)

    def search(width):
        def search_body(i, st):
            ans, reach = st
            cand = ans | jnp.left_shift(jnp.int32(1), 31 - i)
            cs_ = cand ^ jnp.int32(INT_MIN)
            cnt = count(keys, width, lambda k: k >= cs_)
            return jnp.where(cnt >= kf, cand, ans), jnp.where(cnt >= kf, cnt, reach)
        st = lax.fori_loop(0, SEARCH_FIXED_BITS, search_body,
                           (jnp.zeros((1, LANES), I32), jnp.full((1, LANES), width, F32)))
        ans, _, _ = lax.while_loop(
            lambda s: (s[2] < 32) & (jnp.max(jnp.where(s[1] == kf, 0.0, 1.0)) > 0.0),
            lambda s: search_body(s[2], s[:2]) + (s[2] + 1,), st + (jnp.int32(SEARCH_FIXED_BITS),))
        t = _key_float(ans ^ jnp.int32(INT_MIN))
        thr_ref[...] = t
        cge_ref[...] = count(scores, width, lambda s: s >= t)
        cgt_ref[...] = count(scores, width, lambda s: s > t)

    prev = 0
    for n in _search_widths(n_chunks):
        pl.when((j >= prev) & (j < n))(functools.partial(search, n * QB))
        prev = n

    s_len = scores.shape[0]
    thr, cge, cgt = _walk_to_kth(
        lambda: scores[...], thr_ref[...], cge_ref[...], cgt_ref[...], kf,
        lambda t: (count(scores, s_len, lambda s: s >= t), count(scores, s_len, lambda s: s > t)), 4 * topk)

    @pl.when(jnp.max(cge) > kf)
    def _break_ties_by_position():
        need = kf - cgt
        big = jnp.int32(2 * s_len)
        eqrow_ref = keys
        eqrow_ref[...] = jnp.where(scores[...] == thr, lax.broadcasted_iota(I32, scores.shape, 0), big)
        nbits = int(math.log2(s_len))

        def tie_body(i, best):
            cand = best | jnp.left_shift(jnp.int32(1), nbits - 1 - i)
            return jnp.where(count(eqrow_ref, s_len, lambda e: e < cand) < need, cand, best)
        last = lax.fori_loop(0, nbits, tie_body, jnp.zeros((1, LANES), I32))
        eqrow = eqrow_ref[...]
        scores[...] = jnp.where((eqrow > last) & (eqrow < big), -jnp.inf, scores[...])

    def mask_body(cs_, mx):
        mx = list(mx)
        for c in cs_:
            sel = scores[chunk(c), :] >= thr
            for n, (a, p) in enumerate(blocks):
                blk = jnp.where(sel, logits[c, rs(a), cs(p)], -jnp.inf)
                logits[c, rs(a), cs(p)] = blk
                mx[n] = jnp.maximum(mx[n], jnp.max(blk, axis=0, keepdims=True))
        return tuple(mx)
    mx = chunk_loop(mask_body, tuple(jnp.full((1, LANES), -jnp.inf, F32) for _ in blocks))

    acc[...] = jnp.zeros(acc.shape, F32)

    def pv_body(cs_, ls):
        ls = list(ls)
        pv = jnp.zeros(acc.shape, F32)
        for c in cs_:
            rows = []
            for a in range(2):
                cols = []
                for p in range(4):
                    e = jnp.exp(logits[c, rs(a), cs(p)] - mx[a * 4 + p])
                    ls[a * 4 + p] = ls[a * 4 + p] + jnp.sum(e, axis=0, keepdims=True)
                    cols.append(e.astype(BF16))
                rows.append(jnp.concatenate(cols, axis=1))
            pmat = jnp.concatenate(rows, axis=0)
            pv = pv + jnp.dot(vblk_t[c], pmat, preferred_element_type=F32)
        acc[...] = acc[...] + pv
        return tuple(ls)
    ls = chunk_loop(pv_body, tuple(jnp.zeros((1, LANES), F32) for _ in blocks))

    for p in range(4):
        inv = jnp.where(row < HEAD_DIM, 1.0 / ls[p], 1.0 / ls[4 + p])
        o_ref[0, :, cs(p)] = (acc[:, cs(p)] * inv).T.astype(BF16)

    @pl.when(phase == steps_per_seq // 2)
    def _sample_attention():
        _wait_page_copies(kbuf, sems.at[0])
        _wait_page_copies(vbuf, sems.at[1])
        od_ref[0] = _dec_attn_row(dq_ref[0], dkvn_ref[0], dsc_ref[0], dsn_ref[0], dthr_ref[0], dbias_ref[...],
                                  kbuf, vbuf, page)


def _attn(q_st, qi_st, ix, kv, bias_st, page_table, qm, kvn, sc_s, sn_s, thr_s, bias_dec, ck_t, cv_t,
          *, batch, seq, page):
    nq = seq // QB
    assert nq % 2 == 0
    topk = min(TOPK_MAX, seq // 4)
    ix3 = ix.reshape(batch, seq, LANES)
    kv3 = kv.reshape(batch, seq, 2 * LANES)
    db, n_pages = page_table.shape
    past = n_pages * page
    steps_per_seq = (batch * nq) // db
    assert steps_per_seq * db == batch * nq
    drow = lambda n: pl.BlockSpec((1, 1, n), lambda b, j, pt: ((b * nq + j) // steps_per_seq, 0, 0))
    return pl.pallas_call(
        functools.partial(_attn_kernel, n_chunks=nq, topk=topk, n_pages=n_pages, page=page,
                          steps_per_seq=steps_per_seq),
        grid_spec=pltpu.PrefetchScalarGridSpec(
            num_scalar_prefetch=1,
            grid=(batch, nq),
            in_specs=[pl.BlockSpec((1, 4 * QB, LANES), lambda b, j, pt: (b * nq + j, 0, 0)),
                      pl.BlockSpec((1, 4 * QB, LANES), lambda b, j, pt: (b * nq + j, 0, 0)),
                      pl.BlockSpec((1, QB, LANES), lambda b, j, pt: (b, j, 0)),
                      pl.BlockSpec((1, seq, LANES), lambda b, j, pt: (b, 0, 0)),
                      pl.BlockSpec((1, seq, 2 * LANES), lambda b, j, pt: (b, 0, 0)),
                      _full_spec(bias_st.shape),
                      pl.BlockSpec((1, N_HEADS, LANES), lambda b, j, pt: ((b * nq + j) // steps_per_seq, 0, 0)),
                      drow(2 * LANES), drow(past), drow(LANES), drow(LANES), _full_spec(bias_dec.shape),
                      pl.BlockSpec(memory_space=pl.ANY), pl.BlockSpec(memory_space=pl.ANY)],
            out_specs=[pl.BlockSpec((1, QB, 4 * LANES), lambda b, j, pt: (b, j, 0)), drow(4 * LANES)],
            scratch_shapes=[pltpu.VMEM((nq, 2 * QB, LANES), BF16), pltpu.VMEM((nq, LANES, 2 * QB), BF16),
                            pltpu.VMEM((nq, 2 * QB, LANES), BF16), pltpu.VMEM((seq, LANES), I32),
                            pltpu.VMEM((seq, LANES), F32),
                            pltpu.VMEM((nq, 2 * QB, 4 * LANES), F32), pltpu.VMEM((LANES, 4 * LANES), F32),
                            pltpu.VMEM((1, LANES), F32), pltpu.VMEM((1, LANES), F32), pltpu.VMEM((1, LANES), F32),
                            pltpu.VMEM((2 * HEAD_DIM, past), F32), pltpu.VMEM((2 * HEAD_DIM, past), F32),
                            pltpu.SemaphoreType.DMA((2,))]),
        out_shape=[jax.ShapeDtypeStruct((batch, seq, 4 * LANES), BF16),
                   jax.ShapeDtypeStruct((db, 1, 4 * LANES), BF16)],
        compiler_params=_cparams(2),
        name="attn",
    )(page_table, q_st, qi_st, ix3, ix3, kv3, bias_st, qm, kvn, sc_s, sn_s, thr_s, bias_dec, ck_t, cv_t)


def _rglru_gates(xc, wa, ba, wx, bx, lam):
    r = _sigmoid(_dot(xc, wa) + ba)
    i = _sigmoid(_dot(xc, wx) + bx)
    log_a = -RG_C * r * _softplus(-lam)
    a = jnp.exp(log_a)
    u = jnp.sqrt(1.0 - jnp.exp(2.0 * log_a)) * (i * xc)
    return a, u


def _rglru_prompt_kernel(g_ref, xr_ref, buf_ref, h0_ref, cw_ref, cb_ref, wa_ref, ba_ref, wx_ref, bx_ref, lam_ref,
                         o_ref, hl_ref, nb_ref, xs, a_s, u_s, tail, hc, *, tc):
    t = pl.program_id(0)
    width = cw_ref.shape[0]
    nb, _, d = g_ref.shape

    @pl.when(t == 0)
    def _load_state():
        tail[...] = jnp.zeros(tail.shape, F32)
        tail[:, 8 - (width - 1):8, :] = buf_ref[...]
        hc[...] = h0_ref[...]

    xs[:, 0:8, :] = tail[...]
    xs[:, 8:8 + tc, :] = xr_ref[...]
    tail[...] = xs[:, tc:tc + 8, :]
    xc = cb_ref[...]
    for jj in range(width):
        off = 8 - (width - 1) + jj
        xc = xc + cw_ref[jj:jj + 1, :] * xs[:, off:off + tc, :]
    a, u = _rglru_gates(xc.reshape(nb * tc, d), wa_ref[...], ba_ref[...], wx_ref[...], bx_ref[...], lam_ref[...])
    n_lb = d // LANES
    pitch = tc + 8
    for k in range(n_lb):
        for b in range(nb):
            a_s[k, b * pitch:b * pitch + tc, :] = a[b * tc:(b + 1) * tc, k * LANES:(k + 1) * LANES]
            u_s[k, b * pitch:b * pitch + tc, :] = u[b * tc:(b + 1) * tc, k * LANES:(k + 1) * LANES]

    def scan_body(i, hs):
        rows = pl.ds(i, nb, stride=pitch)
        out = []
        for k in range(n_lb):
            h = a_s[k, rows, :] * hs[k] + u_s[k, rows, :]
            u_s[k, rows, :] = h
            out.append(h)
        return tuple(out)
    h0 = hc[...]
    hs = lax.fori_loop(0, tc, scan_body, tuple(h0[:, k * LANES:(k + 1) * LANES] for k in range(n_lb)), unroll=8)
    h = jnp.concatenate(hs, axis=1)
    hc[...] = h
    hseq = jnp.concatenate([jnp.concatenate([u_s[k, b * pitch:b * pitch + tc, :] for b in range(nb)], axis=0)
                            for k in range(n_lb)], axis=1)
    o_ref[...] = (_gelu(g_ref[...].reshape(nb * tc, d)) * hseq).reshape(nb, tc, d).astype(BF16)
    hl_ref[...] = h
    nb_ref[...] = xs[:, tc + 8 - (width - 1):tc + 8, :]


def _rglru_prompt(gate, xr, buf, h0, rg, *, batch, seq, tc=256):
    d = gate.shape[-1]
    width = rg["cw"].shape[0]
    g3, x3 = gate.reshape(batch, seq, d), xr.reshape(batch, seq, d)
    blk = pl.BlockSpec((batch, tc, d), lambda t: (0, t, 0))
    vec = _full_spec((1, d))
    return pl.pallas_call(
        functools.partial(_rglru_prompt_kernel, tc=tc),
        grid=(seq // tc,),
        in_specs=[blk, blk, _full_spec((batch, width - 1, d)), _full_spec((batch, d)), _full_spec((width, d)), vec,
                  _full_spec((d, d)), vec, _full_spec((d, d)), vec, vec],
        out_specs=[blk, pl.BlockSpec((batch, d), lambda t: (0, 0)), pl.BlockSpec((batch, width - 1, d), lambda t: (0, 0, 0))],
        out_shape=[jax.ShapeDtypeStruct((batch, seq, d), BF16), jax.ShapeDtypeStruct((batch, d), F32),
                   jax.ShapeDtypeStruct((batch, width - 1, d), F32)],
        scratch_shapes=[pltpu.VMEM((batch, tc + 8, d), F32), pltpu.VMEM((d // LANES, batch * (tc + 8), LANES), F32),
                        pltpu.VMEM((d // LANES, batch * (tc + 8), LANES), F32), pltpu.VMEM((batch, 8, d), F32),
                        pltpu.VMEM((batch, d), F32)],
        compiler_params=_cparams(1),
        name="rglru_prompt",
    )(g3, x3, buf, h0, rg["cw"], rg["cb"], rg["wa"], rg["ba"], rg["wx"], rg["bx"], rg["lam"])


def _rglru_dec_kernel(g_ref, xr_ref, buf_ref, h0_ref, cw_ref, cb_ref, wa_ref, ba_ref, wx_ref, bx_ref, lam_ref,
                      o_ref, hl_ref):
    width = cw_ref.shape[0]
    xc = cb_ref[...]
    for jj in range(width - 1):
        xc = xc + cw_ref[jj:jj + 1, :] * buf_ref[jj]
    xc = xc + cw_ref[width - 1:width, :] * xr_ref[...]
    a, u = _rglru_gates(xc, wa_ref[...], ba_ref[...], wx_ref[...], bx_ref[...], lam_ref[...])
    h = a * h0_ref[...] + u
    hl_ref[...] = h
    o_ref[...] = (_gelu(g_ref[...]) * h).astype(BF16)


def _rglru_dec(gate, xr, buf_t, h0, rg):
    m, d = gate.shape
    return pl.pallas_call(
        _rglru_dec_kernel,
        out_shape=[jax.ShapeDtypeStruct((m, d), BF16), jax.ShapeDtypeStruct((m, d), F32)],
        name="rglru_dec",
    )(gate, xr, buf_t, h0, rg["cw"], rg["cb"], rg["wa"], rg["ba"], rg["wx"], rg["bx"], rg["lam"])


def _ffn_tile(y1, gf_ref, wup_ref, cw_ref, cb_ref, wdn_ref, conv_prev, n_split):
    d_ff = wdn_ref.shape[0]
    cf = d_ff // n_split
    hn = _rms(y1, gf_ref[...]).astype(BF16)
    out = jnp.zeros(y1.shape, F32)
    gates = []
    for k in range(n_split):
        c0 = k * cf
        g = jnp.dot(hn, wup_ref[:, c0:c0 + cf], preferred_element_type=F32)
        u = jnp.dot(hn, wup_ref[:, d_ff + c0:d_ff + c0 + cf], preferred_element_type=F32)
        g1, g2 = conv_prev(k, g)
        gc = cb_ref[:, c0:c0 + cf] + cw_ref[0:1, c0:c0 + cf] * g2 + cw_ref[1:2, c0:c0 + cf] * g1 \
            + cw_ref[2:3, c0:c0 + cf] * g
        act = (_gelu(gc) * u).astype(BF16)
        out = out + jnp.dot(act, wdn_ref[c0:c0 + cf, :], preferred_element_type=F32)
        gates.append(g)
    return out, gates


def _prompt_conv_prev(gs, carry, fb_ref, nb_ref, tm, cf):
    t = pl.program_id(1)

    @pl.when(t == 0)
    def _load_state():
        carry[...] = jnp.zeros(carry.shape, F32)
        for k in range(carry.shape[0]):
            carry[k, 6:8, :] = fb_ref[0, :, k * cf:(k + 1) * cf]

    def conv_prev(k, g):
        gs[0:8, :] = carry[k]
        gs[8:8 + tm, :] = g
        carry[k] = g[tm - 8:tm, :]
        nb_ref[0, :, k * cf:(k + 1) * cf] = g[tm - 2:tm, :]
        return gs[7:7 + tm, :], gs[6:6 + tm, :]
    return conv_prev


def _mix_ab_tile(y_ref, a_ref, r_ref, woa_ref, wob_ref):
    return y_ref[0] + jnp.dot(a_ref[0], woa_ref[...], preferred_element_type=F32) \
        + jnp.dot(r_ref[0], wob_ref[...], preferred_element_type=F32)


def _post_ab_prompt_kernel(y_ref, a_ref, r_ref, fb_ref, woa_ref, wob_ref, gf_ref, wup_ref, cw_ref, cb_ref, wdn_ref,
                           o_ref, nb_ref, gs, carry, *, tm, n_split):
    y1 = _mix_ab_tile(y_ref, a_ref, r_ref, woa_ref, wob_ref)
    cf = wdn_ref.shape[0] // n_split
    out, _ = _ffn_tile(y1, gf_ref, wup_ref, cw_ref, cb_ref, wdn_ref,
                       _prompt_conv_prev(gs, carry, fb_ref, nb_ref, tm, cf), n_split)
    o_ref[0] = y1 + out


_FFN_KEYS = ("g", "wup", "cw", "cb", "wdn")


def _ffn_specs(ffn):
    layer = ffn["layer"]
    return [pl.BlockSpec((None,) + ffn[k].shape[1:], lambda *_: (layer, 0, 0), pipeline_mode=pl.Buffered(1))
            for k in _FFN_KEYS]


def _ffn_args(ffn):
    return [ffn[k] for k in _FFN_KEYS]


def _post_ab_prompt(y, attn, rgo, fbuf, wo_a, wo_b, ffn, *, tm=512, n_split=2):
    batch, seq, d = y.shape
    d_ff = ffn["wdn"].shape[1]
    cf = d_ff // n_split
    blk = lambda n: pl.BlockSpec((1, tm, n), lambda b, t: (b, t, 0))
    fb = pl.BlockSpec((1, 2, d_ff), lambda b, t: (b, 0, 0))
    return pl.pallas_call(
        functools.partial(_post_ab_prompt_kernel, tm=tm, n_split=n_split),
        grid=(batch, seq // tm),
        in_specs=[blk(d), blk(attn.shape[-1]), blk(rgo.shape[-1]), fb, _full_spec(wo_a.shape), _full_spec(wo_b.shape)]
        + _ffn_specs(ffn),
        out_specs=[blk(d), fb],
        out_shape=[jax.ShapeDtypeStruct((batch, seq, d), F32), jax.ShapeDtypeStruct((batch, 2, d_ff), F32)],
        scratch_shapes=[pltpu.VMEM((tm + 8, cf), F32), pltpu.VMEM((n_split, 8, cf), F32)],
        compiler_params=_cparams(2),
        name="post_ab_prompt",
    )(y, attn, rgo, fbuf, wo_a, wo_b, *_ffn_args(ffn))


def _dec_conv_prev(fb_ref, cf):
    def conv_prev(k, g):
        return fb_ref[1, :, k * cf:(k + 1) * cf], fb_ref[0, :, k * cf:(k + 1) * cf]
    return conv_prev


def _post_ab_dec_kernel(y_ref, a_ref, r_ref, fb_ref, woa_ref, wob_ref, gf_ref, wup_ref, cw_ref, cb_ref, wdn_ref,
                        o_ref, g_ref, *, n_split):
    y1 = y_ref[...] + jnp.dot(a_ref[...], woa_ref[...], preferred_element_type=F32) \
        + jnp.dot(r_ref[...], wob_ref[...], preferred_element_type=F32)
    cf = wdn_ref.shape[0] // n_split
    out, gates = _ffn_tile(y1, gf_ref, wup_ref, cw_ref, cb_ref, wdn_ref, _dec_conv_prev(fb_ref, cf), n_split)
    o_ref[...] = y1 + out
    for k, g in enumerate(gates):
        g_ref[:, k * cf:(k + 1) * cf] = g


def _post_ab_dec(y, attn, rgo, fbuf_t, wo_a, wo_b, ffn, *, n_split=2):
    m, d = y.shape
    d_ff = ffn["wdn"].shape[1]
    args = (y, attn, rgo, fbuf_t, wo_a, wo_b)
    return pl.pallas_call(
        functools.partial(_post_ab_dec_kernel, n_split=n_split),
        grid=(1,),
        in_specs=[_full_spec(a.shape) for a in args] + _ffn_specs(ffn),
        out_specs=[_whole_spec((m, d)), _whole_spec((m, d_ff))],
        out_shape=[jax.ShapeDtypeStruct((m, d), F32), jax.ShapeDtypeStruct((m, d_ff), F32)],
        compiler_params=_cparams(1),
        name="post_ab_dec",
    )(*args, *_ffn_args(ffn))


def _gmlp_in(y, gm_ref, win_ref, bin_ref, sn_ref):
    d_c = win_ref.shape[1] // 2
    z = _gelu(_dot(_rms(y, gm_ref[...]), win_ref[...]) + bin_ref[...])
    return z[:, :d_c], _rms(z[:, d_c:], sn_ref[...])


def _layer_c_prompt_kernel(y_ref, fb_ref, gm_ref, win_ref, bin_ref, sn_ref, sw_ref, sbt_ref, woc_ref,
                           gf_ref, wup_ref, cw_ref, cb_ref, wdn_ref, gfin_ref,
                           o_ref, nb_ref, gs, carry, *, tm, n_split):
    y = y_ref[0]
    u, v = _gmlp_in(y, gm_ref, win_ref, bin_ref, sn_ref)
    vb = v.astype(BF16)
    n_groups = sw_ref.shape[0]
    tril = lax.broadcasted_iota(I32, (CHUNK, CHUNK), 0) >= lax.broadcasted_iota(I32, (CHUNK, CHUNK), 1)
    wm = [jnp.where(tril, sw_ref[gi], 0.0).astype(BF16) for gi in range(n_groups)]
    rows = []
    for r in range(tm // CHUNK):
        cols = []
        for gi in range(n_groups):
            mixed = jnp.dot(wm[gi], vb[r * CHUNK:(r + 1) * CHUNK, gi * LANES:(gi + 1) * LANES],
                            preferred_element_type=F32)
            cols.append(mixed + sbt_ref[:, gi:gi + 1])
        rows.append(jnp.concatenate(cols, axis=1))
    gated = u * jnp.concatenate(rows, axis=0)
    y1 = y + _dot(gated, woc_ref[...])
    cf = wdn_ref.shape[0] // n_split
    out, _ = _ffn_tile(y1, gf_ref, wup_ref, cw_ref, cb_ref, wdn_ref,
                       _prompt_conv_prev(gs, carry, fb_ref, nb_ref, tm, cf), n_split)
    o_ref[0] = _rms(y1 + out, gfin_ref[...])


def _layer_c_prompt(y, fbuf, cp, ffn, g_final, *, tm=512, n_split=2):
    batch, seq, d = y.shape
    d_ff = ffn["wdn"].shape[1]
    cf = d_ff // n_split
    blk = pl.BlockSpec((1, tm, d), lambda b, t: (b, t, 0))
    fb = pl.BlockSpec((1, 2, d_ff), lambda b, t: (b, 0, 0))
    consts = [cp["g"], cp["win"], cp["bin"], cp["sn"], cp["sw"], cp["sbt"], cp["woc"]]
    return pl.pallas_call(
        functools.partial(_layer_c_prompt_kernel, tm=tm, n_split=n_split),
        grid=(batch, seq // tm),
        in_specs=[blk, fb] + [_full_spec(c.shape) for c in consts] + _ffn_specs(ffn) + [_full_spec(g_final.shape)],
        out_specs=[blk, fb],
        out_shape=[jax.ShapeDtypeStruct((batch, seq, d), F32), jax.ShapeDtypeStruct((batch, 2, d_ff), F32)],
        scratch_shapes=[pltpu.VMEM((tm + 8, cf), F32), pltpu.VMEM((n_split, 8, cf), F32)],
        compiler_params=_cparams(2),
        name="layer_c_prompt",
    )(y, fbuf, *consts, *_ffn_args(ffn), g_final)


def _layer_c_dec_kernel(y_ref, fb_ref, gm_ref, win_ref, bin_ref, sn_ref, sw0_ref, sb0_ref, woc_ref,
                        gf_ref, wup_ref, cw_ref, cb_ref, wdn_ref, gfin_ref, o_ref, g_ref, v_ref, *, n_split):
    y = y_ref[...]
    u, v = _gmlp_in(y, gm_ref, win_ref, bin_ref, sn_ref)
    v_ref[...] = v
    y1 = y + _dot(u * (sw0_ref[...] * v + sb0_ref[...]), woc_ref[...])
    cf = wdn_ref.shape[0] // n_split
    out, gates = _ffn_tile(y1, gf_ref, wup_ref, cw_ref, cb_ref, wdn_ref, _dec_conv_prev(fb_ref, cf), n_split)
    o_ref[...] = _rms(y1 + out, gfin_ref[...])
    for k, g in enumerate(gates):
        g_ref[:, k * cf:(k + 1) * cf] = g


def _layer_c_dec(y, fbuf_t, cp, ffn, g_final, *, n_split=2):
    m, d = y.shape
    d_ff = ffn["wdn"].shape[1]
    d_c = cp["woc"].shape[0]
    args = (y, fbuf_t, cp["g"], cp["win"], cp["bin"], cp["sn"], cp["sw0"], cp["sb0"], cp["woc"])
    return pl.pallas_call(
        functools.partial(_layer_c_dec_kernel, n_split=n_split),
        grid=(1,),
        in_specs=[_full_spec(a.shape) for a in args] + _ffn_specs(ffn) + [_full_spec(g_final.shape)],
        out_specs=[_whole_spec((m, d)), _whole_spec((m, d_ff)), _whole_spec((m, d_c))],
        out_shape=[jax.ShapeDtypeStruct((m, d), F32), jax.ShapeDtypeStruct((m, d_ff), F32),
                   jax.ShapeDtypeStruct((m, d_c), F32)],
        compiler_params=_cparams(1),
        name="layer_c_dec",
    )(*args, *_ffn_args(ffn), g_final)


def _start_page_copies(src_ref, pt_ref, b, dst_ref, sem, n_pages, page):
    def body(pg, carry):
        col = pl.multiple_of(pg * page, page)
        pltpu.make_async_copy(src_ref.at[pt_ref[b, pg]], dst_ref.at[:, pl.ds(col, page)], sem).start()
        return carry
    lax.fori_loop(0, n_pages, body, 0, unroll=8)


def _wait_page_copies(dst_ref, sem):
    pltpu.make_async_copy(dst_ref, dst_ref, sem).wait()


def _dec_score_kernel(pt_ref, qi_ref, wi_ref, ixn_ref, cik_ref, keys_ref, knew_ref, ibuf, sems, *, n_pages, page):
    b = pl.program_id(0)
    nb = pl.num_programs(0)
    slot = lax.rem(b, 2)

    def start(bb, sl):
        _start_page_copies(cik_ref, pt_ref, bb, ibuf.at[sl], sems.at[sl], n_pages, page)

    @pl.when(b == 0)
    def _first():
        start(0, 0)

    @pl.when(b + 1 < nb)
    def _prefetch_next():
        start(b + 1, 1 - slot)

    _wait_page_copies(ibuf.at[slot], sems.at[slot])
    qi = qi_ref[0].astype(BF16)
    wi = wi_ref[0]
    s = jnp.dot(qi, ibuf[slot].astype(BF16), preferred_element_type=F32)
    sc = jnp.sum(jnp.maximum(s, 0.0) * wi, axis=0, keepdims=True) * IDX_HEADS ** -0.5
    keys_ref[0] = sc
    kin = ixn_ref[0][:, 0:IDX_DIM]
    sn = jnp.sum(qi_ref[0] * kin, axis=1, keepdims=True)
    scn = jnp.sum(jnp.maximum(sn, 0.0) * wi, axis=0, keepdims=True) * IDX_HEADS ** -0.5
    knew_ref[0] = jnp.broadcast_to(scn, (1, LANES))


def _dec_scores(page_table, qi3, wi3, ix3, cik_t, *, page):
    db, n_pages = page_table.shape
    past = n_pages * page
    return pl.pallas_call(
        functools.partial(_dec_score_kernel, n_pages=n_pages, page=page),
        grid_spec=pltpu.PrefetchScalarGridSpec(
            num_scalar_prefetch=1,
            grid=(db,),
            in_specs=[pl.BlockSpec((1, IDX_HEADS, IDX_DIM), lambda b, pt: (b, 0, 0)),
                      pl.BlockSpec((1, IDX_HEADS, 1), lambda b, pt: (b, 0, 0)),
                      pl.BlockSpec((1, 1, LANES), lambda b, pt: (b, 0, 0)),
                      pl.BlockSpec(memory_space=pl.ANY)],
            out_specs=[pl.BlockSpec((1, 1, past), lambda b, pt: (b, 0, 0)),
                       pl.BlockSpec((1, 1, LANES), lambda b, pt: (b, 0, 0))],
            scratch_shapes=[pltpu.VMEM((2, IDX_DIM, past), F32), pltpu.SemaphoreType.DMA((2,))]),
        out_shape=[jax.ShapeDtypeStruct((db, 1, past), F32), jax.ShapeDtypeStruct((db, 1, LANES), F32)],
        compiler_params=_cparams(1),
        name="dec_scores",
    )(page_table, qi3, wi3, ix3, cik_t)


def _dec_select_kernel(sc_ref, scn_ref, so_ref, sno_ref, thr_ref, *, topk):
    past = sc_ref.shape[1]
    lane0 = lax.broadcasted_iota(I32, scn_ref.shape, 1) == 0
    sc = jnp.concatenate([sc_ref[...], jnp.where(lane0, scn_ref[...], -jnp.inf)], axis=1)
    kk = _float_key(sc)
    kf = jnp.float32(topk)

    def count(pred):
        return jnp.sum(jnp.where(pred, 1.0, 0.0), axis=1, keepdims=True)

    def search_body(i, ans):
        cand = ans | jnp.left_shift(jnp.int32(1), 31 - i)
        return jnp.where(count(kk >= (cand ^ jnp.int32(INT_MIN))) >= kf, cand, ans)
    ans = lax.fori_loop(0, 32, search_body, jnp.zeros((sc.shape[0], 1), I32))
    thr = _key_float(ans ^ jnp.int32(INT_MIN))
    counts = lambda t: (count(sc >= t), count(sc > t))
    thr, cge, cgt = _walk_to_kth(lambda: sc, thr, *counts(thr), kf, counts, 4 * topk)
    need = kf - cgt
    big = jnp.int32(4 * past)
    eqcol = jnp.where(sc == thr, lax.broadcasted_iota(I32, sc.shape, 1), big)
    nbits = int(math.log2(past)) + 1

    def tie_body(i, best):
        cand = best | jnp.left_shift(jnp.int32(1), nbits - 1 - i)
        return jnp.where(count(eqcol < cand) < need, cand, best)
    last = lax.fori_loop(0, nbits, tie_body, jnp.zeros((sc.shape[0], 1), I32))
    sc = jnp.where((eqcol > last) & (eqcol < big), -jnp.inf, sc)
    so_ref[...] = sc[:, :past]
    sno_ref[...] = sc[:, past:]
    thr_ref[...] = jnp.broadcast_to(thr, thr_ref.shape)


def _dec_select(scores, snew, *, topk):
    db, past = scores.shape
    assert past + 1 >= topk
    return pl.pallas_call(
        functools.partial(_dec_select_kernel, topk=topk),
        out_shape=[jax.ShapeDtypeStruct((db, past), F32), jax.ShapeDtypeStruct((db, LANES), F32),
                   jax.ShapeDtypeStruct((db, LANES), F32)],
        name="dec_select",
    )(scores, snew)


def _dec_attn_row(qm, kvn, sc, sn, thr_row, bias, kbuf, vbuf, page):
    past = kbuf.shape[1]
    thr = thr_row[:, 0:1]
    sel = sc >= thr
    sel_new = sn[:, 0:1] >= thr
    far, last, bnew = bias[:, LANES:LANES + 1], bias[:, 0:page], bias[:, LANES + 1:LANES + 2]
    lg = jnp.dot(qm.astype(BF16), kbuf[...].astype(BF16), preferred_element_type=F32)
    lane = lax.broadcasted_iota(I32, lg.shape, 1)
    lastp = jnp.concatenate([jnp.zeros((N_HEADS, past - page), F32), last], axis=1)
    lg = jnp.where(sel, lg + jnp.where(lane >= past - page, lastp, far), -jnp.inf)
    lgn = jnp.sum(qm * kvn[:, 0:LANES], axis=1, keepdims=True) + bnew
    lgn = jnp.where(sel_new, lgn, -jnp.inf)
    m = jnp.maximum(jnp.max(lg, axis=1, keepdims=True), lgn)
    e = jnp.exp(lg - m)
    en = jnp.exp(lgn - m)
    den = jnp.sum(e, axis=1, keepdims=True) + en
    pv = _dot_nt(e.astype(BF16), vbuf[...].astype(BF16)) + en * kvn[:, LANES:2 * LANES]
    pv = pv / den
    lo = lax.broadcasted_iota(I32, (1, LANES), 1) < HEAD_DIM
    return jnp.concatenate([jnp.where(lo, pv[2 * p:2 * p + 1], pv[2 * p + 1:2 * p + 2]) for p in range(4)],
                           axis=1).astype(BF16)


def _prep_in_ab(w):
    d = w.shape[0]
    nq, nkv = N_HEADS * HEAD_DIM, N_KV_HEADS * HEAD_DIM
    offs = np.cumsum([nq, nkv, nkv, IDX_HEADS * IDX_DIM, IDX_DIM, IDX_HEADS, 512])
    q, k, v, qi, ki, wi, g, xr = jnp.split(w, offs.tolist(), axis=1)
    q = q.reshape(d, N_HEADS, HEAD_DIM)[:, np.array(HEAD_PERM), :].reshape(d, nq)
    pad = jnp.zeros((d, _C_G - _C_IX - IDX_DIM - IDX_HEADS), w.dtype)
    return jnp.concatenate([q, k, v, qi, ki, wi, pad, g, xr], axis=1).astype(BF16)


def _block_diag(w):
    n, c, _ = w.shape
    return (jnp.eye(n, dtype=w.dtype)[:, None, :, None] * w[:, :, None, :]).reshape(n * c, n * c).astype(BF16)


def _ffn_params(layer, stacked):
    return dict(stacked, layer=layer)


def kernel(x_prompt, x_sample, cache_k, cache_v, cache_idx_k, state_rglru_h, state_rglru_conv, state_ffn_conv,
           page_table, norm_mix, norm_ffn, norm_final, rel_bias, w_in_ab, w_out_ab, rg_conv_w, rg_conv_b,
           rg_wa, rg_ba, rg_wx, rg_bx, rg_lambda, w_in_c, b_in_c, sgu_norm, sgu_w, sgu_b, w_out_c,
           ffn_w_up, ffn_conv_w, ffn_conv_b, ffn_w_down):
    batch, seq, d = x_prompt.shape
    db = x_sample.shape[0]
    page = cache_k.shape[2]
    d_a = N_HEADS * HEAD_DIM
    d_b = rg_conv_w.shape[-1]
    d_ff = ffn_w_down.shape[1]
    assert x_sample.shape[1] == 1 and seq % 512 == 0 and page == LANES and w_in_ab.shape[0] == 1

    w_in0 = _prep_in_ab(w_in_ab[0])
    wo = w_out_ab[0]
    wo_a = wo[:d_a].reshape(N_HEADS, HEAD_DIM, d)[np.array(HEAD_PERM)].reshape(d_a, d).astype(BF16)
    wo_b = wo[d_a:].astype(BF16)
    rg = {"cw": rg_conv_w[0], "cb": rg_conv_b[0][None], "wa": _block_diag(rg_wa[0]), "ba": rg_ba[0][None],
          "wx": _block_diag(rg_wx[0]), "bx": rg_bx[0][None], "lam": rg_lambda[0][None]}
    ffn_all = {"g": norm_ffn[:, None, :], "wup": ffn_w_up.astype(BF16), "cw": ffn_conv_w, "cb": ffn_conv_b[:, None, :],
               "wdn": ffn_w_down.astype(BF16)}
    ffn0, ffn1 = _ffn_params(0, ffn_all), _ffn_params(1, ffn_all)
    cp = {"g": norm_mix[1][None], "win": w_in_c[0].astype(BF16), "bin": b_in_c[0][None], "sn": sgu_norm[0][None],
          "sw": sgu_w[0], "sbt": sgu_b[0].T, "woc": w_out_c[0].astype(BF16),
          "sw0": jnp.repeat(sgu_w[0][:, 0, 0], d // sgu_w.shape[1])[None],
          "sb0": jnp.repeat(sgu_b[0][:, 0], d // sgu_w.shape[1])[None]}
    g_mix0 = norm_mix[0][None]
    g_final = norm_final[None]
    bias_st, bias_dec = _bias_tables(rel_bias, page)

    xp = x_prompt.reshape(batch * seq, d)
    xs = x_sample.reshape(db, d)
    q_st, qi_st, kv_p, ix_p, gate_p, xr_p = _inproj(xp, g_mix0, w_in0, stack=True, tm=512)
    qm_s, qi_s, kv_s, ix_s, gate_s, xr_s = _inproj(xs, g_mix0, w_in0, stack=False, tm=db)
    cik_t = jnp.transpose(cache_idx_k[0], (0, 2, 1))
    ck_t = jnp.transpose(cache_k[0], (0, 2, 3, 1)).reshape(-1, 2 * HEAD_DIM, page)
    cv_t = jnp.transpose(cache_v[0], (0, 2, 3, 1)).reshape(-1, 2 * HEAD_DIM, page)
    topk_s = min(TOPK_MAX, (page_table.shape[1] * page + 1) // 4)
    sc_s, sn_s = _dec_scores(page_table, qi_s.reshape(db, IDX_HEADS, IDX_DIM),
                             ix_s[:, IDX_DIM:IDX_DIM + IDX_HEADS].reshape(db, IDX_HEADS, 1),
                             ix_s.reshape(db, 1, LANES), cik_t, page=page)
    sc_s, sn_s, thr_s = _dec_select(sc_s.reshape(db, -1), sn_s.reshape(db, LANES), topk=topk_s)

    attn_p, attn_s = _attn(q_st, qi_st, ix_p, kv_p, bias_st, page_table, jnp.transpose(qm_s, (1, 0, 2)),
                           kv_s.reshape(db, 1, 2 * LANES), sc_s.reshape(db, 1, -1), sn_s.reshape(db, 1, LANES),
                           thr_s.reshape(db, 1, LANES), bias_dec, ck_t, cv_t, batch=batch, seq=seq, page=page)
    attn_s = attn_s.reshape(db, d_a)

    rg_p, h_p, cbuf_p = _rglru_prompt(gate_p, xr_p, jnp.zeros((batch, rg["cw"].shape[0] - 1, d_b), F32),
                                      jnp.zeros((batch, d_b), F32), rg, batch=batch, seq=seq)
    zero_fb = jnp.zeros((batch, 2, d_ff), F32)
    y1_p, fb0_p = _post_ab_prompt(x_prompt, attn_p, rg_p, zero_fb, wo_a, wo_b, ffn0)
    y_p, fb1_p = _layer_c_prompt(y1_p, zero_fb, cp, ffn1, g_final)

    cbuf_s_in = state_rglru_conv[0]
    rg_s, h_s = _rglru_dec(gate_s, xr_s, jnp.transpose(cbuf_s_in, (1, 0, 2)), state_rglru_h[0], rg)
    y1_s, g0_s = _post_ab_dec(xs, attn_s, rg_s, jnp.transpose(state_ffn_conv[0], (1, 0, 2)), wo_a, wo_b, ffn0)
    y_s, g1_s, v_s = _layer_c_dec(y1_s, jnp.transpose(state_ffn_conv[1], (1, 0, 2)), cp, ffn1, g_final)

    kv4 = kv_p.reshape(batch, seq, 2, N_KV_HEADS, HEAD_DIM)
    kvs = kv_s.reshape(db, 1, 2, N_KV_HEADS, HEAD_DIM)
    fbuf_s = lambda layer, g: jnp.concatenate([state_ffn_conv[layer][:, 1:], g[:, None]], axis=1)
    return (y_p, y_s.reshape(db, 1, d),
            kv4[None, :, :, 0], kv4[None, :, :, 1], ix_p.reshape(batch, seq, LANES)[None, :, :, :IDX_DIM],
            kvs[None, :, :, 0], kvs[None, :, :, 1], ix_s.reshape(db, 1, LANES)[None, :, :, :IDX_DIM],
            h_p.reshape(batch, d_b)[None], cbuf_p[None],
            h_s[None], jnp.concatenate([cbuf_s_in[:, 1:], xr_s[:, None]], axis=1)[None],
            v_s.reshape(db, 1, -1)[None],
            jnp.stack([fb0_p, fb1_p]), jnp.stack([fbuf_s(0, g0_s), fbuf_s(1, g1_s)]))
```

```python
import functools
import math

import numpy as np
import jax
import jax.numpy as jnp
from jax import lax
from jax.experimental import pallas as pl
from jax.experimental.pallas import tpu as pltpu

F32 = jnp.float32
BF16 = jnp.bfloat16
I32 = jnp.int32

N_HEADS = 8
HEAD_DIM = 64
N_KV_HEADS = 2
Q_PER_KV = N_HEADS // N_KV_HEADS
IDX_HEADS = 8
IDX_DIM = 64
TOPK_MAX = 256
N_BUCKETS = 32
REL_MAX_EXACT = N_BUCKETS // 2
REL_MAX_DIST = 128
RG_C = 8.0
CHUNK = 128
EPS = 1e-6

LANES = 128
QB = 128
DEC_KEY_CHUNK = 2048
INT_MIN = -(2 ** 31)
KEY_MIN_FINITE = INT_MIN + 0x800000
NEG_MAX = float(np.finfo(np.float32).min)
HEAD_PERM = (0, 4, 1, 5, 2, 6, 3, 7)
VMEM_LIMIT = 56 * 1024 * 1024


def _cparams(n_grid):
    return pltpu.CompilerParams(dimension_semantics=("arbitrary",) * n_grid, vmem_limit_bytes=VMEM_LIMIT)


def _full_spec(shape):
    nd = len(shape)
    return pl.BlockSpec(shape, lambda *_: (0,) * nd, pipeline_mode=pl.Buffered(1))


def _whole_spec(shape):
    nd = len(shape)
    return pl.BlockSpec(shape, lambda *_: (0,) * nd)


def _rms(x, g):
    return x * lax.rsqrt(jnp.mean(x * x, axis=-1, keepdims=True) + EPS) * g


def _gelu(x):
    return x * (0.5 * (1.0 + jnp.tanh(math.sqrt(2.0 / math.pi) * (x + 0.044715 * (x * x * x)))))


def _sigmoid(x):
    return 1.0 / (1.0 + jnp.exp(-x))


def _softplus(x):
    return jnp.maximum(x, 0.0) + jnp.log(1.0 + jnp.exp(-jnp.abs(x)))


def _dot(a, b):
    return jnp.dot(a.astype(BF16), b, preferred_element_type=F32)


def _dot_nt(a, b):
    return lax.dot_general(a, b, (((1,), (1,)), ((), ())), preferred_element_type=F32)


def _float_key(x):
    bits = pltpu.bitcast(x, I32)
    key = jnp.where(bits < 0, bits ^ jnp.int32(0x7FFFFFFF), bits)
    return jnp.where(bits == jnp.int32(INT_MIN), jnp.int32(0), key)


def _key_float(key):
    key = jnp.maximum(key, jnp.int32(KEY_MIN_FINITE))
    return pltpu.bitcast(jnp.where(key < 0, key ^ jnp.int32(0x7FFFFFFF), key), F32)


def _walk_to_kth(scores, t, cge, cgt, kf, count_ge_gt, max_steps):
    axis = 0 if t.shape[0] == 1 else 1

    def settled(t, cge, cgt):
        return (cgt < kf) & ((cge >= kf) | (t <= NEG_MAX))

    def unsettled(t, cge, cgt):
        return jnp.max(jnp.where(settled(t, cge, cgt), 0.0, 1.0)) > 0.0

    def body(st):
        t, cge, cgt, it = st
        sc = scores()
        below = jnp.max(jnp.where(sc < t, sc, -jnp.inf), axis=axis, keepdims=True)
        above = jnp.min(jnp.where(sc > t, sc, jnp.inf), axis=axis, keepdims=True)
        t = jnp.where(cgt >= kf, above, jnp.where((cge < kf) & (t > NEG_MAX), jnp.maximum(below, NEG_MAX), t))
        cge, cgt = count_ge_gt(t)
        return t, cge, cgt, it + 1

    t, cge, cgt, _ = lax.while_loop(lambda st: unsettled(st[0], st[1], st[2]) & (st[3] < max_steps), body,
                                    (t, cge, cgt, jnp.int32(0)))
    return t, cge, cgt


def _t5_bucket_np(n):
    n = np.maximum(n, 0)
    nf = np.maximum(n, 1).astype(np.float32)
    large = REL_MAX_EXACT + (np.log(nf / np.float32(REL_MAX_EXACT)) / np.float32(math.log(REL_MAX_DIST / REL_MAX_EXACT))
                             * np.float32(N_BUCKETS - REL_MAX_EXACT)).astype(np.int32)
    large = np.minimum(large, N_BUCKETS - 1)
    return np.where(n < REL_MAX_EXACT, n, large).astype(np.int32)


_C_Q, _C_KV, _C_QI, _C_IX, _C_G, _C_X, _C_END = 0, 512, 768, 1280, 1408, 1920, 2432


def _inproj_kernel(x_ref, g_ref, w_ref, q_ref, qi_ref, kv_ref, ix_ref, gate_ref, xr_ref, *, stack):
    hn = _rms(x_ref[...], g_ref[...])
    z = _dot(hn, w_ref[...])
    q = z[:, _C_Q:_C_KV] * HEAD_DIM ** -0.5
    qi = z[:, _C_QI:_C_IX] * IDX_DIM ** -0.5
    kv_ref[...] = z[:, _C_KV:_C_QI]
    ix_ref[...] = z[:, _C_IX:_C_G]
    gate_ref[...] = z[:, _C_G:_C_X]
    xr_ref[...] = z[:, _C_X:_C_END]
    if stack:
        qb, qib = q.astype(BF16), qi.astype(BF16)
        for r in range(q.shape[0] // QB):
            for p in range(4):
                q_ref[r, p * QB:(p + 1) * QB, :] = qb[r * QB:(r + 1) * QB, p * LANES:(p + 1) * LANES]
                qi_ref[r, p * QB:(p + 1) * QB, :] = qib[r * QB:(r + 1) * QB, p * LANES:(p + 1) * LANES]
    else:
        lo = lax.broadcasted_iota(I32, (q.shape[0], LANES), 1) < HEAD_DIM
        for p in range(4):
            qp = q[:, p * LANES:(p + 1) * LANES]
            q_ref[2 * p] = jnp.where(lo, qp, 0.0)
            q_ref[2 * p + 1] = jnp.where(lo, 0.0, qp)
        qi_ref[...] = qi


def _inproj(x2d, g, w, *, stack, tm):
    m, d = x2d.shape
    if stack:
        q_shape, q_spec = (m // QB, 4 * QB, LANES), pl.BlockSpec((tm // QB, 4 * QB, LANES), lambda i: (i, 0, 0))
        qi_shape, qi_spec, qdt = q_shape, q_spec, BF16
    else:
        q_shape, q_spec = (N_HEADS, m, LANES), pl.BlockSpec((N_HEADS, tm, LANES), lambda i: (0, i, 0))
        qi_shape, qi_spec, qdt = (m, 512), pl.BlockSpec((tm, 512), lambda i: (i, 0)), F32
    row = lambda n: pl.BlockSpec((tm, n), lambda i: (i, 0))
    return pl.pallas_call(
        functools.partial(_inproj_kernel, stack=stack),
        grid=(m // tm,),
        in_specs=[row(d), _full_spec((1, d)), _full_spec(w.shape)],
        out_specs=[q_spec, qi_spec, row(256), row(128), row(512), row(512)],
        out_shape=[jax.ShapeDtypeStruct(q_shape, qdt), jax.ShapeDtypeStruct(qi_shape, qdt),
                   jax.ShapeDtypeStruct((m, 256), F32), jax.ShapeDtypeStruct((m, 128), F32),
                   jax.ShapeDtypeStruct((m, 512), F32), jax.ShapeDtypeStruct((m, 512), F32)],
        compiler_params=_cparams(1),
        name="inproj_stack" if stack else "inproj_dec",
    )(x2d, g, w)


def _bias_kernel(rb_ref, bk_ref, bkd_ref, o_ref, od_ref):
    for d in range(3):
        bk = bk_ref[d]
        for p in range(4):
            for a in range(2):
                h = p + 4 * a
                acc = jnp.zeros((QB, LANES), F32)
                for b in range(N_BUCKETS):
                    acc = jnp.where(bk == b, rb_ref[b, h], acc)
                o_ref[d, a * QB:(a + 1) * QB, p * LANES:(p + 1) * LANES] = acc
    bkd = bkd_ref[...]
    rowi = lax.broadcasted_iota(I32, (N_HEADS, 2 * LANES), 0)
    acc = jnp.zeros((N_HEADS, 2 * LANES), F32)
    for r in range(N_HEADS):
        h = r // 2 + 4 * (r % 2)
        for b in range(N_BUCKETS):
            acc = jnp.where((rowi == r) & (bkd == b), rb_ref[b, h], acc)
    od_ref[...] = acc


def _bias_tables(rel_bias, page):
    key = np.arange(QB)[:, None]
    qry = np.arange(LANES)[None, :]
    bk = np.stack([_t5_bucket_np(d * QB + qry - key) for d in range(3)])
    assert (_t5_bucket_np(np.arange(2 * QB + 1 - LANES, 4 * QB)) == N_BUCKETS - 1).all()
    assert (_t5_bucket_np(np.arange(page, 8 * page)) == N_BUCKETS - 1).all()
    dec = np.zeros((2 * LANES,), np.int64)
    dec[:page] = page - np.arange(page)
    dec[LANES] = 2 * REL_MAX_DIST
    dec[LANES + 1] = 0
    bkd = np.broadcast_to(_t5_bucket_np(dec)[None, :], (N_HEADS, 2 * LANES))
    return pl.pallas_call(
        _bias_kernel,
        in_specs=[pl.BlockSpec(memory_space=pltpu.SMEM), pl.BlockSpec(memory_space=pltpu.VMEM),
                  pl.BlockSpec(memory_space=pltpu.VMEM)],
        out_shape=[jax.ShapeDtypeStruct((3, 2 * QB, 4 * LANES), F32), jax.ShapeDtypeStruct((N_HEADS, 2 * LANES), F32)],
        name="bias_tables",
    )(rel_bias, jnp.asarray(bk, I32), jnp.asarray(bkd, I32))


def _search_widths(n_chunks):
    cuts = sorted({min(c, n_chunks) for c in (2, 4, 8, 12, 16)} | {n_chunks})
    return [c for c in cuts if c <= n_chunks]


def _attn_kernel(pt_ref, q_ref, qi_ref, ixq_ref, ixk_ref, kv_ref, bias_ref,
                 dq_ref, dkvn_ref, dsc_ref, dsn_ref, dthr_ref, dbias_ref, ck_ref, cv_ref,
                 o_ref, od_ref,
                 kblk, vblk_t, kiblk, keys, scores, logits, acc, thr_ref, cge_ref, cgt_ref, kbuf, vbuf, sems,
                 *, n_chunks, topk, n_pages, page, steps_per_seq):
    j = pl.program_id(1)
    step = pl.program_id(0) * n_chunks + j
    seq_s = lax.div(step, jnp.int32(steps_per_seq))
    phase = lax.rem(step, jnp.int32(steps_per_seq))

    @pl.when(phase == 0)
    def _fetch_sample_pages():
        _start_page_copies(ck_ref, pt_ref, seq_s, kbuf, sems.at[0], n_pages, page)
        _start_page_copies(cv_ref, pt_ref, seq_s, vbuf, sems.at[1], n_pages, page)

    lane = lax.broadcasted_iota(I32, (QB, LANES), 1)
    row = lax.broadcasted_iota(I32, (QB, LANES), 0)
    lo = lane < HEAD_DIM
    blocks = [(a, p) for a in range(2) for p in range(4)]
    rs = lambda a: slice(a * QB, (a + 1) * QB)
    cs = lambda p: slice(p * LANES, (p + 1) * LANES)
    chunk = lambda c: pl.ds(pl.multiple_of(c * QB, QB), QB)

    @pl.when(j == 0)
    def _build_block_diagonal_keys():
        def body(c, carry):
            kc = kv_ref[0, chunk(c), 0:LANES]
            vt = kv_ref[0, chunk(c), LANES:2 * LANES].T
            kia = jnp.where(lo, ixk_ref[0, chunk(c), :], 0.0)
            kblk[c, 0:QB, :] = jnp.where(lo, kc, 0.0).astype(BF16)
            kblk[c, QB:2 * QB, :] = jnp.where(lo, 0.0, kc).astype(BF16)
            vblk_t[c, :, 0:QB] = jnp.where(row < HEAD_DIM, vt, 0.0).astype(BF16)
            vblk_t[c, :, QB:2 * QB] = jnp.where(row < HEAD_DIM, 0.0, vt).astype(BF16)
            kiblk[c, 0:QB, :] = kia.astype(BF16)
            kiblk[c, QB:2 * QB, :] = pltpu.roll(kia, HEAD_DIM, 1).astype(BF16)
            return carry
        lax.fori_loop(0, n_chunks, body, 0)

    qi = qi_ref[0]
    q = q_ref[0]
    wt = ixq_ref[0].T
    w_row = {(a, p): wt[IDX_DIM + 2 * p + a:IDX_DIM + 2 * p + a + 1, :] for a, p in blocks}
    qpos = j * QB + lane

    n_pairs = (j + 2) // 2

    def chunk_loop(body, carry):
        n_quads = n_pairs // 2
        carry = lax.fori_loop(0, n_quads, lambda i, cr: body([4 * i + u for u in range(4)], cr), carry)
        return lax.fori_loop(2 * n_quads, n_pairs, lambda i, cr: body([2 * i, 2 * i + 1], cr), carry)

    def score_body(cs_, carry):
        for c in cs_:
            s = _dot_nt(kiblk[c], qi)
            lg = _dot_nt(kblk[c], q)
            sc = jnp.zeros((QB, LANES), F32)
            for a, p in blocks:
                sc = sc + jnp.maximum(s[rs(a), cs(p)], 0.0) * w_row[(a, p)]
            sc = sc * IDX_HEADS ** -0.5
            admissible = c * QB + row <= qpos
            scores[chunk(c), :] = jnp.where(admissible, sc, -jnp.inf)
            keys[chunk(c), :] = jnp.where(admissible, _float_key(sc), jnp.int32(INT_MIN))
            logits[c] = lg + bias_ref[jnp.clip(j - c, 0, 2)]
        return carry
    chunk_loop(score_body, 0)

    def fill_body(c, carry):
        scores[chunk(c), :] = jnp.full((QB, LANES), -jnp.inf, F32)
        keys[chunk(c), :] = jnp.full((QB, LANES), INT_MIN, I32)
        return carry
    lax.fori_loop(2 * n_pairs, n_chunks, fill_body, 0)

    kf = jnp.float32(topk)

    def count(src, width, pred):
        accs = [jnp.zeros((8, LANES), F32) for _ in range(8)]
        for g in range(width // 8):
            accs[g % 8] = accs[g % 8] + jnp.where(pred(src[g * 8:(g + 1) * 8, :]), 1.0, 0.0)
        return jnp.sum(functools.reduce(lambda x, y: x + y, accs), axis=0, keepdims=True)

    def search(width):
        def search_body(i, ans):
            cand = ans | jnp.left_shift(jnp.int32(1), 31 - i)
            cs_ = cand ^ jnp.int32(INT_MIN)
            return jnp.where(count(keys, width, lambda k: k >= cs_) >= kf, cand, ans)
        ans = lax.fori_loop(0, 32, search_body, jnp.zeros((1, LANES), I32))
        t = _key_float(ans ^ jnp.int32(INT_MIN))
        thr_ref[...] = t
        cge_ref[...] = count(scores, width, lambda s: s >= t)
        cgt_ref[...] = count(scores, width, lambda s: s > t)

    prev = 0
    for n in _search_widths(n_chunks):
        pl.when((j >= prev) & (j < n))(functools.partial(search, n * QB))
        prev = n

    s_len = scores.shape[0]
    thr, cge, cgt = _walk_to_kth(
        lambda: scores[...], thr_ref[...], cge_ref[...], cgt_ref[...], kf,
        lambda t: (count(scores, s_len, lambda s: s >= t), count(scores, s_len, lambda s: s > t)), 4 * topk)

    @pl.when(jnp.max(cge) > kf)
    def _break_ties_by_position():
        need = kf - cgt
        big = jnp.int32(2 * s_len)
        eqrow_ref = keys
        eqrow_ref[...] = jnp.where(scores[...] == thr, lax.broadcasted_iota(I32, scores.shape, 0), big)
        nbits = int(math.log2(s_len))

        def tie_body(i, best):
            cand = best | jnp.left_shift(jnp.int32(1), nbits - 1 - i)
            return jnp.where(count(eqrow_ref, s_len, lambda e: e < cand) < need, cand, best)
        last = lax.fori_loop(0, nbits, tie_body, jnp.zeros((1, LANES), I32))
        eqrow = eqrow_ref[...]
        scores[...] = jnp.where((eqrow > last) & (eqrow < big), -jnp.inf, scores[...])

    def mask_body(cs_, mx):
        mx = list(mx)
        for c in cs_:
            sel = scores[chunk(c), :] >= thr
            for n, (a, p) in enumerate(blocks):
                blk = jnp.where(sel, logits[c, rs(a), cs(p)], -jnp.inf)
                logits[c, rs(a), cs(p)] = blk
                mx[n] = jnp.maximum(mx[n], jnp.max(blk, axis=0, keepdims=True))
        return tuple(mx)
    mx = chunk_loop(mask_body, tuple(jnp.full((1, LANES), -jnp.inf, F32) for _ in blocks))

    acc[...] = jnp.zeros(acc.shape, F32)

    def pv_body(cs_, ls):
        ls = list(ls)
        pv = jnp.zeros(acc.shape, F32)
        for c in cs_:
            rows = []
            for a in range(2):
                cols = []
                for p in range(4):
                    e = jnp.exp(logits[c, rs(a), cs(p)] - mx[a * 4 + p])
                    ls[a * 4 + p] = ls[a * 4 + p] + jnp.sum(e, axis=0, keepdims=True)
                    cols.append(e.astype(BF16))
                rows.append(jnp.concatenate(cols, axis=1))
            pmat = jnp.concatenate(rows, axis=0)
            pv = pv + jnp.dot(vblk_t[c], pmat, preferred_element_type=F32)
        acc[...] = acc[...] + pv
        return tuple(ls)
    ls = chunk_loop(pv_body, tuple(jnp.zeros((1, LANES), F32) for _ in blocks))

    for p in range(4):
        inv = jnp.where(row < HEAD_DIM, 1.0 / ls[p], 1.0 / ls[4 + p])
        o_ref[0, :, cs(p)] = (acc[:, cs(p)] * inv).T.astype(BF16)

    @pl.when(phase == steps_per_seq // 2)
    def _sample_attention():
        _wait_page_copies(kbuf, sems.at[0])
        _wait_page_copies(vbuf, sems.at[1])
        od_ref[0] = _dec_attn_row(dq_ref[0], dkvn_ref[0], dsc_ref[0], dsn_ref[0], dthr_ref[0], dbias_ref[...],
                                  kbuf, vbuf, page)


def _attn(q_st, qi_st, ix, kv, bias_st, page_table, qm, kvn, sc_s, sn_s, thr_s, bias_dec, ck_t, cv_t,
          *, batch, seq, page):
    nq = seq // QB
    assert nq % 2 == 0
    topk = min(TOPK_MAX, seq // 4)
    ix3 = ix.reshape(batch, seq, LANES)
    kv3 = kv.reshape(batch, seq, 2 * LANES)
    db, n_pages = page_table.shape
    past = n_pages * page
    steps_per_seq = (batch * nq) // db
    assert steps_per_seq * db == batch * nq
    drow = lambda n: pl.BlockSpec((1, 1, n), lambda b, j, pt: ((b * nq + j) // steps_per_seq, 0, 0))
    return pl.pallas_call(
        functools.partial(_attn_kernel, n_chunks=nq, topk=topk, n_pages=n_pages, page=page,
                          steps_per_seq=steps_per_seq),
        grid_spec=pltpu.PrefetchScalarGridSpec(
            num_scalar_prefetch=1,
            grid=(batch, nq),
            in_specs=[pl.BlockSpec((1, 4 * QB, LANES), lambda b, j, pt: (b * nq + j, 0, 0)),
                      pl.BlockSpec((1, 4 * QB, LANES), lambda b, j, pt: (b * nq + j, 0, 0)),
                      pl.BlockSpec((1, QB, LANES), lambda b, j, pt: (b, j, 0)),
                      pl.BlockSpec((1, seq, LANES), lambda b, j, pt: (b, 0, 0)),
                      pl.BlockSpec((1, seq, 2 * LANES), lambda b, j, pt: (b, 0, 0)),
                      _full_spec(bias_st.shape),
                      pl.BlockSpec((1, N_HEADS, LANES), lambda b, j, pt: ((b * nq + j) // steps_per_seq, 0, 0)),
                      drow(2 * LANES), drow(past), drow(LANES), drow(LANES), _full_spec(bias_dec.shape),
                      pl.BlockSpec(memory_space=pl.ANY), pl.BlockSpec(memory_space=pl.ANY)],
            out_specs=[pl.BlockSpec((1, QB, 4 * LANES), lambda b, j, pt: (b, j, 0)), drow(4 * LANES)],
            scratch_shapes=[pltpu.VMEM((nq, 2 * QB, LANES), BF16), pltpu.VMEM((nq, LANES, 2 * QB), BF16),
                            pltpu.VMEM((nq, 2 * QB, LANES), BF16), pltpu.VMEM((seq, LANES), I32),
                            pltpu.VMEM((seq, LANES), F32),
                            pltpu.VMEM((nq, 2 * QB, 4 * LANES), F32), pltpu.VMEM((LANES, 4 * LANES), F32),
                            pltpu.VMEM((1, LANES), F32), pltpu.VMEM((1, LANES), F32), pltpu.VMEM((1, LANES), F32),
                            pltpu.VMEM((2 * HEAD_DIM, past), F32), pltpu.VMEM((2 * HEAD_DIM, past), F32),
                            pltpu.SemaphoreType.DMA((2,))]),
        out_shape=[jax.ShapeDtypeStruct((batch, seq, 4 * LANES), BF16),
                   jax.ShapeDtypeStruct((db, 1, 4 * LANES), BF16)],
        compiler_params=_cparams(2),
        name="attn",
    )(page_table, q_st, qi_st, ix3, ix3, kv3, bias_st, qm, kvn, sc_s, sn_s, thr_s, bias_dec, ck_t, cv_t)


def _rglru_gates(xc, wa, ba, wx, bx, lam):
    r = _sigmoid(_dot(xc, wa) + ba)
    i = _sigmoid(_dot(xc, wx) + bx)
    log_a = -RG_C * r * _softplus(-lam)
    a = jnp.exp(log_a)
    u = jnp.sqrt(1.0 - jnp.exp(2.0 * log_a)) * (i * xc)
    return a, u


def _rglru_prompt_kernel(g_ref, xr_ref, buf_ref, h0_ref, cw_ref, cb_ref, wa_ref, ba_ref, wx_ref, bx_ref, lam_ref,
                         o_ref, hl_ref, nb_ref, xs, a_s, u_s, tail, hc, *, tc):
    t = pl.program_id(0)
    width = cw_ref.shape[0]
    nb, _, d = g_ref.shape

    @pl.when(t == 0)
    def _load_state():
        tail[...] = jnp.zeros(tail.shape, F32)
        tail[:, 8 - (width - 1):8, :] = buf_ref[...]
        hc[...] = h0_ref[...]

    xs[:, 0:8, :] = tail[...]
    xs[:, 8:8 + tc, :] = xr_ref[...]
    tail[...] = xs[:, tc:tc + 8, :]
    xc = cb_ref[...]
    for jj in range(width):
        off = 8 - (width - 1) + jj
        xc = xc + cw_ref[jj:jj + 1, :] * xs[:, off:off + tc, :]
    a, u = _rglru_gates(xc.reshape(nb * tc, d), wa_ref[...], ba_ref[...], wx_ref[...], bx_ref[...], lam_ref[...])
    n_lb = d // LANES
    pitch = tc + 8
    for k in range(n_lb):
        for b in range(nb):
            a_s[k, b * pitch:b * pitch + tc, :] = a[b * tc:(b + 1) * tc, k * LANES:(k + 1) * LANES]
            u_s[k, b * pitch:b * pitch + tc, :] = u[b * tc:(b + 1) * tc, k * LANES:(k + 1) * LANES]

    def scan_body(i, hs):
        rows = pl.ds(i, nb, stride=pitch)
        out = []
        for k in range(n_lb):
            h = a_s[k, rows, :] * hs[k] + u_s[k, rows, :]
            u_s[k, rows, :] = h
            out.append(h)
        return tuple(out)
    h0 = hc[...]
    hs = lax.fori_loop(0, tc, scan_body, tuple(h0[:, k * LANES:(k + 1) * LANES] for k in range(n_lb)), unroll=8)
    h = jnp.concatenate(hs, axis=1)
    hc[...] = h
    hseq = jnp.concatenate([jnp.concatenate([u_s[k, b * pitch:b * pitch + tc, :] for b in range(nb)], axis=0)
                            for k in range(n_lb)], axis=1)
    o_ref[...] = (_gelu(g_ref[...].reshape(nb * tc, d)) * hseq).reshape(nb, tc, d).astype(BF16)
    hl_ref[...] = h
    nb_ref[...] = xs[:, tc + 8 - (width - 1):tc + 8, :]


def _rglru_prompt(gate, xr, buf, h0, rg, *, batch, seq, tc=256):
    d = gate.shape[-1]
    width = rg["cw"].shape[0]
    g3, x3 = gate.reshape(batch, seq, d), xr.reshape(batch, seq, d)
    blk = pl.BlockSpec((batch, tc, d), lambda t: (0, t, 0))
    vec = _full_spec((1, d))
    return pl.pallas_call(
        functools.partial(_rglru_prompt_kernel, tc=tc),
        grid=(seq // tc,),
        in_specs=[blk, blk, _full_spec((batch, width - 1, d)), _full_spec((batch, d)), _full_spec((width, d)), vec,
                  _full_spec((d, d)), vec, _full_spec((d, d)), vec, vec],
        out_specs=[blk, pl.BlockSpec((batch, d), lambda t: (0, 0)), pl.BlockSpec((batch, width - 1, d), lambda t: (0, 0, 0))],
        out_shape=[jax.ShapeDtypeStruct((batch, seq, d), BF16), jax.ShapeDtypeStruct((batch, d), F32),
                   jax.ShapeDtypeStruct((batch, width - 1, d), F32)],
        scratch_shapes=[pltpu.VMEM((batch, tc + 8, d), F32), pltpu.VMEM((d // LANES, batch * (tc + 8), LANES), F32),
                        pltpu.VMEM((d // LANES, batch * (tc + 8), LANES), F32), pltpu.VMEM((batch, 8, d), F32),
                        pltpu.VMEM((batch, d), F32)],
        compiler_params=_cparams(1),
        name="rglru_prompt",
    )(g3, x3, buf, h0, rg["cw"], rg["cb"], rg["wa"], rg["ba"], rg["wx"], rg["bx"], rg["lam"])


def _rglru_dec_kernel(g_ref, xr_ref, buf_ref, h0_ref, cw_ref, cb_ref, wa_ref, ba_ref, wx_ref, bx_ref, lam_ref,
                      o_ref, hl_ref):
    width = cw_ref.shape[0]
    xc = cb_ref[...]
    for jj in range(width - 1):
        xc = xc + cw_ref[jj:jj + 1, :] * buf_ref[jj]
    xc = xc + cw_ref[width - 1:width, :] * xr_ref[...]
    a, u = _rglru_gates(xc, wa_ref[...], ba_ref[...], wx_ref[...], bx_ref[...], lam_ref[...])
    h = a * h0_ref[...] + u
    hl_ref[...] = h
    o_ref[...] = (_gelu(g_ref[...]) * h).astype(BF16)


def _rglru_dec(gate, xr, buf_t, h0, rg):
    m, d = gate.shape
    return pl.pallas_call(
        _rglru_dec_kernel,
        out_shape=[jax.ShapeDtypeStruct((m, d), BF16), jax.ShapeDtypeStruct((m, d), F32)],
        name="rglru_dec",
    )(gate, xr, buf_t, h0, rg["cw"], rg["cb"], rg["wa"], rg["ba"], rg["wx"], rg["bx"], rg["lam"])


def _ffn_tile(y1, gf_ref, wup_ref, cw_ref, cb_ref, wdn_ref, conv_prev, n_split):
    d_ff = wdn_ref.shape[0]
    cf = d_ff // n_split
    hn = _rms(y1, gf_ref[...]).astype(BF16)
    out = jnp.zeros(y1.shape, F32)
    gates = []
    for k in range(n_split):
        c0 = k * cf
        g = jnp.dot(hn, wup_ref[:, c0:c0 + cf], preferred_element_type=F32)
        u = jnp.dot(hn, wup_ref[:, d_ff + c0:d_ff + c0 + cf], preferred_element_type=F32)
        g1, g2 = conv_prev(k, g)
        gc = cb_ref[:, c0:c0 + cf] + cw_ref[0:1, c0:c0 + cf] * g2 + cw_ref[1:2, c0:c0 + cf] * g1 \
            + cw_ref[2:3, c0:c0 + cf] * g
        act = (_gelu(gc) * u).astype(BF16)
        out = out + jnp.dot(act, wdn_ref[c0:c0 + cf, :], preferred_element_type=F32)
        gates.append(g)
    return out, gates


def _prompt_conv_prev(gs, carry, fb_ref, nb_ref, tm, cf):
    t = pl.program_id(1)

    @pl.when(t == 0)
    def _load_state():
        carry[...] = jnp.zeros(carry.shape, F32)
        for k in range(carry.shape[0]):
            carry[k, 6:8, :] = fb_ref[0, :, k * cf:(k + 1) * cf]

    def conv_prev(k, g):
        gs[0:8, :] = carry[k]
        gs[8:8 + tm, :] = g
        carry[k] = g[tm - 8:tm, :]
        nb_ref[0, :, k * cf:(k + 1) * cf] = g[tm - 2:tm, :]
        return gs[7:7 + tm, :], gs[6:6 + tm, :]
    return conv_prev


def _mix_ab_tile(y_ref, a_ref, r_ref, woa_ref, wob_ref):
    return y_ref[0] + jnp.dot(a_ref[0], woa_ref[...], preferred_element_type=F32) \
        + jnp.dot(r_ref[0], wob_ref[...], preferred_element_type=F32)


def _post_ab_prompt_kernel(y_ref, a_ref, r_ref, fb_ref, woa_ref, wob_ref, gf_ref, wup_ref, cw_ref, cb_ref, wdn_ref,
                           o_ref, nb_ref, gs, carry, *, tm, n_split):
    y1 = _mix_ab_tile(y_ref, a_ref, r_ref, woa_ref, wob_ref)
    cf = wdn_ref.shape[0] // n_split
    out, _ = _ffn_tile(y1, gf_ref, wup_ref, cw_ref, cb_ref, wdn_ref,
                       _prompt_conv_prev(gs, carry, fb_ref, nb_ref, tm, cf), n_split)
    o_ref[0] = y1 + out


_FFN_KEYS = ("g", "wup", "cw", "cb", "wdn")


def _ffn_specs(ffn):
    layer = ffn["layer"]
    return [pl.BlockSpec((None,) + ffn[k].shape[1:], lambda *_: (layer, 0, 0), pipeline_mode=pl.Buffered(1))
            for k in _FFN_KEYS]


def _ffn_args(ffn):
    return [ffn[k] for k in _FFN_KEYS]


def _post_ab_prompt(y, attn, rgo, fbuf, wo_a, wo_b, ffn, *, tm=512, n_split=2):
    batch, seq, d = y.shape
    d_ff = ffn["wdn"].shape[1]
    cf = d_ff // n_split
    blk = lambda n: pl.BlockSpec((1, tm, n), lambda b, t: (b, t, 0))
    fb = pl.BlockSpec((1, 2, d_ff), lambda b, t: (b, 0, 0))
    return pl.pallas_call(
        functools.partial(_post_ab_prompt_kernel, tm=tm, n_split=n_split),
        grid=(batch, seq // tm),
        in_specs=[blk(d), blk(attn.shape[-1]), blk(rgo.shape[-1]), fb, _full_spec(wo_a.shape), _full_spec(wo_b.shape)]
        + _ffn_specs(ffn),
        out_specs=[blk(d), fb],
        out_shape=[jax.ShapeDtypeStruct((batch, seq, d), F32), jax.ShapeDtypeStruct((batch, 2, d_ff), F32)],
        scratch_shapes=[pltpu.VMEM((tm + 8, cf), F32), pltpu.VMEM((n_split, 8, cf), F32)],
        compiler_params=_cparams(2),
        name="post_ab_prompt",
    )(y, attn, rgo, fbuf, wo_a, wo_b, *_ffn_args(ffn))


def _dec_conv_prev(fb_ref, cf):
    def conv_prev(k, g):
        return fb_ref[1, :, k * cf:(k + 1) * cf], fb_ref[0, :, k * cf:(k + 1) * cf]
    return conv_prev


def _post_ab_dec_kernel(y_ref, a_ref, r_ref, fb_ref, woa_ref, wob_ref, gf_ref, wup_ref, cw_ref, cb_ref, wdn_ref,
                        o_ref, g_ref, *, n_split):
    y1 = y_ref[...] + jnp.dot(a_ref[...], woa_ref[...], preferred_element_type=F32) \
        + jnp.dot(r_ref[...], wob_ref[...], preferred_element_type=F32)
    cf = wdn_ref.shape[0] // n_split
    out, gates = _ffn_tile(y1, gf_ref, wup_ref, cw_ref, cb_ref, wdn_ref, _dec_conv_prev(fb_ref, cf), n_split)
    o_ref[...] = y1 + out
    for k, g in enumerate(gates):
        g_ref[:, k * cf:(k + 1) * cf] = g


def _post_ab_dec(y, attn, rgo, fbuf_t, wo_a, wo_b, ffn, *, n_split=2):
    m, d = y.shape
    d_ff = ffn["wdn"].shape[1]
    args = (y, attn, rgo, fbuf_t, wo_a, wo_b)
    return pl.pallas_call(
        functools.partial(_post_ab_dec_kernel, n_split=n_split),
        grid=(1,),
        in_specs=[_full_spec(a.shape) for a in args] + _ffn_specs(ffn),
        out_specs=[_whole_spec((m, d)), _whole_spec((m, d_ff))],
        out_shape=[jax.ShapeDtypeStruct((m, d), F32), jax.ShapeDtypeStruct((m, d_ff), F32)],
        compiler_params=_cparams(1),
        name="post_ab_dec",
    )(*args, *_ffn_args(ffn))


def _gmlp_in(y, gm_ref, win_ref, bin_ref, sn_ref):
    d_c = win_ref.shape[1] // 2
    z = _gelu(_dot(_rms(y, gm_ref[...]), win_ref[...]) + bin_ref[...])
    return z[:, :d_c], _rms(z[:, d_c:], sn_ref[...])


def _layer_c_prompt_kernel(y_ref, fb_ref, gm_ref, win_ref, bin_ref, sn_ref, sw_ref, sbt_ref, woc_ref,
                           gf_ref, wup_ref, cw_ref, cb_ref, wdn_ref, gfin_ref,
                           o_ref, nb_ref, gs, carry, *, tm, n_split):
    y = y_ref[0]
    u, v = _gmlp_in(y, gm_ref, win_ref, bin_ref, sn_ref)
    vb = v.astype(BF16)
    n_groups = sw_ref.shape[0]
    tril = lax.broadcasted_iota(I32, (CHUNK, CHUNK), 0) >= lax.broadcasted_iota(I32, (CHUNK, CHUNK), 1)
    wm = [jnp.where(tril, sw_ref[gi], 0.0).astype(BF16) for gi in range(n_groups)]
    rows = []
    for r in range(tm // CHUNK):
        cols = []
        for gi in range(n_groups):
            mixed = jnp.dot(wm[gi], vb[r * CHUNK:(r + 1) * CHUNK, gi * LANES:(gi + 1) * LANES],
                            preferred_element_type=F32)
            cols.append(mixed + sbt_ref[:, gi:gi + 1])
        rows.append(jnp.concatenate(cols, axis=1))
    gated = u * jnp.concatenate(rows, axis=0)
    y1 = y + _dot(gated, woc_ref[...])
    cf = wdn_ref.shape[0] // n_split
    out, _ = _ffn_tile(y1, gf_ref, wup_ref, cw_ref, cb_ref, wdn_ref,
                       _prompt_conv_prev(gs, carry, fb_ref, nb_ref, tm, cf), n_split)
    o_ref[0] = _rms(y1 + out, gfin_ref[...])


def _layer_c_prompt(y, fbuf, cp, ffn, g_final, *, tm=512, n_split=2):
    batch, seq, d = y.shape
    d_ff = ffn["wdn"].shape[1]
    cf = d_ff // n_split
    blk = pl.BlockSpec((1, tm, d), lambda b, t: (b, t, 0))
    fb = pl.BlockSpec((1, 2, d_ff), lambda b, t: (b, 0, 0))
    consts = [cp["g"], cp["win"], cp["bin"], cp["sn"], cp["sw"], cp["sbt"], cp["woc"]]
    return pl.pallas_call(
        functools.partial(_layer_c_prompt_kernel, tm=tm, n_split=n_split),
        grid=(batch, seq // tm),
        in_specs=[blk, fb] + [_full_spec(c.shape) for c in consts] + _ffn_specs(ffn) + [_full_spec(g_final.shape)],
        out_specs=[blk, fb],
        out_shape=[jax.ShapeDtypeStruct((batch, seq, d), F32), jax.ShapeDtypeStruct((batch, 2, d_ff), F32)],
        scratch_shapes=[pltpu.VMEM((tm + 8, cf), F32), pltpu.VMEM((n_split, 8, cf), F32)],
        compiler_params=_cparams(2),
        name="layer_c_prompt",
    )(y, fbuf, *consts, *_ffn_args(ffn), g_final)


def _layer_c_dec_kernel(y_ref, fb_ref, gm_ref, win_ref, bin_ref, sn_ref, sw0_ref, sb0_ref, woc_ref,
                        gf_ref, wup_ref, cw_ref, cb_ref, wdn_ref, gfin_ref, o_ref, g_ref, v_ref, *, n_split):
    y = y_ref[...]
    u, v = _gmlp_in(y, gm_ref, win_ref, bin_ref, sn_ref)
    v_ref[...] = v
    y1 = y + _dot(u * (sw0_ref[...] * v + sb0_ref[...]), woc_ref[...])
    cf = wdn_ref.shape[0] // n_split
    out, gates = _ffn_tile(y1, gf_ref, wup_ref, cw_ref, cb_ref, wdn_ref, _dec_conv_prev(fb_ref, cf), n_split)
    o_ref[...] = _rms(y1 + out, gfin_ref[...])
    for k, g in enumerate(gates):
        g_ref[:, k * cf:(k + 1) * cf] = g


def _layer_c_dec(y, fbuf_t, cp, ffn, g_final, *, n_split=2):
    m, d = y.shape
    d_ff = ffn["wdn"].shape[1]
    d_c = cp["woc"].shape[0]
    args = (y, fbuf_t, cp["g"], cp["win"], cp["bin"], cp["sn"], cp["sw0"], cp["sb0"], cp["woc"])
    return pl.pallas_call(
        functools.partial(_layer_c_dec_kernel, n_split=n_split),
        grid=(1,),
        in_specs=[_full_spec(a.shape) for a in args] + _ffn_specs(ffn) + [_full_spec(g_final.shape)],
        out_specs=[_whole_spec((m, d)), _whole_spec((m, d_ff)), _whole_spec((m, d_c))],
        out_shape=[jax.ShapeDtypeStruct((m, d), F32), jax.ShapeDtypeStruct((m, d_ff), F32),
                   jax.ShapeDtypeStruct((m, d_c), F32)],
        compiler_params=_cparams(1),
        name="layer_c_dec",
    )(*args, *_ffn_args(ffn), g_final)


def _start_page_copies(src_ref, pt_ref, b, dst_ref, sem, n_pages, page):
    def body(pg, carry):
        col = pl.multiple_of(pg * page, page)
        pltpu.make_async_copy(src_ref.at[pt_ref[b, pg]], dst_ref.at[:, pl.ds(col, page)], sem).start()
        return carry
    lax.fori_loop(0, n_pages, body, 0, unroll=8)


def _wait_page_copies(dst_ref, sem):
    pltpu.make_async_copy(dst_ref, dst_ref, sem).wait()


def _dec_score_kernel(pt_ref, qi_ref, wi_ref, ixn_ref, cik_ref, keys_ref, knew_ref, ibuf, sems, *, n_pages, page):
    b = pl.program_id(0)
    nb = pl.num_programs(0)
    slot = lax.rem(b, 2)

    def start(bb, sl):
        _start_page_copies(cik_ref, pt_ref, bb, ibuf.at[sl], sems.at[sl], n_pages, page)

    @pl.when(b == 0)
    def _first():
        start(0, 0)

    @pl.when(b + 1 < nb)
    def _prefetch_next():
        start(b + 1, 1 - slot)

    _wait_page_copies(ibuf.at[slot], sems.at[slot])
    qi = qi_ref[0].astype(BF16)
    wi = wi_ref[0]
    s = jnp.dot(qi, ibuf[slot].astype(BF16), preferred_element_type=F32)
    sc = jnp.sum(jnp.maximum(s, 0.0) * wi, axis=0, keepdims=True) * IDX_HEADS ** -0.5
    keys_ref[0] = sc
    kin = ixn_ref[0][:, 0:IDX_DIM]
    sn = jnp.sum(qi_ref[0] * kin, axis=1, keepdims=True)
    scn = jnp.sum(jnp.maximum(sn, 0.0) * wi, axis=0, keepdims=True) * IDX_HEADS ** -0.5
    knew_ref[0] = jnp.broadcast_to(scn, (1, LANES))


def _dec_scores(page_table, qi3, wi3, ix3, cik_t, *, page):
    db, n_pages = page_table.shape
    past = n_pages * page
    return pl.pallas_call(
        functools.partial(_dec_score_kernel, n_pages=n_pages, page=page),
        grid_spec=pltpu.PrefetchScalarGridSpec(
            num_scalar_prefetch=1,
            grid=(db,),
            in_specs=[pl.BlockSpec((1, IDX_HEADS, IDX_DIM), lambda b, pt: (b, 0, 0)),
                      pl.BlockSpec((1, IDX_HEADS, 1), lambda b, pt: (b, 0, 0)),
                      pl.BlockSpec((1, 1, LANES), lambda b, pt: (b, 0, 0)),
                      pl.BlockSpec(memory_space=pl.ANY)],
            out_specs=[pl.BlockSpec((1, 1, past), lambda b, pt: (b, 0, 0)),
                       pl.BlockSpec((1, 1, LANES), lambda b, pt: (b, 0, 0))],
            scratch_shapes=[pltpu.VMEM((2, IDX_DIM, past), F32), pltpu.SemaphoreType.DMA((2,))]),
        out_shape=[jax.ShapeDtypeStruct((db, 1, past), F32), jax.ShapeDtypeStruct((db, 1, LANES), F32)],
        compiler_params=_cparams(1),
        name="dec_scores",
    )(page_table, qi3, wi3, ix3, cik_t)


def _dec_select_kernel(sc_ref, scn_ref, so_ref, sno_ref, thr_ref, *, topk):
    past = sc_ref.shape[1]
    lane0 = lax.broadcasted_iota(I32, scn_ref.shape, 1) == 0
    sc = jnp.concatenate([sc_ref[...], jnp.where(lane0, scn_ref[...], -jnp.inf)], axis=1)
    kk = _float_key(sc)
    kf = jnp.float32(topk)

    def count(pred):
        ones = jnp.where(pred, 1.0, 0.0)
        accs = [ones[:, k * LANES:(k + 1) * LANES] for k in range(8)]
        for k in range(8, ones.shape[1] // LANES):
            accs[k % 8] = accs[k % 8] + ones[:, k * LANES:(k + 1) * LANES]
        return jnp.sum(functools.reduce(lambda x, y: x + y, accs), axis=1, keepdims=True)

    def search_body(i, ans):
        cand = ans | jnp.left_shift(jnp.int32(1), 31 - i)
        return jnp.where(count(kk >= (cand ^ jnp.int32(INT_MIN))) >= kf, cand, ans)
    ans = lax.fori_loop(0, 32, search_body, jnp.zeros((sc.shape[0], 1), I32))
    thr = _key_float(ans ^ jnp.int32(INT_MIN))
    counts = lambda t: (count(sc >= t), count(sc > t))
    thr, cge, cgt = _walk_to_kth(lambda: sc, thr, *counts(thr), kf, counts, 4 * topk)
    need = kf - cgt
    big = jnp.int32(4 * past)
    eqcol = jnp.where(sc == thr, lax.broadcasted_iota(I32, sc.shape, 1), big)
    nbits = int(math.log2(past)) + 1

    def tie_body(i, best):
        cand = best | jnp.left_shift(jnp.int32(1), nbits - 1 - i)
        return jnp.where(count(eqcol < cand) < need, cand, best)
    last = lax.fori_loop(0, nbits, tie_body, jnp.zeros((sc.shape[0], 1), I32))
    sc = jnp.where((eqcol > last) & (eqcol < big), -jnp.inf, sc)
    so_ref[...] = sc[:, :past]
    sno_ref[...] = sc[:, past:]
    thr_ref[...] = jnp.broadcast_to(thr, thr_ref.shape)


def _dec_select(scores, snew, *, topk):
    db, past = scores.shape
    assert past + 1 >= topk
    return pl.pallas_call(
        functools.partial(_dec_select_kernel, topk=topk),
        out_shape=[jax.ShapeDtypeStruct((db, past), F32), jax.ShapeDtypeStruct((db, LANES), F32),
                   jax.ShapeDtypeStruct((db, LANES), F32)],
        name="dec_select",
    )(scores, snew)


def _dec_attn_row(qm, kvn, sc, sn, thr_row, bias, kbuf, vbuf, page):
    past = kbuf.shape[1]
    thr = thr_row[:, 0:1]
    sel = sc >= thr
    sel_new = sn[:, 0:1] >= thr
    far, last, bnew = bias[:, LANES:LANES + 1], bias[:, 0:page], bias[:, LANES + 1:LANES + 2]
    step = min(past, DEC_KEY_CHUNK)
    chunks = [(c, min(c + step, past)) for c in range(0, past, step)]
    qb = qm.astype(BF16)
    parts = []
    for c0, c1 in chunks:
        part = jnp.dot(qb, kbuf[:, c0:c1].astype(BF16), preferred_element_type=F32) + far
        if c1 == past:
            part = jnp.concatenate([part[:, :c1 - c0 - page], part[:, c1 - c0 - page:] + (last - far)], axis=1)
        parts.append(jnp.where(sel[:, c0:c1], part, -jnp.inf))
    lgn = jnp.sum(qm * kvn[:, 0:LANES], axis=1, keepdims=True) + bnew
    lgn = jnp.where(sel_new, lgn, -jnp.inf)
    m = lgn
    for part in parts:
        m = jnp.maximum(m, jnp.max(part, axis=1, keepdims=True))
    en = jnp.exp(lgn - m)
    den = en
    pv = en * kvn[:, LANES:2 * LANES]
    for (c0, c1), part in zip(chunks, parts):
        e = jnp.exp(part - m)
        den = den + jnp.sum(e, axis=1, keepdims=True)
        pv = pv + _dot_nt(e.astype(BF16), vbuf[:, c0:c1].astype(BF16))
    pv = pv / den
    lo = lax.broadcasted_iota(I32, (1, LANES), 1) < HEAD_DIM
    return jnp.concatenate([jnp.where(lo, pv[2 * p:2 * p + 1], pv[2 * p + 1:2 * p + 2]) for p in range(4)],
                           axis=1).astype(BF16)


def _prep_in_ab(w):
    d = w.shape[0]
    nq, nkv = N_HEADS * HEAD_DIM, N_KV_HEADS * HEAD_DIM
    offs = np.cumsum([nq, nkv, nkv, IDX_HEADS * IDX_DIM, IDX_DIM, IDX_HEADS, 512])
    q, k, v, qi, ki, wi, g, xr = jnp.split(w, offs.tolist(), axis=1)
    q = q.reshape(d, N_HEADS, HEAD_DIM)[:, np.array(HEAD_PERM), :].reshape(d, nq)
    pad = jnp.zeros((d, _C_G - _C_IX - IDX_DIM - IDX_HEADS), w.dtype)
    return jnp.concatenate([q, k, v, qi, ki, wi, pad, g, xr], axis=1).astype(BF16)


def _block_diag(w):
    n, c, _ = w.shape
    return (jnp.eye(n, dtype=w.dtype)[:, None, :, None] * w[:, :, None, :]).reshape(n * c, n * c).astype(BF16)


def _ffn_params(layer, stacked):
    return dict(stacked, layer=layer)


def kernel(x_prompt, x_sample, cache_k, cache_v, cache_idx_k, state_rglru_h, state_rglru_conv, state_ffn_conv,
           page_table, norm_mix, norm_ffn, norm_final, rel_bias, w_in_ab, w_out_ab, rg_conv_w, rg_conv_b,
           rg_wa, rg_ba, rg_wx, rg_bx, rg_lambda, w_in_c, b_in_c, sgu_norm, sgu_w, sgu_b, w_out_c,
           ffn_w_up, ffn_conv_w, ffn_conv_b, ffn_w_down):
    batch, seq, d = x_prompt.shape
    db = x_sample.shape[0]
    page = cache_k.shape[2]
    d_a = N_HEADS * HEAD_DIM
    d_b = rg_conv_w.shape[-1]
    d_ff = ffn_w_down.shape[1]
    assert x_sample.shape[1] == 1 and seq % 512 == 0 and page == LANES and w_in_ab.shape[0] == 1

    w_in0 = _prep_in_ab(w_in_ab[0])
    wo = w_out_ab[0]
    wo_a = wo[:d_a].reshape(N_HEADS, HEAD_DIM, d)[np.array(HEAD_PERM)].reshape(d_a, d).astype(BF16)
    wo_b = wo[d_a:].astype(BF16)
    rg = {"cw": rg_conv_w[0], "cb": rg_conv_b[0][None], "wa": _block_diag(rg_wa[0]), "ba": rg_ba[0][None],
          "wx": _block_diag(rg_wx[0]), "bx": rg_bx[0][None], "lam": rg_lambda[0][None]}
    ffn_all = {"g": norm_ffn[:, None, :], "wup": ffn_w_up.astype(BF16), "cw": ffn_conv_w, "cb": ffn_conv_b[:, None, :],
               "wdn": ffn_w_down.astype(BF16)}
    ffn0, ffn1 = _ffn_params(0, ffn_all), _ffn_params(1, ffn_all)
    cp = {"g": norm_mix[1][None], "win": w_in_c[0].astype(BF16), "bin": b_in_c[0][None], "sn": sgu_norm[0][None],
          "sw": sgu_w[0], "sbt": sgu_b[0].T, "woc": w_out_c[0].astype(BF16),
          "sw0": jnp.repeat(sgu_w[0][:, 0, 0], d // sgu_w.shape[1])[None],
          "sb0": jnp.repeat(sgu_b[0][:, 0], d // sgu_w.shape[1])[None]}
    g_mix0 = norm_mix[0][None]
    g_final = norm_final[None]
    bias_st, bias_dec = _bias_tables(rel_bias, page)

    xp = x_prompt.reshape(batch * seq, d)
    xs = x_sample.reshape(db, d)
    q_st, qi_st, kv_p, ix_p, gate_p, xr_p = _inproj(xp, g_mix0, w_in0, stack=True, tm=512)
    qm_s, qi_s, kv_s, ix_s, gate_s, xr_s = _inproj(xs, g_mix0, w_in0, stack=False, tm=db)
    cik_t = jnp.transpose(cache_idx_k[0], (0, 2, 1))
    ck_t = jnp.transpose(cache_k[0], (0, 2, 3, 1)).reshape(-1, 2 * HEAD_DIM, page)
    cv_t = jnp.transpose(cache_v[0], (0, 2, 3, 1)).reshape(-1, 2 * HEAD_DIM, page)
    topk_s = min(TOPK_MAX, (page_table.shape[1] * page + 1) // 4)
    sc_s, sn_s = _dec_scores(page_table, qi_s.reshape(db, IDX_HEADS, IDX_DIM),
                             ix_s[:, IDX_DIM:IDX_DIM + IDX_HEADS].reshape(db, IDX_HEADS, 1),
                             ix_s.reshape(db, 1, LANES), cik_t, page=page)
    sc_s, sn_s, thr_s = _dec_select(sc_s.reshape(db, -1), sn_s.reshape(db, LANES), topk=topk_s)

    attn_p, attn_s = _attn(q_st, qi_st, ix_p, kv_p, bias_st, page_table, jnp.transpose(qm_s, (1, 0, 2)),
                           kv_s.reshape(db, 1, 2 * LANES), sc_s.reshape(db, 1, -1), sn_s.reshape(db, 1, LANES),
                           thr_s.reshape(db, 1, LANES), bias_dec, ck_t, cv_t, batch=batch, seq=seq, page=page)
    attn_s = attn_s.reshape(db, d_a)

    rg_p, h_p, cbuf_p = _rglru_prompt(gate_p, xr_p, jnp.zeros((batch, rg["cw"].shape[0] - 1, d_b), F32),
                                      jnp.zeros((batch, d_b), F32), rg, batch=batch, seq=seq)
    zero_fb = jnp.zeros((batch, 2, d_ff), F32)
    y1_p, fb0_p = _post_ab_prompt(x_prompt, attn_p, rg_p, zero_fb, wo_a, wo_b, ffn0)
    y_p, fb1_p = _layer_c_prompt(y1_p, zero_fb, cp, ffn1, g_final)

    cbuf_s_in = state_rglru_conv[0]
    rg_s, h_s = _rglru_dec(gate_s, xr_s, jnp.transpose(cbuf_s_in, (1, 0, 2)), state_rglru_h[0], rg)
    y1_s, g0_s = _post_ab_dec(xs, attn_s, rg_s, jnp.transpose(state_ffn_conv[0], (1, 0, 2)), wo_a, wo_b, ffn0)
    y_s, g1_s, v_s = _layer_c_dec(y1_s, jnp.transpose(state_ffn_conv[1], (1, 0, 2)), cp, ffn1, g_final)

    kv4 = kv_p.reshape(batch, seq, 2, N_KV_HEADS, HEAD_DIM)
    kvs = kv_s.reshape(db, 1, 2, N_KV_HEADS, HEAD_DIM)
    fbuf_s = lambda layer, g: jnp.concatenate([state_ffn_conv[layer][:, 1:], g[:, None]], axis=1)
    return (y_p, y_s.reshape(db, 1, d),
            kv4[None, :, :, 0], kv4[None, :, :, 1], ix_p.reshape(batch, seq, LANES)[None, :, :, :IDX_DIM],
            kvs[None, :, :, 0], kvs[None, :, :, 1], ix_s.reshape(db, 1, LANES)[None, :, :, :IDX_DIM],
            h_p.reshape(batch, d_b)[None], cbuf_p[None],
            h_s[None], jnp.concatenate([cbuf_s_in[:, 1:], xr_s[:, None]], axis=1)[None],
            v_s.reshape(db, 1, -1)[None],
            jnp.stack([fb0_p, fb1_p]), jnp.stack([fbuf_s(0, g0_s), fbuf_s(1, g1_s)]))
```

```python
import functools
import math

import numpy as np
import jax
import jax.numpy as jnp
from jax import lax
from jax.experimental import pallas as pl
from jax.experimental.pallas import tpu as pltpu

F32 = jnp.float32
BF16 = jnp.bfloat16
I32 = jnp.int32

N_HEADS = 8
HEAD_DIM = 64
N_KV_HEADS = 2
Q_PER_KV = N_HEADS // N_KV_HEADS
IDX_HEADS = 8
IDX_DIM = 64
TOPK_MAX = 256
N_BUCKETS = 32
REL_MAX_EXACT = N_BUCKETS // 2
REL_MAX_DIST = 128
RG_C = 8.0
CHUNK = 128
EPS = 1e-6

LANES = 128
QB = 128
DEC_KEY_CHUNK = 2048
INT_MIN = -(2 ** 31)
KEY_MIN_FINITE = INT_MIN + 0x800000
NEG_MAX = float(np.finfo(np.float32).min)
HEAD_PERM = (0, 4, 1, 5, 2, 6, 3, 7)
VMEM_LIMIT = 56 * 1024 * 1024


def _cparams(n_grid):
    return pltpu.CompilerParams(dimension_semantics=("arbitrary",) * n_grid, vmem_limit_bytes=VMEM_LIMIT)


def _full_spec(shape):
    nd = len(shape)
    return pl.BlockSpec(shape, lambda *_: (0,) * nd, pipeline_mode=pl.Buffered(1))


def _whole_spec(shape):
    nd = len(shape)
    return pl.BlockSpec(shape, lambda *_: (0,) * nd)


def _rms(x, g):
    return x * lax.rsqrt(jnp.mean(x * x, axis=-1, keepdims=True) + EPS) * g


def _gelu(x):
    return x * (0.5 * (1.0 + jnp.tanh(math.sqrt(2.0 / math.pi) * (x + 0.044715 * (x * x * x)))))


def _sigmoid(x):
    return 1.0 / (1.0 + jnp.exp(-x))


def _softplus(x):
    return jnp.maximum(x, 0.0) + jnp.log(1.0 + jnp.exp(-jnp.abs(x)))


def _dot(a, b):
    return jnp.dot(a.astype(BF16), b, preferred_element_type=F32)


def _dot_nt(a, b):
    return lax.dot_general(a, b, (((1,), (1,)), ((), ())), preferred_element_type=F32)


def _float_key(x):
    bits = pltpu.bitcast(x, I32)
    key = jnp.where(bits < 0, bits ^ jnp.int32(0x7FFFFFFF), bits)
    return jnp.where(bits == jnp.int32(INT_MIN), jnp.int32(0), key)


def _key_float(key):
    key = jnp.maximum(key, jnp.int32(KEY_MIN_FINITE))
    return pltpu.bitcast(jnp.where(key < 0, key ^ jnp.int32(0x7FFFFFFF), key), F32)


def _walk_to_kth(scores, t, cge, cgt, kf, count_ge_gt, max_steps):
    axis = 0 if t.shape[0] == 1 else 1

    def settled(t, cge, cgt):
        return (cgt < kf) & ((cge >= kf) | (t <= NEG_MAX))

    def unsettled(t, cge, cgt):
        return jnp.max(jnp.where(settled(t, cge, cgt), 0.0, 1.0)) > 0.0

    def body(st):
        t, cge, cgt, it = st
        sc = scores()
        below = jnp.max(jnp.where(sc < t, sc, -jnp.inf), axis=axis, keepdims=True)
        above = jnp.min(jnp.where(sc > t, sc, jnp.inf), axis=axis, keepdims=True)
        t = jnp.where(cgt >= kf, above, jnp.where((cge < kf) & (t > NEG_MAX), jnp.maximum(below, NEG_MAX), t))
        cge, cgt = count_ge_gt(t)
        return t, cge, cgt, it + 1

    t, cge, cgt, _ = lax.while_loop(lambda st: unsettled(st[0], st[1], st[2]) & (st[3] < max_steps), body,
                                    (t, cge, cgt, jnp.int32(0)))
    return t, cge, cgt


def _t5_bucket_np(n):
    n = np.maximum(n, 0)
    nf = np.maximum(n, 1).astype(np.float32)
    large = REL_MAX_EXACT + (np.log(nf / np.float32(REL_MAX_EXACT)) / np.float32(math.log(REL_MAX_DIST / REL_MAX_EXACT))
                             * np.float32(N_BUCKETS - REL_MAX_EXACT)).astype(np.int32)
    large = np.minimum(large, N_BUCKETS - 1)
    return np.where(n < REL_MAX_EXACT, n, large).astype(np.int32)


_C_Q, _C_KV, _C_QI, _C_IX, _C_G, _C_X, _C_END = 0, 512, 768, 1280, 1408, 1920, 2432


def _inproj_kernel(x_ref, g_ref, w_ref, q_ref, qi_ref, kv_ref, ix_ref, gate_ref, xr_ref, *, stack):
    hn = _rms(x_ref[...], g_ref[...])
    z = _dot(hn, w_ref[...])
    q = z[:, _C_Q:_C_KV] * HEAD_DIM ** -0.5
    qi = z[:, _C_QI:_C_IX] * IDX_DIM ** -0.5
    kv_ref[...] = z[:, _C_KV:_C_QI]
    ix_ref[...] = z[:, _C_IX:_C_G]
    gate_ref[...] = z[:, _C_G:_C_X]
    xr_ref[...] = z[:, _C_X:_C_END]
    if stack:
        qb, qib = q.astype(BF16), qi.astype(BF16)
        for r in range(q.shape[0] // QB):
            for p in range(4):
                q_ref[r, p * QB:(p + 1) * QB, :] = qb[r * QB:(r + 1) * QB, p * LANES:(p + 1) * LANES]
                qi_ref[r, p * QB:(p + 1) * QB, :] = qib[r * QB:(r + 1) * QB, p * LANES:(p + 1) * LANES]
    else:
        lo = lax.broadcasted_iota(I32, (q.shape[0], LANES), 1) < HEAD_DIM
        for p in range(4):
            qp = q[:, p * LANES:(p + 1) * LANES]
            q_ref[2 * p] = jnp.where(lo, qp, 0.0)
            q_ref[2 * p + 1] = jnp.where(lo, 0.0, qp)
        qi_ref[...] = qi


def _inproj(x2d, g, w, *, stack, tm):
    m, d = x2d.shape
    if stack:
        q_shape, q_spec = (m // QB, 4 * QB, LANES), pl.BlockSpec((tm // QB, 4 * QB, LANES), lambda i: (i, 0, 0))
        qi_shape, qi_spec, qdt = q_shape, q_spec, BF16
    else:
        q_shape, q_spec = (N_HEADS, m, LANES), pl.BlockSpec((N_HEADS, tm, LANES), lambda i: (0, i, 0))
        qi_shape, qi_spec, qdt = (m, 512), pl.BlockSpec((tm, 512), lambda i: (i, 0)), F32
    row = lambda n: pl.BlockSpec((tm, n), lambda i: (i, 0))
    return pl.pallas_call(
        functools.partial(_inproj_kernel, stack=stack),
        grid=(m // tm,),
        in_specs=[row(d), _full_spec((1, d)), _full_spec(w.shape)],
        out_specs=[q_spec, qi_spec, row(256), row(128), row(512), row(512)],
        out_shape=[jax.ShapeDtypeStruct(q_shape, qdt), jax.ShapeDtypeStruct(qi_shape, qdt),
                   jax.ShapeDtypeStruct((m, 256), F32), jax.ShapeDtypeStruct((m, 128), F32),
                   jax.ShapeDtypeStruct((m, 512), F32), jax.ShapeDtypeStruct((m, 512), F32)],
        compiler_params=_cparams(1),
        name="inproj_stack" if stack else "inproj_dec",
    )(x2d, g, w)


def _bias_kernel(rb_ref, bk_ref, bkd_ref, o_ref, od_ref):
    for d in range(3):
        bk = bk_ref[d]
        for p in range(4):
            for a in range(2):
                h = p + 4 * a
                acc = jnp.zeros((QB, LANES), F32)
                for b in range(N_BUCKETS):
                    acc = jnp.where(bk == b, rb_ref[b, h], acc)
                o_ref[d, a * QB:(a + 1) * QB, p * LANES:(p + 1) * LANES] = acc
    bkd = bkd_ref[...]
    rowi = lax.broadcasted_iota(I32, (N_HEADS, 2 * LANES), 0)
    acc = jnp.zeros((N_HEADS, 2 * LANES), F32)
    for r in range(N_HEADS):
        h = r // 2 + 4 * (r % 2)
        for b in range(N_BUCKETS):
            acc = jnp.where((rowi == r) & (bkd == b), rb_ref[b, h], acc)
    od_ref[...] = acc


def _bias_tables(rel_bias, page):
    key = np.arange(QB)[:, None]
    qry = np.arange(LANES)[None, :]
    bk = np.stack([_t5_bucket_np(d * QB + qry - key) for d in range(3)])
    assert (_t5_bucket_np(np.arange(2 * QB + 1 - LANES, 4 * QB)) == N_BUCKETS - 1).all()
    assert (_t5_bucket_np(np.arange(page, 8 * page)) == N_BUCKETS - 1).all()
    dec = np.zeros((2 * LANES,), np.int64)
    dec[:page] = page - np.arange(page)
    dec[LANES] = 2 * REL_MAX_DIST
    dec[LANES + 1] = 0
    bkd = np.broadcast_to(_t5_bucket_np(dec)[None, :], (N_HEADS, 2 * LANES))
    return pl.pallas_call(
        _bias_kernel,
        in_specs=[pl.BlockSpec(memory_space=pltpu.SMEM), pl.BlockSpec(memory_space=pltpu.VMEM),
                  pl.BlockSpec(memory_space=pltpu.VMEM)],
        out_shape=[jax.ShapeDtypeStruct((3, 2 * QB, 4 * LANES), F32), jax.ShapeDtypeStruct((N_HEADS, 2 * LANES), F32)],
        name="bias_tables",
    )(rel_bias, jnp.asarray(bk, I32), jnp.asarray(bkd, I32))


def _search_widths(n_chunks):
    cuts = sorted({min(c, n_chunks) for c in (2, 4, 8, 12, 16)} | {n_chunks})
    return [c for c in cuts if c <= n_chunks]


def _attn_kernel(pt_ref, q_ref, qi_ref, ixq_ref, ixk_ref, kv_ref, bias_ref,
                 dq_ref, dkvn_ref, dsc_ref, dsn_ref, dthr_ref, dbias_ref, ck_ref, cv_ref,
                 o_ref, od_ref,
                 kblk, vblk_t, kiblk, keys, scores, logits, acc, thr_ref, cge_ref, cgt_ref, kbuf, vbuf, sems,
                 *, n_chunks, topk, n_pages, page, steps_per_seq):
    j = pl.program_id(1)
    step = pl.program_id(0) * n_chunks + j
    seq_s = lax.div(step, jnp.int32(steps_per_seq))
    phase = lax.rem(step, jnp.int32(steps_per_seq))

    @pl.when(phase == 0)
    def _fetch_sample_pages():
        _start_page_copies(ck_ref, pt_ref, seq_s, kbuf, sems.at[0], n_pages, page)
        _start_page_copies(cv_ref, pt_ref, seq_s, vbuf, sems.at[1], n_pages, page)

    lane = lax.broadcasted_iota(I32, (QB, LANES), 1)
    row = lax.broadcasted_iota(I32, (QB, LANES), 0)
    lo = lane < HEAD_DIM
    blocks = [(a, p) for a in range(2) for p in range(4)]
    rs = lambda a: slice(a * QB, (a + 1) * QB)
    cs = lambda p: slice(p * LANES, (p + 1) * LANES)
    chunk = lambda c: pl.ds(pl.multiple_of(c * QB, QB), QB)

    @pl.when(j == 0)
    def _build_block_diagonal_keys():
        def body(c, carry):
            kc = kv_ref[0, chunk(c), 0:LANES]
            vt = kv_ref[0, chunk(c), LANES:2 * LANES].T
            kia = jnp.where(lo, ixk_ref[0, chunk(c), :], 0.0)
            kblk[c, 0:QB, :] = jnp.where(lo, kc, 0.0).astype(BF16)
            kblk[c, QB:2 * QB, :] = jnp.where(lo, 0.0, kc).astype(BF16)
            vblk_t[c, :, 0:QB] = jnp.where(row < HEAD_DIM, vt, 0.0).astype(BF16)
            vblk_t[c, :, QB:2 * QB] = jnp.where(row < HEAD_DIM, 0.0, vt).astype(BF16)
            kiblk[c, 0:QB, :] = kia.astype(BF16)
            kiblk[c, QB:2 * QB, :] = pltpu.roll(kia, HEAD_DIM, 1).astype(BF16)
            return carry
        lax.fori_loop(0, n_chunks, body, 0)

    qi = qi_ref[0]
    q = q_ref[0]
    wt = ixq_ref[0].T
    w_row = {(a, p): wt[IDX_DIM + 2 * p + a:IDX_DIM + 2 * p + a + 1, :] for a, p in blocks}
    qpos = j * QB + lane

    n_pairs = (j + 2) // 2

    def chunk_loop(body, carry):
        n_quads = n_pairs // 2
        carry = lax.fori_loop(0, n_quads, lambda i, cr: body([4 * i + u for u in range(4)], cr), carry)
        return lax.fori_loop(2 * n_quads, n_pairs, lambda i, cr: body([2 * i, 2 * i + 1], cr), carry)

    def score_body(cs_, carry):
        for c in cs_:
            s = _dot_nt(kiblk[c], qi)
            lg = _dot_nt(kblk[c], q)
            sc = jnp.zeros((QB, LANES), F32)
            for a, p in blocks:
                sc = sc + jnp.maximum(s[rs(a), cs(p)], 0.0) * w_row[(a, p)]
            sc = sc * IDX_HEADS ** -0.5
            admissible = c * QB + row <= qpos
            scores[chunk(c), :] = jnp.where(admissible, sc, -jnp.inf)
            keys[chunk(c), :] = jnp.where(admissible, _float_key(sc), jnp.int32(INT_MIN))
            logits[c] = lg + bias_ref[jnp.clip(j - c, 0, 2)]
        return carry
    chunk_loop(score_body, 0)

    def fill_body(c, carry):
        scores[chunk(c), :] = jnp.full((QB, LANES), -jnp.inf, F32)
        keys[chunk(c), :] = jnp.full((QB, LANES), INT_MIN, I32)
        return carry
    lax.fori_loop(2 * n_pairs, n_chunks, fill_body, 0)

    kf = jnp.float32(topk)

    def count(src, width, pred):
        accs = [jnp.zeros((8, LANES), F32) for _ in range(8)]
        for g in range(width // 8):
            accs[g % 8] = accs[g % 8] + jnp.where(pred(src[g * 8:(g + 1) * 8, :]), 1.0, 0.0)
        return jnp.sum(functools.reduce(lambda x, y: x + y, accs), axis=0, keepdims=True)

    def search(width):
        def search_body(i, ans):
            cand = ans | jnp.left_shift(jnp.int32(1), 31 - i)
            cs_ = cand ^ jnp.int32(INT_MIN)
            return jnp.where(count(keys, width, lambda k: k >= cs_) >= kf, cand, ans)
        ans = lax.fori_loop(0, 32, search_body, jnp.zeros((1, LANES), I32))
        t = _key_float(ans ^ jnp.int32(INT_MIN))
        thr_ref[...] = t
        cge_ref[...] = count(scores, width, lambda s: s >= t)
        cgt_ref[...] = count(scores, width, lambda s: s > t)

    prev = 0
    for n in _search_widths(n_chunks):
        pl.when((j >= prev) & (j < n))(functools.partial(search, n * QB))
        prev = n

    s_len = scores.shape[0]
    thr, cge, cgt = _walk_to_kth(
        lambda: scores[...], thr_ref[...], cge_ref[...], cgt_ref[...], kf,
        lambda t: (count(scores, s_len, lambda s: s >= t), count(scores, s_len, lambda s: s > t)), 4 * topk)

    @pl.when(jnp.max(cge) > kf)
    def _break_ties_by_position():
        need = kf - cgt
        big = jnp.int32(2 * s_len)
        eqrow_ref = keys
        eqrow_ref[...] = jnp.where(scores[...] == thr, lax.broadcasted_iota(I32, scores.shape, 0), big)
        nbits = int(math.log2(s_len))

        def tie_body(i, best):
            cand = best | jnp.left_shift(jnp.int32(1), nbits - 1 - i)
            return jnp.where(count(eqrow_ref, s_len, lambda e: e < cand) < need, cand, best)
        last = lax.fori_loop(0, nbits, tie_body, jnp.zeros((1, LANES), I32))
        eqrow = eqrow_ref[...]
        scores[...] = jnp.where((eqrow > last) & (eqrow < big), -jnp.inf, scores[...])

    def mask_body(cs_, mx):
        mx = list(mx)
        for c in cs_:
            sel = scores[chunk(c), :] >= thr
            for n, (a, p) in enumerate(blocks):
                blk = jnp.where(sel, logits[c, rs(a), cs(p)], -jnp.inf)
                logits[c, rs(a), cs(p)] = blk
                mx[n] = jnp.maximum(mx[n], jnp.max(blk, axis=0, keepdims=True))
        return tuple(mx)
    mx = chunk_loop(mask_body, tuple(jnp.full((1, LANES), -jnp.inf, F32) for _ in blocks))

    acc[...] = jnp.zeros(acc.shape, F32)

    def pv_body(cs_, ls):
        ls = list(ls)
        pv = jnp.zeros(acc.shape, F32)
        for c in cs_:
            rows = []
            for a in range(2):
                cols = []
                for p in range(4):
                    e = jnp.exp(logits[c, rs(a), cs(p)] - mx[a * 4 + p])
                    ls[a * 4 + p] = ls[a * 4 + p] + jnp.sum(e, axis=0, keepdims=True)
                    cols.append(e.astype(BF16))
                rows.append(jnp.concatenate(cols, axis=1))
            pmat = jnp.concatenate(rows, axis=0)
            pv = pv + jnp.dot(vblk_t[c], pmat, preferred_element_type=F32)
        acc[...] = acc[...] + pv
        return tuple(ls)
    ls = chunk_loop(pv_body, tuple(jnp.zeros((1, LANES), F32) for _ in blocks))

    for p in range(4):
        inv = jnp.where(row < HEAD_DIM, 1.0 / ls[p], 1.0 / ls[4 + p])
        o_ref[0, :, cs(p)] = (acc[:, cs(p)] * inv).T.astype(BF16)

    @pl.when(phase == steps_per_seq - 1)
    def _sample_attention():
        _wait_page_copies(kbuf, sems.at[0])
        _wait_page_copies(vbuf, sems.at[1])
        od_ref[0] = _dec_attn_row(dq_ref[0], dkvn_ref[0], dsc_ref[0], dsn_ref[0], dthr_ref[0], dbias_ref[...],
                                  kbuf, vbuf, page)


def _attn(q_st, qi_st, ix, kv, bias_st, page_table, qm, kvn, sc_s, sn_s, thr_s, bias_dec, ck_t, cv_t,
          *, batch, seq, page):
    nq = seq // QB
    assert nq % 2 == 0
    topk = min(TOPK_MAX, seq // 4)
    ix3 = ix.reshape(batch, seq, LANES)
    kv3 = kv.reshape(batch, seq, 2 * LANES)
    db, n_pages = page_table.shape
    past = n_pages * page
    steps_per_seq = (batch * nq) // db
    assert steps_per_seq * db == batch * nq
    drow = lambda n: pl.BlockSpec((1, 1, n), lambda b, j, pt: ((b * nq + j) // steps_per_seq, 0, 0))
    return pl.pallas_call(
        functools.partial(_attn_kernel, n_chunks=nq, topk=topk, n_pages=n_pages, page=page,
                          steps_per_seq=steps_per_seq),
        grid_spec=pltpu.PrefetchScalarGridSpec(
            num_scalar_prefetch=1,
            grid=(batch, nq),
            in_specs=[pl.BlockSpec((1, 4 * QB, LANES), lambda b, j, pt: (b * nq + j, 0, 0)),
                      pl.BlockSpec((1, 4 * QB, LANES), lambda b, j, pt: (b * nq + j, 0, 0)),
                      pl.BlockSpec((1, QB, LANES), lambda b, j, pt: (b, j, 0)),
                      pl.BlockSpec((1, seq, LANES), lambda b, j, pt: (b, 0, 0)),
                      pl.BlockSpec((1, seq, 2 * LANES), lambda b, j, pt: (b, 0, 0)),
                      _full_spec(bias_st.shape),
                      pl.BlockSpec((1, N_HEADS, LANES), lambda b, j, pt: ((b * nq + j) // steps_per_seq, 0, 0)),
                      drow(2 * LANES), drow(past), drow(LANES), drow(LANES), _full_spec(bias_dec.shape),
                      pl.BlockSpec(memory_space=pl.ANY), pl.BlockSpec(memory_space=pl.ANY)],
            out_specs=[pl.BlockSpec((1, QB, 4 * LANES), lambda b, j, pt: (b, j, 0)), drow(4 * LANES)],
            scratch_shapes=[pltpu.VMEM((nq, 2 * QB, LANES), BF16), pltpu.VMEM((nq, LANES, 2 * QB), BF16),
                            pltpu.VMEM((nq, 2 * QB, LANES), BF16), pltpu.VMEM((seq, LANES), I32),
                            pltpu.VMEM((seq, LANES), F32),
                            pltpu.VMEM((nq, 2 * QB, 4 * LANES), F32), pltpu.VMEM((LANES, 4 * LANES), F32),
                            pltpu.VMEM((1, LANES), F32), pltpu.VMEM((1, LANES), F32), pltpu.VMEM((1, LANES), F32),
                            pltpu.VMEM((2 * HEAD_DIM, past), F32), pltpu.VMEM((2 * HEAD_DIM, past), F32),
                            pltpu.SemaphoreType.DMA((2,))]),
        out_shape=[jax.ShapeDtypeStruct((batch, seq, 4 * LANES), BF16),
                   jax.ShapeDtypeStruct((db, 1, 4 * LANES), BF16)],
        compiler_params=_cparams(2),
        name="attn",
    )(page_table, q_st, qi_st, ix3, ix3, kv3, bias_st, qm, kvn, sc_s, sn_s, thr_s, bias_dec, ck_t, cv_t)


def _rglru_gates(xc, wa, ba, wx, bx, lam):
    r = _sigmoid(_dot(xc, wa) + ba)
    i = _sigmoid(_dot(xc, wx) + bx)
    log_a = -RG_C * r * _softplus(-lam)
    a = jnp.exp(log_a)
    u = jnp.sqrt(1.0 - jnp.exp(2.0 * log_a)) * (i * xc)
    return a, u


def _rglru_prompt_kernel(g_ref, xr_ref, buf_ref, h0_ref, cw_ref, cb_ref, wa_ref, ba_ref, wx_ref, bx_ref, lam_ref,
                         o_ref, hl_ref, nb_ref, xs, a_s, u_s, tail, hc, *, tc):
    t = pl.program_id(0)
    width = cw_ref.shape[0]
    nb, _, d = g_ref.shape

    @pl.when(t == 0)
    def _load_state():
        tail[...] = jnp.zeros(tail.shape, F32)
        tail[:, 8 - (width - 1):8, :] = buf_ref[...]
        hc[...] = h0_ref[...]

    xs[:, 0:8, :] = tail[...]
    xs[:, 8:8 + tc, :] = xr_ref[...]
    tail[...] = xs[:, tc:tc + 8, :]
    xc = cb_ref[...]
    for jj in range(width):
        off = 8 - (width - 1) + jj
        xc = xc + cw_ref[jj:jj + 1, :] * xs[:, off:off + tc, :]
    a, u = _rglru_gates(xc.reshape(nb * tc, d), wa_ref[...], ba_ref[...], wx_ref[...], bx_ref[...], lam_ref[...])
    n_lb = d // LANES
    pitch = tc + 8
    for k in range(n_lb):
        for b in range(nb):
            a_s[k, b * pitch:b * pitch + tc, :] = a[b * tc:(b + 1) * tc, k * LANES:(k + 1) * LANES]
            u_s[k, b * pitch:b * pitch + tc, :] = u[b * tc:(b + 1) * tc, k * LANES:(k + 1) * LANES]

    def scan_body(i, hs):
        rows = pl.ds(i, nb, stride=pitch)
        out = []
        for k in range(n_lb):
            h = a_s[k, rows, :] * hs[k] + u_s[k, rows, :]
            u_s[k, rows, :] = h
            out.append(h)
        return tuple(out)
    h0 = hc[...]
    hs = lax.fori_loop(0, tc, scan_body, tuple(h0[:, k * LANES:(k + 1) * LANES] for k in range(n_lb)), unroll=8)
    h = jnp.concatenate(hs, axis=1)
    hc[...] = h
    hseq = jnp.concatenate([jnp.concatenate([u_s[k, b * pitch:b * pitch + tc, :] for b in range(nb)], axis=0)
                            for k in range(n_lb)], axis=1)
    o_ref[...] = (_gelu(g_ref[...].reshape(nb * tc, d)) * hseq).reshape(nb, tc, d).astype(BF16)
    hl_ref[...] = h
    nb_ref[...] = xs[:, tc + 8 - (width - 1):tc + 8, :]


def _rglru_prompt(gate, xr, buf, h0, rg, *, batch, seq, tc=256):
    d = gate.shape[-1]
    width = rg["cw"].shape[0]
    g3, x3 = gate.reshape(batch, seq, d), xr.reshape(batch, seq, d)
    blk = pl.BlockSpec((batch, tc, d), lambda t: (0, t, 0))
    vec = _full_spec((1, d))
    return pl.pallas_call(
        functools.partial(_rglru_prompt_kernel, tc=tc),
        grid=(seq // tc,),
        in_specs=[blk, blk, _full_spec((batch, width - 1, d)), _full_spec((batch, d)), _full_spec((width, d)), vec,
                  _full_spec((d, d)), vec, _full_spec((d, d)), vec, vec],
        out_specs=[blk, pl.BlockSpec((batch, d), lambda t: (0, 0)), pl.BlockSpec((batch, width - 1, d), lambda t: (0, 0, 0))],
        out_shape=[jax.ShapeDtypeStruct((batch, seq, d), BF16), jax.ShapeDtypeStruct((batch, d), F32),
                   jax.ShapeDtypeStruct((batch, width - 1, d), F32)],
        scratch_shapes=[pltpu.VMEM((batch, tc + 8, d), F32), pltpu.VMEM((d // LANES, batch * (tc + 8), LANES), F32),
                        pltpu.VMEM((d // LANES, batch * (tc + 8), LANES), F32), pltpu.VMEM((batch, 8, d), F32),
                        pltpu.VMEM((batch, d), F32)],
        compiler_params=_cparams(1),
        name="rglru_prompt",
    )(g3, x3, buf, h0, rg["cw"], rg["cb"], rg["wa"], rg["ba"], rg["wx"], rg["bx"], rg["lam"])


def _rglru_dec_kernel(g_ref, xr_ref, buf_ref, h0_ref, cw_ref, cb_ref, wa_ref, ba_ref, wx_ref, bx_ref, lam_ref,
                      o_ref, hl_ref):
    width = cw_ref.shape[0]
    xc = cb_ref[...]
    for jj in range(width - 1):
        xc = xc + cw_ref[jj:jj + 1, :] * buf_ref[jj]
    xc = xc + cw_ref[width - 1:width, :] * xr_ref[...]
    a, u = _rglru_gates(xc, wa_ref[...], ba_ref[...], wx_ref[...], bx_ref[...], lam_ref[...])
    h = a * h0_ref[...] + u
    hl_ref[...] = h
    o_ref[...] = (_gelu(g_ref[...]) * h).astype(BF16)


def _rglru_dec(gate, xr, buf_t, h0, rg):
    m, d = gate.shape
    return pl.pallas_call(
        _rglru_dec_kernel,
        out_shape=[jax.ShapeDtypeStruct((m, d), BF16), jax.ShapeDtypeStruct((m, d), F32)],
        name="rglru_dec",
    )(gate, xr, buf_t, h0, rg["cw"], rg["cb"], rg["wa"], rg["ba"], rg["wx"], rg["bx"], rg["lam"])


def _ffn_tile(y1, gf_ref, wup_ref, cw_ref, cb_ref, wdn_ref, conv_prev, n_split):
    d_ff = wdn_ref.shape[0]
    cf = d_ff // n_split
    hn = _rms(y1, gf_ref[...]).astype(BF16)
    out = jnp.zeros(y1.shape, F32)
    gates = []
    for k in range(n_split):
        c0 = k * cf
        g = jnp.dot(hn, wup_ref[:, c0:c0 + cf], preferred_element_type=F32)
        u = jnp.dot(hn, wup_ref[:, d_ff + c0:d_ff + c0 + cf], preferred_element_type=F32)
        g1, g2 = conv_prev(k, g)
        gc = cb_ref[:, c0:c0 + cf] + cw_ref[0:1, c0:c0 + cf] * g2 + cw_ref[1:2, c0:c0 + cf] * g1 \
            + cw_ref[2:3, c0:c0 + cf] * g
        act = (_gelu(gc) * u).astype(BF16)
        out = out + jnp.dot(act, wdn_ref[c0:c0 + cf, :], preferred_element_type=F32)
        gates.append(g)
    return out, gates


def _prompt_conv_prev(gs, carry, fb_ref, nb_ref, tm, cf):
    t = pl.program_id(1)

    @pl.when(t == 0)
    def _load_state():
        carry[...] = jnp.zeros(carry.shape, F32)
        for k in range(carry.shape[0]):
            carry[k, 6:8, :] = fb_ref[0, :, k * cf:(k + 1) * cf]

    def conv_prev(k, g):
        gs[0:8, :] = carry[k]
        gs[8:8 + tm, :] = g
        carry[k] = g[tm - 8:tm, :]
        nb_ref[0, :, k * cf:(k + 1) * cf] = g[tm - 2:tm, :]
        return gs[7:7 + tm, :], gs[6:6 + tm, :]
    return conv_prev


def _mix_ab_tile(y_ref, a_ref, r_ref, woa_ref, wob_ref):
    return y_ref[0] + jnp.dot(a_ref[0], woa_ref[...], preferred_element_type=F32) \
        + jnp.dot(r_ref[0], wob_ref[...], preferred_element_type=F32)


def _post_ab_prompt_kernel(y_ref, a_ref, r_ref, fb_ref, woa_ref, wob_ref, gf_ref, wup_ref, cw_ref, cb_ref, wdn_ref,
                           o_ref, nb_ref, gs, carry, *, tm, n_split):
    y1 = _mix_ab_tile(y_ref, a_ref, r_ref, woa_ref, wob_ref)
    cf = wdn_ref.shape[0] // n_split
    out, _ = _ffn_tile(y1, gf_ref, wup_ref, cw_ref, cb_ref, wdn_ref,
                       _prompt_conv_prev(gs, carry, fb_ref, nb_ref, tm, cf), n_split)
    o_ref[0] = y1 + out


_FFN_KEYS = ("g", "wup", "cw", "cb", "wdn")


def _ffn_specs(ffn):
    layer = ffn["layer"]
    return [pl.BlockSpec((None,) + ffn[k].shape[1:], lambda *_: (layer, 0, 0), pipeline_mode=pl.Buffered(1))
            for k in _FFN_KEYS]


def _ffn_args(ffn):
    return [ffn[k] for k in _FFN_KEYS]


def _post_ab_prompt(y, attn, rgo, fbuf, wo_a, wo_b, ffn, *, tm=512, n_split=2):
    batch, seq, d = y.shape
    d_ff = ffn["wdn"].shape[1]
    cf = d_ff // n_split
    blk = lambda n: pl.BlockSpec((1, tm, n), lambda b, t: (b, t, 0))
    fb = pl.BlockSpec((1, 2, d_ff), lambda b, t: (b, 0, 0))
    return pl.pallas_call(
        functools.partial(_post_ab_prompt_kernel, tm=tm, n_split=n_split),
        grid=(batch, seq // tm),
        in_specs=[blk(d), blk(attn.shape[-1]), blk(rgo.shape[-1]), fb, _full_spec(wo_a.shape), _full_spec(wo_b.shape)]
        + _ffn_specs(ffn),
        out_specs=[blk(d), fb],
        out_shape=[jax.ShapeDtypeStruct((batch, seq, d), F32), jax.ShapeDtypeStruct((batch, 2, d_ff), F32)],
        scratch_shapes=[pltpu.VMEM((tm + 8, cf), F32), pltpu.VMEM((n_split, 8, cf), F32)],
        compiler_params=_cparams(2),
        name="post_ab_prompt",
    )(y, attn, rgo, fbuf, wo_a, wo_b, *_ffn_args(ffn))


def _dec_conv_prev(fb_ref, cf):
    def conv_prev(k, g):
        return fb_ref[1, :, k * cf:(k + 1) * cf], fb_ref[0, :, k * cf:(k + 1) * cf]
    return conv_prev


def _post_ab_dec_kernel(y_ref, a_ref, r_ref, fb_ref, woa_ref, wob_ref, gf_ref, wup_ref, cw_ref, cb_ref, wdn_ref,
                        o_ref, g_ref, *, n_split):
    y1 = y_ref[...] + jnp.dot(a_ref[...], woa_ref[...], preferred_element_type=F32) \
        + jnp.dot(r_ref[...], wob_ref[...], preferred_element_type=F32)
    cf = wdn_ref.shape[0] // n_split
    out, gates = _ffn_tile(y1, gf_ref, wup_ref, cw_ref, cb_ref, wdn_ref, _dec_conv_prev(fb_ref, cf), n_split)
    o_ref[...] = y1 + out
    for k, g in enumerate(gates):
        g_ref[:, k * cf:(k + 1) * cf] = g


def _post_ab_dec(y, attn, rgo, fbuf_t, wo_a, wo_b, ffn, *, n_split=2):
    m, d = y.shape
    d_ff = ffn["wdn"].shape[1]
    args = (y, attn, rgo, fbuf_t, wo_a, wo_b)
    return pl.pallas_call(
        functools.partial(_post_ab_dec_kernel, n_split=n_split),
        grid=(1,),
        in_specs=[_full_spec(a.shape) for a in args] + _ffn_specs(ffn),
        out_specs=[_whole_spec((m, d)), _whole_spec((m, d_ff))],
        out_shape=[jax.ShapeDtypeStruct((m, d), F32), jax.ShapeDtypeStruct((m, d_ff), F32)],
        compiler_params=_cparams(1),
        name="post_ab_dec",
    )(*args, *_ffn_args(ffn))


def _gmlp_in(y, gm_ref, win_ref, bin_ref, sn_ref):
    d_c = win_ref.shape[1] // 2
    z = _gelu(_dot(_rms(y, gm_ref[...]), win_ref[...]) + bin_ref[...])
    return z[:, :d_c], _rms(z[:, d_c:], sn_ref[...])


def _layer_c_prompt_kernel(y_ref, fb_ref, gm_ref, win_ref, bin_ref, sn_ref, sw_ref, sbt_ref, woc_ref,
                           gf_ref, wup_ref, cw_ref, cb_ref, wdn_ref, gfin_ref,
                           o_ref, nb_ref, gs, carry, *, tm, n_split):
    y = y_ref[0]
    u, v = _gmlp_in(y, gm_ref, win_ref, bin_ref, sn_ref)
    vb = v.astype(BF16)
    n_groups = sw_ref.shape[0]
    tril = lax.broadcasted_iota(I32, (CHUNK, CHUNK), 0) >= lax.broadcasted_iota(I32, (CHUNK, CHUNK), 1)
    wm = [jnp.where(tril, sw_ref[gi], 0.0).astype(BF16) for gi in range(n_groups)]
    rows = []
    for r in range(tm // CHUNK):
        cols = []
        for gi in range(n_groups):
            mixed = jnp.dot(wm[gi], vb[r * CHUNK:(r + 1) * CHUNK, gi * LANES:(gi + 1) * LANES],
                            preferred_element_type=F32)
            cols.append(mixed + sbt_ref[:, gi:gi + 1])
        rows.append(jnp.concatenate(cols, axis=1))
    gated = u * jnp.concatenate(rows, axis=0)
    y1 = y + _dot(gated, woc_ref[...])
    cf = wdn_ref.shape[0] // n_split
    out, _ = _ffn_tile(y1, gf_ref, wup_ref, cw_ref, cb_ref, wdn_ref,
                       _prompt_conv_prev(gs, carry, fb_ref, nb_ref, tm, cf), n_split)
    o_ref[0] = _rms(y1 + out, gfin_ref[...])


def _layer_c_prompt(y, fbuf, cp, ffn, g_final, *, tm=512, n_split=2):
    batch, seq, d = y.shape
    d_ff = ffn["wdn"].shape[1]
    cf = d_ff // n_split
    blk = pl.BlockSpec((1, tm, d), lambda b, t: (b, t, 0))
    fb = pl.BlockSpec((1, 2, d_ff), lambda b, t: (b, 0, 0))
    consts = [cp["g"], cp["win"], cp["bin"], cp["sn"], cp["sw"], cp["sbt"], cp["woc"]]
    return pl.pallas_call(
        functools.partial(_layer_c_prompt_kernel, tm=tm, n_split=n_split),
        grid=(batch, seq // tm),
        in_specs=[blk, fb] + [_full_spec(c.shape) for c in consts] + _ffn_specs(ffn) + [_full_spec(g_final.shape)],
        out_specs=[blk, fb],
        out_shape=[jax.ShapeDtypeStruct((batch, seq, d), F32), jax.ShapeDtypeStruct((batch, 2, d_ff), F32)],
        scratch_shapes=[pltpu.VMEM((tm + 8, cf), F32), pltpu.VMEM((n_split, 8, cf), F32)],
        compiler_params=_cparams(2),
        name="layer_c_prompt",
    )(y, fbuf, *consts, *_ffn_args(ffn), g_final)


def _layer_c_dec_kernel(y_ref, fb_ref, gm_ref, win_ref, bin_ref, sn_ref, sw0_ref, sb0_ref, woc_ref,
                        gf_ref, wup_ref, cw_ref, cb_ref, wdn_ref, gfin_ref, o_ref, g_ref, v_ref, *, n_split):
    y = y_ref[...]
    u, v = _gmlp_in(y, gm_ref, win_ref, bin_ref, sn_ref)
    v_ref[...] = v
    y1 = y + _dot(u * (sw0_ref[...] * v + sb0_ref[...]), woc_ref[...])
    cf = wdn_ref.shape[0] // n_split
    out, gates = _ffn_tile(y1, gf_ref, wup_ref, cw_ref, cb_ref, wdn_ref, _dec_conv_prev(fb_ref, cf), n_split)
    o_ref[...] = _rms(y1 + out, gfin_ref[...])
    for k, g in enumerate(gates):
        g_ref[:, k * cf:(k + 1) * cf] = g


def _layer_c_dec(y, fbuf_t, cp, ffn, g_final, *, n_split=2):
    m, d = y.shape
    d_ff = ffn["wdn"].shape[1]
    d_c = cp["woc"].shape[0]
    args = (y, fbuf_t, cp["g"], cp["win"], cp["bin"], cp["sn"], cp["sw0"], cp["sb0"], cp["woc"])
    return pl.pallas_call(
        functools.partial(_layer_c_dec_kernel, n_split=n_split),
        grid=(1,),
        in_specs=[_full_spec(a.shape) for a in args] + _ffn_specs(ffn) + [_full_spec(g_final.shape)],
        out_specs=[_whole_spec((m, d)), _whole_spec((m, d_ff)), _whole_spec((m, d_c))],
        out_shape=[jax.ShapeDtypeStruct((m, d), F32), jax.ShapeDtypeStruct((m, d_ff), F32),
                   jax.ShapeDtypeStruct((m, d_c), F32)],
        compiler_params=_cparams(1),
        name="layer_c_dec",
    )(*args, *_ffn_args(ffn), g_final)


def _start_page_copies(src_ref, pt_ref, b, dst_ref, sem, n_pages, page):
    def body(pg, carry):
        col = pl.multiple_of(pg * page, page)
        pltpu.make_async_copy(src_ref.at[pt_ref[b, pg]], dst_ref.at[:, pl.ds(col, page)], sem).start()
        return carry
    lax.fori_loop(0, n_pages, body, 0, unroll=8)


def _wait_page_copies(dst_ref, sem):
    pltpu.make_async_copy(dst_ref, dst_ref, sem).wait()


def _dec_score_kernel(pt_ref, qi_ref, wi_ref, ixn_ref, cik_ref, keys_ref, knew_ref, ibuf, sems, *, n_pages, page):
    b = pl.program_id(0)
    nb = pl.num_programs(0)
    slot = lax.rem(b, 2)

    def start(bb, sl):
        _start_page_copies(cik_ref, pt_ref, bb, ibuf.at[sl], sems.at[sl], n_pages, page)

    @pl.when(b == 0)
    def _first():
        start(0, 0)

    @pl.when(b + 1 < nb)
    def _prefetch_next():
        start(b + 1, 1 - slot)

    _wait_page_copies(ibuf.at[slot], sems.at[slot])
    qi = qi_ref[0].astype(BF16)
    wi = wi_ref[0]
    s = jnp.dot(qi, ibuf[slot].astype(BF16), preferred_element_type=F32)
    sc = jnp.sum(jnp.maximum(s, 0.0) * wi, axis=0, keepdims=True) * IDX_HEADS ** -0.5
    keys_ref[0] = sc
    kin = ixn_ref[0][:, 0:IDX_DIM]
    sn = jnp.sum(qi_ref[0] * kin, axis=1, keepdims=True)
    scn = jnp.sum(jnp.maximum(sn, 0.0) * wi, axis=0, keepdims=True) * IDX_HEADS ** -0.5
    knew_ref[0] = jnp.broadcast_to(scn, (1, LANES))


def _dec_scores(page_table, qi3, wi3, ix3, cik_t, *, page):
    db, n_pages = page_table.shape
    past = n_pages * page
    return pl.pallas_call(
        functools.partial(_dec_score_kernel, n_pages=n_pages, page=page),
        grid_spec=pltpu.PrefetchScalarGridSpec(
            num_scalar_prefetch=1,
            grid=(db,),
            in_specs=[pl.BlockSpec((1, IDX_HEADS, IDX_DIM), lambda b, pt: (b, 0, 0)),
                      pl.BlockSpec((1, IDX_HEADS, 1), lambda b, pt: (b, 0, 0)),
                      pl.BlockSpec((1, 1, LANES), lambda b, pt: (b, 0, 0)),
                      pl.BlockSpec(memory_space=pl.ANY)],
            out_specs=[pl.BlockSpec((1, 1, past), lambda b, pt: (b, 0, 0)),
                       pl.BlockSpec((1, 1, LANES), lambda b, pt: (b, 0, 0))],
            scratch_shapes=[pltpu.VMEM((2, IDX_DIM, past), F32), pltpu.SemaphoreType.DMA((2,))]),
        out_shape=[jax.ShapeDtypeStruct((db, 1, past), F32), jax.ShapeDtypeStruct((db, 1, LANES), F32)],
        compiler_params=_cparams(1),
        name="dec_scores",
    )(page_table, qi3, wi3, ix3, cik_t)


def _dec_select_kernel(sc_ref, scn_ref, so_ref, sno_ref, thr_ref, *, topk):
    past = sc_ref.shape[1]
    lane0 = lax.broadcasted_iota(I32, scn_ref.shape, 1) == 0
    sc = jnp.concatenate([sc_ref[...], jnp.where(lane0, scn_ref[...], -jnp.inf)], axis=1)
    kk = _float_key(sc)
    kf = jnp.float32(topk)

    def count(pred):
        ones = jnp.where(pred, 1.0, 0.0)
        accs = [ones[:, k * LANES:(k + 1) * LANES] for k in range(8)]
        for k in range(8, ones.shape[1] // LANES):
            accs[k % 8] = accs[k % 8] + ones[:, k * LANES:(k + 1) * LANES]
        return jnp.sum(functools.reduce(lambda x, y: x + y, accs), axis=1, keepdims=True)

    def search_body(i, ans):
        cand = ans | jnp.left_shift(jnp.int32(1), 31 - i)
        return jnp.where(count(kk >= (cand ^ jnp.int32(INT_MIN))) >= kf, cand, ans)
    ans = lax.fori_loop(0, 32, search_body, jnp.zeros((sc.shape[0], 1), I32))
    thr = _key_float(ans ^ jnp.int32(INT_MIN))
    counts = lambda t: (count(sc >= t), count(sc > t))
    thr, cge, cgt = _walk_to_kth(lambda: sc, thr, *counts(thr), kf, counts, 4 * topk)
    need = kf - cgt
    big = jnp.int32(4 * past)
    eqcol = jnp.where(sc == thr, lax.broadcasted_iota(I32, sc.shape, 1), big)
    nbits = int(math.log2(past)) + 1

    def tie_body(i, best):
        cand = best | jnp.left_shift(jnp.int32(1), nbits - 1 - i)
        return jnp.where(count(eqcol < cand) < need, cand, best)
    last = lax.fori_loop(0, nbits, tie_body, jnp.zeros((sc.shape[0], 1), I32))
    sc = jnp.where((eqcol > last) & (eqcol < big), -jnp.inf, sc)
    so_ref[...] = sc[:, :past]
    sno_ref[...] = sc[:, past:]
    thr_ref[...] = jnp.broadcast_to(thr, thr_ref.shape)


def _dec_select(scores, snew, *, topk):
    db, past = scores.shape
    assert past + 1 >= topk
    return pl.pallas_call(
        functools.partial(_dec_select_kernel, topk=topk),
        out_shape=[jax.ShapeDtypeStruct((db, past), F32), jax.ShapeDtypeStruct((db, LANES), F32),
                   jax.ShapeDtypeStruct((db, LANES), F32)],
        name="dec_select",
    )(scores, snew)


def _dec_attn_row(qm, kvn, sc, sn, thr_row, bias, kbuf, vbuf, page):
    past = kbuf.shape[1]
    thr = thr_row[:, 0:1]
    sel = sc >= thr
    sel_new = sn[:, 0:1] >= thr
    far, last, bnew = bias[:, LANES:LANES + 1], bias[:, 0:page], bias[:, LANES + 1:LANES + 2]
    step = min(past, DEC_KEY_CHUNK)
    chunks = [(c, min(c + step, past)) for c in range(0, past, step)]
    qb = qm.astype(BF16)
    parts = []
    for c0, c1 in chunks:
        part = jnp.dot(qb, kbuf[:, c0:c1].astype(BF16), preferred_element_type=F32) + far
        if c1 == past:
            part = jnp.concatenate([part[:, :c1 - c0 - page], part[:, c1 - c0 - page:] + (last - far)], axis=1)
        parts.append(jnp.where(sel[:, c0:c1], part, -jnp.inf))
    lgn = jnp.sum(qm * kvn[:, 0:LANES], axis=1, keepdims=True) + bnew
    lgn = jnp.where(sel_new, lgn, -jnp.inf)
    m = lgn
    for part in parts:
        m = jnp.maximum(m, jnp.max(part, axis=1, keepdims=True))
    en = jnp.exp(lgn - m)
    den = en
    pv = en * kvn[:, LANES:2 * LANES]
    for (c0, c1), part in zip(chunks, parts):
        e = jnp.exp(part - m)
        den = den + jnp.sum(e, axis=1, keepdims=True)
        pv = pv + _dot_nt(e.astype(BF16), vbuf[:, c0:c1].astype(BF16))
    pv = pv / den
    lo = lax.broadcasted_iota(I32, (1, LANES), 1) < HEAD_DIM
    return jnp.concatenate([jnp.where(lo, pv[2 * p:2 * p + 1], pv[2 * p + 1:2 * p + 2]) for p in range(4)],
                           axis=1).astype(BF16)


def _prep_in_ab(w):
    d = w.shape[0]
    nq, nkv = N_HEADS * HEAD_DIM, N_KV_HEADS * HEAD_DIM
    offs = np.cumsum([nq, nkv, nkv, IDX_HEADS * IDX_DIM, IDX_DIM, IDX_HEADS, 512])
    q, k, v, qi, ki, wi, g, xr = jnp.split(w, offs.tolist(), axis=1)
    q = q.reshape(d, N_HEADS, HEAD_DIM)[:, np.array(HEAD_PERM), :].reshape(d, nq)
    pad = jnp.zeros((d, _C_G - _C_IX - IDX_DIM - IDX_HEADS), w.dtype)
    return jnp.concatenate([q, k, v, qi, ki, wi, pad, g, xr], axis=1).astype(BF16)


def _block_diag(w):
    n, c, _ = w.shape
    return (jnp.eye(n, dtype=w.dtype)[:, None, :, None] * w[:, :, None, :]).reshape(n * c, n * c).astype(BF16)


def _ffn_params(layer, stacked):
    return dict(stacked, layer=layer)


def kernel(x_prompt, x_sample, cache_k, cache_v, cache_idx_k, state_rglru_h, state_rglru_conv, state_ffn_conv,
           page_table, norm_mix, norm_ffn, norm_final, rel_bias, w_in_ab, w_out_ab, rg_conv_w, rg_conv_b,
           rg_wa, rg_ba, rg_wx, rg_bx, rg_lambda, w_in_c, b_in_c, sgu_norm, sgu_w, sgu_b, w_out_c,
           ffn_w_up, ffn_conv_w, ffn_conv_b, ffn_w_down):
    batch, seq, d = x_prompt.shape
    db = x_sample.shape[0]
    page = cache_k.shape[2]
    d_a = N_HEADS * HEAD_DIM
    d_b = rg_conv_w.shape[-1]
    d_ff = ffn_w_down.shape[1]
    assert x_sample.shape[1] == 1 and seq % 512 == 0 and page == LANES and w_in_ab.shape[0] == 1

    w_in0 = _prep_in_ab(w_in_ab[0])
    wo = w_out_ab[0]
    wo_a = wo[:d_a].reshape(N_HEADS, HEAD_DIM, d)[np.array(HEAD_PERM)].reshape(d_a, d).astype(BF16)
    wo_b = wo[d_a:].astype(BF16)
    rg = {"cw": rg_conv_w[0], "cb": rg_conv_b[0][None], "wa": _block_diag(rg_wa[0]), "ba": rg_ba[0][None],
          "wx": _block_diag(rg_wx[0]), "bx": rg_bx[0][None], "lam": rg_lambda[0][None]}
    ffn_all = {"g": norm_ffn[:, None, :], "wup": ffn_w_up.astype(BF16), "cw": ffn_conv_w, "cb": ffn_conv_b[:, None, :],
               "wdn": ffn_w_down.astype(BF16)}
    ffn0, ffn1 = _ffn_params(0, ffn_all), _ffn_params(1, ffn_all)
    cp = {"g": norm_mix[1][None], "win": w_in_c[0].astype(BF16), "bin": b_in_c[0][None], "sn": sgu_norm[0][None],
          "sw": sgu_w[0], "sbt": sgu_b[0].T, "woc": w_out_c[0].astype(BF16),
          "sw0": jnp.repeat(sgu_w[0][:, 0, 0], d // sgu_w.shape[1])[None],
          "sb0": jnp.repeat(sgu_b[0][:, 0], d // sgu_w.shape[1])[None]}
    g_mix0 = norm_mix[0][None]
    g_final = norm_final[None]
    bias_st, bias_dec = _bias_tables(rel_bias, page)

    xp = x_prompt.reshape(batch * seq, d)
    xs = x_sample.reshape(db, d)
    q_st, qi_st, kv_p, ix_p, gate_p, xr_p = _inproj(xp, g_mix0, w_in0, stack=True, tm=512)
    qm_s, qi_s, kv_s, ix_s, gate_s, xr_s = _inproj(xs, g_mix0, w_in0, stack=False, tm=db)
    cik_t = jnp.transpose(cache_idx_k[0], (0, 2, 1))
    ck_t = jnp.transpose(cache_k[0], (0, 2, 3, 1)).reshape(-1, 2 * HEAD_DIM, page)
    cv_t = jnp.transpose(cache_v[0], (0, 2, 3, 1)).reshape(-1, 2 * HEAD_DIM, page)
    topk_s = min(TOPK_MAX, (page_table.shape[1] * page + 1) // 4)
    sc_s, sn_s = _dec_scores(page_table, qi_s.reshape(db, IDX_HEADS, IDX_DIM),
                             ix_s[:, IDX_DIM:IDX_DIM + IDX_HEADS].reshape(db, IDX_HEADS, 1),
                             ix_s.reshape(db, 1, LANES), cik_t, page=page)
    sc_s, sn_s, thr_s = _dec_select(sc_s.reshape(db, -1), sn_s.reshape(db, LANES), topk=topk_s)

    attn_p, attn_s = _attn(q_st, qi_st, ix_p, kv_p, bias_st, page_table, jnp.transpose(qm_s, (1, 0, 2)),
                           kv_s.reshape(db, 1, 2 * LANES), sc_s.reshape(db, 1, -1), sn_s.reshape(db, 1, LANES),
                           thr_s.reshape(db, 1, LANES), bias_dec, ck_t, cv_t, batch=batch, seq=seq, page=page)
    attn_s = attn_s.reshape(db, d_a)

    rg_p, h_p, cbuf_p = _rglru_prompt(gate_p, xr_p, jnp.zeros((batch, rg["cw"].shape[0] - 1, d_b), F32),
                                      jnp.zeros((batch, d_b), F32), rg, batch=batch, seq=seq)
    zero_fb = jnp.zeros((batch, 2, d_ff), F32)
    y1_p, fb0_p = _post_ab_prompt(x_prompt, attn_p, rg_p, zero_fb, wo_a, wo_b, ffn0)
    y_p, fb1_p = _layer_c_prompt(y1_p, zero_fb, cp, ffn1, g_final)

    cbuf_s_in = state_rglru_conv[0]
    rg_s, h_s = _rglru_dec(gate_s, xr_s, jnp.transpose(cbuf_s_in, (1, 0, 2)), state_rglru_h[0], rg)
    y1_s, g0_s = _post_ab_dec(xs, attn_s, rg_s, jnp.transpose(state_ffn_conv[0], (1, 0, 2)), wo_a, wo_b, ffn0)
    y_s, g1_s, v_s = _layer_c_dec(y1_s, jnp.transpose(state_ffn_conv[1], (1, 0, 2)), cp, ffn1, g_final)

    kv4 = kv_p.reshape(batch, seq, 2, N_KV_HEADS, HEAD_DIM)
    kvs = kv_s.reshape(db, 1, 2, N_KV_HEADS, HEAD_DIM)
    fbuf_s = lambda layer, g: jnp.concatenate([state_ffn_conv[layer][:, 1:], g[:, None]], axis=1)
    return (y_p, y_s.reshape(db, 1, d),
            kv4[None, :, :, 0], kv4[None, :, :, 1], ix_p.reshape(batch, seq, LANES)[None, :, :, :IDX_DIM],
            kvs[None, :, :, 0], kvs[None, :, :, 1], ix_s.reshape(db, 1, LANES)[None, :, :, :IDX_DIM],
            h_p.reshape(batch, d_b)[None], cbuf_p[None],
            h_s[None], jnp.concatenate([cbuf_s_in[:, 1:], xr_s[:, None]], axis=1)[None],
            v_s.reshape(db, 1, -1)[None],
            jnp.stack([fb0_p, fb1_p]), jnp.stack([fbuf_s(0, g0_s), fbuf_s(1, g1_s)]))
```

```python
import functools
import math

import numpy as np
import jax
import jax.numpy as jnp
from jax import lax
from jax.experimental import pallas as pl
from jax.experimental.pallas import tpu as pltpu

F32 = jnp.float32
BF16 = jnp.bfloat16
I32 = jnp.int32

N_HEADS = 8
HEAD_DIM = 64
N_KV_HEADS = 2
Q_PER_KV = N_HEADS // N_KV_HEADS
IDX_HEADS = 8
IDX_DIM = 64
TOPK_MAX = 256
N_BUCKETS = 32
REL_MAX_EXACT = N_BUCKETS // 2
REL_MAX_DIST = 128
RG_C = 8.0
CHUNK = 128
EPS = 1e-6

LANES = 128
QB = 128
DEC_KEY_CHUNK = 2048
INT_MIN = -(2 ** 31)
KEY_MIN_FINITE = INT_MIN + 0x800000
NEG_MAX = float(np.finfo(np.float32).min)
HEAD_PERM = (0, 4, 1, 5, 2, 6, 3, 7)
VMEM_LIMIT = 56 * 1024 * 1024


def _cparams(n_grid):
    return pltpu.CompilerParams(dimension_semantics=("arbitrary",) * n_grid, vmem_limit_bytes=VMEM_LIMIT)


def _full_spec(shape):
    nd = len(shape)
    return pl.BlockSpec(shape, lambda *_: (0,) * nd, pipeline_mode=pl.Buffered(1))


def _whole_spec(shape):
    nd = len(shape)
    return pl.BlockSpec(shape, lambda *_: (0,) * nd)


def _rms(x, g):
    return x * lax.rsqrt(jnp.mean(x * x, axis=-1, keepdims=True) + EPS) * g


def _gelu(x):
    return x * (0.5 * (1.0 + jnp.tanh(math.sqrt(2.0 / math.pi) * (x + 0.044715 * (x * x * x)))))


def _sigmoid(x):
    return 1.0 / (1.0 + jnp.exp(-x))


def _softplus(x):
    return jnp.maximum(x, 0.0) + jnp.log(1.0 + jnp.exp(-jnp.abs(x)))


def _dot(a, b):
    return jnp.dot(a.astype(BF16), b, preferred_element_type=F32)


def _dot_nt(a, b):
    return lax.dot_general(a, b, (((1,), (1,)), ((), ())), preferred_element_type=F32)


def _float_key(x):
    bits = pltpu.bitcast(x, I32)
    key = jnp.where(bits < 0, bits ^ jnp.int32(0x7FFFFFFF), bits)
    return jnp.where(bits == jnp.int32(INT_MIN), jnp.int32(0), key)


def _key_float(key):
    key = jnp.maximum(key, jnp.int32(KEY_MIN_FINITE))
    return pltpu.bitcast(jnp.where(key < 0, key ^ jnp.int32(0x7FFFFFFF), key), F32)


def _walk_to_kth(scores, t, cge, cgt, kf, count_ge_gt, max_steps):
    axis = 0 if t.shape[0] == 1 else 1

    def settled(t, cge, cgt):
        return (cgt < kf) & ((cge >= kf) | (t <= NEG_MAX))

    def unsettled(t, cge, cgt):
        return jnp.max(jnp.where(settled(t, cge, cgt), 0.0, 1.0)) > 0.0

    def body(st):
        t, cge, cgt, it = st
        sc = scores()
        below = jnp.max(jnp.where(sc < t, sc, -jnp.inf), axis=axis, keepdims=True)
        above = jnp.min(jnp.where(sc > t, sc, jnp.inf), axis=axis, keepdims=True)
        t = jnp.where(cgt >= kf, above, jnp.where((cge < kf) & (t > NEG_MAX), jnp.maximum(below, NEG_MAX), t))
        cge, cgt = count_ge_gt(t)
        return t, cge, cgt, it + 1

    t, cge, cgt, _ = lax.while_loop(lambda st: unsettled(st[0], st[1], st[2]) & (st[3] < max_steps), body,
                                    (t, cge, cgt, jnp.int32(0)))
    return t, cge, cgt


def _t5_bucket_np(n):
    n = np.maximum(n, 0)
    nf = np.maximum(n, 1).astype(np.float32)
    large = REL_MAX_EXACT + (np.log(nf / np.float32(REL_MAX_EXACT)) / np.float32(math.log(REL_MAX_DIST / REL_MAX_EXACT))
                             * np.float32(N_BUCKETS - REL_MAX_EXACT)).astype(np.int32)
    large = np.minimum(large, N_BUCKETS - 1)
    return np.where(n < REL_MAX_EXACT, n, large).astype(np.int32)


_C_Q, _C_KV, _C_QI, _C_IX, _C_G, _C_X, _C_END = 0, 512, 768, 1280, 1408, 1920, 2432


def _inproj_kernel(x_ref, g_ref, w_ref, q_ref, qi_ref, kv_ref, ix_ref, gate_ref, xr_ref, *, stack):
    hn = _rms(x_ref[...], g_ref[...])
    z = _dot(hn, w_ref[...])
    q = z[:, _C_Q:_C_KV] * HEAD_DIM ** -0.5
    qi = z[:, _C_QI:_C_IX] * IDX_DIM ** -0.5
    kv_ref[...] = z[:, _C_KV:_C_QI]
    ix_ref[...] = z[:, _C_IX:_C_G]
    gate_ref[...] = z[:, _C_G:_C_X]
    xr_ref[...] = z[:, _C_X:_C_END]
    if stack:
        qb, qib = q.astype(BF16), qi.astype(BF16)
        for r in range(q.shape[0] // QB):
            for p in range(4):
                q_ref[r, p * QB:(p + 1) * QB, :] = qb[r * QB:(r + 1) * QB, p * LANES:(p + 1) * LANES]
                qi_ref[r, p * QB:(p + 1) * QB, :] = qib[r * QB:(r + 1) * QB, p * LANES:(p + 1) * LANES]
    else:
        lo = lax.broadcasted_iota(I32, (q.shape[0], LANES), 1) < HEAD_DIM
        for p in range(4):
            qp = q[:, p * LANES:(p + 1) * LANES]
            q_ref[2 * p] = jnp.where(lo, qp, 0.0)
            q_ref[2 * p + 1] = jnp.where(lo, 0.0, qp)
        qi_ref[...] = qi


def _inproj(x2d, g, w, *, stack, tm):
    m, d = x2d.shape
    if stack:
        q_shape, q_spec = (m // QB, 4 * QB, LANES), pl.BlockSpec((tm // QB, 4 * QB, LANES), lambda i: (i, 0, 0))
        qi_shape, qi_spec, qdt = q_shape, q_spec, BF16
    else:
        q_shape, q_spec = (N_HEADS, m, LANES), pl.BlockSpec((N_HEADS, tm, LANES), lambda i: (0, i, 0))
        qi_shape, qi_spec, qdt = (m, 512), pl.BlockSpec((tm, 512), lambda i: (i, 0)), F32
    row = lambda n: pl.BlockSpec((tm, n), lambda i: (i, 0))
    return pl.pallas_call(
        functools.partial(_inproj_kernel, stack=stack),
        grid=(m // tm,),
        in_specs=[row(d), _full_spec((1, d)), _full_spec(w.shape)],
        out_specs=[q_spec, qi_spec, row(256), row(128), row(512), row(512)],
        out_shape=[jax.ShapeDtypeStruct(q_shape, qdt), jax.ShapeDtypeStruct(qi_shape, qdt),
                   jax.ShapeDtypeStruct((m, 256), F32), jax.ShapeDtypeStruct((m, 128), F32),
                   jax.ShapeDtypeStruct((m, 512), F32), jax.ShapeDtypeStruct((m, 512), F32)],
        compiler_params=_cparams(1),
        name="inproj_stack" if stack else "inproj_dec",
    )(x2d, g, w)


def _bias_kernel(rb_ref, bk_ref, bkd_ref, o_ref, od_ref):
    for d in range(3):
        bk = bk_ref[d]
        for p in range(4):
            for a in range(2):
                h = p + 4 * a
                acc = jnp.zeros((QB, LANES), F32)
                for b in range(N_BUCKETS):
                    acc = jnp.where(bk == b, rb_ref[b, h], acc)
                o_ref[d, a * QB:(a + 1) * QB, p * LANES:(p + 1) * LANES] = acc
    bkd = bkd_ref[...]
    rowi = lax.broadcasted_iota(I32, (N_HEADS, 2 * LANES), 0)
    acc = jnp.zeros((N_HEADS, 2 * LANES), F32)
    for r in range(N_HEADS):
        h = r // 2 + 4 * (r % 2)
        for b in range(N_BUCKETS):
            acc = jnp.where((rowi == r) & (bkd == b), rb_ref[b, h], acc)
    od_ref[...] = acc


def _bias_tables(rel_bias, page):
    key = np.arange(QB)[:, None]
    qry = np.arange(LANES)[None, :]
    bk = np.stack([_t5_bucket_np(d * QB + qry - key) for d in range(3)])
    assert (_t5_bucket_np(np.arange(2 * QB + 1 - LANES, 4 * QB)) == N_BUCKETS - 1).all()
    assert (_t5_bucket_np(np.arange(page, 8 * page)) == N_BUCKETS - 1).all()
    dec = np.zeros((2 * LANES,), np.int64)
    dec[:page] = page - np.arange(page)
    dec[LANES] = 2 * REL_MAX_DIST
    dec[LANES + 1] = 0
    bkd = np.broadcast_to(_t5_bucket_np(dec)[None, :], (N_HEADS, 2 * LANES))
    return pl.pallas_call(
        _bias_kernel,
        in_specs=[pl.BlockSpec(memory_space=pltpu.SMEM), pl.BlockSpec(memory_space=pltpu.VMEM),
                  pl.BlockSpec(memory_space=pltpu.VMEM)],
        out_shape=[jax.ShapeDtypeStruct((3, 2 * QB, 4 * LANES), F32), jax.ShapeDtypeStruct((N_HEADS, 2 * LANES), F32)],
        name="bias_tables",
    )(rel_bias, jnp.asarray(bk, I32), jnp.asarray(bkd, I32))


def _search_widths(n_chunks):
    cuts = sorted({min(c, n_chunks) for c in (2, 4, 8, 12, 16)} | {n_chunks})
    return [c for c in cuts if c <= n_chunks]


def _attn_kernel(pt_ref, q_ref, qi_ref, ixq_ref, ixk_ref, kv_ref, bias_ref,
                 dq_ref, dkvn_ref, dsc_ref, dsn_ref, dthr_ref, dbias_ref, ck_ref, cv_ref,
                 o_ref, od_ref,
                 kblk, vblk_t, kiblk, keys, scores, logits, acc, thr_ref, cge_ref, cgt_ref, kbuf, vbuf, sems,
                 *, n_chunks, topk, n_pages, page, steps_per_seq):
    j = pl.program_id(1)
    step = pl.program_id(0) * n_chunks + j
    seq_s = lax.div(step, jnp.int32(steps_per_seq))
    phase = lax.rem(step, jnp.int32(steps_per_seq))

    pages_per_trip = -(-n_pages // 32)

    def fetch_sample_pages(trip):
        @pl.when(phase == 0)
        def _():
            for u in range(pages_per_trip):
                pg = trip * pages_per_trip + u

                def start(pg=pg):
                    _start_page_copy(ck_ref, pt_ref, seq_s, kbuf, sems.at[0], pg, page)
                    _start_page_copy(cv_ref, pt_ref, seq_s, vbuf, sems.at[1], pg, page)
                if 32 * pages_per_trip == n_pages:
                    start()
                else:
                    pl.when(pg < n_pages)(start)

    lane = lax.broadcasted_iota(I32, (QB, LANES), 1)
    row = lax.broadcasted_iota(I32, (QB, LANES), 0)
    lo = lane < HEAD_DIM
    blocks = [(a, p) for a in range(2) for p in range(4)]
    rs = lambda a: slice(a * QB, (a + 1) * QB)
    cs = lambda p: slice(p * LANES, (p + 1) * LANES)
    chunk = lambda c: pl.ds(pl.multiple_of(c * QB, QB), QB)

    @pl.when(j == 0)
    def _build_block_diagonal_keys():
        def body(c, carry):
            kc = kv_ref[0, chunk(c), 0:LANES]
            vt = kv_ref[0, chunk(c), LANES:2 * LANES].T
            kia = jnp.where(lo, ixk_ref[0, chunk(c), :], 0.0)
            kblk[c, 0:QB, :] = jnp.where(lo, kc, 0.0).astype(BF16)
            kblk[c, QB:2 * QB, :] = jnp.where(lo, 0.0, kc).astype(BF16)
            vblk_t[c, :, 0:QB] = jnp.where(row < HEAD_DIM, vt, 0.0).astype(BF16)
            vblk_t[c, :, QB:2 * QB] = jnp.where(row < HEAD_DIM, 0.0, vt).astype(BF16)
            kiblk[c, 0:QB, :] = kia.astype(BF16)
            kiblk[c, QB:2 * QB, :] = pltpu.roll(kia, HEAD_DIM, 1).astype(BF16)
            return carry
        lax.fori_loop(0, n_chunks, body, 0)

    qi = qi_ref[0]
    q = q_ref[0]
    wt = ixq_ref[0].T
    w_row = {(a, p): wt[IDX_DIM + 2 * p + a:IDX_DIM + 2 * p + a + 1, :] for a, p in blocks}
    qpos = j * QB + lane

    n_pairs = (j + 2) // 2

    def chunk_loop(body, carry):
        n_quads = n_pairs // 2
        carry = lax.fori_loop(0, n_quads, lambda i, cr: body([4 * i + u for u in range(4)], cr), carry)
        return lax.fori_loop(2 * n_quads, n_pairs, lambda i, cr: body([2 * i, 2 * i + 1], cr), carry)

    def score_body(cs_, carry):
        for c in cs_:
            s = _dot_nt(kiblk[c], qi)
            lg = _dot_nt(kblk[c], q)
            sc = jnp.zeros((QB, LANES), F32)
            for a, p in blocks:
                sc = sc + jnp.maximum(s[rs(a), cs(p)], 0.0) * w_row[(a, p)]
            sc = sc * IDX_HEADS ** -0.5
            admissible = c * QB + row <= qpos
            scores[chunk(c), :] = jnp.where(admissible, sc, -jnp.inf)
            keys[chunk(c), :] = jnp.where(admissible, _float_key(sc), jnp.int32(INT_MIN))
            logits[c] = lg + bias_ref[jnp.clip(j - c, 0, 2)]
        return carry
    chunk_loop(score_body, 0)

    def fill_body(c, carry):
        scores[chunk(c), :] = jnp.full((QB, LANES), -jnp.inf, F32)
        keys[chunk(c), :] = jnp.full((QB, LANES), INT_MIN, I32)
        return carry
    lax.fori_loop(2 * n_pairs, n_chunks, fill_body, 0)

    kf = jnp.float32(topk)

    def count(src, width, pred):
        accs = [jnp.zeros((8, LANES), F32) for _ in range(8)]
        for g in range(width // 8):
            accs[g % 8] = accs[g % 8] + jnp.where(pred(src[g * 8:(g + 1) * 8, :]), 1.0, 0.0)
        return jnp.sum(functools.reduce(lambda x, y: x + y, accs), axis=0, keepdims=True)

    def search(width):
        def search_body(i, ans):
            fetch_sample_pages(i)
            cand = ans | jnp.left_shift(jnp.int32(1), 31 - i)
            cs_ = cand ^ jnp.int32(INT_MIN)
            return jnp.where(count(keys, width, lambda k: k >= cs_) >= kf, cand, ans)
        ans = lax.fori_loop(0, 32, search_body, jnp.zeros((1, LANES), I32))
        t = _key_float(ans ^ jnp.int32(INT_MIN))
        thr_ref[...] = t
        cge_ref[...] = count(scores, width, lambda s: s >= t)
        cgt_ref[...] = count(scores, width, lambda s: s > t)

    prev = 0
    for n in _search_widths(n_chunks):
        pl.when((j >= prev) & (j < n))(functools.partial(search, n * QB))
        prev = n

    s_len = scores.shape[0]
    thr, cge, cgt = _walk_to_kth(
        lambda: scores[...], thr_ref[...], cge_ref[...], cgt_ref[...], kf,
        lambda t: (count(scores, s_len, lambda s: s >= t), count(scores, s_len, lambda s: s > t)), 4 * topk)

    @pl.when(jnp.max(cge) > kf)
    def _break_ties_by_position():
        need = kf - cgt
        big = jnp.int32(2 * s_len)
        eqrow_ref = keys
        eqrow_ref[...] = jnp.where(scores[...] == thr, lax.broadcasted_iota(I32, scores.shape, 0), big)
        nbits = int(math.log2(s_len))

        def tie_body(i, best):
            cand = best | jnp.left_shift(jnp.int32(1), nbits - 1 - i)
            return jnp.where(count(eqrow_ref, s_len, lambda e: e < cand) < need, cand, best)
        last = lax.fori_loop(0, nbits, tie_body, jnp.zeros((1, LANES), I32))
        eqrow = eqrow_ref[...]
        scores[...] = jnp.where((eqrow > last) & (eqrow < big), -jnp.inf, scores[...])

    def mask_body(cs_, mx):
        mx = list(mx)
        for c in cs_:
            sel = scores[chunk(c), :] >= thr
            for n, (a, p) in enumerate(blocks):
                blk = jnp.where(sel, logits[c, rs(a), cs(p)], -jnp.inf)
                logits[c, rs(a), cs(p)] = blk
                mx[n] = jnp.maximum(mx[n], jnp.max(blk, axis=0, keepdims=True))
        return tuple(mx)
    mx = chunk_loop(mask_body, tuple(jnp.full((1, LANES), -jnp.inf, F32) for _ in blocks))

    acc[...] = jnp.zeros(acc.shape, F32)

    def pv_body(cs_, ls):
        ls = list(ls)
        pv = jnp.zeros(acc.shape, F32)
        for c in cs_:
            rows = []
            for a in range(2):
                cols = []
                for p in range(4):
                    e = jnp.exp(logits[c, rs(a), cs(p)] - mx[a * 4 + p])
                    ls[a * 4 + p] = ls[a * 4 + p] + jnp.sum(e, axis=0, keepdims=True)
                    cols.append(e.astype(BF16))
                rows.append(jnp.concatenate(cols, axis=1))
            pmat = jnp.concatenate(rows, axis=0)
            pv = pv + jnp.dot(vblk_t[c], pmat, preferred_element_type=F32)
        acc[...] = acc[...] + pv
        return tuple(ls)
    ls = chunk_loop(pv_body, tuple(jnp.zeros((1, LANES), F32) for _ in blocks))

    for p in range(4):
        inv = jnp.where(row < HEAD_DIM, 1.0 / ls[p], 1.0 / ls[4 + p])
        o_ref[0, :, cs(p)] = (acc[:, cs(p)] * inv).T.astype(BF16)

    @pl.when(phase == steps_per_seq - 1)
    def _sample_attention():
        _wait_page_copies(kbuf, sems.at[0])
        _wait_page_copies(vbuf, sems.at[1])
        od_ref[0] = _dec_attn_row(dq_ref[0], dkvn_ref[0], dsc_ref[0], dsn_ref[0], dthr_ref[0], dbias_ref[...],
                                  kbuf, vbuf, page)


def _attn(q_st, qi_st, ix, kv, bias_st, page_table, qm, kvn, sc_s, sn_s, thr_s, bias_dec, ck_t, cv_t,
          *, batch, seq, page):
    nq = seq // QB
    assert nq % 2 == 0
    topk = min(TOPK_MAX, seq // 4)
    ix3 = ix.reshape(batch, seq, LANES)
    kv3 = kv.reshape(batch, seq, 2 * LANES)
    db, n_pages = page_table.shape
    past = n_pages * page
    steps_per_seq = (batch * nq) // db
    assert steps_per_seq * db == batch * nq
    drow = lambda n: pl.BlockSpec((1, 1, n), lambda b, j, pt: ((b * nq + j) // steps_per_seq, 0, 0))
    return pl.pallas_call(
        functools.partial(_attn_kernel, n_chunks=nq, topk=topk, n_pages=n_pages, page=page,
                          steps_per_seq=steps_per_seq),
        grid_spec=pltpu.PrefetchScalarGridSpec(
            num_scalar_prefetch=1,
            grid=(batch, nq),
            in_specs=[pl.BlockSpec((1, 4 * QB, LANES), lambda b, j, pt: (b * nq + j, 0, 0)),
                      pl.BlockSpec((1, 4 * QB, LANES), lambda b, j, pt: (b * nq + j, 0, 0)),
                      pl.BlockSpec((1, QB, LANES), lambda b, j, pt: (b, j, 0)),
                      pl.BlockSpec((1, seq, LANES), lambda b, j, pt: (b, 0, 0)),
                      pl.BlockSpec((1, seq, 2 * LANES), lambda b, j, pt: (b, 0, 0)),
                      _full_spec(bias_st.shape),
                      pl.BlockSpec((1, N_HEADS, LANES), lambda b, j, pt: ((b * nq + j) // steps_per_seq, 0, 0)),
                      drow(2 * LANES), drow(past), drow(LANES), drow(LANES), _full_spec(bias_dec.shape),
                      pl.BlockSpec(memory_space=pl.ANY), pl.BlockSpec(memory_space=pl.ANY)],
            out_specs=[pl.BlockSpec((1, QB, 4 * LANES), lambda b, j, pt: (b, j, 0)), drow(4 * LANES)],
            scratch_shapes=[pltpu.VMEM((nq, 2 * QB, LANES), BF16), pltpu.VMEM((nq, LANES, 2 * QB), BF16),
                            pltpu.VMEM((nq, 2 * QB, LANES), BF16), pltpu.VMEM((seq, LANES), I32),
                            pltpu.VMEM((seq, LANES), F32),
                            pltpu.VMEM((nq, 2 * QB, 4 * LANES), F32), pltpu.VMEM((LANES, 4 * LANES), F32),
                            pltpu.VMEM((1, LANES), F32), pltpu.VMEM((1, LANES), F32), pltpu.VMEM((1, LANES), F32),
                            pltpu.VMEM((2 * HEAD_DIM, past), F32), pltpu.VMEM((2 * HEAD_DIM, past), F32),
                            pltpu.SemaphoreType.DMA((2,))]),
        out_shape=[jax.ShapeDtypeStruct((batch, seq, 4 * LANES), BF16),
                   jax.ShapeDtypeStruct((db, 1, 4 * LANES), BF16)],
        compiler_params=_cparams(2),
        name="attn",
    )(page_table, q_st, qi_st, ix3, ix3, kv3, bias_st, qm, kvn, sc_s, sn_s, thr_s, bias_dec, ck_t, cv_t)


def _rglru_gates(xc, wa, ba, wx, bx, lam):
    r = _sigmoid(_dot(xc, wa) + ba)
    i = _sigmoid(_dot(xc, wx) + bx)
    log_a = -RG_C * r * _softplus(-lam)
    a = jnp.exp(log_a)
    u = jnp.sqrt(1.0 - jnp.exp(2.0 * log_a)) * (i * xc)
    return a, u


def _rglru_prompt_kernel(g_ref, xr_ref, buf_ref, h0_ref, cw_ref, cb_ref, wa_ref, ba_ref, wx_ref, bx_ref, lam_ref,
                         o_ref, hl_ref, nb_ref, xs, a_s, u_s, tail, hc, *, tc):
    t = pl.program_id(0)
    width = cw_ref.shape[0]
    nb, _, d = g_ref.shape

    @pl.when(t == 0)
    def _load_state():
        tail[...] = jnp.zeros(tail.shape, F32)
        tail[:, 8 - (width - 1):8, :] = buf_ref[...]
        hc[...] = h0_ref[...]

    xs[:, 0:8, :] = tail[...]
    xs[:, 8:8 + tc, :] = xr_ref[...]
    tail[...] = xs[:, tc:tc + 8, :]
    xc = cb_ref[...]
    for jj in range(width):
        off = 8 - (width - 1) + jj
        xc = xc + cw_ref[jj:jj + 1, :] * xs[:, off:off + tc, :]
    a, u = _rglru_gates(xc.reshape(nb * tc, d), wa_ref[...], ba_ref[...], wx_ref[...], bx_ref[...], lam_ref[...])
    n_lb = d // LANES
    pitch = tc + 8
    for k in range(n_lb):
        for b in range(nb):
            a_s[k, b * pitch:b * pitch + tc, :] = a[b * tc:(b + 1) * tc, k * LANES:(k + 1) * LANES]
            u_s[k, b * pitch:b * pitch + tc, :] = u[b * tc:(b + 1) * tc, k * LANES:(k + 1) * LANES]

    def scan_body(i, hs):
        rows = pl.ds(i, nb, stride=pitch)
        out = []
        for k in range(n_lb):
            h = a_s[k, rows, :] * hs[k] + u_s[k, rows, :]
            u_s[k, rows, :] = h
            out.append(h)
        return tuple(out)
    h0 = hc[...]
    hs = lax.fori_loop(0, tc, scan_body, tuple(h0[:, k * LANES:(k + 1) * LANES] for k in range(n_lb)), unroll=8)
    h = jnp.concatenate(hs, axis=1)
    hc[...] = h
    hseq = jnp.concatenate([jnp.concatenate([u_s[k, b * pitch:b * pitch + tc, :] for b in range(nb)], axis=0)
                            for k in range(n_lb)], axis=1)
    o_ref[...] = (_gelu(g_ref[...].reshape(nb * tc, d)) * hseq).reshape(nb, tc, d).astype(BF16)
    hl_ref[...] = h
    nb_ref[...] = xs[:, tc + 8 - (width - 1):tc + 8, :]


def _rglru_prompt(gate, xr, buf, h0, rg, *, batch, seq, tc=256):
    d = gate.shape[-1]
    width = rg["cw"].shape[0]
    g3, x3 = gate.reshape(batch, seq, d), xr.reshape(batch, seq, d)
    blk = pl.BlockSpec((batch, tc, d), lambda t: (0, t, 0))
    vec = _full_spec((1, d))
    return pl.pallas_call(
        functools.partial(_rglru_prompt_kernel, tc=tc),
        grid=(seq // tc,),
        in_specs=[blk, blk, _full_spec((batch, width - 1, d)), _full_spec((batch, d)), _full_spec((width, d)), vec,
                  _full_spec((d, d)), vec, _full_spec((d, d)), vec, vec],
        out_specs=[blk, pl.BlockSpec((batch, d), lambda t: (0, 0)), pl.BlockSpec((batch, width - 1, d), lambda t: (0, 0, 0))],
        out_shape=[jax.ShapeDtypeStruct((batch, seq, d), BF16), jax.ShapeDtypeStruct((batch, d), F32),
                   jax.ShapeDtypeStruct((batch, width - 1, d), F32)],
        scratch_shapes=[pltpu.VMEM((batch, tc + 8, d), F32), pltpu.VMEM((d // LANES, batch * (tc + 8), LANES), F32),
                        pltpu.VMEM((d // LANES, batch * (tc + 8), LANES), F32), pltpu.VMEM((batch, 8, d), F32),
                        pltpu.VMEM((batch, d), F32)],
        compiler_params=_cparams(1),
        name="rglru_prompt",
    )(g3, x3, buf, h0, rg["cw"], rg["cb"], rg["wa"], rg["ba"], rg["wx"], rg["bx"], rg["lam"])


def _rglru_dec_kernel(g_ref, xr_ref, buf_ref, h0_ref, cw_ref, cb_ref, wa_ref, ba_ref, wx_ref, bx_ref, lam_ref,
                      o_ref, hl_ref):
    width = cw_ref.shape[0]
    xc = cb_ref[...]
    for jj in range(width - 1):
        xc = xc + cw_ref[jj:jj + 1, :] * buf_ref[jj]
    xc = xc + cw_ref[width - 1:width, :] * xr_ref[...]
    a, u = _rglru_gates(xc, wa_ref[...], ba_ref[...], wx_ref[...], bx_ref[...], lam_ref[...])
    h = a * h0_ref[...] + u
    hl_ref[...] = h
    o_ref[...] = (_gelu(g_ref[...]) * h).astype(BF16)


def _rglru_dec(gate, xr, buf_t, h0, rg):
    m, d = gate.shape
    return pl.pallas_call(
        _rglru_dec_kernel,
        out_shape=[jax.ShapeDtypeStruct((m, d), BF16), jax.ShapeDtypeStruct((m, d), F32)],
        name="rglru_dec",
    )(gate, xr, buf_t, h0, rg["cw"], rg["cb"], rg["wa"], rg["ba"], rg["wx"], rg["bx"], rg["lam"])


def _ffn_tile(y1, gf_ref, wup_ref, cw_ref, cb_ref, wdn_ref, conv_prev, n_split):
    d_ff = wdn_ref.shape[0]
    cf = d_ff // n_split
    hn = _rms(y1, gf_ref[...]).astype(BF16)
    out = jnp.zeros(y1.shape, F32)
    gates = []
    for k in range(n_split):
        c0 = k * cf
        g = jnp.dot(hn, wup_ref[:, c0:c0 + cf], preferred_element_type=F32)
        u = jnp.dot(hn, wup_ref[:, d_ff + c0:d_ff + c0 + cf], preferred_element_type=F32)
        g1, g2 = conv_prev(k, g)
        gc = cb_ref[:, c0:c0 + cf] + cw_ref[0:1, c0:c0 + cf] * g2 + cw_ref[1:2, c0:c0 + cf] * g1 \
            + cw_ref[2:3, c0:c0 + cf] * g
        act = (_gelu(gc) * u).astype(BF16)
        out = out + jnp.dot(act, wdn_ref[c0:c0 + cf, :], preferred_element_type=F32)
        gates.append(g)
    return out, gates


def _prompt_conv_prev(gs, carry, fb_ref, nb_ref, tm, cf):
    t = pl.program_id(1)

    @pl.when(t == 0)
    def _load_state():
        carry[...] = jnp.zeros(carry.shape, F32)
        for k in range(carry.shape[0]):
            carry[k, 6:8, :] = fb_ref[0, :, k * cf:(k + 1) * cf]

    def conv_prev(k, g):
        gs[0:8, :] = carry[k]
        gs[8:8 + tm, :] = g
        carry[k] = g[tm - 8:tm, :]
        nb_ref[0, :, k * cf:(k + 1) * cf] = g[tm - 2:tm, :]
        return gs[7:7 + tm, :], gs[6:6 + tm, :]
    return conv_prev


def _mix_ab_tile(y_ref, a_ref, r_ref, woa_ref, wob_ref):
    return y_ref[0] + jnp.dot(a_ref[0], woa_ref[...], preferred_element_type=F32) \
        + jnp.dot(r_ref[0], wob_ref[...], preferred_element_type=F32)


def _post_ab_prompt_kernel(y_ref, a_ref, r_ref, fb_ref, woa_ref, wob_ref, gf_ref, wup_ref, cw_ref, cb_ref, wdn_ref,
                           o_ref, nb_ref, gs, carry, *, tm, n_split):
    y1 = _mix_ab_tile(y_ref, a_ref, r_ref, woa_ref, wob_ref)
    cf = wdn_ref.shape[0] // n_split
    out, _ = _ffn_tile(y1, gf_ref, wup_ref, cw_ref, cb_ref, wdn_ref,
                       _prompt_conv_prev(gs, carry, fb_ref, nb_ref, tm, cf), n_split)
    o_ref[0] = y1 + out


_FFN_KEYS = ("g", "wup", "cw", "cb", "wdn")


def _ffn_specs(ffn):
    layer = ffn["layer"]
    return [pl.BlockSpec((None,) + ffn[k].shape[1:], lambda *_: (layer, 0, 0), pipeline_mode=pl.Buffered(1))
            for k in _FFN_KEYS]


def _ffn_args(ffn):
    return [ffn[k] for k in _FFN_KEYS]


def _post_ab_prompt(y, attn, rgo, fbuf, wo_a, wo_b, ffn, *, tm=512, n_split=2):
    batch, seq, d = y.shape
    d_ff = ffn["wdn"].shape[1]
    cf = d_ff // n_split
    blk = lambda n: pl.BlockSpec((1, tm, n), lambda b, t: (b, t, 0))
    fb = pl.BlockSpec((1, 2, d_ff), lambda b, t: (b, 0, 0))
    return pl.pallas_call(
        functools.partial(_post_ab_prompt_kernel, tm=tm, n_split=n_split),
        grid=(batch, seq // tm),
        in_specs=[blk(d), blk(attn.shape[-1]), blk(rgo.shape[-1]), fb, _full_spec(wo_a.shape), _full_spec(wo_b.shape)]
        + _ffn_specs(ffn),
        out_specs=[blk(d), fb],
        out_shape=[jax.ShapeDtypeStruct((batch, seq, d), F32), jax.ShapeDtypeStruct((batch, 2, d_ff), F32)],
        scratch_shapes=[pltpu.VMEM((tm + 8, cf), F32), pltpu.VMEM((n_split, 8, cf), F32)],
        compiler_params=_cparams(2),
        name="post_ab_prompt",
    )(y, attn, rgo, fbuf, wo_a, wo_b, *_ffn_args(ffn))


def _dec_conv_prev(fb_ref, cf):
    def conv_prev(k, g):
        return fb_ref[1, :, k * cf:(k + 1) * cf], fb_ref[0, :, k * cf:(k + 1) * cf]
    return conv_prev


def _post_ab_dec_kernel(y_ref, a_ref, r_ref, fb_ref, woa_ref, wob_ref, gf_ref, wup_ref, cw_ref, cb_ref, wdn_ref,
                        o_ref, g_ref, *, n_split):
    y1 = y_ref[...] + jnp.dot(a_ref[...], woa_ref[...], preferred_element_type=F32) \
        + jnp.dot(r_ref[...], wob_ref[...], preferred_element_type=F32)
    cf = wdn_ref.shape[0] // n_split
    out, gates = _ffn_tile(y1, gf_ref, wup_ref, cw_ref, cb_ref, wdn_ref, _dec_conv_prev(fb_ref, cf), n_split)
    o_ref[...] = y1 + out
    for k, g in enumerate(gates):
        g_ref[:, k * cf:(k + 1) * cf] = g


def _post_ab_dec(y, attn, rgo, fbuf_t, wo_a, wo_b, ffn, *, n_split=2):
    m, d = y.shape
    d_ff = ffn["wdn"].shape[1]
    args = (y, attn, rgo, fbuf_t, wo_a, wo_b)
    return pl.pallas_call(
        functools.partial(_post_ab_dec_kernel, n_split=n_split),
        grid=(1,),
        in_specs=[_full_spec(a.shape) for a in args] + _ffn_specs(ffn),
        out_specs=[_whole_spec((m, d)), _whole_spec((m, d_ff))],
        out_shape=[jax.ShapeDtypeStruct((m, d), F32), jax.ShapeDtypeStruct((m, d_ff), F32)],
        compiler_params=_cparams(1),
        name="post_ab_dec",
    )(*args, *_ffn_args(ffn))


def _gmlp_in(y, gm_ref, win_ref, bin_ref, sn_ref):
    d_c = win_ref.shape[1] // 2
    z = _gelu(_dot(_rms(y, gm_ref[...]), win_ref[...]) + bin_ref[...])
    return z[:, :d_c], _rms(z[:, d_c:], sn_ref[...])


def _layer_c_prompt_kernel(y_ref, fb_ref, gm_ref, win_ref, bin_ref, sn_ref, sw_ref, sbt_ref, woc_ref,
                           gf_ref, wup_ref, cw_ref, cb_ref, wdn_ref, gfin_ref,
                           o_ref, nb_ref, gs, carry, *, tm, n_split):
    y = y_ref[0]
    u, v = _gmlp_in(y, gm_ref, win_ref, bin_ref, sn_ref)
    vb = v.astype(BF16)
    n_groups = sw_ref.shape[0]
    tril = lax.broadcasted_iota(I32, (CHUNK, CHUNK), 0) >= lax.broadcasted_iota(I32, (CHUNK, CHUNK), 1)
    wm = [jnp.where(tril, sw_ref[gi], 0.0).astype(BF16) for gi in range(n_groups)]
    rows = []
    for r in range(tm // CHUNK):
        cols = []
        for gi in range(n_groups):
            mixed = jnp.dot(wm[gi], vb[r * CHUNK:(r + 1) * CHUNK, gi * LANES:(gi + 1) * LANES],
                            preferred_element_type=F32)
            cols.append(mixed + sbt_ref[:, gi:gi + 1])
        rows.append(jnp.concatenate(cols, axis=1))
    gated = u * jnp.concatenate(rows, axis=0)
    y1 = y + _dot(gated, woc_ref[...])
    cf = wdn_ref.shape[0] // n_split
    out, _ = _ffn_tile(y1, gf_ref, wup_ref, cw_ref, cb_ref, wdn_ref,
                       _prompt_conv_prev(gs, carry, fb_ref, nb_ref, tm, cf), n_split)
    o_ref[0] = _rms(y1 + out, gfin_ref[...])


def _layer_c_prompt(y, fbuf, cp, ffn, g_final, *, tm=512, n_split=2):
    batch, seq, d = y.shape
    d_ff = ffn["wdn"].shape[1]
    cf = d_ff // n_split
    blk = pl.BlockSpec((1, tm, d), lambda b, t: (b, t, 0))
    fb = pl.BlockSpec((1, 2, d_ff), lambda b, t: (b, 0, 0))
    consts = [cp["g"], cp["win"], cp["bin"], cp["sn"], cp["sw"], cp["sbt"], cp["woc"]]
    return pl.pallas_call(
        functools.partial(_layer_c_prompt_kernel, tm=tm, n_split=n_split),
        grid=(batch, seq // tm),
        in_specs=[blk, fb] + [_full_spec(c.shape) for c in consts] + _ffn_specs(ffn) + [_full_spec(g_final.shape)],
        out_specs=[blk, fb],
        out_shape=[jax.ShapeDtypeStruct((batch, seq, d), F32), jax.ShapeDtypeStruct((batch, 2, d_ff), F32)],
        scratch_shapes=[pltpu.VMEM((tm + 8, cf), F32), pltpu.VMEM((n_split, 8, cf), F32)],
        compiler_params=_cparams(2),
        name="layer_c_prompt",
    )(y, fbuf, *consts, *_ffn_args(ffn), g_final)


def _layer_c_dec_kernel(y_ref, fb_ref, gm_ref, win_ref, bin_ref, sn_ref, sw0_ref, sb0_ref, woc_ref,
                        gf_ref, wup_ref, cw_ref, cb_ref, wdn_ref, gfin_ref, o_ref, g_ref, v_ref, *, n_split):
    y = y_ref[...]
    u, v = _gmlp_in(y, gm_ref, win_ref, bin_ref, sn_ref)
    v_ref[...] = v
    y1 = y + _dot(u * (sw0_ref[...] * v + sb0_ref[...]), woc_ref[...])
    cf = wdn_ref.shape[0] // n_split
    out, gates = _ffn_tile(y1, gf_ref, wup_ref, cw_ref, cb_ref, wdn_ref, _dec_conv_prev(fb_ref, cf), n_split)
    o_ref[...] = _rms(y1 + out, gfin_ref[...])
    for k, g in enumerate(gates):
        g_ref[:, k * cf:(k + 1) * cf] = g


def _layer_c_dec(y, fbuf_t, cp, ffn, g_final, *, n_split=2):
    m, d = y.shape
    d_ff = ffn["wdn"].shape[1]
    d_c = cp["woc"].shape[0]
    args = (y, fbuf_t, cp["g"], cp["win"], cp["bin"], cp["sn"], cp["sw0"], cp["sb0"], cp["woc"])
    return pl.pallas_call(
        functools.partial(_layer_c_dec_kernel, n_split=n_split),
        grid=(1,),
        in_specs=[_full_spec(a.shape) for a in args] + _ffn_specs(ffn) + [_full_spec(g_final.shape)],
        out_specs=[_whole_spec((m, d)), _whole_spec((m, d_ff)), _whole_spec((m, d_c))],
        out_shape=[jax.ShapeDtypeStruct((m, d), F32), jax.ShapeDtypeStruct((m, d_ff), F32),
                   jax.ShapeDtypeStruct((m, d_c), F32)],
        compiler_params=_cparams(1),
        name="layer_c_dec",
    )(*args, *_ffn_args(ffn), g_final)


def _start_page_copy(src_ref, pt_ref, b, dst_ref, sem, pg, page):
    col = pl.multiple_of(pg * page, page)
    pltpu.make_async_copy(src_ref.at[pt_ref[b, pg]], dst_ref.at[:, pl.ds(col, page)], sem).start()


def _start_page_copies(src_ref, pt_ref, b, dst_ref, sem, n_pages, page):
    def body(pg, carry):
        _start_page_copy(src_ref, pt_ref, b, dst_ref, sem, pg, page)
        return carry
    lax.fori_loop(0, n_pages, body, 0, unroll=8)


def _wait_page_copies(dst_ref, sem):
    pltpu.make_async_copy(dst_ref, dst_ref, sem).wait()


def _dec_score_kernel(pt_ref, qi_ref, wi_ref, ixn_ref, cik_ref, keys_ref, knew_ref, ibuf, sems, *, n_pages, page):
    b = pl.program_id(0)
    nb = pl.num_programs(0)
    slot = lax.rem(b, 2)

    def start(bb, sl):
        _start_page_copies(cik_ref, pt_ref, bb, ibuf.at[sl], sems.at[sl], n_pages, page)

    @pl.when(b == 0)
    def _first():
        start(0, 0)

    @pl.when(b + 1 < nb)
    def _prefetch_next():
        start(b + 1, 1 - slot)

    _wait_page_copies(ibuf.at[slot], sems.at[slot])
    qi = qi_ref[0].astype(BF16)
    wi = wi_ref[0]
    s = jnp.dot(qi, ibuf[slot].astype(BF16), preferred_element_type=F32)
    sc = jnp.sum(jnp.maximum(s, 0.0) * wi, axis=0, keepdims=True) * IDX_HEADS ** -0.5
    keys_ref[0] = sc
    kin = ixn_ref[0][:, 0:IDX_DIM]
    sn = jnp.sum(qi_ref[0] * kin, axis=1, keepdims=True)
    scn = jnp.sum(jnp.maximum(sn, 0.0) * wi, axis=0, keepdims=True) * IDX_HEADS ** -0.5
    knew_ref[0] = jnp.broadcast_to(scn, (1, LANES))


def _dec_scores(page_table, qi3, wi3, ix3, cik_t, *, page):
    db, n_pages = page_table.shape
    past = n_pages * page
    return pl.pallas_call(
        functools.partial(_dec_score_kernel, n_pages=n_pages, page=page),
        grid_spec=pltpu.PrefetchScalarGridSpec(
            num_scalar_prefetch=1,
            grid=(db,),
            in_specs=[pl.BlockSpec((1, IDX_HEADS, IDX_DIM), lambda b, pt: (b, 0, 0)),
                      pl.BlockSpec((1, IDX_HEADS, 1), lambda b, pt: (b, 0, 0)),
                      pl.BlockSpec((1, 1, LANES), lambda b, pt: (b, 0, 0)),
                      pl.BlockSpec(memory_space=pl.ANY)],
            out_specs=[pl.BlockSpec((1, 1, past), lambda b, pt: (b, 0, 0)),
                       pl.BlockSpec((1, 1, LANES), lambda b, pt: (b, 0, 0))],
            scratch_shapes=[pltpu.VMEM((2, IDX_DIM, past), F32), pltpu.SemaphoreType.DMA((2,))]),
        out_shape=[jax.ShapeDtypeStruct((db, 1, past), F32), jax.ShapeDtypeStruct((db, 1, LANES), F32)],
        compiler_params=_cparams(1),
        name="dec_scores",
    )(page_table, qi3, wi3, ix3, cik_t)


def _dec_select_kernel(sc_ref, scn_ref, so_ref, sno_ref, thr_ref, *, topk):
    past = sc_ref.shape[1]
    lane0 = lax.broadcasted_iota(I32, scn_ref.shape, 1) == 0
    sc = jnp.concatenate([sc_ref[...], jnp.where(lane0, scn_ref[...], -jnp.inf)], axis=1)
    kk = _float_key(sc)
    kf = jnp.float32(topk)

    def count(pred):
        ones = jnp.where(pred, 1.0, 0.0)
        accs = [ones[:, k * LANES:(k + 1) * LANES] for k in range(8)]
        for k in range(8, ones.shape[1] // LANES):
            accs[k % 8] = accs[k % 8] + ones[:, k * LANES:(k + 1) * LANES]
        return jnp.sum(functools.reduce(lambda x, y: x + y, accs), axis=1, keepdims=True)

    def search_body(i, ans):
        cand = ans | jnp.left_shift(jnp.int32(1), 31 - i)
        return jnp.where(count(kk >= (cand ^ jnp.int32(INT_MIN))) >= kf, cand, ans)
    ans = lax.fori_loop(0, 32, search_body, jnp.zeros((sc.shape[0], 1), I32))
    thr = _key_float(ans ^ jnp.int32(INT_MIN))
    counts = lambda t: (count(sc >= t), count(sc > t))
    thr, cge, cgt = _walk_to_kth(lambda: sc, thr, *counts(thr), kf, counts, 4 * topk)
    need = kf - cgt
    big = jnp.int32(4 * past)
    eqcol = jnp.where(sc == thr, lax.broadcasted_iota(I32, sc.shape, 1), big)
    nbits = int(math.log2(past)) + 1

    def tie_body(i, best):
        cand = best | jnp.left_shift(jnp.int32(1), nbits - 1 - i)
        return jnp.where(count(eqcol < cand) < need, cand, best)
    last = lax.fori_loop(0, nbits, tie_body, jnp.zeros((sc.shape[0], 1), I32))
    sc = jnp.where((eqcol > last) & (eqcol < big), -jnp.inf, sc)
    so_ref[...] = sc[:, :past]
    sno_ref[...] = sc[:, past:]
    thr_ref[...] = jnp.broadcast_to(thr, thr_ref.shape)


def _dec_select(scores, snew, *, topk):
    db, past = scores.shape
    assert past + 1 >= topk
    return pl.pallas_call(
        functools.partial(_dec_select_kernel, topk=topk),
        out_shape=[jax.ShapeDtypeStruct((db, past), F32), jax.ShapeDtypeStruct((db, LANES), F32),
                   jax.ShapeDtypeStruct((db, LANES), F32)],
        name="dec_select",
    )(scores, snew)


def _dec_attn_row(qm, kvn, sc, sn, thr_row, bias, kbuf, vbuf, page):
    past = kbuf.shape[1]
    thr = thr_row[:, 0:1]
    sel = sc >= thr
    sel_new = sn[:, 0:1] >= thr
    far, last, bnew = bias[:, LANES:LANES + 1], bias[:, 0:page], bias[:, LANES + 1:LANES + 2]
    step = min(past, DEC_KEY_CHUNK)
    chunks = [(c, min(c + step, past)) for c in range(0, past, step)]
    qb = qm.astype(BF16)
    parts = []
    for c0, c1 in chunks:
        part = jnp.dot(qb, kbuf[:, c0:c1].astype(BF16), preferred_element_type=F32) + far
        if c1 == past:
            part = jnp.concatenate([part[:, :c1 - c0 - page], part[:, c1 - c0 - page:] + (last - far)], axis=1)
        parts.append(jnp.where(sel[:, c0:c1], part, -jnp.inf))
    lgn = jnp.sum(qm * kvn[:, 0:LANES], axis=1, keepdims=True) + bnew
    lgn = jnp.where(sel_new, lgn, -jnp.inf)
    m = lgn
    for part in parts:
        m = jnp.maximum(m, jnp.max(part, axis=1, keepdims=True))
    en = jnp.exp(lgn - m)
    den = en
    pv = en * kvn[:, LANES:2 * LANES]
    for (c0, c1), part in zip(chunks, parts):
        e = jnp.exp(part - m)
        den = den + jnp.sum(e, axis=1, keepdims=True)
        pv = pv + _dot_nt(e.astype(BF16), vbuf[:, c0:c1].astype(BF16))
    pv = pv / den
    lo = lax.broadcasted_iota(I32, (1, LANES), 1) < HEAD_DIM
    return jnp.concatenate([jnp.where(lo, pv[2 * p:2 * p + 1], pv[2 * p + 1:2 * p + 2]) for p in range(4)],
                           axis=1).astype(BF16)


def _prep_in_ab(w):
    d = w.shape[0]
    nq, nkv = N_HEADS * HEAD_DIM, N_KV_HEADS * HEAD_DIM
    offs = np.cumsum([nq, nkv, nkv, IDX_HEADS * IDX_DIM, IDX_DIM, IDX_HEADS, 512])
    q, k, v, qi, ki, wi, g, xr = jnp.split(w, offs.tolist(), axis=1)
    q = q.reshape(d, N_HEADS, HEAD_DIM)[:, np.array(HEAD_PERM), :].reshape(d, nq)
    pad = jnp.zeros((d, _C_G - _C_IX - IDX_DIM - IDX_HEADS), w.dtype)
    return jnp.concatenate([q, k, v, qi, ki, wi, pad, g, xr], axis=1).astype(BF16)


def _block_diag(w):
    n, c, _ = w.shape
    return (jnp.eye(n, dtype=w.dtype)[:, None, :, None] * w[:, :, None, :]).reshape(n * c, n * c).astype(BF16)


def _ffn_params(layer, stacked):
    return dict(stacked, layer=layer)


def kernel(x_prompt, x_sample, cache_k, cache_v, cache_idx_k, state_rglru_h, state_rglru_conv, state_ffn_conv,
           page_table, norm_mix, norm_ffn, norm_final, rel_bias, w_in_ab, w_out_ab, rg_conv_w, rg_conv_b,
           rg_wa, rg_ba, rg_wx, rg_bx, rg_lambda, w_in_c, b_in_c, sgu_norm, sgu_w, sgu_b, w_out_c,
           ffn_w_up, ffn_conv_w, ffn_conv_b, ffn_w_down):
    batch, seq, d = x_prompt.shape
    db = x_sample.shape[0]
    page = cache_k.shape[2]
    d_a = N_HEADS * HEAD_DIM
    d_b = rg_conv_w.shape[-1]
    d_ff = ffn_w_down.shape[1]
    assert x_sample.shape[1] == 1 and seq % 512 == 0 and page == LANES and w_in_ab.shape[0] == 1

    w_in0 = _prep_in_ab(w_in_ab[0])
    wo = w_out_ab[0]
    wo_a = wo[:d_a].reshape(N_HEADS, HEAD_DIM, d)[np.array(HEAD_PERM)].reshape(d_a, d).astype(BF16)
    wo_b = wo[d_a:].astype(BF16)
    rg = {"cw": rg_conv_w[0], "cb": rg_conv_b[0][None], "wa": _block_diag(rg_wa[0]), "ba": rg_ba[0][None],
          "wx": _block_diag(rg_wx[0]), "bx": rg_bx[0][None], "lam": rg_lambda[0][None]}
    ffn_all = {"g": norm_ffn[:, None, :], "wup": ffn_w_up.astype(BF16), "cw": ffn_conv_w, "cb": ffn_conv_b[:, None, :],
               "wdn": ffn_w_down.astype(BF16)}
    ffn0, ffn1 = _ffn_params(0, ffn_all), _ffn_params(1, ffn_all)
    cp = {"g": norm_mix[1][None], "win": w_in_c[0].astype(BF16), "bin": b_in_c[0][None], "sn": sgu_norm[0][None],
          "sw": sgu_w[0], "sbt": sgu_b[0].T, "woc": w_out_c[0].astype(BF16),
          "sw0": jnp.repeat(sgu_w[0][:, 0, 0], d // sgu_w.shape[1])[None],
          "sb0": jnp.repeat(sgu_b[0][:, 0], d // sgu_w.shape[1])[None]}
    g_mix0 = norm_mix[0][None]
    g_final = norm_final[None]
    bias_st, bias_dec = _bias_tables(rel_bias, page)

    xp = x_prompt.reshape(batch * seq, d)
    xs = x_sample.reshape(db, d)
    q_st, qi_st, kv_p, ix_p, gate_p, xr_p = _inproj(xp, g_mix0, w_in0, stack=True, tm=512)
    qm_s, qi_s, kv_s, ix_s, gate_s, xr_s = _inproj(xs, g_mix0, w_in0, stack=False, tm=db)
    cik_t = jnp.transpose(cache_idx_k[0], (0, 2, 1))
    ck_t = jnp.transpose(cache_k[0], (0, 2, 3, 1)).reshape(-1, 2 * HEAD_DIM, page)
    cv_t = jnp.transpose(cache_v[0], (0, 2, 3, 1)).reshape(-1, 2 * HEAD_DIM, page)
    topk_s = min(TOPK_MAX, (page_table.shape[1] * page + 1) // 4)
    sc_s, sn_s = _dec_scores(page_table, qi_s.reshape(db, IDX_HEADS, IDX_DIM),
                             ix_s[:, IDX_DIM:IDX_DIM + IDX_HEADS].reshape(db, IDX_HEADS, 1),
                             ix_s.reshape(db, 1, LANES), cik_t, page=page)
    sc_s, sn_s, thr_s = _dec_select(sc_s.reshape(db, -1), sn_s.reshape(db, LANES), topk=topk_s)

    attn_p, attn_s = _attn(q_st, qi_st, ix_p, kv_p, bias_st, page_table, jnp.transpose(qm_s, (1, 0, 2)),
                           kv_s.reshape(db, 1, 2 * LANES), sc_s.reshape(db, 1, -1), sn_s.reshape(db, 1, LANES),
                           thr_s.reshape(db, 1, LANES), bias_dec, ck_t, cv_t, batch=batch, seq=seq, page=page)
    attn_s = attn_s.reshape(db, d_a)

    rg_p, h_p, cbuf_p = _rglru_prompt(gate_p, xr_p, jnp.zeros((batch, rg["cw"].shape[0] - 1, d_b), F32),
                                      jnp.zeros((batch, d_b), F32), rg, batch=batch, seq=seq)
    zero_fb = jnp.zeros((batch, 2, d_ff), F32)
    y1_p, fb0_p = _post_ab_prompt(x_prompt, attn_p, rg_p, zero_fb, wo_a, wo_b, ffn0)
    y_p, fb1_p = _layer_c_prompt(y1_p, zero_fb, cp, ffn1, g_final)

    cbuf_s_in = state_rglru_conv[0]
    rg_s, h_s = _rglru_dec(gate_s, xr_s, jnp.transpose(cbuf_s_in, (1, 0, 2)), state_rglru_h[0], rg)
    y1_s, g0_s = _post_ab_dec(xs, attn_s, rg_s, jnp.transpose(state_ffn_conv[0], (1, 0, 2)), wo_a, wo_b, ffn0)
    y_s, g1_s, v_s = _layer_c_dec(y1_s, jnp.transpose(state_ffn_conv[1], (1, 0, 2)), cp, ffn1, g_final)

    kv4 = kv_p.reshape(batch, seq, 2, N_KV_HEADS, HEAD_DIM)
    kvs = kv_s.reshape(db, 1, 2, N_KV_HEADS, HEAD_DIM)
    fbuf_s = lambda layer, g: jnp.concatenate([state_ffn_conv[layer][:, 1:], g[:, None]], axis=1)
    return (y_p, y_s.reshape(db, 1, d),
            kv4[None, :, :, 0], kv4[None, :, :, 1], ix_p.reshape(batch, seq, LANES)[None, :, :, :IDX_DIM],
            kvs[None, :, :, 0], kvs[None, :, :, 1], ix_s.reshape(db, 1, LANES)[None, :, :, :IDX_DIM],
            h_p.reshape(batch, d_b)[None], cbuf_p[None],
            h_s[None], jnp.concatenate([cbuf_s_in[:, 1:], xr_s[:, None]], axis=1)[None],
            v_s.reshape(db, 1, -1)[None],
            jnp.stack([fb0_p, fb1_p]), jnp.stack([fbuf_s(0, g0_s), fbuf_s(1, g1_s)]))
```

```python
import functools
import math

import numpy as np
import jax
import jax.numpy as jnp
from jax import lax
from jax.experimental import pallas as pl
from jax.experimental.pallas import tpu as pltpu

F32 = jnp.float32
BF16 = jnp.bfloat16
I32 = jnp.int32

N_HEADS = 8
HEAD_DIM = 64
N_KV_HEADS = 2
Q_PER_KV = N_HEADS // N_KV_HEADS
IDX_HEADS = 8
IDX_DIM = 64
TOPK_MAX = 256
N_BUCKETS = 32
REL_MAX_EXACT = N_BUCKETS // 2
REL_MAX_DIST = 128
RG_C = 8.0
CHUNK = 128
EPS = 1e-6

LANES = 128
QB = 128
DEC_KEY_CHUNK = 2048
INT_MIN = -(2 ** 31)
KEY_MIN_FINITE = INT_MIN + 0x800000
NEG_MAX = float(np.finfo(np.float32).min)
HEAD_PERM = (0, 4, 1, 5, 2, 6, 3, 7)
VMEM_LIMIT = 56 * 1024 * 1024


def _cparams(n_grid):
    return pltpu.CompilerParams(dimension_semantics=("arbitrary",) * n_grid, vmem_limit_bytes=VMEM_LIMIT)


def _full_spec(shape):
    nd = len(shape)
    return pl.BlockSpec(shape, lambda *_: (0,) * nd, pipeline_mode=pl.Buffered(1))


def _whole_spec(shape):
    nd = len(shape)
    return pl.BlockSpec(shape, lambda *_: (0,) * nd)


def _rms(x, g):
    return x * lax.rsqrt(jnp.mean(x * x, axis=-1, keepdims=True) + EPS) * g


def _gelu(x):
    return x * (0.5 * (1.0 + jnp.tanh(math.sqrt(2.0 / math.pi) * (x + 0.044715 * (x * x * x)))))


def _sigmoid(x):
    return 1.0 / (1.0 + jnp.exp(-x))


def _softplus(x):
    return jnp.maximum(x, 0.0) + jnp.log(1.0 + jnp.exp(-jnp.abs(x)))


def _dot(a, b):
    return jnp.dot(a.astype(BF16), b, preferred_element_type=F32)


def _dot_nt(a, b):
    return lax.dot_general(a, b, (((1,), (1,)), ((), ())), preferred_element_type=F32)


def _float_key(x):
    bits = pltpu.bitcast(x, I32)
    key = jnp.where(bits < 0, bits ^ jnp.int32(0x7FFFFFFF), bits)
    return jnp.where(bits == jnp.int32(INT_MIN), jnp.int32(0), key)


def _key_float(key):
    key = jnp.maximum(key, jnp.int32(KEY_MIN_FINITE))
    return pltpu.bitcast(jnp.where(key < 0, key ^ jnp.int32(0x7FFFFFFF), key), F32)


def _walk_to_kth(scores, t, cge, cgt, kf, count_ge_gt, max_steps):
    axis = 0 if t.shape[0] == 1 else 1

    def settled(t, cge, cgt):
        return (cgt < kf) & ((cge >= kf) | (t <= NEG_MAX))

    def unsettled(t, cge, cgt):
        return jnp.max(jnp.where(settled(t, cge, cgt), 0.0, 1.0)) > 0.0

    def body(st):
        t, cge, cgt, it = st
        sc = scores()
        below = jnp.max(jnp.where(sc < t, sc, -jnp.inf), axis=axis, keepdims=True)
        above = jnp.min(jnp.where(sc > t, sc, jnp.inf), axis=axis, keepdims=True)
        t = jnp.where(cgt >= kf, above, jnp.where((cge < kf) & (t > NEG_MAX), jnp.maximum(below, NEG_MAX), t))
        cge, cgt = count_ge_gt(t)
        return t, cge, cgt, it + 1

    t, cge, cgt, _ = lax.while_loop(lambda st: unsettled(st[0], st[1], st[2]) & (st[3] < max_steps), body,
                                    (t, cge, cgt, jnp.int32(0)))
    return t, cge, cgt


def _t5_bucket_np(n):
    n = np.maximum(n, 0)
    nf = np.maximum(n, 1).astype(np.float32)
    large = REL_MAX_EXACT + (np.log(nf / np.float32(REL_MAX_EXACT)) / np.float32(math.log(REL_MAX_DIST / REL_MAX_EXACT))
                             * np.float32(N_BUCKETS - REL_MAX_EXACT)).astype(np.int32)
    large = np.minimum(large, N_BUCKETS - 1)
    return np.where(n < REL_MAX_EXACT, n, large).astype(np.int32)


_C_Q, _C_KV, _C_QI, _C_IX, _C_G, _C_X, _C_END = 0, 512, 768, 1280, 1408, 1920, 2432


def _inproj_kernel(x_ref, g_ref, w_ref, q_ref, qi_ref, kv_ref, ix_ref, gate_ref, xr_ref, *, stack):
    hn = _rms(x_ref[...], g_ref[...])
    z = _dot(hn, w_ref[...])
    q = z[:, _C_Q:_C_KV] * HEAD_DIM ** -0.5
    qi = z[:, _C_QI:_C_IX] * IDX_DIM ** -0.5
    kv_ref[...] = z[:, _C_KV:_C_QI]
    ix_ref[...] = z[:, _C_IX:_C_G]
    gate_ref[...] = z[:, _C_G:_C_X]
    xr_ref[...] = z[:, _C_X:_C_END]
    if stack:
        qb, qib = q.astype(BF16), qi.astype(BF16)
        for r in range(q.shape[0] // QB):
            for p in range(4):
                q_ref[r, p * QB:(p + 1) * QB, :] = qb[r * QB:(r + 1) * QB, p * LANES:(p + 1) * LANES]
                qi_ref[r, p * QB:(p + 1) * QB, :] = qib[r * QB:(r + 1) * QB, p * LANES:(p + 1) * LANES]
    else:
        lo = lax.broadcasted_iota(I32, (q.shape[0], LANES), 1) < HEAD_DIM
        for p in range(4):
            qp = q[:, p * LANES:(p + 1) * LANES]
            q_ref[2 * p] = jnp.where(lo, qp, 0.0)
            q_ref[2 * p + 1] = jnp.where(lo, 0.0, qp)
        qi_ref[...] = qi


def _inproj(x2d, g, w, *, stack, tm):
    m, d = x2d.shape
    if stack:
        q_shape, q_spec = (m // QB, 4 * QB, LANES), pl.BlockSpec((tm // QB, 4 * QB, LANES), lambda i: (i, 0, 0))
        qi_shape, qi_spec, qdt = q_shape, q_spec, BF16
    else:
        q_shape, q_spec = (N_HEADS, m, LANES), pl.BlockSpec((N_HEADS, tm, LANES), lambda i: (0, i, 0))
        qi_shape, qi_spec, qdt = (m, 512), pl.BlockSpec((tm, 512), lambda i: (i, 0)), F32
    row = lambda n: pl.BlockSpec((tm, n), lambda i: (i, 0))
    return pl.pallas_call(
        functools.partial(_inproj_kernel, stack=stack),
        grid=(m // tm,),
        in_specs=[row(d), _full_spec((1, d)), _full_spec(w.shape)],
        out_specs=[q_spec, qi_spec, row(256), row(128), row(512), row(512)],
        out_shape=[jax.ShapeDtypeStruct(q_shape, qdt), jax.ShapeDtypeStruct(qi_shape, qdt),
                   jax.ShapeDtypeStruct((m, 256), F32), jax.ShapeDtypeStruct((m, 128), F32),
                   jax.ShapeDtypeStruct((m, 512), F32), jax.ShapeDtypeStruct((m, 512), F32)],
        compiler_params=_cparams(1),
        name="inproj_stack" if stack else "inproj_dec",
    )(x2d, g, w)


def _bias_kernel(rb_ref, bk_ref, bkd_ref, o_ref, od_ref):
    for d in range(3):
        bk = bk_ref[d]
        for p in range(4):
            for a in range(2):
                h = p + 4 * a
                acc = jnp.zeros((QB, LANES), F32)
                for b in range(N_BUCKETS):
                    acc = jnp.where(bk == b, rb_ref[b, h], acc)
                o_ref[d, a * QB:(a + 1) * QB, p * LANES:(p + 1) * LANES] = acc
    bkd = bkd_ref[...]
    rowi = lax.broadcasted_iota(I32, (N_HEADS, 2 * LANES), 0)
    acc = jnp.zeros((N_HEADS, 2 * LANES), F32)
    for r in range(N_HEADS):
        h = r // 2 + 4 * (r % 2)
        for b in range(N_BUCKETS):
            acc = jnp.where((rowi == r) & (bkd == b), rb_ref[b, h], acc)
    od_ref[...] = acc


def _bias_tables(rel_bias, page):
    key = np.arange(QB)[:, None]
    qry = np.arange(LANES)[None, :]
    bk = np.stack([_t5_bucket_np(d * QB + qry - key) for d in range(3)])
    assert (_t5_bucket_np(np.arange(2 * QB + 1 - LANES, 4 * QB)) == N_BUCKETS - 1).all()
    assert (_t5_bucket_np(np.arange(page, 8 * page)) == N_BUCKETS - 1).all()
    dec = np.zeros((2 * LANES,), np.int64)
    dec[:page] = page - np.arange(page)
    dec[LANES] = 2 * REL_MAX_DIST
    dec[LANES + 1] = 0
    bkd = np.broadcast_to(_t5_bucket_np(dec)[None, :], (N_HEADS, 2 * LANES))
    return pl.pallas_call(
        _bias_kernel,
        in_specs=[pl.BlockSpec(memory_space=pltpu.SMEM), pl.BlockSpec(memory_space=pltpu.VMEM),
                  pl.BlockSpec(memory_space=pltpu.VMEM)],
        out_shape=[jax.ShapeDtypeStruct((3, 2 * QB, 4 * LANES), F32), jax.ShapeDtypeStruct((N_HEADS, 2 * LANES), F32)],
        name="bias_tables",
    )(rel_bias, jnp.asarray(bk, I32), jnp.asarray(bkd, I32))


def _search_widths(n_chunks):
    cuts = sorted({min(c, n_chunks) for c in (2, 4, 8, 12, 16)} | {n_chunks})
    return [c for c in cuts if c <= n_chunks]


def _attn_kernel(pt_ref, q_ref, qi_ref, ixq_ref, ixk_ref, kv_ref, bias_ref,
                 dq_ref, dkvn_ref, dsc_ref, dsn_ref, dthr_ref, dbias_ref, ck_ref, cv_ref,
                 o_ref, od_ref,
                 kblk, vblk_t, kiblk, keys, scores, logits, acc, thr_ref, cge_ref, cgt_ref, kbuf, vbuf, sems,
                 *, n_chunks, topk, n_pages, page, steps_per_seq):
    j = pl.program_id(1)
    step = pl.program_id(0) * n_chunks + j
    seq_s = lax.div(step, jnp.int32(steps_per_seq))
    phase = lax.rem(step, jnp.int32(steps_per_seq))

    pages_per_trip = -(-n_pages // 32)

    def fetch_sample_pages(trip):
        @pl.when(phase == 0)
        def _():
            for u in range(pages_per_trip):
                pg = trip * pages_per_trip + u

                def start(pg=pg):
                    _start_page_copy(ck_ref, pt_ref, seq_s, kbuf, sems.at[0], pg, page, priority=1)
                    _start_page_copy(cv_ref, pt_ref, seq_s, vbuf, sems.at[1], pg, page, priority=1)
                if 32 * pages_per_trip == n_pages:
                    start()
                else:
                    pl.when(pg < n_pages)(start)

    lane = lax.broadcasted_iota(I32, (QB, LANES), 1)
    row = lax.broadcasted_iota(I32, (QB, LANES), 0)
    lo = lane < HEAD_DIM
    blocks = [(a, p) for a in range(2) for p in range(4)]
    rs = lambda a: slice(a * QB, (a + 1) * QB)
    cs = lambda p: slice(p * LANES, (p + 1) * LANES)
    chunk = lambda c: pl.ds(pl.multiple_of(c * QB, QB), QB)

    @pl.when(j == 0)
    def _build_block_diagonal_keys():
        def body(c, carry):
            kc = kv_ref[0, chunk(c), 0:LANES]
            vt = kv_ref[0, chunk(c), LANES:2 * LANES].T
            kia = jnp.where(lo, ixk_ref[0, chunk(c), :], 0.0)
            kblk[c, 0:QB, :] = jnp.where(lo, kc, 0.0).astype(BF16)
            kblk[c, QB:2 * QB, :] = jnp.where(lo, 0.0, kc).astype(BF16)
            vblk_t[c, :, 0:QB] = jnp.where(row < HEAD_DIM, vt, 0.0).astype(BF16)
            vblk_t[c, :, QB:2 * QB] = jnp.where(row < HEAD_DIM, 0.0, vt).astype(BF16)
            kiblk[c, 0:QB, :] = kia.astype(BF16)
            kiblk[c, QB:2 * QB, :] = pltpu.roll(kia, HEAD_DIM, 1).astype(BF16)
            return carry
        lax.fori_loop(0, n_chunks, body, 0)

    qi = qi_ref[0]
    q = q_ref[0]
    wt = ixq_ref[0].T
    w_row = {(a, p): wt[IDX_DIM + 2 * p + a:IDX_DIM + 2 * p + a + 1, :] for a, p in blocks}
    qpos = j * QB + lane

    n_pairs = (j + 2) // 2

    def chunk_loop(body, carry):
        n_quads = n_pairs // 2
        carry = lax.fori_loop(0, n_quads, lambda i, cr: body([4 * i + u for u in range(4)], cr), carry)
        return lax.fori_loop(2 * n_quads, n_pairs, lambda i, cr: body([2 * i, 2 * i + 1], cr), carry)

    def score_body(cs_, carry):
        for c in cs_:
            s = _dot_nt(kiblk[c], qi)
            lg = _dot_nt(kblk[c], q)
            sc = jnp.zeros((QB, LANES), F32)
            for a, p in blocks:
                sc = sc + jnp.maximum(s[rs(a), cs(p)], 0.0) * w_row[(a, p)]
            sc = sc * IDX_HEADS ** -0.5
            admissible = c * QB + row <= qpos
            scores[chunk(c), :] = jnp.where(admissible, sc, -jnp.inf)
            keys[chunk(c), :] = jnp.where(admissible, _float_key(sc), jnp.int32(INT_MIN))
            logits[c] = lg + bias_ref[jnp.clip(j - c, 0, 2)]
        return carry
    chunk_loop(score_body, 0)

    def fill_body(c, carry):
        scores[chunk(c), :] = jnp.full((QB, LANES), -jnp.inf, F32)
        keys[chunk(c), :] = jnp.full((QB, LANES), INT_MIN, I32)
        return carry
    lax.fori_loop(2 * n_pairs, n_chunks, fill_body, 0)

    kf = jnp.float32(topk)

    def count(src, width, pred):
        accs = [jnp.zeros((8, LANES), F32) for _ in range(8)]
        for g in range(width // 8):
            accs[g % 8] = accs[g % 8] + jnp.where(pred(src[g * 8:(g + 1) * 8, :]), 1.0, 0.0)
        return jnp.sum(functools.reduce(lambda x, y: x + y, accs), axis=0, keepdims=True)

    def search(width):
        def search_body(i, ans):
            fetch_sample_pages(i)
            cand = ans | jnp.left_shift(jnp.int32(1), 31 - i)
            cs_ = cand ^ jnp.int32(INT_MIN)
            return jnp.where(count(keys, width, lambda k: k >= cs_) >= kf, cand, ans)
        ans = lax.fori_loop(0, 32, search_body, jnp.zeros((1, LANES), I32))
        t = _key_float(ans ^ jnp.int32(INT_MIN))
        thr_ref[...] = t
        cge_ref[...] = count(scores, width, lambda s: s >= t)
        cgt_ref[...] = count(scores, width, lambda s: s > t)

    prev = 0
    for n in _search_widths(n_chunks):
        pl.when((j >= prev) & (j < n))(functools.partial(search, n * QB))
        prev = n

    s_len = scores.shape[0]
    thr, cge, cgt = _walk_to_kth(
        lambda: scores[...], thr_ref[...], cge_ref[...], cgt_ref[...], kf,
        lambda t: (count(scores, s_len, lambda s: s >= t), count(scores, s_len, lambda s: s > t)), 4 * topk)

    @pl.when(jnp.max(cge) > kf)
    def _break_ties_by_position():
        need = kf - cgt
        big = jnp.int32(2 * s_len)
        eqrow_ref = keys
        eqrow_ref[...] = jnp.where(scores[...] == thr, lax.broadcasted_iota(I32, scores.shape, 0), big)
        nbits = int(math.log2(s_len))

        def tie_body(i, best):
            cand = best | jnp.left_shift(jnp.int32(1), nbits - 1 - i)
            return jnp.where(count(eqrow_ref, s_len, lambda e: e < cand) < need, cand, best)
        last = lax.fori_loop(0, nbits, tie_body, jnp.zeros((1, LANES), I32))
        eqrow = eqrow_ref[...]
        scores[...] = jnp.where((eqrow > last) & (eqrow < big), -jnp.inf, scores[...])

    def mask_body(cs_, mx):
        mx = list(mx)
        for c in cs_:
            sel = scores[chunk(c), :] >= thr
            for n, (a, p) in enumerate(blocks):
                blk = jnp.where(sel, logits[c, rs(a), cs(p)], -jnp.inf)
                logits[c, rs(a), cs(p)] = blk
                mx[n] = jnp.maximum(mx[n], jnp.max(blk, axis=0, keepdims=True))
        return tuple(mx)
    mx = chunk_loop(mask_body, tuple(jnp.full((1, LANES), -jnp.inf, F32) for _ in blocks))

    acc[...] = jnp.zeros(acc.shape, F32)

    def pv_body(cs_, ls):
        ls = list(ls)
        pv = jnp.zeros(acc.shape, F32)
        for c in cs_:
            rows = []
            for a in range(2):
                cols = []
                for p in range(4):
                    e = jnp.exp(logits[c, rs(a), cs(p)] - mx[a * 4 + p])
                    ls[a * 4 + p] = ls[a * 4 + p] + jnp.sum(e, axis=0, keepdims=True)
                    cols.append(e.astype(BF16))
                rows.append(jnp.concatenate(cols, axis=1))
            pmat = jnp.concatenate(rows, axis=0)
            pv = pv + jnp.dot(vblk_t[c], pmat, preferred_element_type=F32)
        acc[...] = acc[...] + pv
        return tuple(ls)
    ls = chunk_loop(pv_body, tuple(jnp.zeros((1, LANES), F32) for _ in blocks))

    for p in range(4):
        inv = jnp.where(row < HEAD_DIM, 1.0 / ls[p], 1.0 / ls[4 + p])
        o_ref[0, :, cs(p)] = (acc[:, cs(p)] * inv).T.astype(BF16)

    @pl.when(phase == steps_per_seq - 1)
    def _sample_attention():
        _wait_page_copies(kbuf, sems.at[0])
        _wait_page_copies(vbuf, sems.at[1])
        od_ref[0] = _dec_attn_row(dq_ref[0], dkvn_ref[0], dsc_ref[0], dsn_ref[0], dthr_ref[0], dbias_ref[...],
                                  kbuf, vbuf, page)


def _attn(q_st, qi_st, ix, kv, bias_st, page_table, qm, kvn, sc_s, sn_s, thr_s, bias_dec, ck_t, cv_t,
          *, batch, seq, page):
    nq = seq // QB
    assert nq % 2 == 0
    topk = min(TOPK_MAX, seq // 4)
    ix3 = ix.reshape(batch, seq, LANES)
    kv3 = kv.reshape(batch, seq, 2 * LANES)
    db, n_pages = page_table.shape
    past = n_pages * page
    steps_per_seq = (batch * nq) // db
    assert steps_per_seq * db == batch * nq
    drow = lambda n: pl.BlockSpec((1, 1, n), lambda b, j, pt: ((b * nq + j) // steps_per_seq, 0, 0))
    return pl.pallas_call(
        functools.partial(_attn_kernel, n_chunks=nq, topk=topk, n_pages=n_pages, page=page,
                          steps_per_seq=steps_per_seq),
        grid_spec=pltpu.PrefetchScalarGridSpec(
            num_scalar_prefetch=1,
            grid=(batch, nq),
            in_specs=[pl.BlockSpec((1, 4 * QB, LANES), lambda b, j, pt: (b * nq + j, 0, 0)),
                      pl.BlockSpec((1, 4 * QB, LANES), lambda b, j, pt: (b * nq + j, 0, 0)),
                      pl.BlockSpec((1, QB, LANES), lambda b, j, pt: (b, j, 0)),
                      pl.BlockSpec((1, seq, LANES), lambda b, j, pt: (b, 0, 0)),
                      pl.BlockSpec((1, seq, 2 * LANES), lambda b, j, pt: (b, 0, 0)),
                      _full_spec(bias_st.shape),
                      pl.BlockSpec((1, N_HEADS, LANES), lambda b, j, pt: ((b * nq + j) // steps_per_seq, 0, 0)),
                      drow(2 * LANES), drow(past), drow(LANES), drow(LANES), _full_spec(bias_dec.shape),
                      pl.BlockSpec(memory_space=pl.ANY), pl.BlockSpec(memory_space=pl.ANY)],
            out_specs=[pl.BlockSpec((1, QB, 4 * LANES), lambda b, j, pt: (b, j, 0)), drow(4 * LANES)],
            scratch_shapes=[pltpu.VMEM((nq, 2 * QB, LANES), BF16), pltpu.VMEM((nq, LANES, 2 * QB), BF16),
                            pltpu.VMEM((nq, 2 * QB, LANES), BF16), pltpu.VMEM((seq, LANES), I32),
                            pltpu.VMEM((seq, LANES), F32),
                            pltpu.VMEM((nq, 2 * QB, 4 * LANES), F32), pltpu.VMEM((LANES, 4 * LANES), F32),
                            pltpu.VMEM((1, LANES), F32), pltpu.VMEM((1, LANES), F32), pltpu.VMEM((1, LANES), F32),
                            pltpu.VMEM((2 * HEAD_DIM, past), F32), pltpu.VMEM((2 * HEAD_DIM, past), F32),
                            pltpu.SemaphoreType.DMA((2,))]),
        out_shape=[jax.ShapeDtypeStruct((batch, seq, 4 * LANES), BF16),
                   jax.ShapeDtypeStruct((db, 1, 4 * LANES), BF16)],
        compiler_params=_cparams(2),
        name="attn",
    )(page_table, q_st, qi_st, ix3, ix3, kv3, bias_st, qm, kvn, sc_s, sn_s, thr_s, bias_dec, ck_t, cv_t)


def _rglru_gates(xc, wa, ba, wx, bx, lam):
    r = _sigmoid(_dot(xc, wa) + ba)
    i = _sigmoid(_dot(xc, wx) + bx)
    log_a = -RG_C * r * _softplus(-lam)
    a = jnp.exp(log_a)
    u = jnp.sqrt(1.0 - jnp.exp(2.0 * log_a)) * (i * xc)
    return a, u


def _rglru_prompt_kernel(g_ref, xr_ref, buf_ref, h0_ref, cw_ref, cb_ref, wa_ref, ba_ref, wx_ref, bx_ref, lam_ref,
                         o_ref, hl_ref, nb_ref, xs, a_s, u_s, tail, hc, *, tc):
    t = pl.program_id(0)
    width = cw_ref.shape[0]
    nb, _, d = g_ref.shape

    @pl.when(t == 0)
    def _load_state():
        tail[...] = jnp.zeros(tail.shape, F32)
        tail[:, 8 - (width - 1):8, :] = buf_ref[...]
        hc[...] = h0_ref[...]

    xs[:, 0:8, :] = tail[...]
    xs[:, 8:8 + tc, :] = xr_ref[...]
    tail[...] = xs[:, tc:tc + 8, :]
    xc = cb_ref[...]
    for jj in range(width):
        off = 8 - (width - 1) + jj
        xc = xc + cw_ref[jj:jj + 1, :] * xs[:, off:off + tc, :]
    a, u = _rglru_gates(xc.reshape(nb * tc, d), wa_ref[...], ba_ref[...], wx_ref[...], bx_ref[...], lam_ref[...])
    n_lb = d // LANES
    pitch = tc + 8
    for k in range(n_lb):
        for b in range(nb):
            a_s[k, b * pitch:b * pitch + tc, :] = a[b * tc:(b + 1) * tc, k * LANES:(k + 1) * LANES]
            u_s[k, b * pitch:b * pitch + tc, :] = u[b * tc:(b + 1) * tc, k * LANES:(k + 1) * LANES]

    def scan_body(i, hs):
        rows = pl.ds(i, nb, stride=pitch)
        out = []
        for k in range(n_lb):
            h = a_s[k, rows, :] * hs[k] + u_s[k, rows, :]
            u_s[k, rows, :] = h
            out.append(h)
        return tuple(out)
    h0 = hc[...]
    hs = lax.fori_loop(0, tc, scan_body, tuple(h0[:, k * LANES:(k + 1) * LANES] for k in range(n_lb)), unroll=8)
    h = jnp.concatenate(hs, axis=1)
    hc[...] = h
    hseq = jnp.concatenate([jnp.concatenate([u_s[k, b * pitch:b * pitch + tc, :] for b in range(nb)], axis=0)
                            for k in range(n_lb)], axis=1)
    o_ref[...] = (_gelu(g_ref[...].reshape(nb * tc, d)) * hseq).reshape(nb, tc, d).astype(BF16)
    hl_ref[...] = h
    nb_ref[...] = xs[:, tc + 8 - (width - 1):tc + 8, :]


def _rglru_prompt(gate, xr, buf, h0, rg, *, batch, seq, tc=256):
    d = gate.shape[-1]
    width = rg["cw"].shape[0]
    g3, x3 = gate.reshape(batch, seq, d), xr.reshape(batch, seq, d)
    blk = pl.BlockSpec((batch, tc, d), lambda t: (0, t, 0))
    vec = _full_spec((1, d))
    return pl.pallas_call(
        functools.partial(_rglru_prompt_kernel, tc=tc),
        grid=(seq // tc,),
        in_specs=[blk, blk, _full_spec((batch, width - 1, d)), _full_spec((batch, d)), _full_spec((width, d)), vec,
                  _full_spec((d, d)), vec, _full_spec((d, d)), vec, vec],
        out_specs=[blk, pl.BlockSpec((batch, d), lambda t: (0, 0)), pl.BlockSpec((batch, width - 1, d), lambda t: (0, 0, 0))],
        out_shape=[jax.ShapeDtypeStruct((batch, seq, d), BF16), jax.ShapeDtypeStruct((batch, d), F32),
                   jax.ShapeDtypeStruct((batch, width - 1, d), F32)],
        scratch_shapes=[pltpu.VMEM((batch, tc + 8, d), F32), pltpu.VMEM((d // LANES, batch * (tc + 8), LANES), F32),
                        pltpu.VMEM((d // LANES, batch * (tc + 8), LANES), F32), pltpu.VMEM((batch, 8, d), F32),
                        pltpu.VMEM((batch, d), F32)],
        compiler_params=_cparams(1),
        name="rglru_prompt",
    )(g3, x3, buf, h0, rg["cw"], rg["cb"], rg["wa"], rg["ba"], rg["wx"], rg["bx"], rg["lam"])


def _rglru_dec_kernel(g_ref, xr_ref, buf_ref, h0_ref, cw_ref, cb_ref, wa_ref, ba_ref, wx_ref, bx_ref, lam_ref,
                      o_ref, hl_ref):
    width = cw_ref.shape[0]
    xc = cb_ref[...]
    for jj in range(width - 1):
        xc = xc + cw_ref[jj:jj + 1, :] * buf_ref[jj]
    xc = xc + cw_ref[width - 1:width, :] * xr_ref[...]
    a, u = _rglru_gates(xc, wa_ref[...], ba_ref[...], wx_ref[...], bx_ref[...], lam_ref[...])
    h = a * h0_ref[...] + u
    hl_ref[...] = h
    o_ref[...] = (_gelu(g_ref[...]) * h).astype(BF16)


def _rglru_dec(gate, xr, buf_t, h0, rg):
    m, d = gate.shape
    return pl.pallas_call(
        _rglru_dec_kernel,
        out_shape=[jax.ShapeDtypeStruct((m, d), BF16), jax.ShapeDtypeStruct((m, d), F32)],
        name="rglru_dec",
    )(gate, xr, buf_t, h0, rg["cw"], rg["cb"], rg["wa"], rg["ba"], rg["wx"], rg["bx"], rg["lam"])


def _ffn_tile(y1, gf_ref, wup_ref, cw_ref, cb_ref, wdn_ref, conv_prev, n_split):
    d_ff = wdn_ref.shape[0]
    cf = d_ff // n_split
    hn = _rms(y1, gf_ref[...]).astype(BF16)
    out = jnp.zeros(y1.shape, F32)
    gates = []
    for k in range(n_split):
        c0 = k * cf
        g = jnp.dot(hn, wup_ref[:, c0:c0 + cf], preferred_element_type=F32)
        u = jnp.dot(hn, wup_ref[:, d_ff + c0:d_ff + c0 + cf], preferred_element_type=F32)
        g1, g2 = conv_prev(k, g)
        gc = cb_ref[:, c0:c0 + cf] + cw_ref[0:1, c0:c0 + cf] * g2 + cw_ref[1:2, c0:c0 + cf] * g1 \
            + cw_ref[2:3, c0:c0 + cf] * g
        act = (_gelu(gc) * u).astype(BF16)
        out = out + jnp.dot(act, wdn_ref[c0:c0 + cf, :], preferred_element_type=F32)
        gates.append(g)
    return out, gates


def _prompt_conv_prev(gs, carry, fb_ref, nb_ref, tm, cf):
    t = pl.program_id(1)

    @pl.when(t == 0)
    def _load_state():
        carry[...] = jnp.zeros(carry.shape, F32)
        for k in range(carry.shape[0]):
            carry[k, 6:8, :] = fb_ref[0, :, k * cf:(k + 1) * cf]

    def conv_prev(k, g):
        gs[0:8, :] = carry[k]
        gs[8:8 + tm, :] = g
        carry[k] = g[tm - 8:tm, :]
        nb_ref[0, :, k * cf:(k + 1) * cf] = g[tm - 2:tm, :]
        return gs[7:7 + tm, :], gs[6:6 + tm, :]
    return conv_prev


def _mix_ab_tile(y_ref, a_ref, r_ref, woa_ref, wob_ref):
    return y_ref[0] + jnp.dot(a_ref[0], woa_ref[...], preferred_element_type=F32) \
        + jnp.dot(r_ref[0], wob_ref[...], preferred_element_type=F32)


def _post_ab_prompt_kernel(y_ref, a_ref, r_ref, fb_ref, woa_ref, wob_ref, gf_ref, wup_ref, cw_ref, cb_ref, wdn_ref,
                           o_ref, nb_ref, gs, carry, *, tm, n_split):
    y1 = _mix_ab_tile(y_ref, a_ref, r_ref, woa_ref, wob_ref)
    cf = wdn_ref.shape[0] // n_split
    out, _ = _ffn_tile(y1, gf_ref, wup_ref, cw_ref, cb_ref, wdn_ref,
                       _prompt_conv_prev(gs, carry, fb_ref, nb_ref, tm, cf), n_split)
    o_ref[0] = y1 + out


_FFN_KEYS = ("g", "wup", "cw", "cb", "wdn")


def _ffn_specs(ffn):
    layer = ffn["layer"]
    return [pl.BlockSpec((None,) + ffn[k].shape[1:], lambda *_: (layer, 0, 0), pipeline_mode=pl.Buffered(1))
            for k in _FFN_KEYS]


def _ffn_args(ffn):
    return [ffn[k] for k in _FFN_KEYS]


def _post_ab_prompt(y, attn, rgo, fbuf, wo_a, wo_b, ffn, *, tm=512, n_split=2):
    batch, seq, d = y.shape
    d_ff = ffn["wdn"].shape[1]
    cf = d_ff // n_split
    blk = lambda n: pl.BlockSpec((1, tm, n), lambda b, t: (b, t, 0))
    fb = pl.BlockSpec((1, 2, d_ff), lambda b, t: (b, 0, 0))
    return pl.pallas_call(
        functools.partial(_post_ab_prompt_kernel, tm=tm, n_split=n_split),
        grid=(batch, seq // tm),
        in_specs=[blk(d), blk(attn.shape[-1]), blk(rgo.shape[-1]), fb, _full_spec(wo_a.shape), _full_spec(wo_b.shape)]
        + _ffn_specs(ffn),
        out_specs=[blk(d), fb],
        out_shape=[jax.ShapeDtypeStruct((batch, seq, d), F32), jax.ShapeDtypeStruct((batch, 2, d_ff), F32)],
        scratch_shapes=[pltpu.VMEM((tm + 8, cf), F32), pltpu.VMEM((n_split, 8, cf), F32)],
        compiler_params=_cparams(2),
        name="post_ab_prompt",
    )(y, attn, rgo, fbuf, wo_a, wo_b, *_ffn_args(ffn))


def _dec_conv_prev(fb_ref, cf):
    def conv_prev(k, g):
        return fb_ref[1, :, k * cf:(k + 1) * cf], fb_ref[0, :, k * cf:(k + 1) * cf]
    return conv_prev


def _post_ab_dec_kernel(y_ref, a_ref, r_ref, fb_ref, woa_ref, wob_ref, gf_ref, wup_ref, cw_ref, cb_ref, wdn_ref,
                        o_ref, g_ref, *, n_split):
    y1 = y_ref[...] + jnp.dot(a_ref[...], woa_ref[...], preferred_element_type=F32) \
        + jnp.dot(r_ref[...], wob_ref[...], preferred_element_type=F32)
    cf = wdn_ref.shape[0] // n_split
    out, gates = _ffn_tile(y1, gf_ref, wup_ref, cw_ref, cb_ref, wdn_ref, _dec_conv_prev(fb_ref, cf), n_split)
    o_ref[...] = y1 + out
    for k, g in enumerate(gates):
        g_ref[:, k * cf:(k + 1) * cf] = g


def _post_ab_dec(y, attn, rgo, fbuf_t, wo_a, wo_b, ffn, *, n_split=2):
    m, d = y.shape
    d_ff = ffn["wdn"].shape[1]
    args = (y, attn, rgo, fbuf_t, wo_a, wo_b)
    return pl.pallas_call(
        functools.partial(_post_ab_dec_kernel, n_split=n_split),
        grid=(1,),
        in_specs=[_full_spec(a.shape) for a in args] + _ffn_specs(ffn),
        out_specs=[_whole_spec((m, d)), _whole_spec((m, d_ff))],
        out_shape=[jax.ShapeDtypeStruct((m, d), F32), jax.ShapeDtypeStruct((m, d_ff), F32)],
        compiler_params=_cparams(1),
        name="post_ab_dec",
    )(*args, *_ffn_args(ffn))


def _gmlp_in(y, gm_ref, win_ref, bin_ref, sn_ref):
    d_c = win_ref.shape[1] // 2
    z = _gelu(_dot(_rms(y, gm_ref[...]), win_ref[...]) + bin_ref[...])
    return z[:, :d_c], _rms(z[:, d_c:], sn_ref[...])


def _layer_c_prompt_kernel(y_ref, fb_ref, gm_ref, win_ref, bin_ref, sn_ref, sw_ref, sbt_ref, woc_ref,
                           gf_ref, wup_ref, cw_ref, cb_ref, wdn_ref, gfin_ref,
                           o_ref, nb_ref, gs, carry, *, tm, n_split):
    y = y_ref[0]
    u, v = _gmlp_in(y, gm_ref, win_ref, bin_ref, sn_ref)
    vb = v.astype(BF16)
    n_groups = sw_ref.shape[0]
    tril = lax.broadcasted_iota(I32, (CHUNK, CHUNK), 0) >= lax.broadcasted_iota(I32, (CHUNK, CHUNK), 1)
    wm = [jnp.where(tril, sw_ref[gi], 0.0).astype(BF16) for gi in range(n_groups)]
    rows = []
    for r in range(tm // CHUNK):
        cols = []
        for gi in range(n_groups):
            mixed = jnp.dot(wm[gi], vb[r * CHUNK:(r + 1) * CHUNK, gi * LANES:(gi + 1) * LANES],
                            preferred_element_type=F32)
            cols.append(mixed + sbt_ref[:, gi:gi + 1])
        rows.append(jnp.concatenate(cols, axis=1))
    gated = u * jnp.concatenate(rows, axis=0)
    y1 = y + _dot(gated, woc_ref[...])
    cf = wdn_ref.shape[0] // n_split
    out, _ = _ffn_tile(y1, gf_ref, wup_ref, cw_ref, cb_ref, wdn_ref,
                       _prompt_conv_prev(gs, carry, fb_ref, nb_ref, tm, cf), n_split)
    o_ref[0] = _rms(y1 + out, gfin_ref[...])


def _layer_c_prompt(y, fbuf, cp, ffn, g_final, *, tm=512, n_split=2):
    batch, seq, d = y.shape
    d_ff = ffn["wdn"].shape[1]
    cf = d_ff // n_split
    blk = pl.BlockSpec((1, tm, d), lambda b, t: (b, t, 0))
    fb = pl.BlockSpec((1, 2, d_ff), lambda b, t: (b, 0, 0))
    consts = [cp["g"], cp["win"], cp["bin"], cp["sn"], cp["sw"], cp["sbt"], cp["woc"]]
    return pl.pallas_call(
        functools.partial(_layer_c_prompt_kernel, tm=tm, n_split=n_split),
        grid=(batch, seq // tm),
        in_specs=[blk, fb] + [_full_spec(c.shape) for c in consts] + _ffn_specs(ffn) + [_full_spec(g_final.shape)],
        out_specs=[blk, fb],
        out_shape=[jax.ShapeDtypeStruct((batch, seq, d), F32), jax.ShapeDtypeStruct((batch, 2, d_ff), F32)],
        scratch_shapes=[pltpu.VMEM((tm + 8, cf), F32), pltpu.VMEM((n_split, 8, cf), F32)],
        compiler_params=_cparams(2),
        name="layer_c_prompt",
    )(y, fbuf, *consts, *_ffn_args(ffn), g_final)


def _layer_c_dec_kernel(y_ref, fb_ref, gm_ref, win_ref, bin_ref, sn_ref, sw0_ref, sb0_ref, woc_ref,
                        gf_ref, wup_ref, cw_ref, cb_ref, wdn_ref, gfin_ref, o_ref, g_ref, v_ref, *, n_split):
    y = y_ref[...]
    u, v = _gmlp_in(y, gm_ref, win_ref, bin_ref, sn_ref)
    v_ref[...] = v
    y1 = y + _dot(u * (sw0_ref[...] * v + sb0_ref[...]), woc_ref[...])
    cf = wdn_ref.shape[0] // n_split
    out, gates = _ffn_tile(y1, gf_ref, wup_ref, cw_ref, cb_ref, wdn_ref, _dec_conv_prev(fb_ref, cf), n_split)
    o_ref[...] = _rms(y1 + out, gfin_ref[...])
    for k, g in enumerate(gates):
        g_ref[:, k * cf:(k + 1) * cf] = g


def _layer_c_dec(y, fbuf_t, cp, ffn, g_final, *, n_split=2):
    m, d = y.shape
    d_ff = ffn["wdn"].shape[1]
    d_c = cp["woc"].shape[0]
    args = (y, fbuf_t, cp["g"], cp["win"], cp["bin"], cp["sn"], cp["sw0"], cp["sb0"], cp["woc"])
    return pl.pallas_call(
        functools.partial(_layer_c_dec_kernel, n_split=n_split),
        grid=(1,),
        in_specs=[_full_spec(a.shape) for a in args] + _ffn_specs(ffn) + [_full_spec(g_final.shape)],
        out_specs=[_whole_spec((m, d)), _whole_spec((m, d_ff)), _whole_spec((m, d_c))],
        out_shape=[jax.ShapeDtypeStruct((m, d), F32), jax.ShapeDtypeStruct((m, d_ff), F32),
                   jax.ShapeDtypeStruct((m, d_c), F32)],
        compiler_params=_cparams(1),
        name="layer_c_dec",
    )(*args, *_ffn_args(ffn), g_final)


def _start_page_copy(src_ref, pt_ref, b, dst_ref, sem, pg, page, priority=0):
    col = pl.multiple_of(pg * page, page)
    pltpu.make_async_copy(src_ref.at[pt_ref[b, pg]], dst_ref.at[:, pl.ds(col, page)], sem).start(priority=priority)


def _start_page_copies(src_ref, pt_ref, b, dst_ref, sem, n_pages, page):
    def body(pg, carry):
        _start_page_copy(src_ref, pt_ref, b, dst_ref, sem, pg, page)
        return carry
    lax.fori_loop(0, n_pages, body, 0, unroll=8)


def _wait_page_copies(dst_ref, sem):
    pltpu.make_async_copy(dst_ref, dst_ref, sem).wait()


def _dec_score_kernel(pt_ref, qi_ref, wi_ref, ixn_ref, cik_ref, keys_ref, knew_ref, ibuf, sems, *, n_pages, page):
    b = pl.program_id(0)
    nb = pl.num_programs(0)
    slot = lax.rem(b, 2)

    def start(bb, sl):
        _start_page_copies(cik_ref, pt_ref, bb, ibuf.at[sl], sems.at[sl], n_pages, page)

    @pl.when(b == 0)
    def _first():
        start(0, 0)

    @pl.when(b + 1 < nb)
    def _prefetch_next():
        start(b + 1, 1 - slot)

    _wait_page_copies(ibuf.at[slot], sems.at[slot])
    qi = qi_ref[0].astype(BF16)
    wi = wi_ref[0]
    s = jnp.dot(qi, ibuf[slot].astype(BF16), preferred_element_type=F32)
    sc = jnp.sum(jnp.maximum(s, 0.0) * wi, axis=0, keepdims=True) * IDX_HEADS ** -0.5
    keys_ref[0] = sc
    kin = ixn_ref[0][:, 0:IDX_DIM]
    sn = jnp.sum(qi_ref[0] * kin, axis=1, keepdims=True)
    scn = jnp.sum(jnp.maximum(sn, 0.0) * wi, axis=0, keepdims=True) * IDX_HEADS ** -0.5
    knew_ref[0] = jnp.broadcast_to(scn, (1, LANES))


def _dec_scores(page_table, qi3, wi3, ix3, cik_t, *, page):
    db, n_pages = page_table.shape
    past = n_pages * page
    return pl.pallas_call(
        functools.partial(_dec_score_kernel, n_pages=n_pages, page=page),
        grid_spec=pltpu.PrefetchScalarGridSpec(
            num_scalar_prefetch=1,
            grid=(db,),
            in_specs=[pl.BlockSpec((1, IDX_HEADS, IDX_DIM), lambda b, pt: (b, 0, 0)),
                      pl.BlockSpec((1, IDX_HEADS, 1), lambda b, pt: (b, 0, 0)),
                      pl.BlockSpec((1, 1, LANES), lambda b, pt: (b, 0, 0)),
                      pl.BlockSpec(memory_space=pl.ANY)],
            out_specs=[pl.BlockSpec((1, 1, past), lambda b, pt: (b, 0, 0)),
                       pl.BlockSpec((1, 1, LANES), lambda b, pt: (b, 0, 0))],
            scratch_shapes=[pltpu.VMEM((2, IDX_DIM, past), F32), pltpu.SemaphoreType.DMA((2,))]),
        out_shape=[jax.ShapeDtypeStruct((db, 1, past), F32), jax.ShapeDtypeStruct((db, 1, LANES), F32)],
        compiler_params=_cparams(1),
        name="dec_scores",
    )(page_table, qi3, wi3, ix3, cik_t)


def _dec_select_kernel(sc_ref, scn_ref, so_ref, sno_ref, thr_ref, *, topk):
    past = sc_ref.shape[1]
    lane0 = lax.broadcasted_iota(I32, scn_ref.shape, 1) == 0
    sc = jnp.concatenate([sc_ref[...], jnp.where(lane0, scn_ref[...], -jnp.inf)], axis=1)
    kk = _float_key(sc)
    kf = jnp.float32(topk)

    def count(pred):
        ones = jnp.where(pred, 1.0, 0.0)
        accs = [ones[:, k * LANES:(k + 1) * LANES] for k in range(8)]
        for k in range(8, ones.shape[1] // LANES):
            accs[k % 8] = accs[k % 8] + ones[:, k * LANES:(k + 1) * LANES]
        return jnp.sum(functools.reduce(lambda x, y: x + y, accs), axis=1, keepdims=True)

    def search_body(i, ans):
        cand = ans | jnp.left_shift(jnp.int32(1), 31 - i)
        return jnp.where(count(kk >= (cand ^ jnp.int32(INT_MIN))) >= kf, cand, ans)
    ans = lax.fori_loop(0, 32, search_body, jnp.zeros((sc.shape[0], 1), I32))
    thr = _key_float(ans ^ jnp.int32(INT_MIN))
    counts = lambda t: (count(sc >= t), count(sc > t))
    thr, cge, cgt = _walk_to_kth(lambda: sc, thr, *counts(thr), kf, counts, 4 * topk)
    need = kf - cgt
    big = jnp.int32(4 * past)
    eqcol = jnp.where(sc == thr, lax.broadcasted_iota(I32, sc.shape, 1), big)
    nbits = int(math.log2(past)) + 1

    def tie_body(i, best):
        cand = best | jnp.left_shift(jnp.int32(1), nbits - 1 - i)
        return jnp.where(count(eqcol < cand) < need, cand, best)
    last = lax.fori_loop(0, nbits, tie_body, jnp.zeros((sc.shape[0], 1), I32))
    sc = jnp.where((eqcol > last) & (eqcol < big), -jnp.inf, sc)
    so_ref[...] = sc[:, :past]
    sno_ref[...] = sc[:, past:]
    thr_ref[...] = jnp.broadcast_to(thr, thr_ref.shape)


def _dec_select(scores, snew, *, topk):
    db, past = scores.shape
    assert past + 1 >= topk
    return pl.pallas_call(
        functools.partial(_dec_select_kernel, topk=topk),
        out_shape=[jax.ShapeDtypeStruct((db, past), F32), jax.ShapeDtypeStruct((db, LANES), F32),
                   jax.ShapeDtypeStruct((db, LANES), F32)],
        name="dec_select",
    )(scores, snew)


def _dec_attn_row(qm, kvn, sc, sn, thr_row, bias, kbuf, vbuf, page):
    past = kbuf.shape[1]
    thr = thr_row[:, 0:1]
    sel = sc >= thr
    sel_new = sn[:, 0:1] >= thr
    far, last, bnew = bias[:, LANES:LANES + 1], bias[:, 0:page], bias[:, LANES + 1:LANES + 2]
    step = min(past, DEC_KEY_CHUNK)
    chunks = [(c, min(c + step, past)) for c in range(0, past, step)]
    qb = qm.astype(BF16)
    parts = []
    for c0, c1 in chunks:
        part = jnp.dot(qb, kbuf[:, c0:c1].astype(BF16), preferred_element_type=F32) + far
        if c1 == past:
            part = jnp.concatenate([part[:, :c1 - c0 - page], part[:, c1 - c0 - page:] + (last - far)], axis=1)
        parts.append(jnp.where(sel[:, c0:c1], part, -jnp.inf))
    lgn = jnp.sum(qm * kvn[:, 0:LANES], axis=1, keepdims=True) + bnew
    lgn = jnp.where(sel_new, lgn, -jnp.inf)
    m = lgn
    for part in parts:
        m = jnp.maximum(m, jnp.max(part, axis=1, keepdims=True))
    en = jnp.exp(lgn - m)
    den = en
    pv = en * kvn[:, LANES:2 * LANES]
    for (c0, c1), part in zip(chunks, parts):
        e = jnp.exp(part - m)
        den = den + jnp.sum(e, axis=1, keepdims=True)
        pv = pv + _dot_nt(e.astype(BF16), vbuf[:, c0:c1].astype(BF16))
    pv = pv / den
    lo = lax.broadcasted_iota(I32, (1, LANES), 1) < HEAD_DIM
    return jnp.concatenate([jnp.where(lo, pv[2 * p:2 * p + 1], pv[2 * p + 1:2 * p + 2]) for p in range(4)],
                           axis=1).astype(BF16)


def _prep_in_ab(w):
    d = w.shape[0]
    nq, nkv = N_HEADS * HEAD_DIM, N_KV_HEADS * HEAD_DIM
    offs = np.cumsum([nq, nkv, nkv, IDX_HEADS * IDX_DIM, IDX_DIM, IDX_HEADS, 512])
    q, k, v, qi, ki, wi, g, xr = jnp.split(w, offs.tolist(), axis=1)
    q = q.reshape(d, N_HEADS, HEAD_DIM)[:, np.array(HEAD_PERM), :].reshape(d, nq)
    pad = jnp.zeros((d, _C_G - _C_IX - IDX_DIM - IDX_HEADS), w.dtype)
    return jnp.concatenate([q, k, v, qi, ki, wi, pad, g, xr], axis=1).astype(BF16)


def _block_diag(w):
    n, c, _ = w.shape
    return (jnp.eye(n, dtype=w.dtype)[:, None, :, None] * w[:, :, None, :]).reshape(n * c, n * c).astype(BF16)


def _ffn_params(layer, stacked):
    return dict(stacked, layer=layer)


def kernel(x_prompt, x_sample, cache_k, cache_v, cache_idx_k, state_rglru_h, state_rglru_conv, state_ffn_conv,
           page_table, norm_mix, norm_ffn, norm_final, rel_bias, w_in_ab, w_out_ab, rg_conv_w, rg_conv_b,
           rg_wa, rg_ba, rg_wx, rg_bx, rg_lambda, w_in_c, b_in_c, sgu_norm, sgu_w, sgu_b, w_out_c,
           ffn_w_up, ffn_conv_w, ffn_conv_b, ffn_w_down):
    batch, seq, d = x_prompt.shape
    db = x_sample.shape[0]
    page = cache_k.shape[2]
    d_a = N_HEADS * HEAD_DIM
    d_b = rg_conv_w.shape[-1]
    d_ff = ffn_w_down.shape[1]
    assert x_sample.shape[1] == 1 and seq % 512 == 0 and page == LANES and w_in_ab.shape[0] == 1

    w_in0 = _prep_in_ab(w_in_ab[0])
    wo = w_out_ab[0]
    wo_a = wo[:d_a].reshape(N_HEADS, HEAD_DIM, d)[np.array(HEAD_PERM)].reshape(d_a, d).astype(BF16)
    wo_b = wo[d_a:].astype(BF16)
    rg = {"cw": rg_conv_w[0], "cb": rg_conv_b[0][None], "wa": _block_diag(rg_wa[0]), "ba": rg_ba[0][None],
          "wx": _block_diag(rg_wx[0]), "bx": rg_bx[0][None], "lam": rg_lambda[0][None]}
    ffn_all = {"g": norm_ffn[:, None, :], "wup": ffn_w_up.astype(BF16), "cw": ffn_conv_w, "cb": ffn_conv_b[:, None, :],
               "wdn": ffn_w_down.astype(BF16)}
    ffn0, ffn1 = _ffn_params(0, ffn_all), _ffn_params(1, ffn_all)
    cp = {"g": norm_mix[1][None], "win": w_in_c[0].astype(BF16), "bin": b_in_c[0][None], "sn": sgu_norm[0][None],
          "sw": sgu_w[0], "sbt": sgu_b[0].T, "woc": w_out_c[0].astype(BF16),
          "sw0": jnp.repeat(sgu_w[0][:, 0, 0], d // sgu_w.shape[1])[None],
          "sb0": jnp.repeat(sgu_b[0][:, 0], d // sgu_w.shape[1])[None]}
    g_mix0 = norm_mix[0][None]
    g_final = norm_final[None]
    bias_st, bias_dec = _bias_tables(rel_bias, page)

    xp = x_prompt.reshape(batch * seq, d)
    xs = x_sample.reshape(db, d)
    q_st, qi_st, kv_p, ix_p, gate_p, xr_p = _inproj(xp, g_mix0, w_in0, stack=True, tm=512)
    qm_s, qi_s, kv_s, ix_s, gate_s, xr_s = _inproj(xs, g_mix0, w_in0, stack=False, tm=db)
    cik_t = jnp.transpose(cache_idx_k[0], (0, 2, 1))
    ck_t = jnp.transpose(cache_k[0], (0, 2, 3, 1)).reshape(-1, 2 * HEAD_DIM, page)
    cv_t = jnp.transpose(cache_v[0], (0, 2, 3, 1)).reshape(-1, 2 * HEAD_DIM, page)
    topk_s = min(TOPK_MAX, (page_table.shape[1] * page + 1) // 4)
    sc_s, sn_s = _dec_scores(page_table, qi_s.reshape(db, IDX_HEADS, IDX_DIM),
                             ix_s[:, IDX_DIM:IDX_DIM + IDX_HEADS].reshape(db, IDX_HEADS, 1),
                             ix_s.reshape(db, 1, LANES), cik_t, page=page)
    sc_s, sn_s, thr_s = _dec_select(sc_s.reshape(db, -1), sn_s.reshape(db, LANES), topk=topk_s)

    attn_p, attn_s = _attn(q_st, qi_st, ix_p, kv_p, bias_st, page_table, jnp.transpose(qm_s, (1, 0, 2)),
                           kv_s.reshape(db, 1, 2 * LANES), sc_s.reshape(db, 1, -1), sn_s.reshape(db, 1, LANES),
                           thr_s.reshape(db, 1, LANES), bias_dec, ck_t, cv_t, batch=batch, seq=seq, page=page)
    attn_s = attn_s.reshape(db, d_a)

    rg_p, h_p, cbuf_p = _rglru_prompt(gate_p, xr_p, jnp.zeros((batch, rg["cw"].shape[0] - 1, d_b), F32),
                                      jnp.zeros((batch, d_b), F32), rg, batch=batch, seq=seq)
    zero_fb = jnp.zeros((batch, 2, d_ff), F32)
    y1_p, fb0_p = _post_ab_prompt(x_prompt, attn_p, rg_p, zero_fb, wo_a, wo_b, ffn0)
    y_p, fb1_p = _layer_c_prompt(y1_p, zero_fb, cp, ffn1, g_final)

    cbuf_s_in = state_rglru_conv[0]
    rg_s, h_s = _rglru_dec(gate_s, xr_s, jnp.transpose(cbuf_s_in, (1, 0, 2)), state_rglru_h[0], rg)
    y1_s, g0_s = _post_ab_dec(xs, attn_s, rg_s, jnp.transpose(state_ffn_conv[0], (1, 0, 2)), wo_a, wo_b, ffn0)
    y_s, g1_s, v_s = _layer_c_dec(y1_s, jnp.transpose(state_ffn_conv[1], (1, 0, 2)), cp, ffn1, g_final)

    kv4 = kv_p.reshape(batch, seq, 2, N_KV_HEADS, HEAD_DIM)
    kvs = kv_s.reshape(db, 1, 2, N_KV_HEADS, HEAD_DIM)
    fbuf_s = lambda layer, g: jnp.concatenate([state_ffn_conv[layer][:, 1:], g[:, None]], axis=1)
    return (y_p, y_s.reshape(db, 1, d),
            kv4[None, :, :, 0], kv4[None, :, :, 1], ix_p.reshape(batch, seq, LANES)[None, :, :, :IDX_DIM],
            kvs[None, :, :, 0], kvs[None, :, :, 1], ix_s.reshape(db, 1, LANES)[None, :, :, :IDX_DIM],
            h_p.reshape(batch, d_b)[None], cbuf_p[None],
            h_s[None], jnp.concatenate([cbuf_s_in[:, 1:], xr_s[:, None]], axis=1)[None],
            v_s.reshape(db, 1, -1)[None],
            jnp.stack([fb0_p, fb1_p]), jnp.stack([fbuf_s(0, g0_s), fbuf_s(1, g1_s)]))
```

```python
import functools
import math

import numpy as np
import jax
import jax.numpy as jnp
from jax import lax
from jax.experimental import pallas as pl
from jax.experimental.pallas import tpu as pltpu

F32 = jnp.float32
BF16 = jnp.bfloat16
I32 = jnp.int32

N_HEADS = 8
HEAD_DIM = 64
N_KV_HEADS = 2
Q_PER_KV = N_HEADS // N_KV_HEADS
IDX_HEADS = 8
IDX_DIM = 64
TOPK_MAX = 256
N_BUCKETS = 32
REL_MAX_EXACT = N_BUCKETS // 2
REL_MAX_DIST = 128
RG_C = 8.0
CHUNK = 128
EPS = 1e-6

LANES = 128
QB = 128
DEC_KEY_CHUNK = 2048
INT_MIN = -(2 ** 31)
KEY_MIN_FINITE = INT_MIN + 0x800000
NEG_MAX = float(np.finfo(np.float32).min)
HEAD_PERM = (0, 4, 1, 5, 2, 6, 3, 7)
VMEM_LIMIT = 56 * 1024 * 1024


def _cparams(n_grid):
    return pltpu.CompilerParams(dimension_semantics=("arbitrary",) * n_grid, vmem_limit_bytes=VMEM_LIMIT)


def _full_spec(shape):
    nd = len(shape)
    return pl.BlockSpec(shape, lambda *_: (0,) * nd, pipeline_mode=pl.Buffered(1))


def _whole_spec(shape):
    nd = len(shape)
    return pl.BlockSpec(shape, lambda *_: (0,) * nd)


def _rms(x, g):
    return x * lax.rsqrt(jnp.mean(x * x, axis=-1, keepdims=True) + EPS) * g


def _gelu(x):
    return x * (0.5 * (1.0 + jnp.tanh(math.sqrt(2.0 / math.pi) * (x + 0.044715 * (x * x * x)))))


def _sigmoid(x):
    return 1.0 / (1.0 + jnp.exp(-x))


def _softplus(x):
    return jnp.maximum(x, 0.0) + jnp.log(1.0 + jnp.exp(-jnp.abs(x)))


def _dot(a, b):
    return jnp.dot(a.astype(BF16), b, preferred_element_type=F32)


def _dot_nt(a, b):
    return lax.dot_general(a, b, (((1,), (1,)), ((), ())), preferred_element_type=F32)


def _float_key(x):
    bits = pltpu.bitcast(x, I32)
    key = jnp.where(bits < 0, bits ^ jnp.int32(0x7FFFFFFF), bits)
    return jnp.where(bits == jnp.int32(INT_MIN), jnp.int32(0), key)


def _key_float(key):
    key = jnp.maximum(key, jnp.int32(KEY_MIN_FINITE))
    return pltpu.bitcast(jnp.where(key < 0, key ^ jnp.int32(0x7FFFFFFF), key), F32)


def _walk_to_kth(scores, t, cge, cgt, kf, count_ge_gt, max_steps):
    axis = 0 if t.shape[0] == 1 else 1

    def settled(t, cge, cgt):
        return (cgt < kf) & ((cge >= kf) | (t <= NEG_MAX))

    def unsettled(t, cge, cgt):
        return jnp.max(jnp.where(settled(t, cge, cgt), 0.0, 1.0)) > 0.0

    def body(st):
        t, cge, cgt, it = st
        sc = scores()
        below = jnp.max(jnp.where(sc < t, sc, -jnp.inf), axis=axis, keepdims=True)
        above = jnp.min(jnp.where(sc > t, sc, jnp.inf), axis=axis, keepdims=True)
        t = jnp.where(cgt >= kf, above, jnp.where((cge < kf) & (t > NEG_MAX), jnp.maximum(below, NEG_MAX), t))
        cge, cgt = count_ge_gt(t)
        return t, cge, cgt, it + 1

    t, cge, cgt, _ = lax.while_loop(lambda st: unsettled(st[0], st[1], st[2]) & (st[3] < max_steps), body,
                                    (t, cge, cgt, jnp.int32(0)))
    return t, cge, cgt


def _t5_bucket_np(n):
    n = np.maximum(n, 0)
    nf = np.maximum(n, 1).astype(np.float32)
    large = REL_MAX_EXACT + (np.log(nf / np.float32(REL_MAX_EXACT)) / np.float32(math.log(REL_MAX_DIST / REL_MAX_EXACT))
                             * np.float32(N_BUCKETS - REL_MAX_EXACT)).astype(np.int32)
    large = np.minimum(large, N_BUCKETS - 1)
    return np.where(n < REL_MAX_EXACT, n, large).astype(np.int32)


_C_Q, _C_KV, _C_QI, _C_IX, _C_G, _C_X, _C_END = 0, 512, 768, 1280, 1408, 1920, 2432


def _inproj_kernel(x_ref, g_ref, w_ref, q_ref, qi_ref, kv_ref, ix_ref, gate_ref, xr_ref, *, stack):
    hn = _rms(x_ref[...], g_ref[...])
    z = _dot(hn, w_ref[...])
    q = z[:, _C_Q:_C_KV] * HEAD_DIM ** -0.5
    qi = z[:, _C_QI:_C_IX] * IDX_DIM ** -0.5
    kv_ref[...] = z[:, _C_KV:_C_QI]
    ix_ref[...] = z[:, _C_IX:_C_G]
    gate_ref[...] = z[:, _C_G:_C_X]
    xr_ref[...] = z[:, _C_X:_C_END]
    if stack:
        qb, qib = q.astype(BF16), qi.astype(BF16)
        for r in range(q.shape[0] // QB):
            for p in range(4):
                q_ref[r, p * QB:(p + 1) * QB, :] = qb[r * QB:(r + 1) * QB, p * LANES:(p + 1) * LANES]
                qi_ref[r, p * QB:(p + 1) * QB, :] = qib[r * QB:(r + 1) * QB, p * LANES:(p + 1) * LANES]
    else:
        lo = lax.broadcasted_iota(I32, (q.shape[0], LANES), 1) < HEAD_DIM
        for p in range(4):
            qp = q[:, p * LANES:(p + 1) * LANES]
            q_ref[2 * p] = jnp.where(lo, qp, 0.0)
            q_ref[2 * p + 1] = jnp.where(lo, 0.0, qp)
        qi_ref[...] = qi


def _inproj(x2d, g, w, *, stack, tm):
    m, d = x2d.shape
    if stack:
        q_shape, q_spec = (m // QB, 4 * QB, LANES), pl.BlockSpec((tm // QB, 4 * QB, LANES), lambda i: (i, 0, 0))
        qi_shape, qi_spec, qdt = q_shape, q_spec, BF16
    else:
        q_shape, q_spec = (N_HEADS, m, LANES), pl.BlockSpec((N_HEADS, tm, LANES), lambda i: (0, i, 0))
        qi_shape, qi_spec, qdt = (m, 512), pl.BlockSpec((tm, 512), lambda i: (i, 0)), F32
    row = lambda n: pl.BlockSpec((tm, n), lambda i: (i, 0))
    return pl.pallas_call(
        functools.partial(_inproj_kernel, stack=stack),
        grid=(m // tm,),
        in_specs=[row(d), _full_spec((1, d)), _full_spec(w.shape)],
        out_specs=[q_spec, qi_spec, row(256), row(128), row(512), row(512)],
        out_shape=[jax.ShapeDtypeStruct(q_shape, qdt), jax.ShapeDtypeStruct(qi_shape, qdt),
                   jax.ShapeDtypeStruct((m, 256), F32), jax.ShapeDtypeStruct((m, 128), F32),
                   jax.ShapeDtypeStruct((m, 512), F32), jax.ShapeDtypeStruct((m, 512), F32)],
        compiler_params=_cparams(1),
        name="inproj_stack" if stack else "inproj_dec",
    )(x2d, g, w)


def _bias_kernel(rb_ref, bk_ref, bkd_ref, o_ref, od_ref):
    for d in range(3):
        bk = bk_ref[d]
        for p in range(4):
            for a in range(2):
                h = p + 4 * a
                acc = jnp.zeros((QB, LANES), F32)
                for b in range(N_BUCKETS):
                    acc = jnp.where(bk == b, rb_ref[b, h], acc)
                o_ref[d, a * QB:(a + 1) * QB, p * LANES:(p + 1) * LANES] = acc
    bkd = bkd_ref[...]
    rowi = lax.broadcasted_iota(I32, (N_HEADS, 2 * LANES), 0)
    acc = jnp.zeros((N_HEADS, 2 * LANES), F32)
    for r in range(N_HEADS):
        h = r // 2 + 4 * (r % 2)
        for b in range(N_BUCKETS):
            acc = jnp.where((rowi == r) & (bkd == b), rb_ref[b, h], acc)
    od_ref[...] = acc


def _bias_tables(rel_bias, page):
    key = np.arange(QB)[:, None]
    qry = np.arange(LANES)[None, :]
    bk = np.stack([_t5_bucket_np(d * QB + qry - key) for d in range(3)])
    assert (_t5_bucket_np(np.arange(2 * QB + 1 - LANES, 4 * QB)) == N_BUCKETS - 1).all()
    assert (_t5_bucket_np(np.arange(page, 8 * page)) == N_BUCKETS - 1).all()
    dec = np.zeros((2 * LANES,), np.int64)
    dec[:page] = page - np.arange(page)
    dec[LANES] = 2 * REL_MAX_DIST
    dec[LANES + 1] = 0
    bkd = np.broadcast_to(_t5_bucket_np(dec)[None, :], (N_HEADS, 2 * LANES))
    return pl.pallas_call(
        _bias_kernel,
        in_specs=[pl.BlockSpec(memory_space=pltpu.SMEM), pl.BlockSpec(memory_space=pltpu.VMEM),
                  pl.BlockSpec(memory_space=pltpu.VMEM)],
        out_shape=[jax.ShapeDtypeStruct((3, 2 * QB, 4 * LANES), F32), jax.ShapeDtypeStruct((N_HEADS, 2 * LANES), F32)],
        name="bias_tables",
    )(rel_bias, jnp.asarray(bk, I32), jnp.asarray(bkd, I32))


def _search_widths(n_chunks):
    cuts = sorted({min(c, n_chunks) for c in (2, 4, 8, 12, 16)} | {n_chunks})
    return [c for c in cuts if c <= n_chunks]


def _attn_kernel(pt_ref, q_ref, qi_ref, ixq_ref, ixk_ref, kv_ref, bias_ref,
                 dq_ref, dkvn_ref, dsc_ref, dsn_ref, dthr_ref, dbias_ref, ck_ref, cv_ref,
                 o_ref, od_ref,
                 kblk, vblk_t, kiblk, keys, scores, logits, acc, thr_ref, cge_ref, cgt_ref, kbuf, vbuf, sems,
                 *, n_chunks, topk, n_pages, page, steps_per_seq):
    j = pl.program_id(1)
    step = pl.program_id(0) * n_chunks + j
    seq_s = lax.div(step, jnp.int32(steps_per_seq))
    phase = lax.rem(step, jnp.int32(steps_per_seq))

    pages_per_trip = -(-n_pages // 32)

    def fetch_sample_pages(trip):
        @pl.when(phase == 0)
        def _():
            for u in range(pages_per_trip):
                pg = trip * pages_per_trip + u

                def start(pg=pg):
                    _start_page_copy(ck_ref, pt_ref, seq_s, kbuf, sems.at[0], pg, page, priority=1)
                    _start_page_copy(cv_ref, pt_ref, seq_s, vbuf, sems.at[1], pg, page, priority=1)
                if 32 * pages_per_trip == n_pages:
                    start()
                else:
                    pl.when(pg < n_pages)(start)

    lane = lax.broadcasted_iota(I32, (QB, LANES), 1)
    row = lax.broadcasted_iota(I32, (QB, LANES), 0)
    lo = lane < HEAD_DIM
    blocks = [(a, p) for a in range(2) for p in range(4)]
    rs = lambda a: slice(a * QB, (a + 1) * QB)
    cs = lambda p: slice(p * LANES, (p + 1) * LANES)
    chunk = lambda c: pl.ds(pl.multiple_of(c * QB, QB), QB)

    @pl.when(j == 0)
    def _build_block_diagonal_keys():
        def body(c, carry):
            kc = kv_ref[0, chunk(c), 0:LANES]
            vt = kv_ref[0, chunk(c), LANES:2 * LANES].T
            kia = jnp.where(lo, ixk_ref[0, chunk(c), :], 0.0)
            kblk[c, 0:QB, :] = jnp.where(lo, kc, 0.0).astype(BF16)
            kblk[c, QB:2 * QB, :] = jnp.where(lo, 0.0, kc).astype(BF16)
            vblk_t[c, :, 0:QB] = jnp.where(row < HEAD_DIM, vt, 0.0).astype(BF16)
            vblk_t[c, :, QB:2 * QB] = jnp.where(row < HEAD_DIM, 0.0, vt).astype(BF16)
            kiblk[c, 0:QB, :] = kia.astype(BF16)
            kiblk[c, QB:2 * QB, :] = pltpu.roll(kia, HEAD_DIM, 1).astype(BF16)
            return carry
        lax.fori_loop(0, n_chunks, body, 0)

    qi = qi_ref[0]
    q = q_ref[0]
    wt = ixq_ref[0].T
    w_row = {(a, p): wt[IDX_DIM + 2 * p + a:IDX_DIM + 2 * p + a + 1, :] for a, p in blocks}
    qpos = j * QB + lane

    n_pairs = (j + 2) // 2

    def chunk_loop(body, carry):
        n_quads = n_pairs // 2
        carry = lax.fori_loop(0, n_quads, lambda i, cr: body([4 * i + u for u in range(4)], cr), carry)
        return lax.fori_loop(2 * n_quads, n_pairs, lambda i, cr: body([2 * i, 2 * i + 1], cr), carry)

    def score_body(cs_, carry):
        for c in cs_:
            s = _dot_nt(kiblk[c], qi)
            lg = _dot_nt(kblk[c], q)
            sc = jnp.zeros((QB, LANES), F32)
            for a, p in blocks:
                sc = sc + jnp.maximum(s[rs(a), cs(p)], 0.0) * w_row[(a, p)]
            sc = sc * IDX_HEADS ** -0.5
            admissible = c * QB + row <= qpos
            scores[chunk(c), :] = jnp.where(admissible, sc, -jnp.inf)
            keys[chunk(c), :] = jnp.where(admissible, _float_key(sc), jnp.int32(INT_MIN))
            logits[c] = lg + bias_ref[jnp.clip(j - c, 0, 2)]
        return carry
    chunk_loop(score_body, 0)

    def fill_body(c, carry):
        scores[chunk(c), :] = jnp.full((QB, LANES), -jnp.inf, F32)
        keys[chunk(c), :] = jnp.full((QB, LANES), INT_MIN, I32)
        return carry
    lax.fori_loop(2 * n_pairs, n_chunks, fill_body, 0)

    kf = jnp.float32(topk)

    def count(src, width, pred):
        accs = [jnp.zeros((8, LANES), F32) for _ in range(8)]
        for g in range(width // 8):
            accs[g % 8] = accs[g % 8] + jnp.where(pred(src[g * 8:(g + 1) * 8, :]), 1.0, 0.0)
        return jnp.sum(functools.reduce(lambda x, y: x + y, accs), axis=0, keepdims=True)

    def search(width):
        def search_body(i, ans):
            fetch_sample_pages(i)
            cand = ans | jnp.left_shift(jnp.int32(1), 31 - i)
            cs_ = cand ^ jnp.int32(INT_MIN)
            return jnp.where(count(keys, width, lambda k: k >= cs_) >= kf, cand, ans)
        ans = lax.fori_loop(0, 32, search_body, jnp.zeros((1, LANES), I32))
        t = _key_float(ans ^ jnp.int32(INT_MIN))
        thr_ref[...] = t
        cge_ref[...] = count(scores, width, lambda s: s >= t)
        cgt_ref[...] = count(scores, width, lambda s: s > t)

    prev = 0
    for n in _search_widths(n_chunks):
        pl.when((j >= prev) & (j < n))(functools.partial(search, n * QB))
        prev = n

    s_len = scores.shape[0]
    thr, cge, cgt = _walk_to_kth(
        lambda: scores[...], thr_ref[...], cge_ref[...], cgt_ref[...], kf,
        lambda t: (count(scores, s_len, lambda s: s >= t), count(scores, s_len, lambda s: s > t)), 4 * topk)

    @pl.when(jnp.max(cge) > kf)
    def _break_ties_by_position():
        need = kf - cgt
        big = jnp.int32(2 * s_len)
        eqrow_ref = keys
        eqrow_ref[...] = jnp.where(scores[...] == thr, lax.broadcasted_iota(I32, scores.shape, 0), big)
        nbits = int(math.log2(s_len))

        def tie_body(i, best):
            cand = best | jnp.left_shift(jnp.int32(1), nbits - 1 - i)
            return jnp.where(count(eqrow_ref, s_len, lambda e: e < cand) < need, cand, best)
        last = lax.fori_loop(0, nbits, tie_body, jnp.zeros((1, LANES), I32))
        eqrow = eqrow_ref[...]
        scores[...] = jnp.where((eqrow > last) & (eqrow < big), -jnp.inf, scores[...])

    def mask_body(cs_, mx):
        mx = list(mx)
        for c in cs_:
            sel = scores[chunk(c), :] >= thr
            for n, (a, p) in enumerate(blocks):
                blk = jnp.where(sel, logits[c, rs(a), cs(p)], -jnp.inf)
                logits[c, rs(a), cs(p)] = blk
                mx[n] = jnp.maximum(mx[n], jnp.max(blk, axis=0, keepdims=True))
        return tuple(mx)
    mx = chunk_loop(mask_body, tuple(jnp.full((1, LANES), -jnp.inf, F32) for _ in blocks))

    acc[...] = jnp.zeros(acc.shape, F32)

    def pv_body(cs_, ls):
        ls = list(ls)
        pv = jnp.zeros(acc.shape, F32)
        for c in cs_:
            rows = []
            for a in range(2):
                cols = []
                for p in range(4):
                    e = jnp.exp(logits[c, rs(a), cs(p)] - mx[a * 4 + p])
                    ls[a * 4 + p] = ls[a * 4 + p] + jnp.sum(e, axis=0, keepdims=True)
                    cols.append(e.astype(BF16))
                rows.append(jnp.concatenate(cols, axis=1))
            pmat = jnp.concatenate(rows, axis=0)
            pv = pv + jnp.dot(vblk_t[c], pmat, preferred_element_type=F32)
        acc[...] = acc[...] + pv
        return tuple(ls)
    ls = chunk_loop(pv_body, tuple(jnp.zeros((1, LANES), F32) for _ in blocks))

    for p in range(4):
        inv = jnp.where(row < HEAD_DIM, 1.0 / ls[p], 1.0 / ls[4 + p])
        o_ref[0, :, cs(p)] = (acc[:, cs(p)] * inv).T.astype(BF16)

    @pl.when(phase == steps_per_seq - 1)
    def _sample_attention():
        _wait_page_copies(kbuf, sems.at[0])
        _wait_page_copies(vbuf, sems.at[1])
        od_ref[0] = _dec_attn_row(dq_ref[0], dkvn_ref[0], dsc_ref[0], dsn_ref[0], dthr_ref[0], dbias_ref[...],
                                  kbuf, vbuf, page)


def _attn(q_st, qi_st, ix, kv, bias_st, page_table, qm, kvn, sc_s, sn_s, thr_s, bias_dec, ck_t, cv_t,
          *, batch, seq, page):
    nq = seq // QB
    assert nq % 2 == 0
    topk = min(TOPK_MAX, seq // 4)
    ix3 = ix.reshape(batch, seq, LANES)
    kv3 = kv.reshape(batch, seq, 2 * LANES)
    db, n_pages = page_table.shape
    past = n_pages * page
    steps_per_seq = (batch * nq) // db
    assert steps_per_seq * db == batch * nq
    drow = lambda n: pl.BlockSpec((1, 1, n), lambda b, j, pt: ((b * nq + j) // steps_per_seq, 0, 0))
    return pl.pallas_call(
        functools.partial(_attn_kernel, n_chunks=nq, topk=topk, n_pages=n_pages, page=page,
                          steps_per_seq=steps_per_seq),
        grid_spec=pltpu.PrefetchScalarGridSpec(
            num_scalar_prefetch=1,
            grid=(batch, nq),
            in_specs=[pl.BlockSpec((1, 4 * QB, LANES), lambda b, j, pt: (b * nq + j, 0, 0)),
                      pl.BlockSpec((1, 4 * QB, LANES), lambda b, j, pt: (b * nq + j, 0, 0)),
                      pl.BlockSpec((1, QB, LANES), lambda b, j, pt: (b, j, 0)),
                      pl.BlockSpec((1, seq, LANES), lambda b, j, pt: (b, 0, 0)),
                      pl.BlockSpec((1, seq, 2 * LANES), lambda b, j, pt: (b, 0, 0)),
                      _full_spec(bias_st.shape),
                      pl.BlockSpec((1, N_HEADS, LANES), lambda b, j, pt: ((b * nq + j) // steps_per_seq, 0, 0)),
                      drow(2 * LANES), drow(past), drow(LANES), drow(LANES), _full_spec(bias_dec.shape),
                      pl.BlockSpec(memory_space=pl.ANY), pl.BlockSpec(memory_space=pl.ANY)],
            out_specs=[pl.BlockSpec((1, QB, 4 * LANES), lambda b, j, pt: (b, j, 0)), drow(4 * LANES)],
            scratch_shapes=[pltpu.VMEM((nq, 2 * QB, LANES), BF16), pltpu.VMEM((nq, LANES, 2 * QB), BF16),
                            pltpu.VMEM((nq, 2 * QB, LANES), BF16), pltpu.VMEM((seq, LANES), I32),
                            pltpu.VMEM((seq, LANES), F32),
                            pltpu.VMEM((nq, 2 * QB, 4 * LANES), F32), pltpu.VMEM((LANES, 4 * LANES), F32),
                            pltpu.VMEM((1, LANES), F32), pltpu.VMEM((1, LANES), F32), pltpu.VMEM((1, LANES), F32),
                            pltpu.VMEM((2 * HEAD_DIM, past), F32), pltpu.VMEM((2 * HEAD_DIM, past), F32),
                            pltpu.SemaphoreType.DMA((2,))]),
        out_shape=[jax.ShapeDtypeStruct((batch, seq, 4 * LANES), BF16),
                   jax.ShapeDtypeStruct((db, 1, 4 * LANES), BF16)],
        compiler_params=_cparams(2),
        name="attn",
    )(page_table, q_st, qi_st, ix3, ix3, kv3, bias_st, qm, kvn, sc_s, sn_s, thr_s, bias_dec, ck_t, cv_t)


def _rglru_gates(xc, wa, ba, wx, bx, lam):
    r = _sigmoid(_dot(xc, wa) + ba)
    i = _sigmoid(_dot(xc, wx) + bx)
    log_a = -RG_C * r * _softplus(-lam)
    a = jnp.exp(log_a)
    u = jnp.sqrt(1.0 - jnp.exp(2.0 * log_a)) * (i * xc)
    return a, u


def _rglru_prompt_kernel(g_ref, xr_ref, buf_ref, h0_ref, cw_ref, cb_ref, wa_ref, ba_ref, wx_ref, bx_ref, lam_ref,
                         o_ref, hl_ref, nb_ref, xs, a_s, u_s, tail, hc, *, tc):
    t = pl.program_id(0)
    width = cw_ref.shape[0]
    nb, _, d = g_ref.shape

    @pl.when(t == 0)
    def _load_state():
        tail[...] = jnp.zeros(tail.shape, F32)
        tail[:, 8 - (width - 1):8, :] = buf_ref[...]
        hc[...] = h0_ref[...]

    xs[:, 0:8, :] = tail[...]
    xs[:, 8:8 + tc, :] = xr_ref[...]
    tail[...] = xs[:, tc:tc + 8, :]
    xc = cb_ref[...]
    for jj in range(width):
        off = 8 - (width - 1) + jj
        xc = xc + cw_ref[jj:jj + 1, :] * xs[:, off:off + tc, :]
    a, u = _rglru_gates(xc.reshape(nb * tc, d), wa_ref[...], ba_ref[...], wx_ref[...], bx_ref[...], lam_ref[...])
    n_lb = d // LANES
    pitch = tc + 8
    for k in range(n_lb):
        for b in range(nb):
            a_s[k, b * pitch:b * pitch + tc, :] = a[b * tc:(b + 1) * tc, k * LANES:(k + 1) * LANES]
            u_s[k, b * pitch:b * pitch + tc, :] = u[b * tc:(b + 1) * tc, k * LANES:(k + 1) * LANES]

    def scan_body(i, hs):
        rows = pl.ds(i, nb, stride=pitch)
        out = []
        for k in range(n_lb):
            h = a_s[k, rows, :] * hs[k] + u_s[k, rows, :]
            u_s[k, rows, :] = h
            out.append(h)
        return tuple(out)
    h0 = hc[...]
    hs = lax.fori_loop(0, tc, scan_body, tuple(h0[:, k * LANES:(k + 1) * LANES] for k in range(n_lb)), unroll=8)
    h = jnp.concatenate(hs, axis=1)
    hc[...] = h
    hseq = jnp.concatenate([jnp.concatenate([u_s[k, b * pitch:b * pitch + tc, :] for b in range(nb)], axis=0)
                            for k in range(n_lb)], axis=1)
    o_ref[...] = (_gelu(g_ref[...].reshape(nb * tc, d)) * hseq).reshape(nb, tc, d).astype(BF16)
    hl_ref[...] = h
    nb_ref[...] = xs[:, tc + 8 - (width - 1):tc + 8, :]


def _rglru_prompt(gate, xr, buf, h0, rg, *, batch, seq, tc=256):
    d = gate.shape[-1]
    width = rg["cw"].shape[0]
    g3, x3 = gate.reshape(batch, seq, d), xr.reshape(batch, seq, d)
    blk = pl.BlockSpec((batch, tc, d), lambda t: (0, t, 0))
    vec = _full_spec((1, d))
    return pl.pallas_call(
        functools.partial(_rglru_prompt_kernel, tc=tc),
        grid=(seq // tc,),
        in_specs=[blk, blk, _full_spec((batch, width - 1, d)), _full_spec((batch, d)), _full_spec((width, d)), vec,
                  _full_spec((d, d)), vec, _full_spec((d, d)), vec, vec],
        out_specs=[blk, pl.BlockSpec((batch, d), lambda t: (0, 0)), pl.BlockSpec((batch, width - 1, d), lambda t: (0, 0, 0))],
        out_shape=[jax.ShapeDtypeStruct((batch, seq, d), BF16), jax.ShapeDtypeStruct((batch, d), F32),
                   jax.ShapeDtypeStruct((batch, width - 1, d), F32)],
        scratch_shapes=[pltpu.VMEM((batch, tc + 8, d), F32), pltpu.VMEM((d // LANES, batch * (tc + 8), LANES), F32),
                        pltpu.VMEM((d // LANES, batch * (tc + 8), LANES), F32), pltpu.VMEM((batch, 8, d), F32),
                        pltpu.VMEM((batch, d), F32)],
        compiler_params=_cparams(1),
        name="rglru_prompt",
    )(g3, x3, buf, h0, rg["cw"], rg["cb"], rg["wa"], rg["ba"], rg["wx"], rg["bx"], rg["lam"])


def _rglru_dec_kernel(g_ref, xr_ref, buf_ref, h0_ref, cw_ref, cb_ref, wa_ref, ba_ref, wx_ref, bx_ref, lam_ref,
                      o_ref, hl_ref):
    width = cw_ref.shape[0]
    xc = cb_ref[...]
    for jj in range(width - 1):
        xc = xc + cw_ref[jj:jj + 1, :] * buf_ref[jj]
    xc = xc + cw_ref[width - 1:width, :] * xr_ref[...]
    a, u = _rglru_gates(xc, wa_ref[...], ba_ref[...], wx_ref[...], bx_ref[...], lam_ref[...])
    h = a * h0_ref[...] + u
    hl_ref[...] = h
    o_ref[...] = (_gelu(g_ref[...]) * h).astype(BF16)


def _rglru_dec(gate, xr, buf_t, h0, rg):
    m, d = gate.shape
    return pl.pallas_call(
        _rglru_dec_kernel,
        out_shape=[jax.ShapeDtypeStruct((m, d), BF16), jax.ShapeDtypeStruct((m, d), F32)],
        name="rglru_dec",
    )(gate, xr, buf_t, h0, rg["cw"], rg["cb"], rg["wa"], rg["ba"], rg["wx"], rg["bx"], rg["lam"])


def _ffn_tile(y1, gf_ref, wup_ref, cw_ref, cb_ref, wdn_ref, conv_prev, n_split):
    d_ff = wdn_ref.shape[0]
    cf = d_ff // n_split
    hn = _rms(y1, gf_ref[...]).astype(BF16)
    out = jnp.zeros(y1.shape, F32)
    gates = []
    for k in range(n_split):
        c0 = k * cf
        g = jnp.dot(hn, wup_ref[:, c0:c0 + cf], preferred_element_type=F32)
        u = jnp.dot(hn, wup_ref[:, d_ff + c0:d_ff + c0 + cf], preferred_element_type=F32)
        g1, g2 = conv_prev(k, g)
        gc = cb_ref[:, c0:c0 + cf] + cw_ref[0:1, c0:c0 + cf] * g2 + cw_ref[1:2, c0:c0 + cf] * g1 \
            + cw_ref[2:3, c0:c0 + cf] * g
        act = (_gelu(gc) * u).astype(BF16)
        out = out + jnp.dot(act, wdn_ref[c0:c0 + cf, :], preferred_element_type=F32)
        gates.append(g)
    return out, gates


def _prompt_conv_prev(gs, carry, fb_ref, nb_ref, tm, cf):
    t = pl.program_id(1)

    @pl.when(t == 0)
    def _load_state():
        carry[...] = jnp.zeros(carry.shape, F32)
        for k in range(carry.shape[0]):
            carry[k, 6:8, :] = fb_ref[0, :, k * cf:(k + 1) * cf]

    def conv_prev(k, g):
        gs[0:8, :] = carry[k]
        gs[8:8 + tm, :] = g
        carry[k] = g[tm - 8:tm, :]
        nb_ref[0, :, k * cf:(k + 1) * cf] = g[tm - 2:tm, :]
        return gs[7:7 + tm, :], gs[6:6 + tm, :]
    return conv_prev


def _mix_ab_tile(y_ref, a_ref, r_ref, woa_ref, wob_ref):
    return y_ref[0] + jnp.dot(a_ref[0], woa_ref[...], preferred_element_type=F32) \
        + jnp.dot(r_ref[0], wob_ref[...], preferred_element_type=F32)


def _post_ab_prompt_kernel(y_ref, a_ref, r_ref, fb_ref, woa_ref, wob_ref, gf_ref, wup_ref, cw_ref, cb_ref, wdn_ref,
                           o_ref, nb_ref, gs, carry, *, tm, n_split):
    y1 = _mix_ab_tile(y_ref, a_ref, r_ref, woa_ref, wob_ref)
    cf = wdn_ref.shape[0] // n_split
    out, _ = _ffn_tile(y1, gf_ref, wup_ref, cw_ref, cb_ref, wdn_ref,
                       _prompt_conv_prev(gs, carry, fb_ref, nb_ref, tm, cf), n_split)
    o_ref[0] = y1 + out


_FFN_KEYS = ("g", "wup", "cw", "cb", "wdn")


def _ffn_specs(ffn):
    layer = ffn["layer"]
    return [pl.BlockSpec((None,) + ffn[k].shape[1:], lambda *_: (layer, 0, 0), pipeline_mode=pl.Buffered(1))
            for k in _FFN_KEYS]


def _ffn_args(ffn):
    return [ffn[k] for k in _FFN_KEYS]


def _post_ab_prompt(y, attn, rgo, fbuf, wo_a, wo_b, ffn, *, tm=512, n_split=2):
    batch, seq, d = y.shape
    d_ff = ffn["wdn"].shape[1]
    cf = d_ff // n_split
    blk = lambda n: pl.BlockSpec((1, tm, n), lambda b, t: (b, t, 0))
    fb = pl.BlockSpec((1, 2, d_ff), lambda b, t: (b, 0, 0))
    return pl.pallas_call(
        functools.partial(_post_ab_prompt_kernel, tm=tm, n_split=n_split),
        grid=(batch, seq // tm),
        in_specs=[blk(d), blk(attn.shape[-1]), blk(rgo.shape[-1]), fb, _full_spec(wo_a.shape), _full_spec(wo_b.shape)]
        + _ffn_specs(ffn),
        out_specs=[blk(d), fb],
        out_shape=[jax.ShapeDtypeStruct((batch, seq, d), F32), jax.ShapeDtypeStruct((batch, 2, d_ff), F32)],
        scratch_shapes=[pltpu.VMEM((tm + 8, cf), F32), pltpu.VMEM((n_split, 8, cf), F32)],
        compiler_params=_cparams(2),
        name="post_ab_prompt",
    )(y, attn, rgo, fbuf, wo_a, wo_b, *_ffn_args(ffn))


def _dec_conv_prev(fb_ref, cf):
    def conv_prev(k, g):
        return fb_ref[1, :, k * cf:(k + 1) * cf], fb_ref[0, :, k * cf:(k + 1) * cf]
    return conv_prev


def _post_ab_dec_kernel(y_ref, a_ref, r_ref, fb_ref, woa_ref, wob_ref, gf_ref, wup_ref, cw_ref, cb_ref, wdn_ref,
                        o_ref, g_ref, *, n_split):
    y1 = y_ref[...] + jnp.dot(a_ref[...], woa_ref[...], preferred_element_type=F32) \
        + jnp.dot(r_ref[...], wob_ref[...], preferred_element_type=F32)
    cf = wdn_ref.shape[0] // n_split
    out, gates = _ffn_tile(y1, gf_ref, wup_ref, cw_ref, cb_ref, wdn_ref, _dec_conv_prev(fb_ref, cf), n_split)
    o_ref[...] = y1 + out
    for k, g in enumerate(gates):
        g_ref[:, k * cf:(k + 1) * cf] = g


def _post_ab_dec(y, attn, rgo, fbuf_t, wo_a, wo_b, ffn, *, n_split=2):
    m, d = y.shape
    d_ff = ffn["wdn"].shape[1]
    args = (y, attn, rgo, fbuf_t, wo_a, wo_b)
    return pl.pallas_call(
        functools.partial(_post_ab_dec_kernel, n_split=n_split),
        grid=(1,),
        in_specs=[_full_spec(a.shape) for a in args] + _ffn_specs(ffn),
        out_specs=[_whole_spec((m, d)), _whole_spec((m, d_ff))],
        out_shape=[jax.ShapeDtypeStruct((m, d), F32), jax.ShapeDtypeStruct((m, d_ff), F32)],
        compiler_params=_cparams(1),
        name="post_ab_dec",
    )(*args, *_ffn_args(ffn))


def _gmlp_in(y, gm_ref, win_ref, bin_ref, sn_ref):
    d_c = win_ref.shape[1] // 2
    z = _gelu(_dot(_rms(y, gm_ref[...]), win_ref[...]) + bin_ref[...])
    return z[:, :d_c], _rms(z[:, d_c:], sn_ref[...])


def _layer_c_prompt_kernel(y_ref, fb_ref, gm_ref, win_ref, bin_ref, sn_ref, sw_ref, sbt_ref, woc_ref,
                           gf_ref, wup_ref, cw_ref, cb_ref, wdn_ref, gfin_ref,
                           o_ref, nb_ref, gs, carry, *, tm, n_split):
    y = y_ref[0]
    u, v = _gmlp_in(y, gm_ref, win_ref, bin_ref, sn_ref)
    vb = v.astype(BF16)
    n_groups = sw_ref.shape[0]
    tril = lax.broadcasted_iota(I32, (CHUNK, CHUNK), 0) >= lax.broadcasted_iota(I32, (CHUNK, CHUNK), 1)
    wm = [jnp.where(tril, sw_ref[gi], 0.0).astype(BF16) for gi in range(n_groups)]
    rows = []
    for r in range(tm // CHUNK):
        cols = []
        for gi in range(n_groups):
            mixed = jnp.dot(wm[gi], vb[r * CHUNK:(r + 1) * CHUNK, gi * LANES:(gi + 1) * LANES],
                            preferred_element_type=F32)
            cols.append(mixed + sbt_ref[:, gi:gi + 1])
        rows.append(jnp.concatenate(cols, axis=1))
    gated = u * jnp.concatenate(rows, axis=0)
    y1 = y + _dot(gated, woc_ref[...])
    cf = wdn_ref.shape[0] // n_split
    out, _ = _ffn_tile(y1, gf_ref, wup_ref, cw_ref, cb_ref, wdn_ref,
                       _prompt_conv_prev(gs, carry, fb_ref, nb_ref, tm, cf), n_split)
    o_ref[0] = _rms(y1 + out, gfin_ref[...])


def _layer_c_prompt(y, fbuf, cp, ffn, g_final, *, tm=512, n_split=2):
    batch, seq, d = y.shape
    d_ff = ffn["wdn"].shape[1]
    cf = d_ff // n_split
    blk = pl.BlockSpec((1, tm, d), lambda b, t: (b, t, 0))
    fb = pl.BlockSpec((1, 2, d_ff), lambda b, t: (b, 0, 0))
    consts = [cp["g"], cp["win"], cp["bin"], cp["sn"], cp["sw"], cp["sbt"], cp["woc"]]
    return pl.pallas_call(
        functools.partial(_layer_c_prompt_kernel, tm=tm, n_split=n_split),
        grid=(batch, seq // tm),
        in_specs=[blk, fb] + [_full_spec(c.shape) for c in consts] + _ffn_specs(ffn) + [_full_spec(g_final.shape)],
        out_specs=[blk, fb],
        out_shape=[jax.ShapeDtypeStruct((batch, seq, d), F32), jax.ShapeDtypeStruct((batch, 2, d_ff), F32)],
        scratch_shapes=[pltpu.VMEM((tm + 8, cf), F32), pltpu.VMEM((n_split, 8, cf), F32)],
        compiler_params=_cparams(2),
        name="layer_c_prompt",
    )(y, fbuf, *consts, *_ffn_args(ffn), g_final)


def _layer_c_dec_kernel(y_ref, fb_ref, gm_ref, win_ref, bin_ref, sn_ref, sw0_ref, sb0_ref, woc_ref,
                        gf_ref, wup_ref, cw_ref, cb_ref, wdn_ref, gfin_ref, o_ref, g_ref, v_ref, *, n_split):
    y = y_ref[...]
    u, v = _gmlp_in(y, gm_ref, win_ref, bin_ref, sn_ref)
    v_ref[...] = v
    y1 = y + _dot(u * (sw0_ref[...] * v + sb0_ref[...]), woc_ref[...])
    cf = wdn_ref.shape[0] // n_split
    out, gates = _ffn_tile(y1, gf_ref, wup_ref, cw_ref, cb_ref, wdn_ref, _dec_conv_prev(fb_ref, cf), n_split)
    o_ref[...] = _rms(y1 + out, gfin_ref[...])
    for k, g in enumerate(gates):
        g_ref[:, k * cf:(k + 1) * cf] = g


def _layer_c_dec(y, fbuf_t, cp, ffn, g_final, *, n_split=2):
    m, d = y.shape
    d_ff = ffn["wdn"].shape[1]
    d_c = cp["woc"].shape[0]
    args = (y, fbuf_t, cp["g"], cp["win"], cp["bin"], cp["sn"], cp["sw0"], cp["sb0"], cp["woc"])
    return pl.pallas_call(
        functools.partial(_layer_c_dec_kernel, n_split=n_split),
        grid=(1,),
        in_specs=[_full_spec(a.shape) for a in args] + _ffn_specs(ffn) + [_full_spec(g_final.shape)],
        out_specs=[_whole_spec((m, d)), _whole_spec((m, d_ff)), _whole_spec((m, d_c))],
        out_shape=[jax.ShapeDtypeStruct((m, d), F32), jax.ShapeDtypeStruct((m, d_ff), F32),
                   jax.ShapeDtypeStruct((m, d_c), F32)],
        compiler_params=_cparams(1),
        name="layer_c_dec",
    )(*args, *_ffn_args(ffn), g_final)


def _start_page_copy(src_ref, pt_ref, b, dst_ref, sem, pg, page, priority=0):
    col = pl.multiple_of(pg * page, page)
    pltpu.make_async_copy(src_ref.at[pt_ref[b, pg]], dst_ref.at[:, pl.ds(col, page)], sem).start(priority=priority)


def _start_page_copies(src_ref, pt_ref, b, dst_ref, sem, n_pages, page):
    assert n_pages % 8 == 0

    def body(g, carry):
        for u in range(8):
            _start_page_copy(src_ref, pt_ref, b, dst_ref, sem, g * 8 + u, page, priority=u % 2)
        return carry
    lax.fori_loop(0, n_pages // 8, body, 0)


def _wait_page_copies(dst_ref, sem):
    pltpu.make_async_copy(dst_ref, dst_ref, sem).wait()


def _dec_score_kernel(pt_ref, qi_ref, wi_ref, ixn_ref, cik_ref, keys_ref, knew_ref, ibuf, sems, *, n_pages, page):
    b = pl.program_id(0)
    nb = pl.num_programs(0)
    slot = lax.rem(b, 2)

    def start(bb, sl):
        _start_page_copies(cik_ref, pt_ref, bb, ibuf.at[sl], sems.at[sl], n_pages, page)

    @pl.when(b == 0)
    def _first():
        start(0, 0)

    @pl.when(b + 1 < nb)
    def _prefetch_next():
        start(b + 1, 1 - slot)

    _wait_page_copies(ibuf.at[slot], sems.at[slot])
    qi = qi_ref[0].astype(BF16)
    wi = wi_ref[0]
    s = jnp.dot(qi, ibuf[slot].astype(BF16), preferred_element_type=F32)
    sc = jnp.sum(jnp.maximum(s, 0.0) * wi, axis=0, keepdims=True) * IDX_HEADS ** -0.5
    keys_ref[0] = sc
    kin = ixn_ref[0][:, 0:IDX_DIM]
    sn = jnp.sum(qi_ref[0] * kin, axis=1, keepdims=True)
    scn = jnp.sum(jnp.maximum(sn, 0.0) * wi, axis=0, keepdims=True) * IDX_HEADS ** -0.5
    knew_ref[0] = jnp.broadcast_to(scn, (1, LANES))


def _dec_scores(page_table, qi3, wi3, ix3, cik_t, *, page):
    db, n_pages = page_table.shape
    past = n_pages * page
    return pl.pallas_call(
        functools.partial(_dec_score_kernel, n_pages=n_pages, page=page),
        grid_spec=pltpu.PrefetchScalarGridSpec(
            num_scalar_prefetch=1,
            grid=(db,),
            in_specs=[pl.BlockSpec((1, IDX_HEADS, IDX_DIM), lambda b, pt: (b, 0, 0)),
                      pl.BlockSpec((1, IDX_HEADS, 1), lambda b, pt: (b, 0, 0)),
                      pl.BlockSpec((1, 1, LANES), lambda b, pt: (b, 0, 0)),
                      pl.BlockSpec(memory_space=pl.ANY)],
            out_specs=[pl.BlockSpec((1, 1, past), lambda b, pt: (b, 0, 0)),
                       pl.BlockSpec((1, 1, LANES), lambda b, pt: (b, 0, 0))],
            scratch_shapes=[pltpu.VMEM((2, IDX_DIM, past), F32), pltpu.SemaphoreType.DMA((2,))]),
        out_shape=[jax.ShapeDtypeStruct((db, 1, past), F32), jax.ShapeDtypeStruct((db, 1, LANES), F32)],
        compiler_params=_cparams(1),
        name="dec_scores",
    )(page_table, qi3, wi3, ix3, cik_t)


def _dec_select_kernel(sc_ref, scn_ref, so_ref, sno_ref, thr_ref, *, topk):
    past = sc_ref.shape[1]
    lane0 = lax.broadcasted_iota(I32, scn_ref.shape, 1) == 0
    sc = jnp.concatenate([sc_ref[...], jnp.where(lane0, scn_ref[...], -jnp.inf)], axis=1)
    kk = _float_key(sc)
    kf = jnp.float32(topk)

    def count(pred):
        ones = jnp.where(pred, 1.0, 0.0)
        accs = [ones[:, k * LANES:(k + 1) * LANES] for k in range(8)]
        for k in range(8, ones.shape[1] // LANES):
            accs[k % 8] = accs[k % 8] + ones[:, k * LANES:(k + 1) * LANES]
        return jnp.sum(functools.reduce(lambda x, y: x + y, accs), axis=1, keepdims=True)

    def search_body(i, ans):
        cand = ans | jnp.left_shift(jnp.int32(1), 31 - i)
        return jnp.where(count(kk >= (cand ^ jnp.int32(INT_MIN))) >= kf, cand, ans)
    ans = lax.fori_loop(0, 32, search_body, jnp.zeros((sc.shape[0], 1), I32))
    thr = _key_float(ans ^ jnp.int32(INT_MIN))
    counts = lambda t: (count(sc >= t), count(sc > t))
    thr, cge, cgt = _walk_to_kth(lambda: sc, thr, *counts(thr), kf, counts, 4 * topk)
    need = kf - cgt
    big = jnp.int32(4 * past)
    eqcol = jnp.where(sc == thr, lax.broadcasted_iota(I32, sc.shape, 1), big)
    nbits = int(math.log2(past)) + 1

    def tie_body(i, best):
        cand = best | jnp.left_shift(jnp.int32(1), nbits - 1 - i)
        return jnp.where(count(eqcol < cand) < need, cand, best)
    last = lax.fori_loop(0, nbits, tie_body, jnp.zeros((sc.shape[0], 1), I32))
    sc = jnp.where((eqcol > last) & (eqcol < big), -jnp.inf, sc)
    so_ref[...] = sc[:, :past]
    sno_ref[...] = sc[:, past:]
    thr_ref[...] = jnp.broadcast_to(thr, thr_ref.shape)


def _dec_select(scores, snew, *, topk):
    db, past = scores.shape
    assert past + 1 >= topk
    return pl.pallas_call(
        functools.partial(_dec_select_kernel, topk=topk),
        out_shape=[jax.ShapeDtypeStruct((db, past), F32), jax.ShapeDtypeStruct((db, LANES), F32),
                   jax.ShapeDtypeStruct((db, LANES), F32)],
        name="dec_select",
    )(scores, snew)


def _dec_attn_row(qm, kvn, sc, sn, thr_row, bias, kbuf, vbuf, page):
    past = kbuf.shape[1]
    thr = thr_row[:, 0:1]
    sel = sc >= thr
    sel_new = sn[:, 0:1] >= thr
    far, last, bnew = bias[:, LANES:LANES + 1], bias[:, 0:page], bias[:, LANES + 1:LANES + 2]
    step = min(past, DEC_KEY_CHUNK)
    chunks = [(c, min(c + step, past)) for c in range(0, past, step)]
    qb = qm.astype(BF16)
    parts = []
    for c0, c1 in chunks:
        part = jnp.dot(qb, kbuf[:, c0:c1].astype(BF16), preferred_element_type=F32) + far
        if c1 == past:
            part = jnp.concatenate([part[:, :c1 - c0 - page], part[:, c1 - c0 - page:] + (last - far)], axis=1)
        parts.append(jnp.where(sel[:, c0:c1], part, -jnp.inf))
    lgn = jnp.sum(qm * kvn[:, 0:LANES], axis=1, keepdims=True) + bnew
    lgn = jnp.where(sel_new, lgn, -jnp.inf)
    m = lgn
    for part in parts:
        m = jnp.maximum(m, jnp.max(part, axis=1, keepdims=True))
    en = jnp.exp(lgn - m)
    den = en
    pv = en * kvn[:, LANES:2 * LANES]
    for (c0, c1), part in zip(chunks, parts):
        e = jnp.exp(part - m)
        den = den + jnp.sum(e, axis=1, keepdims=True)
        pv = pv + _dot_nt(e.astype(BF16), vbuf[:, c0:c1].astype(BF16))
    pv = pv / den
    lo = lax.broadcasted_iota(I32, (1, LANES), 1) < HEAD_DIM
    return jnp.concatenate([jnp.where(lo, pv[2 * p:2 * p + 1], pv[2 * p + 1:2 * p + 2]) for p in range(4)],
                           axis=1).astype(BF16)


def _prep_in_ab(w):
    d = w.shape[0]
    nq, nkv = N_HEADS * HEAD_DIM, N_KV_HEADS * HEAD_DIM
    offs = np.cumsum([nq, nkv, nkv, IDX_HEADS * IDX_DIM, IDX_DIM, IDX_HEADS, 512])
    q, k, v, qi, ki, wi, g, xr = jnp.split(w, offs.tolist(), axis=1)
    q = q.reshape(d, N_HEADS, HEAD_DIM)[:, np.array(HEAD_PERM), :].reshape(d, nq)
    pad = jnp.zeros((d, _C_G - _C_IX - IDX_DIM - IDX_HEADS), w.dtype)
    return jnp.concatenate([q, k, v, qi, ki, wi, pad, g, xr], axis=1).astype(BF16)


def _block_diag(w):
    n, c, _ = w.shape
    return (jnp.eye(n, dtype=w.dtype)[:, None, :, None] * w[:, :, None, :]).reshape(n * c, n * c).astype(BF16)


def _ffn_params(layer, stacked):
    return dict(stacked, layer=layer)


def kernel(x_prompt, x_sample, cache_k, cache_v, cache_idx_k, state_rglru_h, state_rglru_conv, state_ffn_conv,
           page_table, norm_mix, norm_ffn, norm_final, rel_bias, w_in_ab, w_out_ab, rg_conv_w, rg_conv_b,
           rg_wa, rg_ba, rg_wx, rg_bx, rg_lambda, w_in_c, b_in_c, sgu_norm, sgu_w, sgu_b, w_out_c,
           ffn_w_up, ffn_conv_w, ffn_conv_b, ffn_w_down):
    batch, seq, d = x_prompt.shape
    db = x_sample.shape[0]
    page = cache_k.shape[2]
    d_a = N_HEADS * HEAD_DIM
    d_b = rg_conv_w.shape[-1]
    d_ff = ffn_w_down.shape[1]
    assert x_sample.shape[1] == 1 and seq % 512 == 0 and page == LANES and w_in_ab.shape[0] == 1

    w_in0 = _prep_in_ab(w_in_ab[0])
    wo = w_out_ab[0]
    wo_a = wo[:d_a].reshape(N_HEADS, HEAD_DIM, d)[np.array(HEAD_PERM)].reshape(d_a, d).astype(BF16)
    wo_b = wo[d_a:].astype(BF16)
    rg = {"cw": rg_conv_w[0], "cb": rg_conv_b[0][None], "wa": _block_diag(rg_wa[0]), "ba": rg_ba[0][None],
          "wx": _block_diag(rg_wx[0]), "bx": rg_bx[0][None], "lam": rg_lambda[0][None]}
    ffn_all = {"g": norm_ffn[:, None, :], "wup": ffn_w_up.astype(BF16), "cw": ffn_conv_w, "cb": ffn_conv_b[:, None, :],
               "wdn": ffn_w_down.astype(BF16)}
    ffn0, ffn1 = _ffn_params(0, ffn_all), _ffn_params(1, ffn_all)
    cp = {"g": norm_mix[1][None], "win": w_in_c[0].astype(BF16), "bin": b_in_c[0][None], "sn": sgu_norm[0][None],
          "sw": sgu_w[0], "sbt": sgu_b[0].T, "woc": w_out_c[0].astype(BF16),
          "sw0": jnp.repeat(sgu_w[0][:, 0, 0], d // sgu_w.shape[1])[None],
          "sb0": jnp.repeat(sgu_b[0][:, 0], d // sgu_w.shape[1])[None]}
    g_mix0 = norm_mix[0][None]
    g_final = norm_final[None]
    bias_st, bias_dec = _bias_tables(rel_bias, page)

    xp = x_prompt.reshape(batch * seq, d)
    xs = x_sample.reshape(db, d)
    q_st, qi_st, kv_p, ix_p, gate_p, xr_p = _inproj(xp, g_mix0, w_in0, stack=True, tm=512)
    qm_s, qi_s, kv_s, ix_s, gate_s, xr_s = _inproj(xs, g_mix0, w_in0, stack=False, tm=db)
    cik_t = jnp.transpose(cache_idx_k[0], (0, 2, 1))
    ck_t = jnp.transpose(cache_k[0], (0, 2, 3, 1)).reshape(-1, 2 * HEAD_DIM, page)
    cv_t = jnp.transpose(cache_v[0], (0, 2, 3, 1)).reshape(-1, 2 * HEAD_DIM, page)
    topk_s = min(TOPK_MAX, (page_table.shape[1] * page + 1) // 4)
    sc_s, sn_s = _dec_scores(page_table, qi_s.reshape(db, IDX_HEADS, IDX_DIM),
                             ix_s[:, IDX_DIM:IDX_DIM + IDX_HEADS].reshape(db, IDX_HEADS, 1),
                             ix_s.reshape(db, 1, LANES), cik_t, page=page)
    sc_s, sn_s, thr_s = _dec_select(sc_s.reshape(db, -1), sn_s.reshape(db, LANES), topk=topk_s)

    attn_p, attn_s = _attn(q_st, qi_st, ix_p, kv_p, bias_st, page_table, jnp.transpose(qm_s, (1, 0, 2)),
                           kv_s.reshape(db, 1, 2 * LANES), sc_s.reshape(db, 1, -1), sn_s.reshape(db, 1, LANES),
                           thr_s.reshape(db, 1, LANES), bias_dec, ck_t, cv_t, batch=batch, seq=seq, page=page)
    attn_s = attn_s.reshape(db, d_a)

    rg_p, h_p, cbuf_p = _rglru_prompt(gate_p, xr_p, jnp.zeros((batch, rg["cw"].shape[0] - 1, d_b), F32),
                                      jnp.zeros((batch, d_b), F32), rg, batch=batch, seq=seq)
    zero_fb = jnp.zeros((batch, 2, d_ff), F32)
    y1_p, fb0_p = _post_ab_prompt(x_prompt, attn_p, rg_p, zero_fb, wo_a, wo_b, ffn0)
    y_p, fb1_p = _layer_c_prompt(y1_p, zero_fb, cp, ffn1, g_final)

    cbuf_s_in = state_rglru_conv[0]
    rg_s, h_s = _rglru_dec(gate_s, xr_s, jnp.transpose(cbuf_s_in, (1, 0, 2)), state_rglru_h[0], rg)
    y1_s, g0_s = _post_ab_dec(xs, attn_s, rg_s, jnp.transpose(state_ffn_conv[0], (1, 0, 2)), wo_a, wo_b, ffn0)
    y_s, g1_s, v_s = _layer_c_dec(y1_s, jnp.transpose(state_ffn_conv[1], (1, 0, 2)), cp, ffn1, g_final)

    kv4 = kv_p.reshape(batch, seq, 2, N_KV_HEADS, HEAD_DIM)
    kvs = kv_s.reshape(db, 1, 2, N_KV_HEADS, HEAD_DIM)
    fbuf_s = lambda layer, g: jnp.concatenate([state_ffn_conv[layer][:, 1:], g[:, None]], axis=1)
    return (y_p, y_s.reshape(db, 1, d),
            kv4[None, :, :, 0], kv4[None, :, :, 1], ix_p.reshape(batch, seq, LANES)[None, :, :, :IDX_DIM],
            kvs[None, :, :, 0], kvs[None, :, :, 1], ix_s.reshape(db, 1, LANES)[None, :, :, :IDX_DIM],
            h_p.reshape(batch, d_b)[None], cbuf_p[None],
            h_s[None], jnp.concatenate([cbuf_s_in[:, 1:], xr_s[:, None]], axis=1)[None],
            v_s.reshape(db, 1, -1)[None],
            jnp.stack([fb0_p, fb1_p]), jnp.stack([fbuf_s(0, g0_s), fbuf_s(1, g1_s)]))
```
